```python
import math
import jax, jax.numpy as jnp
from jax import lax
import numpy as np

D_MODEL = 1024
BATCH = 2
SEQ = 8192
DEPTH = 1

CHUNK = 64
SSD_HEADS = 16
SSD_HEAD_DIM = 64
SSD_WIDTH = SSD_HEADS * SSD_HEAD_DIM
SSD_GROUPS = 2
SSD_HPG = SSD_HEADS // SSD_GROUPS
SSD_STATE = 128
CONV_WIDTH = 4
CONV_CH = SSD_WIDTH + 2 * SSD_GROUPS * SSD_STATE
DA_HEADS = 8
DA_HEAD_DIM = 64
DA_QK = DA_HEADS * 2 * DA_HEAD_DIM
DA_WIDTH = DA_HEADS * 2 * DA_HEAD_DIM
ROPE_THETA = 500000.0
ROT_DIM = DA_HEAD_DIM // 4
Q_BLOCK = 128
N_BRANCH = 2
EPS = 1e-5
ALPHA = (2.0 * DEPTH) ** 0.25
BETA = (8.0 * DEPTH) ** -0.25
IN_SIZES = (SSD_WIDTH, CONV_CH, SSD_HEADS, DA_QK, DA_QK, DA_WIDTH, DA_WIDTH, N_BRANCH * D_MODEL)
IN_TOTAL = sum(IN_SIZES)

kernel_name = "hybrid_ssd_diffattn_gated_deepnorm"


def _split_points(sizes):
    pts, acc = [], 0
    for s in sizes[:-1]:
        acc += s
        pts.append(acc)
    return pts


def _rms(t, eps=EPS):
    tf = t.astype(jnp.float32)
    return tf * lax.rsqrt(jnp.mean(tf * tf, axis=-1, keepdims=True) + eps)


def _layernorm(t, g, b):
    tf = t.astype(jnp.float32)
    mu = jnp.mean(tf, axis=-1, keepdims=True)
    var = jnp.mean(jnp.square(tf - mu), axis=-1, keepdims=True)
    return ((tf - mu) * lax.rsqrt(var + EPS) * g + b).astype(t.dtype)


def _causal_conv(u, w, b):
    out = lax.conv_general_dilated(
        u, w[:, None, :].astype(u.dtype), window_strides=(1,),
        padding=[(CONV_WIDTH - 1, 0)], dimension_numbers=("NWC", "WIO", "NWC"),
        feature_group_count=u.shape[-1])
    return out + b


def _rope(t, cos, sin):
    half = ROT_DIM // 2
    c = cos[:, None, None, :]
    s = sin[:, None, None, :]
    t1, t2, tp = t[..., :half], t[..., half:ROT_DIM], t[..., ROT_DIM:]
    out = jnp.concatenate([t1 * c - t2 * s, t2 * c + t1 * s, tp.astype(jnp.float32)], axis=-1)
    return out.astype(t.dtype)


def _ssd(xs, dt, a, bm, cm):
    b, s = xs.shape[0], xs.shape[1]
    nc = s // CHUNK
    X = xs.reshape(b, nc, CHUNK, SSD_GROUPS, SSD_HPG, SSD_HEAD_DIM)
    dtc = dt.reshape(b, nc, CHUNK, SSD_GROUPS, SSD_HPG)
    a_dt = dtc * a.astype(jnp.float32).reshape(SSD_GROUPS, SSD_HPG)
    xdt = X * dtc[..., None]
    Bc = bm.reshape(b, nc, CHUNK, SSD_GROUPS, SSD_STATE)
    Cc = cm.reshape(b, nc, CHUNK, SSD_GROUPS, SSD_STATE)
    a_cs = jnp.cumsum(a_dt, axis=2)
    seg = a_cs[:, :, :, None] - a_cs[:, :, None, :]
    causal = jnp.tril(jnp.ones((CHUNK, CHUNK), dtype=bool))[:, :, None, None]
    L = jnp.exp(jnp.where(causal, seg, -jnp.inf))
    cb = jnp.einsum("bclgn,bcsgn->bclsg", Cc, Bc)
    y_diag = jnp.einsum("bclsgj,bcsgjp->bclgjp", cb[..., None] * L, xdt)
    decay = jnp.exp(a_cs[:, :, -1:] - a_cs)
    states = jnp.einsum("bclgn,bclgj,bclgjp->bcgjpn", Bc, decay, xdt)
    chunk_decay = jnp.exp(a_cs[:, :, -1])

    def step(h, inp):
        s_c, d_c = inp
        return d_c[..., None, None] * h + s_c, h

    h0 = jnp.zeros_like(states[:, 0])
    _, prev = lax.scan(step, h0, (jnp.moveaxis(states, 1, 0), jnp.moveaxis(chunk_decay, 1, 0)))
    prev = jnp.moveaxis(prev, 0, 1)
    y_off = jnp.einsum("bclgn,bcgjpn,bclgj->bclgjp", Cc, prev, jnp.exp(a_cs))
    return (y_diag + y_off).reshape(b, s, SSD_HEADS, SSD_HEAD_DIM)


def _diff_attention(q, k, v, lam):
    b, s = q.shape[0], q.shape[1]
    nb = s // Q_BLOCK
    scale = DA_HEAD_DIM ** -0.5
    qb = jnp.moveaxis(q.reshape(b, nb, Q_BLOCK, DA_HEADS, 2, DA_HEAD_DIM), 1, 0)
    k_chunk = jnp.arange(s) // CHUNK

    def block(args):
        qi, i = args
        q_chunk = (i * Q_BLOCK + jnp.arange(Q_BLOCK)) // CHUNK
        mask = k_chunk[None, :] <= q_chunk[:, None]
        sc = jnp.einsum("bqhmd,bkhmd->bhmqk", qi, k).astype(jnp.float32) * scale
        p = jax.nn.softmax(jnp.where(mask, sc, -jnp.inf), axis=-1)
        att = p[:, :, 0] - lam * p[:, :, 1]
        return jnp.einsum("bhqk,bkhe->bqhe", att.astype(v.dtype), v)

    out = lax.map(block, (qb, jnp.arange(nb)))
    return jnp.moveaxis(out, 0, 1).reshape(b, s, DA_HEADS, 2 * DA_HEAD_DIM)


def setup_inputs(seed: int = 0) -> dict:
    key = jax.random.key(seed)
    ks = jax.random.split(key, 20)
    f32 = jnp.float32
    x = jax.random.normal(ks[0], (BATCH, SEQ, D_MODEL), f32)
    w_in = jax.random.normal(ks[1], (DEPTH, D_MODEL, IN_TOTAL), f32) * D_MODEL ** -0.5
    b_gate = 0.02 * jax.random.normal(ks[2], (DEPTH, N_BRANCH * D_MODEL), f32)
    conv_w = 0.5 * jax.random.normal(ks[3], (DEPTH, CONV_WIDTH, CONV_CH), f32)
    conv_b = 0.02 * jax.random.normal(ks[4], (DEPTH, CONV_CH), f32)
    dt0 = jnp.exp(jax.random.uniform(ks[5], (DEPTH, SSD_HEADS), f32, math.log(1e-3), math.log(1e-1)))
    dt_bias = dt0 + jnp.log(-jnp.expm1(-dt0))
    a_log = jnp.log(jax.random.uniform(ks[6], (DEPTH, SSD_HEADS), f32, 1.0, 16.0))
    d_skip = 1.0 + 0.02 * jax.random.normal(ks[7], (DEPTH, SSD_HEADS), f32)
    ssd_norm_w = 1.0 + 0.02 * jax.random.normal(ks[8], (DEPTH, SSD_WIDTH), f32)
    lambda_q1 = 0.1 * jax.random.normal(ks[9], (DEPTH, DA_HEAD_DIM), f32)
    lambda_k1 = 0.1 * jax.random.normal(ks[10], (DEPTH, DA_HEAD_DIM), f32)
    lambda_q2 = 0.1 * jax.random.normal(ks[11], (DEPTH, DA_HEAD_DIM), f32)
    lambda_k2 = 0.1 * jax.random.normal(ks[12], (DEPTH, DA_HEAD_DIM), f32)
    subln_w = 1.0 + 0.02 * jax.random.normal(ks[13], (DEPTH, 2 * DA_HEAD_DIM), f32)
    w_a = jax.random.normal(ks[14], (DEPTH, SSD_WIDTH, D_MODEL), f32) * SSD_WIDTH ** -0.5 * BETA
    w_b = jax.random.normal(ks[15], (DEPTH, DA_WIDTH, D_MODEL), f32) * DA_WIDTH ** -0.5 * BETA
    w_o = jax.random.normal(ks[16], (DEPTH, D_MODEL, D_MODEL), f32) * D_MODEL ** -0.5 * BETA
    ln_g = 1.0 + 0.02 * jax.random.normal(ks[17], (DEPTH, D_MODEL), f32)
    ln_b = 0.02 * jax.random.normal(ks[18], (DEPTH, D_MODEL), f32)
    return {"x": x, "w_in": w_in, "b_gate": b_gate, "conv_w": conv_w, "conv_b": conv_b,
            "dt_bias": dt_bias, "a_log": a_log, "d_skip": d_skip, "ssd_norm_w": ssd_norm_w,
            "lambda_q1": lambda_q1, "lambda_k1": lambda_k1, "lambda_q2": lambda_q2,
            "lambda_k2": lambda_k2, "subln_w": subln_w, "w_a": w_a, "w_b": w_b, "w_o": w_o,
            "ln_g": ln_g, "ln_b": ln_b}


def reference(x, w_in, b_gate, conv_w, conv_b, dt_bias, a_log, d_skip, ssd_norm_w,
              lambda_q1, lambda_k1, lambda_q2, lambda_k2, subln_w, w_a, w_b, w_o,
              ln_g, ln_b):
    b, s, _ = x.shape
    pos = jnp.arange(s, dtype=jnp.float32)
    inv_freq = ROPE_THETA ** (-jnp.arange(0, ROT_DIM, 2, dtype=jnp.float32) / ROT_DIM)
    ang = pos[:, None] * inv_freq[None, :]
    cos, sin = jnp.cos(ang), jnp.sin(ang)

    for l in range(DEPTH):
        h = jnp.einsum("bsd,de->bse", x, w_in[l])
        z, xbc, dt_raw, q, k, v, g_b, g_merge = jnp.split(h, _split_points(IN_SIZES), axis=-1)

        xbc = jax.nn.silu(_causal_conv(xbc, conv_w[l], conv_b[l]))
        xs, bm, cm = jnp.split(xbc, [SSD_WIDTH, SSD_WIDTH + SSD_GROUPS * SSD_STATE], axis=-1)
        xs = xs.reshape(b, s, SSD_HEADS, SSD_HEAD_DIM)
        bm = bm.reshape(b, s, SSD_GROUPS, SSD_STATE)
        cm = cm.reshape(b, s, SSD_GROUPS, SSD_STATE)
        dt = jax.nn.softplus(dt_raw.astype(jnp.float32) + dt_bias[l].astype(jnp.float32))
        a = -jnp.exp(a_log[l].astype(jnp.float32))
        y_ssd = _ssd(xs, dt, a, bm, cm) + d_skip[l][:, None] * xs
        gated = (y_ssd.reshape(b, s, SSD_WIDTH) * jax.nn.silu(z)).reshape(
            b, s, SSD_GROUPS, SSD_WIDTH // SSD_GROUPS)
        y_ssd = (_rms(gated).reshape(b, s, SSD_WIDTH) * ssd_norm_w[l]).astype(x.dtype)
        branch_a = jnp.einsum("bse,ed->bsd", y_ssd, w_a[l])

        q = _rope(q.reshape(b, s, DA_HEADS, 2, DA_HEAD_DIM), cos, sin)
        k = _rope(k.reshape(b, s, DA_HEADS, 2, DA_HEAD_DIM), cos, sin)
        v = v.reshape(b, s, DA_HEADS, 2 * DA_HEAD_DIM)
        lambda_init = 0.8 - 0.6 * math.exp(-0.3 * l)
        lam = (jnp.exp(jnp.sum(lambda_q1[l].astype(jnp.float32) * lambda_k1[l]))
               - jnp.exp(jnp.sum(lambda_q2[l].astype(jnp.float32) * lambda_k2[l]))
               + lambda_init)
        o = _diff_attention(q, k, v, lam)
        o = _rms(o) * subln_w[l] * (1.0 - lambda_init)
        o = (o.reshape(b, s, DA_WIDTH) * jax.nn.silu(g_b)).astype(x.dtype)
        branch_b = jnp.einsum("bse,ed->bsd", o, w_b[l])

        gates = jax.nn.sigmoid(g_merge + b_gate[l]).reshape(b, s, N_BRANCH, D_MODEL)
        merged = gates[:, :, 0] * branch_a + gates[:, :, 1] * branch_b
        y = jnp.einsum("bsd,de->bse", merged, w_o[l])
        x = _layernorm(ALPHA * x + y, ln_g[l], ln_b[l])
    return x
```

```python
import functools
import math

import jax
import jax.numpy as jnp
from jax import lax
from jax.experimental import pallas as pl
from jax.experimental.pallas import tpu as pltpu

F32 = jnp.float32
BF16 = jnp.bfloat16

D_MODEL = 1024
CHUNK = 64
SSD_HEADS = 16
SSD_HEAD_DIM = 64
SSD_WIDTH = SSD_HEADS * SSD_HEAD_DIM
SSD_GROUPS = 2
SSD_STATE = 128
CONV_WIDTH = 4
BC_WIDTH = SSD_GROUPS * SSD_STATE
CONV_CH = SSD_WIDTH + 2 * BC_WIDTH
DA_HEADS = 8
DA_HEAD_DIM = 64
DA_VDIM = 2 * DA_HEAD_DIM
DA_WIDTH = DA_HEADS * DA_VDIM
ROPE_THETA = 500000.0
ROT_DIM = DA_HEAD_DIM // 4
ROT_HALF = ROT_DIM // 2
N_BRANCH = 2
EPS = 1e-5
IN_SIZES = (SSD_WIDTH, CONV_CH, SSD_HEADS, DA_WIDTH, DA_WIDTH, DA_WIDTH, DA_WIDTH, N_BRANCH * D_MODEL)

SZ_OFF = 0
XS_OFF = SZ_OFF + SSD_WIDTH
Q_OFF = XS_OFF + SSD_WIDTH
K_OFF = Q_OFF + DA_WIDTH
V_OFF = K_OFF + DA_WIDTH
GB_OFF = V_OFF + DA_WIDTH
GM_OFF = GB_OFF + DA_WIDTH
B_OFF = GM_OFF + N_BRANCH * D_MODEL
C_OFF = B_OFF + BC_WIDTH
H_COLS = C_OFF + BC_WIDTH

LANES = 128
SUBLANES = 8
VMEM_LIMIT_BYTES = 56 * 1024 * 1024

IN_TM = 1024
IN_TN = 512
SSD_ROWS = 512
ATT_TQ = 256
ATT_TK = 1024
OUT_TM = 512

NEG_BIG = -1e30
LOG2E = 1.4426950408889634


def _sigmoid(t):
    return 1.0 / (1.0 + jnp.exp(-t))


def _split3(v):
    hi = v.astype(BF16)
    r1 = v - hi.astype(F32)
    mid = r1.astype(BF16)
    lo = (r1 - mid.astype(F32)).astype(BF16)
    return hi, mid, lo


def _dot01_right(v, m01):
    hi, mid, lo = _split3(v)
    d = functools.partial(jnp.dot, preferred_element_type=F32)
    return d(hi, m01) + d(mid, m01) + d(lo, m01)


def _dot01_left(m01, v):
    hi, mid, lo = _split3(v)
    d = functools.partial(jnp.dot, preferred_element_type=F32)
    return d(m01, hi) + d(m01, mid) + d(m01, lo)


def _inproj_kernel(x_ref, xh_ref, w_ref, p_ref, cos_ref, sa_ref, sb_ref, wdt_ref, dtb_ref,
                   h_ref, dt_ref, ext_ref, *, tiles_per_seq, q_scale):
    i = pl.program_id(0)
    j = pl.program_id(1)
    tm, tn = h_ref.shape
    x = x_ref[...]
    acc = jnp.dot(x, w_ref[...], preferred_element_type=F32)

    def blk(off):
        return off // tn

    @pl.when(j == 0)
    def _():
        raw = jnp.dot(x, wdt_ref[...], preferred_element_type=F32) + dtb_ref[...]
        dt_ref[...] = jnp.maximum(raw, 0.0) + jnp.log(1.0 + jnp.exp(-jnp.abs(raw)))

    is_silu = (j < blk(XS_OFF)) | ((j >= blk(GB_OFF)) & (j < blk(GM_OFF)))

    @pl.when(is_silu)
    def _():
        h_ref[...] = (acc * _sigmoid(acc)).astype(h_ref.dtype)

    @pl.when(((j >= blk(XS_OFF)) & (j < blk(Q_OFF))) | (j >= blk(B_OFF)))
    def _():
        halo = jnp.dot(xh_ref[...], w_ref[...], preferred_element_type=F32)
        halo = jnp.where(i % tiles_per_seq == 0, 0.0, halo)
        ext_ref[0:SUBLANES, :] = halo
        ext_ref[SUBLANES:, :] = acc
        out = p_ref[CONV_WIDTH:CONV_WIDTH + 1, :]
        for k in range(CONV_WIDTH):
            start = SUBLANES - (CONV_WIDTH - 1) + k
            out = out + p_ref[k:k + 1, :] * ext_ref[pl.ds(start, tm), :]
        h_ref[...] = (out * _sigmoid(out)).astype(h_ref.dtype)

    @pl.when((j >= blk(Q_OFF)) & (j < blk(V_OFF)))
    def _():
        reps = tn // LANES
        cos = jnp.concatenate([cos_ref[...]] * reps, axis=1)
        sa = jnp.concatenate([sa_ref[...]] * reps, axis=1)
        sb = jnp.concatenate([sb_ref[...]] * reps, axis=1)
        up = pltpu.roll(acc, tn - ROT_HALF, axis=1)
        dn = pltpu.roll(acc, ROT_HALF, axis=1)
        rot = acc * cos + up * sa + dn * sb
        rot = jnp.where(j < blk(K_OFF), rot * q_scale, rot)
        h_ref[...] = rot.astype(h_ref.dtype)

    @pl.when((j >= blk(V_OFF)) & (j < blk(GB_OFF)))
    def _():
        h_ref[...] = acc.astype(h_ref.dtype)

    @pl.when((j >= blk(GM_OFF)) & (j < blk(B_OFF)))
    def _():
        h_ref[...] = _sigmoid(acc + p_ref[CONV_WIDTH:CONV_WIDTH + 1, :]).astype(h_ref.dtype)


def _inproj(x2d, w_main, params, cos_t, sa_t, sb_t, w_dt, dt_bias, seq):
    t_rows = x2d.shape[0]
    tm, tn = IN_TM, IN_TN
    assert t_rows % tm == 0 and seq % tm == 0 and H_COLS % tn == 0
    for off in (XS_OFF, Q_OFF, K_OFF, V_OFF, GB_OFF, GM_OFF, B_OFF):
        assert off % tn == 0
    tiles_per_seq = seq // tm
    halo_blocks = tm // SUBLANES
    kern = functools.partial(_inproj_kernel, tiles_per_seq=tiles_per_seq,
                             q_scale=DA_HEAD_DIM ** -0.5 * LOG2E)
    return pl.pallas_call(
        kern,
        grid=(t_rows // tm, H_COLS // tn),
        in_specs=[
            pl.BlockSpec((tm, D_MODEL), lambda i, j: (i, 0)),
            pl.BlockSpec((SUBLANES, D_MODEL), lambda i, j: (jnp.maximum(i * halo_blocks - 1, 0), 0)),
            pl.BlockSpec((D_MODEL, tn), lambda i, j: (0, j)),
            pl.BlockSpec((SUBLANES, tn), lambda i, j: (0, j)),
            pl.BlockSpec((tm, LANES), lambda i, j: (i % tiles_per_seq, 0)),
            pl.BlockSpec((tm, LANES), lambda i, j: (i % tiles_per_seq, 0)),
            pl.BlockSpec((tm, LANES), lambda i, j: (i % tiles_per_seq, 0)),
            pl.BlockSpec((D_MODEL, SSD_HEADS), lambda i, j: (0, 0)),
            pl.BlockSpec((1, SSD_HEADS), lambda i, j: (0, 0)),
        ],
        out_specs=[
            pl.BlockSpec((tm, tn), lambda i, j: (i, j)),
            pl.BlockSpec((tm, SSD_HEADS), lambda i, j: (i, 0)),
        ],
        out_shape=[
            jax.ShapeDtypeStruct((t_rows, H_COLS), BF16),
            jax.ShapeDtypeStruct((t_rows, SSD_HEADS), F32),
        ],
        scratch_shapes=[pltpu.VMEM((tm + SUBLANES, tn), F32)],
        compiler_params=pltpu.CompilerParams(
            dimension_semantics=("arbitrary", "arbitrary"),
            vmem_limit_bytes=VMEM_LIMIT_BYTES),
        name="inproj",
    )(x2d, x2d, w_main, params, cos_t, sa_t, sb_t, w_dt, dt_bias)


def _ssd_kernel(xs_ref, b_ref, c_ref, sz_ref, dt_ref, a_ref, dskip_ref, normw_ref,
                expand_ref, bdtril_ref, bdones_ref, tile8_ref,
                y_ref, acol_s, e2_s, dtx_s, ht_s):
    rows = xs_ref.shape[0]
    n_chunks = rows // CHUNK
    gw = SSD_WIDTH // SSD_GROUPS

    @pl.when(pl.program_id(1) == 0)
    def _():
        ht_s[...] = jnp.zeros_like(ht_s)

    dt = dt_ref[...]
    adt = dt * a_ref[...]
    a_cs = _dot01_left(bdtril_ref[...], adt)
    a_tot = _dot01_left(bdones_ref[...], adt)
    expand = expand_ref[...]
    acol_s[...] = _dot01_right(a_cs, expand)
    e2_s[...] = _dot01_right(a_tot - a_cs, expand)
    dtx_s[...] = _dot01_right(dt, expand)

    lane = lax.broadcasted_iota(jnp.int32, (CHUNK, SSD_WIDTH), 1)
    row = lax.broadcasted_iota(jnp.int32, (CHUNK, SSD_WIDTH), 0)
    s_of_lane = lane & (CHUNK - 1)
    diag_mask = s_of_lane == row
    causal_mask = s_of_lane <= row
    bd_r = lax.broadcasted_iota(jnp.int32, (4 * CHUNK, 4 * SSD_HEAD_DIM), 0) // CHUNK
    bd_c = lax.broadcasted_iota(jnp.int32, (4 * CHUNK, 4 * SSD_HEAD_DIM), 1) // SSD_HEAD_DIM
    bd_mask = bd_r == bd_c
    tile8 = tile8_ref[...]
    dskip = dskip_ref[...]
    normw = normw_ref[...]

    def chunk_body(c, carry):
        r0 = pl.multiple_of(c * CHUNK, CHUNK)
        acol = acol_s[pl.ds(r0, CHUNK), :]
        e2 = e2_s[pl.ds(r0, CHUNK), :]
        dtx = dtx_s[pl.ds(r0, CHUNK), :]
        xs = xs_ref[pl.ds(r0, CHUNK), :].astype(F32)
        bc = b_ref[pl.ds(r0, CHUNK), :]
        cc = c_ref[pl.ds(r0, CHUNK), :]

        arow = jnp.sum(jnp.where(diag_mask, acol, 0.0), axis=0, keepdims=True)
        decay_ls = jnp.exp(jnp.where(causal_mask, acol - arow, NEG_BIG))

        cb = []
        for g in range(SSD_GROUPS):
            cg = cc[:, g * SSD_STATE:(g + 1) * SSD_STATE]
            bg = bc[:, g * SSD_STATE:(g + 1) * SSD_STATE]
            cb_g = lax.dot_general(cg, bg, (((1,), (1,)), ((), ())), preferred_element_type=F32)
            cb.append(_dot01_right(cb_g, tile8))
        w_ls = (jnp.concatenate(cb, axis=1) * decay_ls).astype(BF16)

        xdt = (xs * dtx).astype(BF16)
        y_parts = []
        for qd in range(SSD_WIDTH // (4 * SSD_HEAD_DIM)):
            sl = slice(qd * 4 * SSD_HEAD_DIM, (qd + 1) * 4 * SSD_HEAD_DIM)
            xq = xdt[:, sl]
            bd = jnp.where(bd_mask, jnp.concatenate([xq] * 4, axis=0), jnp.zeros((), BF16))
            y_parts.append(jnp.dot(w_ls[:, sl], bd, preferred_element_type=F32))
        y_diag = jnp.concatenate(y_parts, axis=1)

        ht = ht_s[...]
        ht_b = ht.astype(BF16)
        y_off = jnp.concatenate(
            [jnp.dot(cc[:, g * SSD_STATE:(g + 1) * SSD_STATE], ht_b[:, g * gw:(g + 1) * gw],
                     preferred_element_type=F32) for g in range(SSD_GROUPS)], axis=1)
        y = y_diag + y_off * jnp.exp(acol) + dskip * xs

        gated = y * sz_ref[pl.ds(r0, CHUNK), :].astype(F32)
        normed = []
        for g in range(SSD_GROUPS):
            gg = gated[:, g * gw:(g + 1) * gw]
            ms = jnp.mean(gg * gg, axis=-1, keepdims=True)
            normed.append(gg * lax.rsqrt(ms + EPS))
        y_ref[pl.ds(r0, CHUNK), :] = (jnp.concatenate(normed, axis=1) * normw).astype(y_ref.dtype)

        xd = (xs * (jnp.exp(e2) * dtx)).astype(BF16)
        st = jnp.concatenate(
            [lax.dot_general(bc[:, g * SSD_STATE:(g + 1) * SSD_STATE], xd[:, g * gw:(g + 1) * gw],
                             (((0,), (0,)), ((), ())), preferred_element_type=F32)
             for g in range(SSD_GROUPS)], axis=1)
        chunk_decay = jnp.exp(acol[CHUNK - 1:CHUNK, :])
        ht_s[...] = ht * chunk_decay + st
        return carry

    lax.fori_loop(0, n_chunks, chunk_body, 0)


def _ssd(h, dt, a_row, dskip_row, normw_row, batch, seq):
    rows = SSD_ROWS
    assert seq % rows == 0 and rows % CHUNK == 0
    blocks_per_seq = seq // rows
    hh = jnp.arange(SSD_WIDTH) // SSD_HEAD_DIM
    expand = (jnp.arange(SSD_HEADS)[:, None] == hh[None, :]).astype(BF16)
    rr = jnp.arange(rows)
    same_chunk = (rr[:, None] // CHUNK) == (rr[None, :] // CHUNK)
    bdones = same_chunk.astype(BF16)
    bdtril = (same_chunk & (rr[None, :] <= rr[:, None])).astype(BF16)
    gw = SSD_WIDTH // SSD_GROUPS
    tile8 = (jnp.arange(CHUNK)[:, None] == (jnp.arange(gw)[None, :] % CHUNK)).astype(BF16)

    def row_map(b, r):
        return b * blocks_per_seq + r

    def col_spec(width, off):
        return pl.BlockSpec((rows, width), lambda b, r: (row_map(b, r), off // width))

    const = lambda shape: pl.BlockSpec(shape, lambda b, r: (0, 0))
    return pl.pallas_call(
        _ssd_kernel,
        grid=(batch, blocks_per_seq),
        in_specs=[
            col_spec(SSD_WIDTH, XS_OFF),
            col_spec(BC_WIDTH, B_OFF),
            col_spec(BC_WIDTH, C_OFF),
            col_spec(SSD_WIDTH, SZ_OFF),
            pl.BlockSpec((rows, SSD_HEADS), lambda b, r: (row_map(b, r), 0)),
            const((1, SSD_HEADS)),
            const((1, SSD_WIDTH)),
            const((1, SSD_WIDTH)),
            const((SSD_HEADS, SSD_WIDTH)),
            const((rows, rows)),
            const((rows, rows)),
            const((CHUNK, gw)),
        ],
        out_specs=pl.BlockSpec((rows, SSD_WIDTH), lambda b, r: (row_map(b, r), 0)),
        out_shape=jax.ShapeDtypeStruct((batch * seq, SSD_WIDTH), BF16),
        scratch_shapes=[
            pltpu.VMEM((rows, SSD_WIDTH), F32),
            pltpu.VMEM((rows, SSD_WIDTH), F32),
            pltpu.VMEM((rows, SSD_WIDTH), F32),
            pltpu.VMEM((SSD_STATE, SSD_WIDTH), F32),
        ],
        compiler_params=pltpu.CompilerParams(
            dimension_semantics=("arbitrary", "arbitrary"),
            vmem_limit_bytes=VMEM_LIMIT_BYTES),
        name="ssd",
    )(h, h, h, h, dt, a_row, dskip_row, normw_row, expand, bdtril, bdones, tile8)


def _attn_kernel(q_ref, k_ref, v_ref, gb_ref, lq1_ref, lk1_ref, lq2_ref, lk2_ref, subw_ref,
                 o_ref, vaug_s, m_s, acc_s, *, lambda_init):
    qi = pl.program_id(2)
    tq = q_ref.shape[0]
    tk = ATT_TK
    seq = k_ref.shape[0]

    @pl.when(qi == 0)
    def _():
        vaug_s[:, :DA_VDIM] = v_ref[...]
        vaug_s[:, DA_VDIM:] = jnp.ones((seq, DA_VDIM), vaug_s.dtype)

    q = q_ref[...]
    lane = lax.broadcasted_iota(jnp.int32, q.shape, 1)
    zero = jnp.zeros((), q.dtype)
    qq = jnp.concatenate([jnp.where(lane < DA_HEAD_DIM, q, zero),
                          jnp.where(lane >= DA_HEAD_DIM, q, zero)], axis=0)

    m_s[...] = jnp.full_like(m_s, NEG_BIG)
    acc_s[...] = jnp.zeros_like(acc_s)

    def step(kt, masked):
        k0 = pl.multiple_of(kt * tk, tk)
        k = k_ref[pl.ds(k0, tk), :]
        s = lax.dot_general(qq, k, (((1,), (1,)), ((), ())), preferred_element_type=F32)
        if masked:
            q_chunk = (qi * tq + lax.broadcasted_iota(jnp.int32, (tq, tk), 0)) // CHUNK
            k_chunk = (k0 + lax.broadcasted_iota(jnp.int32, (tq, tk), 1)) // CHUNK
            ok = k_chunk <= q_chunk
            ok2 = jnp.concatenate([ok, ok], axis=0)
            s = jnp.where(ok2, s, NEG_BIG)
        m_old = m_s[...]
        m_new = jnp.maximum(m_old, jnp.max(s, axis=-1, keepdims=True))
        alpha = jnp.exp2(m_old - m_new)
        p = jnp.exp2(s - m_new[:, 0:1]).astype(BF16)
        pv = jnp.dot(p, vaug_s[pl.ds(k0, tk), :], preferred_element_type=F32)
        acc_s[...] = jnp.concatenate([alpha, alpha], axis=1) * acc_s[...] + pv
        m_s[...] = m_new

    n_full = (qi * tq) // tk

    def body(kt, carry):
        step(kt, masked=False)
        return carry

    lax.fori_loop(0, n_full, body, 0)
    step(n_full, masked=True)

    acc = acc_s[...]
    o_all = acc[:, :DA_VDIM] / acc[:, DA_VDIM:]
    lam = (jnp.exp(jnp.sum(lq1_ref[...] * lk1_ref[...], axis=-1, keepdims=True))
           - jnp.exp(jnp.sum(lq2_ref[...] * lk2_ref[...], axis=-1, keepdims=True)) + lambda_init)
    o = o_all[:tq] - lam * o_all[tq:]
    ms = jnp.mean(o * o, axis=-1, keepdims=True)
    o = o * lax.rsqrt(ms + EPS) * (subw_ref[...] * (1.0 - lambda_init))
    o_ref[...] = (o * gb_ref[...].astype(F32)).astype(o_ref.dtype)


def _attention(h, lq1, lk1, lq2, lk2, subw, batch, seq, lambda_init):
    tq = ATT_TQ
    assert seq % ATT_TK == 0 and ATT_TK % tq == 0 and tq % CHUNK == 0
    q_tiles = seq // tq

    def head_spec(rows_blk, off):
        if rows_blk == seq:
            return pl.BlockSpec((seq, DA_VDIM), lambda b, hd, qi: (b, off // DA_VDIM + hd))
        return pl.BlockSpec((rows_blk, DA_VDIM), lambda b, hd, qi: (b * q_tiles + qi, off // DA_VDIM + hd))

    vec = lambda n: pl.BlockSpec((1, n), lambda b, hd, qi: (0, 0))
    kern = functools.partial(_attn_kernel, lambda_init=lambda_init)
    return pl.pallas_call(
        kern,
        grid=(batch, DA_HEADS, q_tiles),
        in_specs=[
            head_spec(tq, Q_OFF),
            head_spec(seq, K_OFF),
            head_spec(seq, V_OFF),
            head_spec(tq, GB_OFF),
            vec(DA_HEAD_DIM), vec(DA_HEAD_DIM), vec(DA_HEAD_DIM), vec(DA_HEAD_DIM),
            vec(DA_VDIM),
        ],
        out_specs=pl.BlockSpec((tq, DA_VDIM), lambda b, hd, qi: (b * q_tiles + qi, hd)),
        out_shape=jax.ShapeDtypeStruct((batch * seq, DA_WIDTH), BF16),
        scratch_shapes=[
            pltpu.VMEM((seq, 2 * DA_VDIM), BF16),
            pltpu.VMEM((2 * tq, LANES), F32),
            pltpu.VMEM((2 * tq, 2 * DA_VDIM), F32),
        ],
        compiler_params=pltpu.CompilerParams(
            dimension_semantics=("arbitrary", "arbitrary", "arbitrary"),
            vmem_limit_bytes=VMEM_LIMIT_BYTES),
        name="diffattn",
    )(h, h, h, h, lq1, lk1, lq2, lk2, subw)


def _out_kernel(x_ref, ya_ref, ob_ref, g0_ref, g1_ref, wa_ref, wb_ref, wo_ref, lng_ref, lnb_ref,
                out_ref, *, alpha):
    d = functools.partial(jnp.dot, preferred_element_type=F32)
    branch_a = d(ya_ref[...], wa_ref[...])
    branch_b = d(ob_ref[...], wb_ref[...])
    merged = g0_ref[...].astype(F32) * branch_a + g1_ref[...].astype(F32) * branch_b
    y = d(merged.astype(BF16), wo_ref[...])
    r = alpha * x_ref[...] + y
    mu = jnp.mean(r, axis=-1, keepdims=True)
    rc = r - mu
    var = jnp.mean(rc * rc, axis=-1, keepdims=True)
    out_ref[...] = rc * lax.rsqrt(var + EPS) * lng_ref[...] + lnb_ref[...]


def _output_stage(x2d, y_ssd, o_att, h, w_a, w_b, w_o, ln_g, ln_b, alpha):
    t_rows = x2d.shape[0]
    tm = OUT_TM
    assert t_rows % tm == 0
    row = lambda width: pl.BlockSpec((tm, width), lambda i: (i, 0))
    full = lambda shape: pl.BlockSpec(shape, lambda i: (0, 0))
    gate_spec = lambda k: pl.BlockSpec((tm, D_MODEL), lambda i: (i, GM_OFF // D_MODEL + k))
    assert GM_OFF % D_MODEL == 0
    return pl.pallas_call(
        functools.partial(_out_kernel, alpha=alpha),
        grid=(t_rows // tm,),
        in_specs=[row(D_MODEL), row(SSD_WIDTH), row(DA_WIDTH), gate_spec(0), gate_spec(1),
                  full((SSD_WIDTH, D_MODEL)), full((DA_WIDTH, D_MODEL)), full((D_MODEL, D_MODEL)),
                  full((1, D_MODEL)), full((1, D_MODEL))],
        out_specs=row(D_MODEL),
        out_shape=jax.ShapeDtypeStruct((t_rows, D_MODEL), F32),
        compiler_params=pltpu.CompilerParams(
            dimension_semantics=("arbitrary",),
            vmem_limit_bytes=VMEM_LIMIT_BYTES),
        name="outproj",
    )(x2d, y_ssd, o_att, h, h, w_a, w_b, w_o, ln_g, ln_b)


def _rope_tables(seq):
    pos = jnp.arange(seq, dtype=F32)
    inv_freq = ROPE_THETA ** (-jnp.arange(0, ROT_DIM, 2, dtype=F32) / ROT_DIM)
    ang = pos[:, None] * inv_freq[None, :]
    cos, sin = jnp.cos(ang), jnp.sin(ang)
    d = jnp.arange(LANES) % DA_HEAD_DIM
    f = d % ROT_HALF
    first = d < ROT_HALF
    second = (d >= ROT_HALF) & (d < ROT_DIM)
    cos_t = jnp.where((first | second)[None, :], cos[:, f], 1.0)
    sa_t = jnp.where(first[None, :], -sin[:, f], 0.0)
    sb_t = jnp.where(second[None, :], sin[:, f], 0.0)
    return cos_t, sa_t, sb_t


def _layer(x2d, batch, seq, depth_total, l, w_in, b_gate, conv_w, conv_b, dt_bias, a_log, d_skip,
           ssd_norm_w, lambda_q1, lambda_k1, lambda_q2, lambda_k2, subln_w, w_a, w_b, w_o, ln_g, ln_b,
           tables):
    alpha = (2.0 * depth_total) ** 0.25
    lambda_init = 0.8 - 0.6 * math.exp(-0.3 * l)
    offs = [0]
    for s in IN_SIZES:
        offs.append(offs[-1] + s)
    z_w, xbc_w, dt_w, q_w, k_w, v_w, gb_w, gm_w = [w_in[:, offs[n]:offs[n + 1]] for n in range(len(IN_SIZES))]
    w_main = jnp.concatenate([z_w, xbc_w[:, :SSD_WIDTH], q_w, k_w, v_w, gb_w, gm_w, xbc_w[:, SSD_WIDTH:]],
                             axis=1).astype(BF16)
    params = jnp.zeros((SUBLANES, H_COLS), F32)
    params = params.at[:CONV_WIDTH, XS_OFF:Q_OFF].set(conv_w[:, :SSD_WIDTH])
    params = params.at[:CONV_WIDTH, B_OFF:].set(conv_w[:, SSD_WIDTH:])
    params = params.at[CONV_WIDTH, XS_OFF:Q_OFF].set(conv_b[:SSD_WIDTH])
    params = params.at[CONV_WIDTH, B_OFF:].set(conv_b[SSD_WIDTH:])
    params = params.at[CONV_WIDTH, GM_OFF:B_OFF].set(b_gate)

    xb = x2d.astype(BF16)
    h, dt = _inproj(xb, w_main, params, *tables, dt_w.astype(BF16), dt_bias[None, :].astype(F32), seq)

    a_row = -jnp.exp(a_log.astype(F32))[None, :]
    dskip_row = jnp.repeat(d_skip.astype(F32), SSD_HEAD_DIM)[None, :]
    y_ssd = _ssd(h, dt, a_row, dskip_row, ssd_norm_w[None, :].astype(F32), batch, seq)

    o_att = _attention(h, lambda_q1[None, :], lambda_k1[None, :], lambda_q2[None, :], lambda_k2[None, :],
                       subln_w[None, :], batch, seq, lambda_init)

    return _output_stage(x2d, y_ssd, o_att, h, w_a.astype(BF16), w_b.astype(BF16), w_o.astype(BF16),
                         ln_g[None, :], ln_b[None, :], alpha)


def kernel(x, w_in, b_gate, conv_w, conv_b, dt_bias, a_log, d_skip, ssd_norm_w, lambda_q1, lambda_k1,
           lambda_q2, lambda_k2, subln_w, w_a, w_b, w_o, ln_g, ln_b):
    batch, seq, _ = x.shape
    depth = w_in.shape[0]
    tables = _rope_tables(seq)
    x2d = x.reshape(batch * seq, D_MODEL)
    for l in range(depth):
        x2d = _layer(x2d, batch, seq, depth, l, w_in[l], b_gate[l], conv_w[l], conv_b[l], dt_bias[l],
                     a_log[l], d_skip[l], ssd_norm_w[l], lambda_q1[l], lambda_k1[l], lambda_q2[l],
                     lambda_k2[l], subln_w[l], w_a[l], w_b[l], w_o[l], ln_g[l], ln_b[l], tables)
    return x2d.reshape(batch, seq, D_MODEL)
```

```python
import functools
import math

import jax
import jax.numpy as jnp
from jax import lax
from jax.experimental import pallas as pl
from jax.experimental.pallas import tpu as pltpu

F32 = jnp.float32
BF16 = jnp.bfloat16

D_MODEL = 1024
CHUNK = 64
SSD_HEADS = 16
SSD_HEAD_DIM = 64
SSD_WIDTH = SSD_HEADS * SSD_HEAD_DIM
SSD_GROUPS = 2
SSD_STATE = 128
CONV_WIDTH = 4
BC_WIDTH = SSD_GROUPS * SSD_STATE
CONV_CH = SSD_WIDTH + 2 * BC_WIDTH
DA_HEADS = 8
DA_HEAD_DIM = 64
DA_VDIM = 2 * DA_HEAD_DIM
DA_WIDTH = DA_HEADS * DA_VDIM
ROPE_THETA = 500000.0
ROT_DIM = DA_HEAD_DIM // 4
ROT_HALF = ROT_DIM // 2
N_BRANCH = 2
EPS = 1e-5
IN_SIZES = (SSD_WIDTH, CONV_CH, SSD_HEADS, DA_WIDTH, DA_WIDTH, DA_WIDTH, DA_WIDTH, N_BRANCH * D_MODEL)

SZ_OFF = 0
XS_OFF = SZ_OFF + SSD_WIDTH
Q_OFF = XS_OFF + SSD_WIDTH
K_OFF = Q_OFF + DA_WIDTH
V_OFF = K_OFF + DA_WIDTH
GB_OFF = V_OFF + DA_WIDTH
GM_OFF = GB_OFF + DA_WIDTH
B_OFF = GM_OFF + N_BRANCH * D_MODEL
C_OFF = B_OFF + BC_WIDTH
H_COLS = C_OFF + BC_WIDTH

LANES = 128
SUBLANES = 8
VMEM_LIMIT_BYTES = 56 * 1024 * 1024

IN_TM = 1024
IN_TN = 512
SSD_ROWS = 512
ATT_TQ = 1024
ATT_TK = 1024
ATT_BAND = 256
OUT_TM = 512

NEG_BIG = -1e30
LOG2E = 1.4426950408889634


def _sigmoid(t):
    return 1.0 / (1.0 + jnp.exp(-t))


def _split3(v):
    hi = v.astype(BF16)
    r1 = v - hi.astype(F32)
    mid = r1.astype(BF16)
    lo = (r1 - mid.astype(F32)).astype(BF16)
    return hi, mid, lo


def _dot01_right(v, m01):
    hi, mid, lo = _split3(v)
    d = functools.partial(jnp.dot, preferred_element_type=F32)
    return d(hi, m01) + d(mid, m01) + d(lo, m01)


def _dot01_left(m01, v):
    hi, mid, lo = _split3(v)
    d = functools.partial(jnp.dot, preferred_element_type=F32)
    return d(m01, hi) + d(m01, mid) + d(m01, lo)


def _inproj_kernel(x_ref, xh_ref, w_ref, p_ref, cos_ref, sa_ref, sb_ref, wdt_ref, dtb_ref,
                   h_ref, dt_ref, ext_ref, *, tiles_per_seq, q_scale):
    i = pl.program_id(0)
    j = pl.program_id(1)
    tm, tn = h_ref.shape
    x = x_ref[...]
    acc = jnp.dot(x, w_ref[...], preferred_element_type=F32)

    def blk(off):
        return off // tn

    @pl.when(j == 0)
    def _():
        raw = jnp.dot(x, wdt_ref[...], preferred_element_type=F32) + dtb_ref[...]
        dt_ref[...] = jnp.maximum(raw, 0.0) + jnp.log(1.0 + jnp.exp(-jnp.abs(raw)))

    is_silu = (j < blk(XS_OFF)) | ((j >= blk(GB_OFF)) & (j < blk(GM_OFF)))

    @pl.when(is_silu)
    def _():
        h_ref[...] = (acc * _sigmoid(acc)).astype(h_ref.dtype)

    @pl.when(((j >= blk(XS_OFF)) & (j < blk(Q_OFF))) | (j >= blk(B_OFF)))
    def _():
        halo = jnp.dot(xh_ref[...], w_ref[...], preferred_element_type=F32)
        halo = jnp.where(i % tiles_per_seq == 0, 0.0, halo)
        ext_ref[0:SUBLANES, :] = halo
        ext_ref[SUBLANES:, :] = acc
        out = p_ref[CONV_WIDTH:CONV_WIDTH + 1, :]
        for k in range(CONV_WIDTH):
            start = SUBLANES - (CONV_WIDTH - 1) + k
            out = out + p_ref[k:k + 1, :] * ext_ref[pl.ds(start, tm), :]
        h_ref[...] = (out * _sigmoid(out)).astype(h_ref.dtype)

    @pl.when((j >= blk(Q_OFF)) & (j < blk(V_OFF)))
    def _():
        reps = tn // LANES
        cos = jnp.concatenate([cos_ref[...]] * reps, axis=1)
        sa = jnp.concatenate([sa_ref[...]] * reps, axis=1)
        sb = jnp.concatenate([sb_ref[...]] * reps, axis=1)
        up = pltpu.roll(acc, tn - ROT_HALF, axis=1)
        dn = pltpu.roll(acc, ROT_HALF, axis=1)
        rot = acc * cos + up * sa + dn * sb
        rot = jnp.where(j < blk(K_OFF), rot * q_scale, rot)
        h_ref[...] = rot.astype(h_ref.dtype)

    @pl.when((j >= blk(V_OFF)) & (j < blk(GB_OFF)))
    def _():
        h_ref[...] = acc.astype(h_ref.dtype)

    @pl.when((j >= blk(GM_OFF)) & (j < blk(B_OFF)))
    def _():
        h_ref[...] = _sigmoid(acc + p_ref[CONV_WIDTH:CONV_WIDTH + 1, :]).astype(h_ref.dtype)


def _inproj(x2d, w_main, params, cos_t, sa_t, sb_t, w_dt, dt_bias, seq):
    t_rows = x2d.shape[0]
    tm, tn = IN_TM, IN_TN
    assert t_rows % tm == 0 and seq % tm == 0 and H_COLS % tn == 0
    for off in (XS_OFF, Q_OFF, K_OFF, V_OFF, GB_OFF, GM_OFF, B_OFF):
        assert off % tn == 0
    tiles_per_seq = seq // tm
    halo_blocks = tm // SUBLANES
    kern = functools.partial(_inproj_kernel, tiles_per_seq=tiles_per_seq,
                             q_scale=DA_HEAD_DIM ** -0.5 * LOG2E)
    return pl.pallas_call(
        kern,
        grid=(t_rows // tm, H_COLS // tn),
        in_specs=[
            pl.BlockSpec((tm, D_MODEL), lambda i, j: (i, 0)),
            pl.BlockSpec((SUBLANES, D_MODEL), lambda i, j: (jnp.maximum(i * halo_blocks - 1, 0), 0)),
            pl.BlockSpec((D_MODEL, tn), lambda i, j: (0, j)),
            pl.BlockSpec((SUBLANES, tn), lambda i, j: (0, j)),
            pl.BlockSpec((tm, LANES), lambda i, j: (i % tiles_per_seq, 0)),
            pl.BlockSpec((tm, LANES), lambda i, j: (i % tiles_per_seq, 0)),
            pl.BlockSpec((tm, LANES), lambda i, j: (i % tiles_per_seq, 0)),
            pl.BlockSpec((D_MODEL, SSD_HEADS), lambda i, j: (0, 0)),
            pl.BlockSpec((1, SSD_HEADS), lambda i, j: (0, 0)),
        ],
        out_specs=[
            pl.BlockSpec((tm, tn), lambda i, j: (i, j)),
            pl.BlockSpec((tm, SSD_HEADS), lambda i, j: (i, 0)),
        ],
        out_shape=[
            jax.ShapeDtypeStruct((t_rows, H_COLS), BF16),
            jax.ShapeDtypeStruct((t_rows, SSD_HEADS), F32),
        ],
        scratch_shapes=[pltpu.VMEM((tm + SUBLANES, tn), F32)],
        compiler_params=pltpu.CompilerParams(
            dimension_semantics=("arbitrary", "arbitrary"),
            vmem_limit_bytes=VMEM_LIMIT_BYTES),
        name="inproj",
    )(x2d, x2d, w_main, params, cos_t, sa_t, sb_t, w_dt, dt_bias)


def _ssd_kernel(xs_ref, b_ref, c_ref, sz_ref, dt_ref, a_ref, dskip_ref, normw_ref,
                expand_ref, bdtril_ref, bdones_ref, tile8_ref,
                y_ref, acol_s, e2_s, dtx_s, ht_s):
    rows = xs_ref.shape[0]
    n_chunks = rows // CHUNK
    gw = SSD_WIDTH // SSD_GROUPS

    @pl.when(pl.program_id(1) == 0)
    def _():
        ht_s[...] = jnp.zeros_like(ht_s)

    dt = dt_ref[...]
    adt = dt * a_ref[...]
    a_cs = _dot01_left(bdtril_ref[...], adt)
    a_tot = _dot01_left(bdones_ref[...], adt)
    expand = expand_ref[...]
    acol_s[...] = _dot01_right(a_cs, expand)
    e2_s[...] = _dot01_right(a_tot - a_cs, expand)
    dtx_s[...] = _dot01_right(dt, expand)

    lane = lax.broadcasted_iota(jnp.int32, (CHUNK, SSD_WIDTH), 1)
    row = lax.broadcasted_iota(jnp.int32, (CHUNK, SSD_WIDTH), 0)
    s_of_lane = lane & (CHUNK - 1)
    diag_mask = s_of_lane == row
    causal_mask = s_of_lane <= row
    bd_r = lax.broadcasted_iota(jnp.int32, (4 * CHUNK, 4 * SSD_HEAD_DIM), 0) // CHUNK
    bd_c = lax.broadcasted_iota(jnp.int32, (4 * CHUNK, 4 * SSD_HEAD_DIM), 1) // SSD_HEAD_DIM
    bd_mask = bd_r == bd_c
    tile8 = tile8_ref[...]
    dskip = dskip_ref[...]
    normw = normw_ref[...]

    def chunk_body(c, carry):
        r0 = pl.multiple_of(c * CHUNK, CHUNK)
        acol = acol_s[pl.ds(r0, CHUNK), :]
        e2 = e2_s[pl.ds(r0, CHUNK), :]
        dtx = dtx_s[pl.ds(r0, CHUNK), :]
        xs = xs_ref[pl.ds(r0, CHUNK), :].astype(F32)
        bc = b_ref[pl.ds(r0, CHUNK), :]
        cc = c_ref[pl.ds(r0, CHUNK), :]

        arow = jnp.sum(jnp.where(diag_mask, acol, 0.0), axis=0, keepdims=True)
        decay_ls = jnp.exp(jnp.where(causal_mask, acol - arow, NEG_BIG))

        cb = []
        for g in range(SSD_GROUPS):
            cg = cc[:, g * SSD_STATE:(g + 1) * SSD_STATE]
            bg = bc[:, g * SSD_STATE:(g + 1) * SSD_STATE]
            cb_g = lax.dot_general(cg, bg, (((1,), (1,)), ((), ())), preferred_element_type=F32)
            cb.append(_dot01_right(cb_g, tile8))
        w_ls = (jnp.concatenate(cb, axis=1) * decay_ls).astype(BF16)

        xdt = (xs * dtx).astype(BF16)
        y_parts = []
        for qd in range(SSD_WIDTH // (4 * SSD_HEAD_DIM)):
            sl = slice(qd * 4 * SSD_HEAD_DIM, (qd + 1) * 4 * SSD_HEAD_DIM)
            xq = xdt[:, sl]
            bd = jnp.where(bd_mask, jnp.concatenate([xq] * 4, axis=0), jnp.zeros((), BF16))
            y_parts.append(jnp.dot(w_ls[:, sl], bd, preferred_element_type=F32))
        y_diag = jnp.concatenate(y_parts, axis=1)

        ht = ht_s[...]
        ht_b = ht.astype(BF16)
        y_off = jnp.concatenate(
            [jnp.dot(cc[:, g * SSD_STATE:(g + 1) * SSD_STATE], ht_b[:, g * gw:(g + 1) * gw],
                     preferred_element_type=F32) for g in range(SSD_GROUPS)], axis=1)
        y = y_diag + y_off * jnp.exp(acol) + dskip * xs

        gated = y * sz_ref[pl.ds(r0, CHUNK), :].astype(F32)
        normed = []
        for g in range(SSD_GROUPS):
            gg = gated[:, g * gw:(g + 1) * gw]
            ms = jnp.mean(gg * gg, axis=-1, keepdims=True)
            normed.append(gg * lax.rsqrt(ms + EPS))
        y_ref[pl.ds(r0, CHUNK), :] = (jnp.concatenate(normed, axis=1) * normw).astype(y_ref.dtype)

        xd = (xs * (jnp.exp(e2) * dtx)).astype(BF16)
        st = jnp.concatenate(
            [lax.dot_general(bc[:, g * SSD_STATE:(g + 1) * SSD_STATE], xd[:, g * gw:(g + 1) * gw],
                             (((0,), (0,)), ((), ())), preferred_element_type=F32)
             for g in range(SSD_GROUPS)], axis=1)
        chunk_decay = jnp.exp(acol[CHUNK - 1:CHUNK, :])
        ht_s[...] = ht * chunk_decay + st
        return carry

    lax.fori_loop(0, n_chunks, chunk_body, 0)


def _ssd(h, dt, a_row, dskip_row, normw_row, batch, seq):
    rows = SSD_ROWS
    assert seq % rows == 0 and rows % CHUNK == 0
    blocks_per_seq = seq // rows
    hh = jnp.arange(SSD_WIDTH) // SSD_HEAD_DIM
    expand = (jnp.arange(SSD_HEADS)[:, None] == hh[None, :]).astype(BF16)
    rr = jnp.arange(rows)
    same_chunk = (rr[:, None] // CHUNK) == (rr[None, :] // CHUNK)
    bdones = same_chunk.astype(BF16)
    bdtril = (same_chunk & (rr[None, :] <= rr[:, None])).astype(BF16)
    gw = SSD_WIDTH // SSD_GROUPS
    tile8 = (jnp.arange(CHUNK)[:, None] == (jnp.arange(gw)[None, :] % CHUNK)).astype(BF16)

    def row_map(b, r):
        return b * blocks_per_seq + r

    def col_spec(width, off):
        return pl.BlockSpec((rows, width), lambda b, r: (row_map(b, r), off // width))

    const = lambda shape: pl.BlockSpec(shape, lambda b, r: (0, 0))
    return pl.pallas_call(
        _ssd_kernel,
        grid=(batch, blocks_per_seq),
        in_specs=[
            col_spec(SSD_WIDTH, XS_OFF),
            col_spec(BC_WIDTH, B_OFF),
            col_spec(BC_WIDTH, C_OFF),
            col_spec(SSD_WIDTH, SZ_OFF),
            pl.BlockSpec((rows, SSD_HEADS), lambda b, r: (row_map(b, r), 0)),
            const((1, SSD_HEADS)),
            const((1, SSD_WIDTH)),
            const((1, SSD_WIDTH)),
            const((SSD_HEADS, SSD_WIDTH)),
            const((rows, rows)),
            const((rows, rows)),
            const((CHUNK, gw)),
        ],
        out_specs=pl.BlockSpec((rows, SSD_WIDTH), lambda b, r: (row_map(b, r), 0)),
        out_shape=jax.ShapeDtypeStruct((batch * seq, SSD_WIDTH), BF16),
        scratch_shapes=[
            pltpu.VMEM((rows, SSD_WIDTH), F32),
            pltpu.VMEM((rows, SSD_WIDTH), F32),
            pltpu.VMEM((rows, SSD_WIDTH), F32),
            pltpu.VMEM((SSD_STATE, SSD_WIDTH), F32),
        ],
        compiler_params=pltpu.CompilerParams(
            dimension_semantics=("arbitrary", "arbitrary"),
            vmem_limit_bytes=VMEM_LIMIT_BYTES),
        name="ssd",
    )(h, h, h, h, dt, a_row, dskip_row, normw_row, expand, bdtril, bdones, tile8)


def _attn_kernel(q_ref, k_ref, v_ref, gb_ref, lq1_ref, lk1_ref, lq2_ref, lk2_ref, subw_ref,
                 o_ref, vaug_s, m_s, acc_s, *, lambda_init):
    qi = pl.program_id(2)
    tq = q_ref.shape[0]
    tk = ATT_TK
    seq = k_ref.shape[0]

    @pl.when(qi == 0)
    def _():
        vaug_s[:, :DA_VDIM] = v_ref[...]
        vaug_s[:, DA_VDIM:] = jnp.ones((seq, DA_VDIM), vaug_s.dtype)

    assert tq == tk
    band = ATT_BAND
    n_bands = tq // band
    lane = lax.broadcasted_iota(jnp.int32, (band, DA_VDIM), 1)
    zero = jnp.zeros((), q_ref.dtype)

    qq = []
    for c in range(n_bands):
        q = q_ref[c * band:(c + 1) * band, :]
        qq.append(jnp.concatenate([jnp.where(lane < DA_HEAD_DIM, q, zero),
                                   jnp.where(lane >= DA_HEAD_DIM, q, zero)], axis=0))

    m_s[...] = jnp.full_like(m_s, NEG_BIG)
    acc_s[...] = jnp.zeros_like(acc_s)

    def chain_step(c, k, vt, mask):
        rows = slice(c * 2 * band, (c + 1) * 2 * band)
        s = lax.dot_general(qq[c], k, (((1,), (1,)), ((), ())), preferred_element_type=F32)
        if mask is not None:
            s = jnp.where(mask, s, NEG_BIG)
        m_old = m_s[rows, :]
        m_new = jnp.maximum(m_old, jnp.max(s, axis=-1, keepdims=True))
        alpha = jnp.exp2(m_old - m_new)
        p = jnp.exp2(s - m_new[:, 0:1]).astype(BF16)
        pv = jnp.dot(p, vt, preferred_element_type=F32)
        acc_s[rows, :] = jnp.concatenate([alpha, alpha], axis=1) * acc_s[rows, :] + pv
        m_s[rows, :] = m_new

    def body(kt, carry):
        k0 = pl.multiple_of(kt * tk, tk)
        k = k_ref[pl.ds(k0, tk), :]
        vt = vaug_s[pl.ds(k0, tk), :]
        for c in range(n_bands):
            chain_step(c, k, vt, None)
        return carry

    lax.fori_loop(0, qi, body, 0)

    k0 = pl.multiple_of(qi * tk, tk)
    for c in range(n_bands):
        nk = (c + 1) * band
        r_chunk = (c * band + lax.broadcasted_iota(jnp.int32, (band, nk), 0)) // CHUNK
        k_chunk = lax.broadcasted_iota(jnp.int32, (band, nk), 1) // CHUNK
        ok = k_chunk <= r_chunk
        chain_step(c, k_ref[pl.ds(k0, nk), :], vaug_s[pl.ds(k0, nk), :], jnp.concatenate([ok, ok], axis=0))

    lam = (jnp.exp(jnp.sum(lq1_ref[...] * lk1_ref[...], axis=-1, keepdims=True))
           - jnp.exp(jnp.sum(lq2_ref[...] * lk2_ref[...], axis=-1, keepdims=True)) + lambda_init)
    scale = subw_ref[...] * (1.0 - lambda_init)
    for c in range(n_bands):
        acc = acc_s[c * 2 * band:(c + 1) * 2 * band, :]
        o_all = acc[:, :DA_VDIM] / acc[:, DA_VDIM:]
        o = o_all[:band] - lam * o_all[band:]
        ms = jnp.mean(o * o, axis=-1, keepdims=True)
        o = o * lax.rsqrt(ms + EPS) * scale
        rows = slice(c * band, (c + 1) * band)
        o_ref[rows, :] = (o * gb_ref[rows, :].astype(F32)).astype(o_ref.dtype)


def _attention(h, lq1, lk1, lq2, lk2, subw, batch, seq, lambda_init):
    tq = ATT_TQ
    assert seq % ATT_TK == 0 and ATT_TK == tq and tq % ATT_BAND == 0 and ATT_BAND % CHUNK == 0
    q_tiles = seq // tq

    def head_spec(rows_blk, off):
        if rows_blk == seq:
            return pl.BlockSpec((seq, DA_VDIM), lambda b, hd, qi: (b, off // DA_VDIM + hd))
        return pl.BlockSpec((rows_blk, DA_VDIM), lambda b, hd, qi: (b * q_tiles + qi, off // DA_VDIM + hd))

    vec = lambda n: pl.BlockSpec((1, n), lambda b, hd, qi: (0, 0))
    kern = functools.partial(_attn_kernel, lambda_init=lambda_init)
    return pl.pallas_call(
        kern,
        grid=(batch, DA_HEADS, q_tiles),
        in_specs=[
            head_spec(tq, Q_OFF),
            head_spec(seq, K_OFF),
            head_spec(seq, V_OFF),
            head_spec(tq, GB_OFF),
            vec(DA_HEAD_DIM), vec(DA_HEAD_DIM), vec(DA_HEAD_DIM), vec(DA_HEAD_DIM),
            vec(DA_VDIM),
        ],
        out_specs=pl.BlockSpec((tq, DA_VDIM), lambda b, hd, qi: (b * q_tiles + qi, hd)),
        out_shape=jax.ShapeDtypeStruct((batch * seq, DA_WIDTH), BF16),
        scratch_shapes=[
            pltpu.VMEM((seq, 2 * DA_VDIM), BF16),
            pltpu.VMEM((2 * tq, LANES), F32),
            pltpu.VMEM((2 * tq, 2 * DA_VDIM), F32),
        ],
        compiler_params=pltpu.CompilerParams(
            dimension_semantics=("arbitrary", "arbitrary", "arbitrary"),
            vmem_limit_bytes=VMEM_LIMIT_BYTES),
        name="diffattn",
    )(h, h, h, h, lq1, lk1, lq2, lk2, subw)


def _out_kernel(x_ref, ya_ref, ob_ref, g0_ref, g1_ref, wa_ref, wb_ref, wo_ref, lng_ref, lnb_ref,
                out_ref, *, alpha):
    d = functools.partial(jnp.dot, preferred_element_type=F32)
    branch_a = d(ya_ref[...], wa_ref[...])
    branch_b = d(ob_ref[...], wb_ref[...])
    merged = g0_ref[...].astype(F32) * branch_a + g1_ref[...].astype(F32) * branch_b
    y = d(merged.astype(BF16), wo_ref[...])
    r = alpha * x_ref[...] + y
    mu = jnp.mean(r, axis=-1, keepdims=True)
    rc = r - mu
    var = jnp.mean(rc * rc, axis=-1, keepdims=True)
    out_ref[...] = rc * lax.rsqrt(var + EPS) * lng_ref[...] + lnb_ref[...]


def _output_stage(x2d, y_ssd, o_att, h, w_a, w_b, w_o, ln_g, ln_b, alpha):
    t_rows = x2d.shape[0]
    tm = OUT_TM
    assert t_rows % tm == 0
    row = lambda width: pl.BlockSpec((tm, width), lambda i: (i, 0))
    full = lambda shape: pl.BlockSpec(shape, lambda i: (0, 0))
    gate_spec = lambda k: pl.BlockSpec((tm, D_MODEL), lambda i: (i, GM_OFF // D_MODEL + k))
    assert GM_OFF % D_MODEL == 0
    return pl.pallas_call(
        functools.partial(_out_kernel, alpha=alpha),
        grid=(t_rows // tm,),
        in_specs=[row(D_MODEL), row(SSD_WIDTH), row(DA_WIDTH), gate_spec(0), gate_spec(1),
                  full((SSD_WIDTH, D_MODEL)), full((DA_WIDTH, D_MODEL)), full((D_MODEL, D_MODEL)),
                  full((1, D_MODEL)), full((1, D_MODEL))],
        out_specs=row(D_MODEL),
        out_shape=jax.ShapeDtypeStruct((t_rows, D_MODEL), F32),
        compiler_params=pltpu.CompilerParams(
            dimension_semantics=("arbitrary",),
            vmem_limit_bytes=VMEM_LIMIT_BYTES),
        name="outproj",
    )(x2d, y_ssd, o_att, h, h, w_a, w_b, w_o, ln_g, ln_b)


def _rope_tables(seq):
    pos = jnp.arange(seq, dtype=F32)
    inv_freq = ROPE_THETA ** (-jnp.arange(0, ROT_DIM, 2, dtype=F32) / ROT_DIM)
    ang = pos[:, None] * inv_freq[None, :]
    cos, sin = jnp.cos(ang), jnp.sin(ang)
    d = jnp.arange(LANES) % DA_HEAD_DIM
    f = d % ROT_HALF
    first = d < ROT_HALF
    second = (d >= ROT_HALF) & (d < ROT_DIM)
    cos_t = jnp.where((first | second)[None, :], cos[:, f], 1.0)
    sa_t = jnp.where(first[None, :], -sin[:, f], 0.0)
    sb_t = jnp.where(second[None, :], sin[:, f], 0.0)
    return cos_t, sa_t, sb_t


def _layer(x2d, batch, seq, depth_total, l, w_in, b_gate, conv_w, conv_b, dt_bias, a_log, d_skip,
           ssd_norm_w, lambda_q1, lambda_k1, lambda_q2, lambda_k2, subln_w, w_a, w_b, w_o, ln_g, ln_b,
           tables):
    alpha = (2.0 * depth_total) ** 0.25
    lambda_init = 0.8 - 0.6 * math.exp(-0.3 * l)
    offs = [0]
    for s in IN_SIZES:
        offs.append(offs[-1] + s)
    z_w, xbc_w, dt_w, q_w, k_w, v_w, gb_w, gm_w = [w_in[:, offs[n]:offs[n + 1]] for n in range(len(IN_SIZES))]
    w_main = jnp.concatenate([z_w, xbc_w[:, :SSD_WIDTH], q_w, k_w, v_w, gb_w, gm_w, xbc_w[:, SSD_WIDTH:]],
                             axis=1).astype(BF16)
    params = jnp.zeros((SUBLANES, H_COLS), F32)
    params = params.at[:CONV_WIDTH, XS_OFF:Q_OFF].set(conv_w[:, :SSD_WIDTH])
    params = params.at[:CONV_WIDTH, B_OFF:].set(conv_w[:, SSD_WIDTH:])
    params = params.at[CONV_WIDTH, XS_OFF:Q_OFF].set(conv_b[:SSD_WIDTH])
    params = params.at[CONV_WIDTH, B_OFF:].set(conv_b[SSD_WIDTH:])
    params = params.at[CONV_WIDTH, GM_OFF:B_OFF].set(b_gate)

    xb = x2d.astype(BF16)
    h, dt = _inproj(xb, w_main, params, *tables, dt_w.astype(BF16), dt_bias[None, :].astype(F32), seq)

    a_row = -jnp.exp(a_log.astype(F32))[None, :]
    dskip_row = jnp.repeat(d_skip.astype(F32), SSD_HEAD_DIM)[None, :]
    y_ssd = _ssd(h, dt, a_row, dskip_row, ssd_norm_w[None, :].astype(F32), batch, seq)

    o_att = _attention(h, lambda_q1[None, :], lambda_k1[None, :], lambda_q2[None, :], lambda_k2[None, :],
                       subln_w[None, :], batch, seq, lambda_init)

    return _output_stage(x2d, y_ssd, o_att, h, w_a.astype(BF16), w_b.astype(BF16), w_o.astype(BF16),
                         ln_g[None, :], ln_b[None, :], alpha)


def kernel(x, w_in, b_gate, conv_w, conv_b, dt_bias, a_log, d_skip, ssd_norm_w, lambda_q1, lambda_k1,
           lambda_q2, lambda_k2, subln_w, w_a, w_b, w_o, ln_g, ln_b):
    batch, seq, _ = x.shape
    depth = w_in.shape[0]
    tables = _rope_tables(seq)
    x2d = x.reshape(batch * seq, D_MODEL)
    for l in range(depth):
        x2d = _layer(x2d, batch, seq, depth, l, w_in[l], b_gate[l], conv_w[l], conv_b[l], dt_bias[l],
                     a_log[l], d_skip[l], ssd_norm_w[l], lambda_q1[l], lambda_k1[l], lambda_q2[l],
                     lambda_k2[l], subln_w[l], w_a[l], w_b[l], w_o[l], ln_g[l], ln_b[l], tables)
    return x2d.reshape(batch, seq, D_MODEL)
```

```python
import functools
import math

import jax
import jax.numpy as jnp
from jax import lax
from jax.experimental import pallas as pl
from jax.experimental.pallas import tpu as pltpu

F32 = jnp.float32
BF16 = jnp.bfloat16

D_MODEL = 1024
CHUNK = 64
SSD_HEADS = 16
SSD_HEAD_DIM = 64
SSD_WIDTH = SSD_HEADS * SSD_HEAD_DIM
SSD_GROUPS = 2
SSD_STATE = 128
CONV_WIDTH = 4
BC_WIDTH = SSD_GROUPS * SSD_STATE
CONV_CH = SSD_WIDTH + 2 * BC_WIDTH
DA_HEADS = 8
DA_HEAD_DIM = 64
DA_VDIM = 2 * DA_HEAD_DIM
DA_WIDTH = DA_HEADS * DA_VDIM
ROPE_THETA = 500000.0
ROT_DIM = DA_HEAD_DIM // 4
ROT_HALF = ROT_DIM // 2
N_BRANCH = 2
EPS = 1e-5
IN_SIZES = (SSD_WIDTH, CONV_CH, SSD_HEADS, DA_WIDTH, DA_WIDTH, DA_WIDTH, DA_WIDTH, N_BRANCH * D_MODEL)

GA_SZ, GA_GB, GA_GM = 0, SSD_WIDTH, SSD_WIDTH + DA_WIDTH
GA_COLS = GA_GM + N_BRANCH * D_MODEL
AT_Q, AT_K, AT_V = 0, DA_WIDTH, 2 * DA_WIDTH
AT_COLS = 3 * DA_WIDTH
CV_XS, CV_B, CV_C = 0, SSD_WIDTH, SSD_WIDTH + BC_WIDTH

ROT_PASS = (DA_HEAD_DIM - ROT_DIM) // 2

LANES = 128
SUBLANES = 8
BF16_ROWS = 16
VMEM_LIMIT_BYTES = 56 * 1024 * 1024

IN_TM = 1024
IN_TN = 1024
SSD_ROWS = 512
ATT_TQ = 1024
ATT_TK = 1024
ATT_BAND = 256
OUT_TM = 512

NEG_BIG = -1e30
LOG2E = 1.4426950408889634


def _sigmoid(t):
    return 1.0 / (1.0 + jnp.exp(-t))


def _split2(v):
    hi = v.astype(BF16)
    mid = (v - hi.astype(F32)).astype(BF16)
    return hi, mid


def _dot01_right(v, m01):
    hi, mid = _split2(v)
    d = functools.partial(jnp.dot, preferred_element_type=F32)
    return d(hi, m01) + d(mid, m01)


def _dot01_left(m01, v):
    hi, mid = _split2(v)
    d = functools.partial(jnp.dot, preferred_element_type=F32)
    return d(m01, hi) + d(m01, mid)


def _proj_params():
    return pltpu.CompilerParams(dimension_semantics=("arbitrary", "arbitrary"),
                                vmem_limit_bytes=VMEM_LIMIT_BYTES)


def _proj_gate_kernel(x_ref, w_ref, p_ref, o_ref, xb_s):
    @pl.when(pl.program_id(1) == 0)
    def _():
        xb_s[...] = x_ref[...].astype(BF16)

    acc = jnp.dot(xb_s[...], w_ref[...], preferred_element_type=F32)
    sg = _sigmoid(acc + p_ref[0:1, :])
    o_ref[...] = (jnp.where(p_ref[1:2, :] > 0.5, acc, 1.0) * sg).astype(o_ref.dtype)


def _proj_gate(x2d, w, params):
    t_rows = x2d.shape[0]
    tm, tn = IN_TM, IN_TN
    assert t_rows % tm == 0 and GA_COLS % tn == 0
    return pl.pallas_call(
        _proj_gate_kernel,
        grid=(t_rows // tm, GA_COLS // tn),
        in_specs=[
            pl.BlockSpec((tm, D_MODEL), lambda i, j: (i, 0)),
            pl.BlockSpec((D_MODEL, tn), lambda i, j: (0, j)),
            pl.BlockSpec((SUBLANES, tn), lambda i, j: (0, j)),
        ],
        out_specs=pl.BlockSpec((tm, tn), lambda i, j: (i, j)),
        out_shape=jax.ShapeDtypeStruct((t_rows, GA_COLS), BF16),
        scratch_shapes=[pltpu.VMEM((tm, D_MODEL), BF16)],
        compiler_params=_proj_params(),
        name="proj_gate",
    )(x2d, w, params)


def _proj_attn_kernel(x_ref, w_ref, p_ref, cos_ref, sin_ref, o_ref, xb_s):
    @pl.when(pl.program_id(1) == 0)
    def _():
        xb_s[...] = x_ref[...].astype(BF16)

    tn = o_ref.shape[1]
    acc = jnp.dot(xb_s[...], w_ref[...], preferred_element_type=F32)
    cos = cos_ref[...]
    sin = sin_ref[...]
    scale = p_ref[0:1, :]
    for hd in range(tn // LANES):
        sl = slice(hd * LANES, (hd + 1) * LANES)
        a = acc[:, sl]
        partner = pltpu.roll(a, LANES // 2, axis=1)
        o_ref[:, sl] = ((a * cos + partner * sin) * scale[:, sl]).astype(o_ref.dtype)


def _proj_attn(x2d, w, params, cos_t, sin_t, seq):
    t_rows = x2d.shape[0]
    tm, tn = IN_TM, IN_TN
    assert t_rows % tm == 0 and seq % tm == 0 and tn == DA_WIDTH
    tiles_per_seq = seq // tm
    v_block = AT_V // tn
    table = lambda: pl.BlockSpec((tm, LANES), lambda i, j: (i % tiles_per_seq, j // v_block))
    return pl.pallas_call(
        _proj_attn_kernel,
        grid=(t_rows // tm, AT_COLS // tn),
        in_specs=[
            pl.BlockSpec((tm, D_MODEL), lambda i, j: (i, 0)),
            pl.BlockSpec((D_MODEL, tn), lambda i, j: (0, j)),
            pl.BlockSpec((SUBLANES, tn), lambda i, j: (0, j)),
            table(), table(),
        ],
        out_specs=pl.BlockSpec((tm, tn), lambda i, j: (i, j)),
        out_shape=jax.ShapeDtypeStruct((t_rows, AT_COLS), BF16),
        scratch_shapes=[pltpu.VMEM((tm, D_MODEL), BF16)],
        compiler_params=_proj_params(),
        name="proj_attn",
    )(x2d, w, params, cos_t, sin_t)


def _proj_conv_kernel(x_ref, xh_ref, w_ref, p_ref, wdt_ref, dtb_ref, o_ref, dt_ref, *, tiles_per_seq):
    i = pl.program_id(0)
    tm = o_ref.shape[0]
    xb = x_ref[...].astype(BF16)
    w = w_ref[...]
    acc = jnp.dot(xb, w, preferred_element_type=F32)
    halo = jnp.dot(xh_ref[...].astype(BF16), w, preferred_element_type=F32)
    halo = jnp.where(i % tiles_per_seq == 0, 0.0, halo)

    bias = p_ref[CONV_WIDTH:CONV_WIDTH + 1, :]
    taps = [p_ref[k:k + 1, :] for k in range(CONV_WIDTH)]

    def conv(rows_val):
        out = bias + taps[CONV_WIDTH - 1] * rows_val
        for back in range(1, CONV_WIDTH):
            out = out + taps[CONV_WIDTH - 1 - back] * pltpu.roll(rows_val, back, axis=0)
        return out * _sigmoid(out)

    head = BF16_ROWS
    ext = jnp.concatenate([halo, acc[:head]], axis=0)
    o_ref[:head, :] = conv(ext)[SUBLANES:, :].astype(o_ref.dtype)
    o_ref[head:, :] = conv(acc)[head:, :].astype(o_ref.dtype)

    raw = jnp.dot(xb, wdt_ref[...], preferred_element_type=F32) + dtb_ref[...]
    dt_ref[...] = jnp.maximum(raw, 0.0) + jnp.log(1.0 + jnp.exp(-jnp.abs(raw)))


def _proj_conv(x2d, w, params, w_dt, dt_bias, seq):
    t_rows = x2d.shape[0]
    tm = IN_TM
    assert t_rows % tm == 0 and seq % tm == 0
    tiles_per_seq = seq // tm
    halo_blocks = tm // SUBLANES
    full = lambda shape: pl.BlockSpec(shape, lambda i: (0, 0))
    return pl.pallas_call(
        functools.partial(_proj_conv_kernel, tiles_per_seq=tiles_per_seq),
        grid=(t_rows // tm,),
        in_specs=[
            pl.BlockSpec((tm, D_MODEL), lambda i: (i, 0)),
            pl.BlockSpec((SUBLANES, D_MODEL), lambda i: (jnp.maximum(i * halo_blocks - 1, 0), 0)),
            full((D_MODEL, CONV_CH)),
            full((SUBLANES, CONV_CH)),
            full((D_MODEL, SSD_HEADS)),
            full((1, SSD_HEADS)),
        ],
        out_specs=[
            pl.BlockSpec((tm, CONV_CH), lambda i: (i, 0)),
            pl.BlockSpec((tm, SSD_HEADS), lambda i: (i, 0)),
        ],
        out_shape=[
            jax.ShapeDtypeStruct((t_rows, CONV_CH), BF16),
            jax.ShapeDtypeStruct((t_rows, SSD_HEADS), F32),
        ],
        compiler_params=pltpu.CompilerParams(dimension_semantics=("arbitrary",),
                                             vmem_limit_bytes=VMEM_LIMIT_BYTES),
        name="proj_conv",
    )(x2d, x2d, w, params, w_dt, dt_bias)


def _ssd_kernel(xs_ref, b_ref, c_ref, sz_ref, dt_ref, a_ref, dskip_ref, normw_ref,
                expand_ref, bdtril_ref, tile8_ref,
                y_ref, acol_s, dtx_s, ht_s):
    rows = xs_ref.shape[0]
    n_chunks = rows // CHUNK
    gw = SSD_WIDTH // SSD_GROUPS

    @pl.when(pl.program_id(1) == 0)
    def _():
        ht_s[...] = jnp.zeros_like(ht_s)

    dt = dt_ref[...]
    a_cs = _dot01_left(bdtril_ref[...], dt * a_ref[...])
    expand = expand_ref[...]
    acol_s[...] = _dot01_right(a_cs, expand)
    dtx_s[...] = _dot01_right(dt, expand)

    lane = lax.broadcasted_iota(jnp.int32, (CHUNK, SSD_WIDTH), 1)
    row = lax.broadcasted_iota(jnp.int32, (CHUNK, SSD_WIDTH), 0)
    s_of_lane = lane & (CHUNK - 1)
    diag_mask = s_of_lane == row
    causal_mask = s_of_lane <= row
    bd_r = lax.broadcasted_iota(jnp.int32, (4 * CHUNK, 4 * SSD_HEAD_DIM), 0) // CHUNK
    bd_c = lax.broadcasted_iota(jnp.int32, (4 * CHUNK, 4 * SSD_HEAD_DIM), 1) // SSD_HEAD_DIM
    bd_mask = bd_r == bd_c
    tile8 = tile8_ref[...]
    dskip = dskip_ref[...]
    normw = normw_ref[...]

    def chunk_body(c, carry):
        r0 = pl.multiple_of(c * CHUNK, CHUNK)
        acol = acol_s[pl.ds(r0, CHUNK), :]
        dtx = dtx_s[pl.ds(r0, CHUNK), :]
        xs = xs_ref[pl.ds(r0, CHUNK), :].astype(F32)
        bc = b_ref[pl.ds(r0, CHUNK), :]
        cc = c_ref[pl.ds(r0, CHUNK), :]
        a_last = acol[CHUNK - 1:CHUNK, :]

        arow = jnp.sum(jnp.where(diag_mask, acol, 0.0), axis=0, keepdims=True)
        decay_ls = jnp.exp(jnp.where(causal_mask, acol - arow, NEG_BIG))

        cb = []
        for g in range(SSD_GROUPS):
            cg = cc[:, g * SSD_STATE:(g + 1) * SSD_STATE]
            bg = bc[:, g * SSD_STATE:(g + 1) * SSD_STATE]
            cb_g = lax.dot_general(cg, bg, (((1,), (1,)), ((), ())), preferred_element_type=F32)
            cb.append(_dot01_right(cb_g, tile8))
        w_ls = (jnp.concatenate(cb, axis=1) * decay_ls).astype(BF16)

        xdt = (xs * dtx).astype(BF16)
        y_parts = []
        for qd in range(SSD_WIDTH // (4 * SSD_HEAD_DIM)):
            sl = slice(qd * 4 * SSD_HEAD_DIM, (qd + 1) * 4 * SSD_HEAD_DIM)
            xq = xdt[:, sl]
            bd = jnp.where(bd_mask, jnp.concatenate([xq] * 4, axis=0), jnp.zeros((), BF16))
            y_parts.append(jnp.dot(w_ls[:, sl], bd, preferred_element_type=F32))
        y_diag = jnp.concatenate(y_parts, axis=1)

        ht = ht_s[...]
        ht_b = ht.astype(BF16)
        y_off = jnp.concatenate(
            [jnp.dot(cc[:, g * SSD_STATE:(g + 1) * SSD_STATE], ht_b[:, g * gw:(g + 1) * gw],
                     preferred_element_type=F32) for g in range(SSD_GROUPS)], axis=1)
        y = y_diag + y_off * jnp.exp(acol) + dskip * xs

        gated = y * sz_ref[pl.ds(r0, CHUNK), :].astype(F32)
        normed = []
        for g in range(SSD_GROUPS):
            gg = gated[:, g * gw:(g + 1) * gw]
            ms = jnp.mean(gg * gg, axis=-1, keepdims=True)
            normed.append(gg * lax.rsqrt(ms + EPS))
        y_ref[pl.ds(r0, CHUNK), :] = (jnp.concatenate(normed, axis=1) * normw).astype(y_ref.dtype)

        xd = (xs * (jnp.exp(a_last - acol) * dtx)).astype(BF16)
        st = jnp.concatenate(
            [lax.dot_general(bc[:, g * SSD_STATE:(g + 1) * SSD_STATE], xd[:, g * gw:(g + 1) * gw],
                             (((0,), (0,)), ((), ())), preferred_element_type=F32)
             for g in range(SSD_GROUPS)], axis=1)
        ht_s[...] = ht * jnp.exp(a_last) + st
        return carry

    lax.fori_loop(0, n_chunks, chunk_body, 0, unroll=2)


def _ssd(conv_act, gate_act, dt, a_row, dskip_row, normw_row, batch, seq):
    rows = SSD_ROWS
    assert seq % rows == 0 and rows % (2 * CHUNK) == 0
    blocks_per_seq = seq // rows
    hh = jnp.arange(SSD_WIDTH) // SSD_HEAD_DIM
    expand = (jnp.arange(SSD_HEADS)[:, None] == hh[None, :]).astype(BF16)
    rr = jnp.arange(rows)
    same_chunk = (rr[:, None] // CHUNK) == (rr[None, :] // CHUNK)
    bdtril = (same_chunk & (rr[None, :] <= rr[:, None])).astype(BF16)
    gw = SSD_WIDTH // SSD_GROUPS
    tile8 = (jnp.arange(CHUNK)[:, None] == (jnp.arange(gw)[None, :] % CHUNK)).astype(BF16)

    def row_map(b, r):
        return b * blocks_per_seq + r

    def col_spec(width, off):
        assert off % width == 0
        return pl.BlockSpec((rows, width), lambda b, r: (row_map(b, r), off // width))

    const = lambda shape: pl.BlockSpec(shape, lambda b, r: (0, 0))
    return pl.pallas_call(
        _ssd_kernel,
        grid=(batch, blocks_per_seq),
        in_specs=[
            col_spec(SSD_WIDTH, CV_XS),
            col_spec(BC_WIDTH, CV_B),
            col_spec(BC_WIDTH, CV_C),
            col_spec(SSD_WIDTH, GA_SZ),
            pl.BlockSpec((rows, SSD_HEADS), lambda b, r: (row_map(b, r), 0)),
            const((1, SSD_HEADS)),
            const((1, SSD_WIDTH)),
            const((1, SSD_WIDTH)),
            const((SSD_HEADS, SSD_WIDTH)),
            const((rows, rows)),
            const((CHUNK, gw)),
        ],
        out_specs=pl.BlockSpec((rows, SSD_WIDTH), lambda b, r: (row_map(b, r), 0)),
        out_shape=jax.ShapeDtypeStruct((batch * seq, SSD_WIDTH), BF16),
        scratch_shapes=[
            pltpu.VMEM((rows, SSD_WIDTH), F32),
            pltpu.VMEM((rows, SSD_WIDTH), F32),
            pltpu.VMEM((SSD_STATE, SSD_WIDTH), F32),
        ],
        compiler_params=pltpu.CompilerParams(
            dimension_semantics=("arbitrary", "arbitrary"),
            vmem_limit_bytes=VMEM_LIMIT_BYTES),
        name="ssd",
    )(conv_act, conv_act, conv_act, gate_act, dt, a_row, dskip_row, normw_row, expand, bdtril, tile8)


def _attn_kernel(q_ref, k_ref, v_ref, gb_ref, lq1_ref, lk1_ref, lq2_ref, lk2_ref, subw_ref,
                 o_ref, vaug_s, m_s, acc_s, s_s, *, lambda_init):
    qi = pl.program_id(2)
    tq = q_ref.shape[0]
    tk = ATT_TK
    seq = k_ref.shape[0]

    @pl.when(qi == 0)
    def _():
        vaug_s[:, :DA_VDIM] = v_ref[...]
        vaug_s[:, DA_VDIM:] = jnp.ones((seq, DA_VDIM), vaug_s.dtype)

    assert tq == tk
    band = ATT_BAND
    n_bands = tq // band
    half_lane = lax.broadcasted_iota(jnp.int32, (band, DA_VDIM), 1) & (LANES // 2 - 1)
    in_map0 = (half_lane < ROT_HALF) | ((half_lane >= ROT_DIM) & (half_lane < ROT_DIM + ROT_PASS))
    zero = jnp.zeros((), q_ref.dtype)

    qq = []
    for c in range(n_bands):
        q = q_ref[c * band:(c + 1) * band, :]
        qq.append(jnp.concatenate([jnp.where(in_map0, q, zero), jnp.where(in_map0, zero, q)], axis=0))

    m_s[...] = jnp.full_like(m_s, NEG_BIG)
    acc_s[...] = jnp.zeros_like(acc_s)

    def scores(kt):
        k0 = pl.multiple_of(kt * tk, tk)
        k = k_ref[pl.ds(k0, tk), :]
        for c in range(n_bands):
            s_s[c] = lax.dot_general(qq[c], k, (((1,), (1,)), ((), ())), preferred_element_type=F32)

    def softmax_pv(c, s, vt, mask):
        rows = slice(c * 2 * band, (c + 1) * 2 * band)
        if mask is not None:
            s = jnp.where(mask, s, NEG_BIG)
        m_old = m_s[rows, :]
        m_new = jnp.maximum(m_old, jnp.max(s, axis=-1, keepdims=True))
        alpha = jnp.exp2(m_old - m_new)
        p = jnp.exp2(s - m_new[:, 0:1]).astype(BF16)
        pv = jnp.dot(p, vt, preferred_element_type=F32)
        acc_s[rows, :] = jnp.concatenate([alpha, alpha], axis=1) * acc_s[rows, :] + pv
        m_s[rows, :] = m_new

    scores(0)

    def body(kt, carry):
        k0 = pl.multiple_of(kt * tk, tk)
        vt = vaug_s[pl.ds(k0, tk), :]
        for c in range(n_bands):
            softmax_pv(c, s_s[c], vt, None)
        scores(kt + 1)
        return carry

    lax.fori_loop(0, qi, body, 0)

    k0 = pl.multiple_of(qi * tk, tk)
    for c in range(n_bands):
        nk = (c + 1) * band
        r_chunk = (c * band + lax.broadcasted_iota(jnp.int32, (band, nk), 0)) // CHUNK
        k_chunk = lax.broadcasted_iota(jnp.int32, (band, nk), 1) // CHUNK
        ok = k_chunk <= r_chunk
        softmax_pv(c, s_s[c, :, :nk], vaug_s[pl.ds(k0, nk), :], jnp.concatenate([ok, ok], axis=0))

    lam = (jnp.exp(jnp.sum(lq1_ref[...] * lk1_ref[...], axis=-1, keepdims=True))
           - jnp.exp(jnp.sum(lq2_ref[...] * lk2_ref[...], axis=-1, keepdims=True)) + lambda_init)
    scale = subw_ref[...] * (1.0 - lambda_init)
    for c in range(n_bands):
        acc = acc_s[c * 2 * band:(c + 1) * 2 * band, :]
        o_all = acc[:, :DA_VDIM] / acc[:, DA_VDIM:]
        o = o_all[:band] - lam * o_all[band:]
        ms = jnp.mean(o * o, axis=-1, keepdims=True)
        o = o * lax.rsqrt(ms + EPS) * scale
        rows = slice(c * band, (c + 1) * band)
        o_ref[rows, :] = (o * gb_ref[rows, :].astype(F32)).astype(o_ref.dtype)


def _attention(attn_act, gate_act, lq1, lk1, lq2, lk2, subw, batch, seq, lambda_init):
    tq = ATT_TQ
    assert seq % ATT_TK == 0 and ATT_TK == tq and tq % ATT_BAND == 0 and ATT_BAND % CHUNK == 0
    q_tiles = seq // tq

    def head_spec(rows_blk, off):
        assert off % DA_VDIM == 0
        if rows_blk == seq:
            return pl.BlockSpec((seq, DA_VDIM), lambda b, hd, qi: (b, off // DA_VDIM + hd))
        return pl.BlockSpec((rows_blk, DA_VDIM), lambda b, hd, qi: (b * q_tiles + qi, off // DA_VDIM + hd))

    vec = lambda n: pl.BlockSpec((1, n), lambda b, hd, qi: (0, 0))
    kern = functools.partial(_attn_kernel, lambda_init=lambda_init)
    return pl.pallas_call(
        kern,
        grid=(batch, DA_HEADS, q_tiles),
        in_specs=[
            head_spec(tq, AT_Q),
            head_spec(seq, AT_K),
            head_spec(seq, AT_V),
            head_spec(tq, GA_GB),
            vec(DA_HEAD_DIM), vec(DA_HEAD_DIM), vec(DA_HEAD_DIM), vec(DA_HEAD_DIM),
            vec(DA_VDIM),
        ],
        out_specs=pl.BlockSpec((tq, DA_VDIM), lambda b, hd, qi: (b * q_tiles + qi, hd)),
        out_shape=jax.ShapeDtypeStruct((batch * seq, DA_WIDTH), BF16),
        scratch_shapes=[
            pltpu.VMEM((seq, 2 * DA_VDIM), BF16),
            pltpu.VMEM((2 * tq, LANES), F32),
            pltpu.VMEM((2 * tq, 2 * DA_VDIM), F32),
            pltpu.VMEM((tq // ATT_BAND, 2 * ATT_BAND, ATT_TK), F32),
        ],
        compiler_params=pltpu.CompilerParams(
            dimension_semantics=("arbitrary", "arbitrary", "arbitrary"),
            vmem_limit_bytes=VMEM_LIMIT_BYTES),
        name="diffattn",
    )(attn_act, attn_act, attn_act, gate_act, lq1, lk1, lq2, lk2, subw)


def _out_kernel(x_ref, ya_ref, ob_ref, g0_ref, g1_ref, wa_ref, wb_ref, wo_ref, lng_ref, lnb_ref,
                out_ref, *, alpha):
    d = functools.partial(jnp.dot, preferred_element_type=F32)
    branch_a = d(ya_ref[...], wa_ref[...])
    branch_b = d(ob_ref[...], wb_ref[...])
    merged = g0_ref[...].astype(F32) * branch_a + g1_ref[...].astype(F32) * branch_b
    y = d(merged.astype(BF16), wo_ref[...])
    r = alpha * x_ref[...] + y
    mu = jnp.mean(r, axis=-1, keepdims=True)
    rc = r - mu
    var = jnp.mean(rc * rc, axis=-1, keepdims=True)
    out_ref[...] = rc * lax.rsqrt(var + EPS) * lng_ref[...] + lnb_ref[...]


def _output_stage(x2d, y_ssd, o_att, gate_act, w_a, w_b, w_o, ln_g, ln_b, alpha):
    t_rows = x2d.shape[0]
    tm = OUT_TM
    assert t_rows % tm == 0 and GA_GM % D_MODEL == 0
    row = lambda width: pl.BlockSpec((tm, width), lambda i: (i, 0))
    full = lambda shape: pl.BlockSpec(shape, lambda i: (0, 0))
    gate_spec = lambda k: pl.BlockSpec((tm, D_MODEL), lambda i: (i, GA_GM // D_MODEL + k))
    return pl.pallas_call(
        functools.partial(_out_kernel, alpha=alpha),
        grid=(t_rows // tm,),
        in_specs=[row(D_MODEL), row(SSD_WIDTH), row(DA_WIDTH), gate_spec(0), gate_spec(1),
                  full((SSD_WIDTH, D_MODEL)), full((DA_WIDTH, D_MODEL)), full((D_MODEL, D_MODEL)),
                  full((1, D_MODEL)), full((1, D_MODEL))],
        out_specs=row(D_MODEL),
        out_shape=jax.ShapeDtypeStruct((t_rows, D_MODEL), F32),
        compiler_params=pltpu.CompilerParams(
            dimension_semantics=("arbitrary",),
            vmem_limit_bytes=VMEM_LIMIT_BYTES),
        name="outproj",
    )(x2d, y_ssd, o_att, gate_act, gate_act, w_a, w_b, w_o, ln_g, ln_b)


def _rope_tables(seq):
    pos = jnp.arange(seq, dtype=F32)
    inv_freq = ROPE_THETA ** (-jnp.arange(0, ROT_DIM, 2, dtype=F32) / ROT_DIM)
    ang = pos[:, None] * inv_freq[None, :]
    cos, sin = jnp.cos(ang), jnp.sin(ang)
    half = jnp.arange(LANES) % (LANES // 2)
    f = half % ROT_HALF
    rotary = half < ROT_DIM
    first_half = jnp.arange(LANES) < LANES // 2
    cos_t = jnp.where(rotary[None, :], cos[:, f], 1.0)
    sin_t = jnp.where(rotary[None, :], jnp.where(first_half[None, :], -sin[:, f], sin[:, f]), 0.0)
    ident = jnp.ones((seq, LANES), F32)
    return (jnp.concatenate([cos_t, ident], axis=1), jnp.concatenate([sin_t, jnp.zeros_like(ident)], axis=1))


def _permute_head_lanes(w):
    d_in = w.shape[0]
    w4 = w.reshape(d_in, DA_HEADS, 2, DA_HEAD_DIM)
    parts = [w4[..., :ROT_HALF], w4[..., ROT_DIM:ROT_DIM + ROT_PASS],
             w4[..., ROT_HALF:ROT_DIM], w4[..., ROT_DIM + ROT_PASS:]]
    parts = [p.reshape(d_in, DA_HEADS, -1) for p in parts]
    return jnp.concatenate(parts, axis=-1).reshape(d_in, DA_WIDTH)


def _layer(x2d, batch, seq, depth_total, l, w_in, b_gate, conv_w, conv_b, dt_bias, a_log, d_skip,
           ssd_norm_w, lambda_q1, lambda_k1, lambda_q2, lambda_k2, subln_w, w_a, w_b, w_o, ln_g, ln_b,
           tables):
    alpha = (2.0 * depth_total) ** 0.25
    lambda_init = 0.8 - 0.6 * math.exp(-0.3 * l)
    offs = [0]
    for s in IN_SIZES:
        offs.append(offs[-1] + s)
    z_w, xbc_w, dt_w, q_w, k_w, v_w, gb_w, gm_w = [w_in[:, offs[n]:offs[n + 1]] for n in range(len(IN_SIZES))]

    w_gate = jnp.concatenate([z_w, gb_w, gm_w], axis=1).astype(BF16)
    p_gate = jnp.zeros((SUBLANES, GA_COLS), F32)
    p_gate = p_gate.at[0, GA_GM:].set(b_gate)
    p_gate = p_gate.at[1, :GA_GM].set(1.0)
    w_attn = jnp.concatenate([_permute_head_lanes(q_w), _permute_head_lanes(k_w), v_w], axis=1).astype(BF16)
    p_attn = jnp.ones((SUBLANES, AT_COLS), F32).at[0, :AT_K].set(DA_HEAD_DIM ** -0.5 * LOG2E)
    p_conv = jnp.zeros((SUBLANES, CONV_CH), F32).at[:CONV_WIDTH].set(conv_w).at[CONV_WIDTH].set(conv_b)

    gate_act = _proj_gate(x2d, w_gate, p_gate)
    attn_act = _proj_attn(x2d, w_attn, p_attn, *tables, seq)
    conv_act, dt = _proj_conv(x2d, xbc_w.astype(BF16), p_conv, dt_w.astype(BF16),
                              dt_bias[None, :].astype(F32), seq)

    a_row = -jnp.exp(a_log.astype(F32))[None, :]
    dskip_row = jnp.repeat(d_skip.astype(F32), SSD_HEAD_DIM)[None, :]
    y_ssd = _ssd(conv_act, gate_act, dt, a_row, dskip_row, ssd_norm_w[None, :].astype(F32), batch, seq)

    o_att = _attention(attn_act, gate_act, lambda_q1[None, :], lambda_k1[None, :], lambda_q2[None, :],
                       lambda_k2[None, :], subln_w[None, :], batch, seq, lambda_init)

    return _output_stage(x2d, y_ssd, o_att, gate_act, w_a.astype(BF16), w_b.astype(BF16), w_o.astype(BF16),
                         ln_g[None, :], ln_b[None, :], alpha)


def kernel(x, w_in, b_gate, conv_w, conv_b, dt_bias, a_log, d_skip, ssd_norm_w, lambda_q1, lambda_k1,
           lambda_q2, lambda_k2, subln_w, w_a, w_b, w_o, ln_g, ln_b):
    batch, seq, _ = x.shape
    depth = w_in.shape[0]
    tables = _rope_tables(seq)
    x2d = x.reshape(batch * seq, D_MODEL)
    for l in range(depth):
        x2d = _layer(x2d, batch, seq, depth, l, w_in[l], b_gate[l], conv_w[l], conv_b[l], dt_bias[l],
                     a_log[l], d_skip[l], ssd_norm_w[l], lambda_q1[l], lambda_k1[l], lambda_q2[l],
                     lambda_k2[l], subln_w[l], w_a[l], w_b[l], w_o[l], ln_g[l], ln_b[l], tables)
    return x2d.reshape(batch, seq, D_MODEL)
```

```python
import functools
import math

import jax
import jax.numpy as jnp
from jax import lax
from jax.experimental import pallas as pl
from jax.experimental.pallas import tpu as pltpu

F32 = jnp.float32
BF16 = jnp.bfloat16

D_MODEL = 1024
CHUNK = 64
SSD_HEADS = 16
SSD_HEAD_DIM = 64
SSD_WIDTH = SSD_HEADS * SSD_HEAD_DIM
SSD_GROUPS = 2
SSD_STATE = 128
CONV_WIDTH = 4
BC_WIDTH = SSD_GROUPS * SSD_STATE
CONV_CH = SSD_WIDTH + 2 * BC_WIDTH
DA_HEADS = 8
DA_HEAD_DIM = 64
DA_VDIM = 2 * DA_HEAD_DIM
DA_WIDTH = DA_HEADS * DA_VDIM
ROPE_THETA = 500000.0
ROT_DIM = DA_HEAD_DIM // 4
ROT_HALF = ROT_DIM // 2
N_BRANCH = 2
EPS = 1e-5
IN_SIZES = (SSD_WIDTH, CONV_CH, SSD_HEADS, DA_WIDTH, DA_WIDTH, DA_WIDTH, DA_WIDTH, N_BRANCH * D_MODEL)

GA_SZ, GA_GB, GA_GM = 0, SSD_WIDTH, SSD_WIDTH + DA_WIDTH
GA_COLS = GA_GM + N_BRANCH * D_MODEL
AT_Q, AT_K, AT_V = 0, DA_WIDTH, 2 * DA_WIDTH
AT_COLS = 3 * DA_WIDTH
CV_XS, CV_B, CV_C = 0, SSD_WIDTH, SSD_WIDTH + BC_WIDTH

ROT_PASS = (DA_HEAD_DIM - ROT_DIM) // 2

LANES = 128
SUBLANES = 8
BF16_ROWS = 16
VMEM_LIMIT_BYTES = 56 * 1024 * 1024

IN_TM = 512
SSD_ROWS = 512
ATT_TQ = 1024
ATT_TK = 1024
ATT_BAND = 256
OUT_TM = 512

NEG_BIG = -1e30
LOG2E = 1.4426950408889634


def _sigmoid(t):
    return 1.0 / (1.0 + jnp.exp2(t * (-LOG2E)))


def _split2(v):
    hi = v.astype(BF16)
    mid = (v - hi.astype(F32)).astype(BF16)
    return hi, mid


def _dot01_right(v, m01):
    hi, mid = _split2(v)
    d = functools.partial(jnp.dot, preferred_element_type=F32)
    return d(hi, m01) + d(mid, m01)


def _dot01_left(m01, v):
    hi, mid = _split2(v)
    d = functools.partial(jnp.dot, preferred_element_type=F32)
    return d(m01, hi) + d(m01, mid)


def _proj_params():
    return pltpu.CompilerParams(dimension_semantics=("arbitrary",),
                                vmem_limit_bytes=VMEM_LIMIT_BYTES)


def _proj_gate_kernel(x_ref, w_ref, bias_ref, o_ref):
    acc = jnp.dot(x_ref[...].astype(BF16), w_ref[...], preferred_element_type=F32)
    t = acc[:, :GA_GM]
    o_ref[:, :GA_GM] = (t * _sigmoid(t)).astype(o_ref.dtype)
    o_ref[:, GA_GM:] = _sigmoid(acc[:, GA_GM:] + bias_ref[...]).astype(o_ref.dtype)


def _proj_gate(x2d, w, b_gate_row):
    t_rows = x2d.shape[0]
    tm = IN_TM
    assert t_rows % tm == 0
    full = lambda shape: pl.BlockSpec(shape, lambda i: (0, 0))
    return pl.pallas_call(
        _proj_gate_kernel,
        grid=(t_rows // tm,),
        in_specs=[
            pl.BlockSpec((tm, D_MODEL), lambda i: (i, 0)),
            full((D_MODEL, GA_COLS)),
            full((1, N_BRANCH * D_MODEL)),
        ],
        out_specs=pl.BlockSpec((tm, GA_COLS), lambda i: (i, 0)),
        out_shape=jax.ShapeDtypeStruct((t_rows, GA_COLS), BF16),
        compiler_params=_proj_params(),
        name="proj_gate",
    )(x2d, w, b_gate_row)


def _proj_attn_kernel(x_ref, w_ref, cos_ref, sin_ref, o_ref, *, q_scale):
    acc = jnp.dot(x_ref[...].astype(BF16), w_ref[...], preferred_element_type=F32)
    cos = cos_ref[...]
    sin = sin_ref[...]
    cos_q = cos * q_scale
    sin_q = sin * q_scale
    for hd in range(AT_V // LANES):
        sl = slice(hd * LANES, (hd + 1) * LANES)
        a = acc[:, sl]
        partner = pltpu.roll(a, LANES // 2, axis=1)
        c, s = (cos_q, sin_q) if hd < AT_K // LANES else (cos, sin)
        o_ref[:, sl] = (a * c + partner * s).astype(o_ref.dtype)
    o_ref[:, AT_V:] = acc[:, AT_V:].astype(o_ref.dtype)


def _proj_attn(x2d, w, cos_t, sin_t, seq):
    t_rows = x2d.shape[0]
    tm = IN_TM
    assert t_rows % tm == 0 and seq % tm == 0
    tiles_per_seq = seq // tm
    table = lambda: pl.BlockSpec((tm, LANES), lambda i: (i % tiles_per_seq, 0))
    return pl.pallas_call(
        functools.partial(_proj_attn_kernel, q_scale=DA_HEAD_DIM ** -0.5 * LOG2E),
        grid=(t_rows // tm,),
        in_specs=[
            pl.BlockSpec((tm, D_MODEL), lambda i: (i, 0)),
            pl.BlockSpec((D_MODEL, AT_COLS), lambda i: (0, 0)),
            table(), table(),
        ],
        out_specs=pl.BlockSpec((tm, AT_COLS), lambda i: (i, 0)),
        out_shape=jax.ShapeDtypeStruct((t_rows, AT_COLS), BF16),
        compiler_params=_proj_params(),
        name="proj_attn",
    )(x2d, w, cos_t, sin_t)


def _proj_conv_kernel(x_ref, xh_ref, w_ref, p_ref, wdt_ref, dtb_ref, o_ref, dt_ref, *, tiles_per_seq):
    i = pl.program_id(0)
    tm = o_ref.shape[0]
    xb = x_ref[...].astype(BF16)
    w = w_ref[...]
    acc = jnp.dot(xb, w, preferred_element_type=F32)
    halo = jnp.dot(xh_ref[...].astype(BF16), w, preferred_element_type=F32)
    halo = jnp.where(i % tiles_per_seq == 0, 0.0, halo)

    bias = p_ref[CONV_WIDTH:CONV_WIDTH + 1, :]
    taps = [p_ref[k:k + 1, :] for k in range(CONV_WIDTH)]

    def conv(rows_val):
        out = taps[0] * rows_val
        for k in range(1, CONV_WIDTH):
            out = pltpu.roll(out, 1, axis=0) + taps[k] * rows_val
        out = out + bias
        return out * _sigmoid(out)

    head = BF16_ROWS
    ext = jnp.concatenate([halo, acc[:head]], axis=0)
    o_ref[:head, :] = conv(ext)[SUBLANES:, :].astype(o_ref.dtype)
    o_ref[head:, :] = conv(acc)[head:, :].astype(o_ref.dtype)

    raw = jnp.dot(xb, wdt_ref[...], preferred_element_type=F32) + dtb_ref[...]
    dt_ref[...] = jnp.maximum(raw, 0.0) + jnp.log(1.0 + jnp.exp(-jnp.abs(raw)))


def _proj_conv(x2d, w, params, w_dt, dt_bias, seq):
    t_rows = x2d.shape[0]
    tm = IN_TM
    assert t_rows % tm == 0 and seq % tm == 0
    tiles_per_seq = seq // tm
    halo_blocks = tm // SUBLANES
    full = lambda shape: pl.BlockSpec(shape, lambda i: (0, 0))
    return pl.pallas_call(
        functools.partial(_proj_conv_kernel, tiles_per_seq=tiles_per_seq),
        grid=(t_rows // tm,),
        in_specs=[
            pl.BlockSpec((tm, D_MODEL), lambda i: (i, 0)),
            pl.BlockSpec((SUBLANES, D_MODEL), lambda i: (jnp.maximum(i * halo_blocks - 1, 0), 0)),
            full((D_MODEL, CONV_CH)),
            full((SUBLANES, CONV_CH)),
            full((D_MODEL, SSD_HEADS)),
            full((1, SSD_HEADS)),
        ],
        out_specs=[
            pl.BlockSpec((tm, CONV_CH), lambda i: (i, 0)),
            pl.BlockSpec((tm, SSD_HEADS), lambda i: (i, 0)),
        ],
        out_shape=[
            jax.ShapeDtypeStruct((t_rows, CONV_CH), BF16),
            jax.ShapeDtypeStruct((t_rows, SSD_HEADS), F32),
        ],
        compiler_params=pltpu.CompilerParams(dimension_semantics=("arbitrary",),
                                             vmem_limit_bytes=VMEM_LIMIT_BYTES),
        name="proj_conv",
    )(x2d, x2d, w, params, w_dt, dt_bias)


def _ssd_kernel(xs_ref, b_ref, c_ref, sz_ref, dt_ref, a_ref, dskip_ref, normw_ref,
                expand_ref, bdtril_ref, tile8_ref,
                y_ref, acol_s, dtx_s, ht_s):
    rows = xs_ref.shape[0]
    n_chunks = rows // CHUNK
    gw = SSD_WIDTH // SSD_GROUPS

    @pl.when(pl.program_id(1) == 0)
    def _():
        ht_s[...] = jnp.zeros_like(ht_s)

    dt = dt_ref[...]
    a_cs = _dot01_left(bdtril_ref[...], dt * a_ref[...])
    expand = expand_ref[...]
    acol_s[...] = _dot01_right(a_cs, expand)
    dtx_s[...] = _dot01_right(dt, expand)

    lane = lax.broadcasted_iota(jnp.int32, (CHUNK, SSD_WIDTH), 1)
    row = lax.broadcasted_iota(jnp.int32, (CHUNK, SSD_WIDTH), 0)
    s_of_lane = lane & (CHUNK - 1)
    diag_mask = s_of_lane == row
    causal_mask = s_of_lane <= row
    bd_r = lax.broadcasted_iota(jnp.int32, (4 * CHUNK, 4 * SSD_HEAD_DIM), 0) // CHUNK
    bd_c = lax.broadcasted_iota(jnp.int32, (4 * CHUNK, 4 * SSD_HEAD_DIM), 1) // SSD_HEAD_DIM
    bd_mask = bd_r == bd_c
    tile8 = tile8_ref[...]
    dskip = dskip_ref[...]
    normw = normw_ref[...]

    def chunk_body(c, carry):
        r0 = pl.multiple_of(c * CHUNK, CHUNK)
        acol = acol_s[pl.ds(r0, CHUNK), :]
        dtx = dtx_s[pl.ds(r0, CHUNK), :]
        xs = xs_ref[pl.ds(r0, CHUNK), :].astype(F32)
        bc = b_ref[pl.ds(r0, CHUNK), :]
        cc = c_ref[pl.ds(r0, CHUNK), :]
        a_last = acol[CHUNK - 1:CHUNK, :]

        arow = jnp.sum(jnp.where(diag_mask, acol, 0.0), axis=0, keepdims=True)
        decay_ls = jnp.exp(jnp.where(causal_mask, acol - arow, NEG_BIG))

        cb = []
        for g in range(SSD_GROUPS):
            cg = cc[:, g * SSD_STATE:(g + 1) * SSD_STATE]
            bg = bc[:, g * SSD_STATE:(g + 1) * SSD_STATE]
            cb_g = lax.dot_general(cg, bg, (((1,), (1,)), ((), ())), preferred_element_type=F32)
            cb.append(_dot01_right(cb_g, tile8))
        w_ls = (jnp.concatenate(cb, axis=1) * decay_ls).astype(BF16)

        xdt = (xs * dtx).astype(BF16)
        y_parts = []
        for qd in range(SSD_WIDTH // (4 * SSD_HEAD_DIM)):
            sl = slice(qd * 4 * SSD_HEAD_DIM, (qd + 1) * 4 * SSD_HEAD_DIM)
            xq = xdt[:, sl]
            bd = jnp.where(bd_mask, jnp.concatenate([xq] * 4, axis=0), jnp.zeros((), BF16))
            y_parts.append(jnp.dot(w_ls[:, sl], bd, preferred_element_type=F32))
        y_diag = jnp.concatenate(y_parts, axis=1)

        ht = ht_s[...]
        ht_b = ht.astype(BF16)
        y_off = jnp.concatenate(
            [jnp.dot(cc[:, g * SSD_STATE:(g + 1) * SSD_STATE], ht_b[:, g * gw:(g + 1) * gw],
                     preferred_element_type=F32) for g in range(SSD_GROUPS)], axis=1)
        y = y_diag + y_off * jnp.exp(acol) + dskip * xs

        gated = y * sz_ref[pl.ds(r0, CHUNK), :].astype(F32)
        normed = []
        for g in range(SSD_GROUPS):
            gg = gated[:, g * gw:(g + 1) * gw]
            ms = jnp.mean(gg * gg, axis=-1, keepdims=True)
            normed.append(gg * lax.rsqrt(ms + EPS))
        y_ref[pl.ds(r0, CHUNK), :] = (jnp.concatenate(normed, axis=1) * normw).astype(y_ref.dtype)

        xd = (xs * (jnp.exp(a_last - acol) * dtx)).astype(BF16)
        st = jnp.concatenate(
            [lax.dot_general(bc[:, g * SSD_STATE:(g + 1) * SSD_STATE], xd[:, g * gw:(g + 1) * gw],
                             (((0,), (0,)), ((), ())), preferred_element_type=F32)
             for g in range(SSD_GROUPS)], axis=1)
        ht_s[...] = ht * jnp.exp(a_last) + st
        return carry

    lax.fori_loop(0, n_chunks, chunk_body, 0, unroll=2)


def _ssd(conv_act, gate_act, dt, a_row, dskip_row, normw_row, batch, seq):
    rows = SSD_ROWS
    assert seq % rows == 0 and rows % (2 * CHUNK) == 0
    blocks_per_seq = seq // rows
    hh = jnp.arange(SSD_WIDTH) // SSD_HEAD_DIM
    expand = (jnp.arange(SSD_HEADS)[:, None] == hh[None, :]).astype(BF16)
    rr = jnp.arange(rows)
    same_chunk = (rr[:, None] // CHUNK) == (rr[None, :] // CHUNK)
    bdtril = (same_chunk & (rr[None, :] <= rr[:, None])).astype(BF16)
    gw = SSD_WIDTH // SSD_GROUPS
    tile8 = (jnp.arange(CHUNK)[:, None] == (jnp.arange(gw)[None, :] % CHUNK)).astype(BF16)

    def row_map(b, r):
        return b * blocks_per_seq + r

    def col_spec(width, off):
        assert off % width == 0
        return pl.BlockSpec((rows, width), lambda b, r: (row_map(b, r), off // width))

    const = lambda shape: pl.BlockSpec(shape, lambda b, r: (0, 0))
    return pl.pallas_call(
        _ssd_kernel,
        grid=(batch, blocks_per_seq),
        in_specs=[
            col_spec(SSD_WIDTH, CV_XS),
            col_spec(BC_WIDTH, CV_B),
            col_spec(BC_WIDTH, CV_C),
            col_spec(SSD_WIDTH, GA_SZ),
            pl.BlockSpec((rows, SSD_HEADS), lambda b, r: (row_map(b, r), 0)),
            const((1, SSD_HEADS)),
            const((1, SSD_WIDTH)),
            const((1, SSD_WIDTH)),
            const((SSD_HEADS, SSD_WIDTH)),
            const((rows, rows)),
            const((CHUNK, gw)),
        ],
        out_specs=pl.BlockSpec((rows, SSD_WIDTH), lambda b, r: (row_map(b, r), 0)),
        out_shape=jax.ShapeDtypeStruct((batch * seq, SSD_WIDTH), BF16),
        scratch_shapes=[
            pltpu.VMEM((rows, SSD_WIDTH), F32),
            pltpu.VMEM((rows, SSD_WIDTH), F32),
            pltpu.VMEM((SSD_STATE, SSD_WIDTH), F32),
        ],
        compiler_params=pltpu.CompilerParams(
            dimension_semantics=("arbitrary", "arbitrary"),
            vmem_limit_bytes=VMEM_LIMIT_BYTES),
        name="ssd",
    )(conv_act, conv_act, conv_act, gate_act, dt, a_row, dskip_row, normw_row, expand, bdtril, tile8)


def _attn_kernel(q_ref, k_ref, v_ref, gb_ref, lq1_ref, lk1_ref, lq2_ref, lk2_ref, subw_ref,
                 o_ref, vaug_s, m_s, acc_s, sa_s, sb_s, *, lambda_init):
    qi = pl.program_id(2)
    tq = q_ref.shape[0]
    tk = ATT_TK
    seq = k_ref.shape[0]

    @pl.when(qi == 0)
    def _():
        vaug_s[:, :DA_VDIM] = v_ref[...].astype(vaug_s.dtype)
        vaug_s[:, DA_VDIM:] = jnp.ones((seq, DA_VDIM), vaug_s.dtype)

    assert tq == tk
    band = ATT_BAND
    n_bands = tq // band
    half_lane = lax.broadcasted_iota(jnp.int32, (band, DA_VDIM), 1) & (LANES // 2 - 1)
    in_map0 = (half_lane < ROT_HALF) | ((half_lane >= ROT_DIM) & (half_lane < ROT_DIM + ROT_PASS))
    zero = jnp.zeros((), q_ref.dtype)

    qq = []
    for c in range(n_bands):
        q = q_ref[c * band:(c + 1) * band, :]
        qq.append(jnp.concatenate([jnp.where(in_map0, q, zero), jnp.where(in_map0, zero, q)], axis=0))

    m_s[...] = jnp.full_like(m_s, NEG_BIG)
    acc_s[...] = jnp.zeros_like(acc_s)

    def scores(dst, kt):
        k0 = pl.multiple_of(kt * tk, tk)
        k = k_ref[pl.ds(k0, tk), :]
        for c in range(n_bands):
            dst[c] = lax.dot_general(qq[c], k, (((1,), (1,)), ((), ())), preferred_element_type=F32)

    def softmax_pv(c, s, vt, mask):
        rows = slice(c * 2 * band, (c + 1) * 2 * band)
        if mask is not None:
            s = jnp.where(mask, s, NEG_BIG)
        m_old = m_s[rows, :]
        m_new = jnp.maximum(m_old, jnp.max(s, axis=-1, keepdims=True))
        alpha = jnp.exp2(m_old - m_new)
        p = jnp.exp2(s - m_new[:, 0:1]).astype(vaug_s.dtype)
        pv = jnp.dot(p, vt, preferred_element_type=F32)
        acc_s[rows, :] = jnp.concatenate([alpha, alpha], axis=1) * acc_s[rows, :] + pv
        m_s[rows, :] = m_new

    def step(src, dst, kt):
        scores(dst, kt + 1)
        k0 = pl.multiple_of(kt * tk, tk)
        vt = vaug_s[pl.ds(k0, tk), :]
        for c in range(n_bands):
            softmax_pv(c, src[c], vt, None)

    def diagonal(src):
        k0 = pl.multiple_of(qi * tk, tk)
        for c in range(n_bands):
            nk = (c + 1) * band
            r_chunk = (c * band + lax.broadcasted_iota(jnp.int32, (band, nk), 0)) // CHUNK
            k_chunk = lax.broadcasted_iota(jnp.int32, (band, nk), 1) // CHUNK
            ok = k_chunk <= r_chunk
            softmax_pv(c, src[c, :, :nk], vaug_s[pl.ds(k0, nk), :], jnp.concatenate([ok, ok], axis=0))

    scores(sa_s, 0)

    def pair(i, carry):
        step(sa_s, sb_s, 2 * i)
        step(sb_s, sa_s, 2 * i + 1)
        return carry

    lax.fori_loop(0, lax.shift_right_logical(qi, 1), pair, 0)
    odd = (qi & 1) == 1

    @pl.when(odd)
    def _():
        step(sa_s, sb_s, qi - 1)
        diagonal(sb_s)

    @pl.when(jnp.logical_not(odd))
    def _():
        diagonal(sa_s)

    lam = (jnp.exp(jnp.sum(lq1_ref[...] * lk1_ref[...], axis=-1, keepdims=True))
           - jnp.exp(jnp.sum(lq2_ref[...] * lk2_ref[...], axis=-1, keepdims=True)) + lambda_init)
    scale = subw_ref[...] * (1.0 - lambda_init)
    for c in range(n_bands):
        acc = acc_s[c * 2 * band:(c + 1) * 2 * band, :]
        o_all = acc[:, :DA_VDIM] / acc[:, DA_VDIM:]
        o = o_all[:band] - lam * o_all[band:]
        ms = jnp.mean(o * o, axis=-1, keepdims=True)
        o = o * lax.rsqrt(ms + EPS) * scale
        rows = slice(c * band, (c + 1) * band)
        o_ref[rows, :] = (o * gb_ref[rows, :].astype(F32)).astype(o_ref.dtype)


def _attention(attn_act, gate_act, lq1, lk1, lq2, lk2, subw, batch, seq, lambda_init):
    tq = ATT_TQ
    assert seq % ATT_TK == 0 and ATT_TK == tq and tq % ATT_BAND == 0 and ATT_BAND % CHUNK == 0
    q_tiles = seq // tq

    def head_spec(rows_blk, off):
        assert off % DA_VDIM == 0
        if rows_blk == seq:
            return pl.BlockSpec((seq, DA_VDIM), lambda b, hd, qi: (b, off // DA_VDIM + hd))
        return pl.BlockSpec((rows_blk, DA_VDIM), lambda b, hd, qi: (b * q_tiles + qi, off // DA_VDIM + hd))

    vec = lambda n: pl.BlockSpec((1, n), lambda b, hd, qi: (0, 0))
    kern = functools.partial(_attn_kernel, lambda_init=lambda_init)
    return pl.pallas_call(
        kern,
        grid=(batch, DA_HEADS, q_tiles),
        in_specs=[
            head_spec(tq, AT_Q),
            head_spec(seq, AT_K),
            head_spec(seq, AT_V),
            head_spec(tq, GA_GB),
            vec(DA_HEAD_DIM), vec(DA_HEAD_DIM), vec(DA_HEAD_DIM), vec(DA_HEAD_DIM),
            vec(DA_VDIM),
        ],
        out_specs=pl.BlockSpec((tq, DA_VDIM), lambda b, hd, qi: (b * q_tiles + qi, hd)),
        out_shape=jax.ShapeDtypeStruct((batch * seq, DA_WIDTH), BF16),
        scratch_shapes=[
            pltpu.VMEM((seq, 2 * DA_VDIM), BF16),
            pltpu.VMEM((2 * tq, LANES), F32),
            pltpu.VMEM((2 * tq, 2 * DA_VDIM), F32),
            pltpu.VMEM((tq // ATT_BAND, 2 * ATT_BAND, ATT_TK), F32),
            pltpu.VMEM((tq // ATT_BAND, 2 * ATT_BAND, ATT_TK), F32),
        ],
        compiler_params=pltpu.CompilerParams(
            dimension_semantics=("arbitrary", "arbitrary", "arbitrary"),
            vmem_limit_bytes=VMEM_LIMIT_BYTES),
        name="diffattn",
    )(attn_act, attn_act, attn_act, gate_act, lq1, lk1, lq2, lk2, subw)


def _out_kernel(x_ref, ya_ref, ob_ref, g0_ref, g1_ref, wa_ref, wb_ref, wo_ref, lng_ref, lnb_ref,
                out_ref, *, alpha):
    d = functools.partial(jnp.dot, preferred_element_type=F32)
    branch_a = d(ya_ref[...], wa_ref[...])
    branch_b = d(ob_ref[...], wb_ref[...])
    merged = g0_ref[...].astype(F32) * branch_a + g1_ref[...].astype(F32) * branch_b
    y = d(merged.astype(BF16), wo_ref[...])
    r = alpha * x_ref[...] + y
    mu = jnp.mean(r, axis=-1, keepdims=True)
    rc = r - mu
    var = jnp.mean(rc * rc, axis=-1, keepdims=True)
    out_ref[...] = rc * lax.rsqrt(var + EPS) * lng_ref[...] + lnb_ref[...]


def _output_stage(x2d, y_ssd, o_att, gate_act, w_a, w_b, w_o, ln_g, ln_b, alpha):
    t_rows = x2d.shape[0]
    tm = OUT_TM
    assert t_rows % tm == 0 and GA_GM % D_MODEL == 0
    row = lambda width: pl.BlockSpec((tm, width), lambda i: (i, 0))
    full = lambda shape: pl.BlockSpec(shape, lambda i: (0, 0))
    gate_spec = lambda k: pl.BlockSpec((tm, D_MODEL), lambda i: (i, GA_GM // D_MODEL + k))
    return pl.pallas_call(
        functools.partial(_out_kernel, alpha=alpha),
        grid=(t_rows // tm,),
        in_specs=[row(D_MODEL), row(SSD_WIDTH), row(DA_WIDTH), gate_spec(0), gate_spec(1),
                  full((SSD_WIDTH, D_MODEL)), full((DA_WIDTH, D_MODEL)), full((D_MODEL, D_MODEL)),
                  full((1, D_MODEL)), full((1, D_MODEL))],
        out_specs=row(D_MODEL),
        out_shape=jax.ShapeDtypeStruct((t_rows, D_MODEL), F32),
        compiler_params=pltpu.CompilerParams(
            dimension_semantics=("arbitrary",),
            vmem_limit_bytes=VMEM_LIMIT_BYTES),
        name="outproj",
    )(x2d, y_ssd, o_att, gate_act, gate_act, w_a, w_b, w_o, ln_g, ln_b)


def _rope_tables(seq):
    pos = jnp.arange(seq, dtype=F32)
    inv_freq = ROPE_THETA ** (-jnp.arange(0, ROT_DIM, 2, dtype=F32) / ROT_DIM)
    ang = pos[:, None] * inv_freq[None, :]
    cos, sin = jnp.cos(ang), jnp.sin(ang)
    half = jnp.arange(LANES) % (LANES // 2)
    f = half % ROT_HALF
    rotary = half < ROT_DIM
    first_half = jnp.arange(LANES) < LANES // 2
    cos_t = jnp.where(rotary[None, :], cos[:, f], 1.0)
    sin_t = jnp.where(rotary[None, :], jnp.where(first_half[None, :], -sin[:, f], sin[:, f]), 0.0)
    return cos_t, sin_t


def _permute_head_lanes(w):
    d_in = w.shape[0]
    w4 = w.reshape(d_in, DA_HEADS, 2, DA_HEAD_DIM)
    parts = [w4[..., :ROT_HALF], w4[..., ROT_DIM:ROT_DIM + ROT_PASS],
             w4[..., ROT_HALF:ROT_DIM], w4[..., ROT_DIM + ROT_PASS:]]
    parts = [p.reshape(d_in, DA_HEADS, -1) for p in parts]
    return jnp.concatenate(parts, axis=-1).reshape(d_in, DA_WIDTH)


def _layer(x2d, batch, seq, depth_total, l, w_in, b_gate, conv_w, conv_b, dt_bias, a_log, d_skip,
           ssd_norm_w, lambda_q1, lambda_k1, lambda_q2, lambda_k2, subln_w, w_a, w_b, w_o, ln_g, ln_b,
           tables):
    alpha = (2.0 * depth_total) ** 0.25
    lambda_init = 0.8 - 0.6 * math.exp(-0.3 * l)
    offs = [0]
    for s in IN_SIZES:
        offs.append(offs[-1] + s)
    z_w, xbc_w, dt_w, q_w, k_w, v_w, gb_w, gm_w = [w_in[:, offs[n]:offs[n + 1]] for n in range(len(IN_SIZES))]

    w_gate = jnp.concatenate([z_w, gb_w, gm_w], axis=1).astype(BF16)
    w_attn = jnp.concatenate([_permute_head_lanes(q_w), _permute_head_lanes(k_w), v_w], axis=1).astype(BF16)
    p_conv = jnp.zeros((SUBLANES, CONV_CH), F32).at[:CONV_WIDTH].set(conv_w).at[CONV_WIDTH].set(conv_b)

    gate_act = _proj_gate(x2d, w_gate, b_gate[None, :].astype(F32))
    attn_act = _proj_attn(x2d, w_attn, *tables, seq)
    conv_act, dt = _proj_conv(x2d, xbc_w.astype(BF16), p_conv, dt_w.astype(BF16),
                              dt_bias[None, :].astype(F32), seq)

    a_row = -jnp.exp(a_log.astype(F32))[None, :]
    dskip_row = jnp.repeat(d_skip.astype(F32), SSD_HEAD_DIM)[None, :]
    y_ssd = _ssd(conv_act, gate_act, dt, a_row, dskip_row, ssd_norm_w[None, :].astype(F32), batch, seq)

    o_att = _attention(attn_act, gate_act, lambda_q1[None, :], lambda_k1[None, :], lambda_q2[None, :],
                       lambda_k2[None, :], subln_w[None, :], batch, seq, lambda_init)

    return _output_stage(x2d, y_ssd, o_att, gate_act, w_a.astype(BF16), w_b.astype(BF16), w_o.astype(BF16),
                         ln_g[None, :], ln_b[None, :], alpha)


def kernel(x, w_in, b_gate, conv_w, conv_b, dt_bias, a_log, d_skip, ssd_norm_w, lambda_q1, lambda_k1,
           lambda_q2, lambda_k2, subln_w, w_a, w_b, w_o, ln_g, ln_b):
    batch, seq, _ = x.shape
    depth = w_in.shape[0]
    tables = _rope_tables(seq)
    x2d = x.reshape(batch * seq, D_MODEL)
    for l in range(depth):
        x2d = _layer(x2d, batch, seq, depth, l, w_in[l], b_gate[l], conv_w[l], conv_b[l], dt_bias[l],
                     a_log[l], d_skip[l], ssd_norm_w[l], lambda_q1[l], lambda_k1[l], lambda_q2[l],
                     lambda_k2[l], subln_w[l], w_a[l], w_b[l], w_o[l], ln_g[l], ln_b[l], tables)
    return x2d.reshape(batch, seq, D_MODEL)
```

```python
import functools
import math

import jax
import jax.numpy as jnp
from jax import lax
from jax.experimental import pallas as pl
from jax.experimental.pallas import tpu as pltpu

F32 = jnp.float32
BF16 = jnp.bfloat16

D_MODEL = 1024
CHUNK = 64
SSD_HEADS = 16
SSD_HEAD_DIM = 64
SSD_WIDTH = SSD_HEADS * SSD_HEAD_DIM
SSD_GROUPS = 2
SSD_STATE = 128
CONV_WIDTH = 4
BC_WIDTH = SSD_GROUPS * SSD_STATE
CONV_CH = SSD_WIDTH + 2 * BC_WIDTH
DA_HEADS = 8
DA_HEAD_DIM = 64
DA_VDIM = 2 * DA_HEAD_DIM
DA_WIDTH = DA_HEADS * DA_VDIM
ROPE_THETA = 500000.0
ROT_DIM = DA_HEAD_DIM // 4
ROT_HALF = ROT_DIM // 2
N_BRANCH = 2
EPS = 1e-5
IN_SIZES = (SSD_WIDTH, CONV_CH, SSD_HEADS, DA_WIDTH, DA_WIDTH, DA_WIDTH, DA_WIDTH, N_BRANCH * D_MODEL)

GA_SZ, GA_GB, GA_GM = 0, SSD_WIDTH, SSD_WIDTH + DA_WIDTH
GA_COLS = GA_GM + N_BRANCH * D_MODEL
AT_Q, AT_K, AT_V = 0, DA_WIDTH, 2 * DA_WIDTH
AT_COLS = 3 * DA_WIDTH
CV_XS, CV_B, CV_C = 0, SSD_WIDTH, SSD_WIDTH + BC_WIDTH

ROT_PASS = (DA_HEAD_DIM - ROT_DIM) // 2

LANES = 128
SUBLANES = 8
BF16_ROWS = 16
VMEM_LIMIT_BYTES = 56 * 1024 * 1024

IN_TM = 512
SSD_ROWS = 512
ATT_TQ = 1024
ATT_TK = 1024
ATT_BAND = 256
OUT_TM = 512

NEG_BIG = -1e30
LOG2E = 1.4426950408889634


def _sigmoid(t):
    return 1.0 / (1.0 + jnp.exp2(t * (-LOG2E)))


def _split2(v):
    hi = v.astype(BF16)
    mid = (v - hi.astype(F32)).astype(BF16)
    return hi, mid


def _dot01_right(v, m01):
    hi, mid = _split2(v)
    d = functools.partial(jnp.dot, preferred_element_type=F32)
    return d(hi, m01) + d(mid, m01)


def _dot01_left(m01, v):
    hi, mid = _split2(v)
    d = functools.partial(jnp.dot, preferred_element_type=F32)
    return d(m01, hi) + d(m01, mid)


def _proj_params():
    return pltpu.CompilerParams(dimension_semantics=("arbitrary",),
                                vmem_limit_bytes=VMEM_LIMIT_BYTES)


def _proj_gate_kernel(x_ref, w_ref, bias_ref, o_ref):
    acc = jnp.dot(x_ref[...].astype(BF16), w_ref[...], preferred_element_type=F32)
    t = acc[:, :GA_GM]
    o_ref[:, :GA_GM] = (t * _sigmoid(t)).astype(o_ref.dtype)
    o_ref[:, GA_GM:] = _sigmoid(acc[:, GA_GM:] + bias_ref[...]).astype(o_ref.dtype)


def _proj_gate(x2d, w, b_gate_row):
    t_rows = x2d.shape[0]
    tm = IN_TM
    assert t_rows % tm == 0
    full = lambda shape: pl.BlockSpec(shape, lambda i: (0, 0))
    return pl.pallas_call(
        _proj_gate_kernel,
        grid=(t_rows // tm,),
        in_specs=[
            pl.BlockSpec((tm, D_MODEL), lambda i: (i, 0)),
            full((D_MODEL, GA_COLS)),
            full((1, N_BRANCH * D_MODEL)),
        ],
        out_specs=pl.BlockSpec((tm, GA_COLS), lambda i: (i, 0)),
        out_shape=jax.ShapeDtypeStruct((t_rows, GA_COLS), BF16),
        compiler_params=_proj_params(),
        name="proj_gate",
    )(x2d, w, b_gate_row)


def _proj_attn_kernel(x_ref, w_ref, cos_ref, sin_ref, o_ref, *, q_scale):
    acc = jnp.dot(x_ref[...].astype(BF16), w_ref[...], preferred_element_type=F32)
    cos = cos_ref[...]
    sin = sin_ref[...]
    cos_q = cos * q_scale
    sin_q = sin * q_scale
    for hd in range(AT_V // LANES):
        sl = slice(hd * LANES, (hd + 1) * LANES)
        a = acc[:, sl]
        partner = pltpu.roll(a, LANES // 2, axis=1)
        c, s = (cos_q, sin_q) if hd < AT_K // LANES else (cos, sin)
        o_ref[:, sl] = (a * c + partner * s).astype(o_ref.dtype)
    o_ref[:, AT_V:] = acc[:, AT_V:].astype(o_ref.dtype)


def _proj_attn(x2d, w, cos_t, sin_t, seq):
    t_rows = x2d.shape[0]
    tm = IN_TM
    assert t_rows % tm == 0 and seq % tm == 0
    tiles_per_seq = seq // tm
    table = lambda: pl.BlockSpec((tm, LANES), lambda i: (i % tiles_per_seq, 0))
    return pl.pallas_call(
        functools.partial(_proj_attn_kernel, q_scale=DA_HEAD_DIM ** -0.5 * LOG2E),
        grid=(t_rows // tm,),
        in_specs=[
            pl.BlockSpec((tm, D_MODEL), lambda i: (i, 0)),
            pl.BlockSpec((D_MODEL, AT_COLS), lambda i: (0, 0)),
            table(), table(),
        ],
        out_specs=pl.BlockSpec((tm, AT_COLS), lambda i: (i, 0)),
        out_shape=jax.ShapeDtypeStruct((t_rows, AT_COLS), BF16),
        compiler_params=_proj_params(),
        name="proj_attn",
    )(x2d, w, cos_t, sin_t)


def _proj_conv_kernel(x_ref, xh_ref, w_ref, p_ref, wdt_ref, dtb_ref, o_ref, dt_ref, *, tiles_per_seq):
    i = pl.program_id(0)
    tm = o_ref.shape[0]
    xb = x_ref[...].astype(BF16)
    w = w_ref[...]
    acc = jnp.dot(xb, w, preferred_element_type=F32)
    halo = jnp.dot(xh_ref[...].astype(BF16), w, preferred_element_type=F32)
    halo = jnp.where(i % tiles_per_seq == 0, 0.0, halo)

    bias = p_ref[CONV_WIDTH:CONV_WIDTH + 1, :]
    taps = [p_ref[k:k + 1, :] for k in range(CONV_WIDTH)]

    def conv(rows_val):
        out = taps[0] * rows_val
        for k in range(1, CONV_WIDTH):
            out = pltpu.roll(out, 1, axis=0) + taps[k] * rows_val
        out = out + bias
        return out * _sigmoid(out)

    head = BF16_ROWS
    ext = jnp.concatenate([halo, acc[:head]], axis=0)
    o_ref[:head, :] = conv(ext)[SUBLANES:, :].astype(o_ref.dtype)
    o_ref[head:, :] = conv(acc)[head:, :].astype(o_ref.dtype)

    raw = jnp.dot(xb, wdt_ref[...], preferred_element_type=F32) + dtb_ref[...]
    dt_ref[...] = jnp.maximum(raw, 0.0) + jnp.log(1.0 + jnp.exp(-jnp.abs(raw)))


def _proj_conv(x2d, w, params, w_dt, dt_bias, seq):
    t_rows = x2d.shape[0]
    tm = IN_TM
    assert t_rows % tm == 0 and seq % tm == 0
    tiles_per_seq = seq // tm
    halo_blocks = tm // SUBLANES
    full = lambda shape: pl.BlockSpec(shape, lambda i: (0, 0))
    return pl.pallas_call(
        functools.partial(_proj_conv_kernel, tiles_per_seq=tiles_per_seq),
        grid=(t_rows // tm,),
        in_specs=[
            pl.BlockSpec((tm, D_MODEL), lambda i: (i, 0)),
            pl.BlockSpec((SUBLANES, D_MODEL), lambda i: (jnp.maximum(i * halo_blocks - 1, 0), 0)),
            full((D_MODEL, CONV_CH)),
            full((SUBLANES, CONV_CH)),
            full((D_MODEL, SSD_HEADS)),
            full((1, SSD_HEADS)),
        ],
        out_specs=[
            pl.BlockSpec((tm, CONV_CH), lambda i: (i, 0)),
            pl.BlockSpec((tm, SSD_HEADS), lambda i: (i, 0)),
        ],
        out_shape=[
            jax.ShapeDtypeStruct((t_rows, CONV_CH), BF16),
            jax.ShapeDtypeStruct((t_rows, SSD_HEADS), F32),
        ],
        compiler_params=pltpu.CompilerParams(dimension_semantics=("arbitrary",),
                                             vmem_limit_bytes=VMEM_LIMIT_BYTES),
        name="proj_conv",
    )(x2d, x2d, w, params, w_dt, dt_bias)


def _ssd_kernel(xs_ref, b_ref, c_ref, sz_ref, dt_ref, a_ref, dskip_ref, normw_ref,
                expand_ref, bdtril_ref, tile8_ref,
                y_ref, acol_s, dtx_s, ht_s):
    rows = xs_ref.shape[0]
    n_chunks = rows // CHUNK
    gw = SSD_WIDTH // SSD_GROUPS

    @pl.when(pl.program_id(1) == 0)
    def _():
        ht_s[...] = jnp.zeros_like(ht_s)

    dt = dt_ref[...]
    a_cs = _dot01_left(bdtril_ref[...], dt * a_ref[...])
    expand = expand_ref[...]
    acol_s[...] = _dot01_right(a_cs, expand)
    dtx_s[...] = _dot01_right(dt, expand)

    lane = lax.broadcasted_iota(jnp.int32, (CHUNK, SSD_WIDTH), 1)
    row = lax.broadcasted_iota(jnp.int32, (CHUNK, SSD_WIDTH), 0)
    s_of_lane = lane & (CHUNK - 1)
    diag_mask = s_of_lane == row
    causal_mask = s_of_lane <= row
    bd_r = lax.broadcasted_iota(jnp.int32, (4 * CHUNK, 4 * SSD_HEAD_DIM), 0) // CHUNK
    bd_c = lax.broadcasted_iota(jnp.int32, (4 * CHUNK, 4 * SSD_HEAD_DIM), 1) // SSD_HEAD_DIM
    bd_mask = bd_r == bd_c
    tile8 = tile8_ref[...]
    dskip = dskip_ref[...]
    normw = normw_ref[...]

    def chunk_body(c, carry):
        r0 = pl.multiple_of(c * CHUNK, CHUNK)
        acol = acol_s[pl.ds(r0, CHUNK), :]
        dtx = dtx_s[pl.ds(r0, CHUNK), :]
        xs = xs_ref[pl.ds(r0, CHUNK), :].astype(F32)
        bc = b_ref[pl.ds(r0, CHUNK), :]
        cc = c_ref[pl.ds(r0, CHUNK), :]
        a_last = acol[CHUNK - 1:CHUNK, :]

        arow = jnp.sum(jnp.where(diag_mask, acol, 0.0), axis=0, keepdims=True)
        decay_ls = jnp.exp(jnp.where(causal_mask, acol - arow, NEG_BIG))

        cb = []
        for g in range(SSD_GROUPS):
            cg = cc[:, g * SSD_STATE:(g + 1) * SSD_STATE]
            bg = bc[:, g * SSD_STATE:(g + 1) * SSD_STATE]
            cb_g = lax.dot_general(cg, bg, (((1,), (1,)), ((), ())), preferred_element_type=F32)
            cb.append(_dot01_right(cb_g, tile8))
        w_ls = (jnp.concatenate(cb, axis=1) * decay_ls).astype(BF16)

        xdt = (xs * dtx).astype(BF16)
        y_parts = []
        for qd in range(SSD_WIDTH // (4 * SSD_HEAD_DIM)):
            sl = slice(qd * 4 * SSD_HEAD_DIM, (qd + 1) * 4 * SSD_HEAD_DIM)
            xq = xdt[:, sl]
            bd = jnp.where(bd_mask, jnp.concatenate([xq] * 4, axis=0), jnp.zeros((), BF16))
            y_parts.append(jnp.dot(w_ls[:, sl], bd, preferred_element_type=F32))
        y_diag = jnp.concatenate(y_parts, axis=1)

        ht = ht_s[...]
        ht_b = ht.astype(BF16)
        y_off = jnp.concatenate(
            [jnp.dot(cc[:, g * SSD_STATE:(g + 1) * SSD_STATE], ht_b[:, g * gw:(g + 1) * gw],
                     preferred_element_type=F32) for g in range(SSD_GROUPS)], axis=1)
        y = y_diag + y_off * jnp.exp(acol) + dskip * xs

        gated = y * sz_ref[pl.ds(r0, CHUNK), :].astype(F32)
        normed = []
        for g in range(SSD_GROUPS):
            gg = gated[:, g * gw:(g + 1) * gw]
            ms = jnp.mean(gg * gg, axis=-1, keepdims=True)
            normed.append(gg * lax.rsqrt(ms + EPS))
        y_ref[pl.ds(r0, CHUNK), :] = (jnp.concatenate(normed, axis=1) * normw).astype(y_ref.dtype)

        xd = (xs * (jnp.exp(a_last - acol) * dtx)).astype(BF16)
        st = jnp.concatenate(
            [lax.dot_general(bc[:, g * SSD_STATE:(g + 1) * SSD_STATE], xd[:, g * gw:(g + 1) * gw],
                             (((0,), (0,)), ((), ())), preferred_element_type=F32)
             for g in range(SSD_GROUPS)], axis=1)
        ht_s[...] = ht * jnp.exp(a_last) + st
        return carry

    lax.fori_loop(0, n_chunks, chunk_body, 0, unroll=2)


def _ssd(conv_act, gate_act, dt, a_row, dskip_row, normw_row, batch, seq):
    rows = SSD_ROWS
    assert seq % rows == 0 and rows % (2 * CHUNK) == 0
    blocks_per_seq = seq // rows
    hh = jnp.arange(SSD_WIDTH) // SSD_HEAD_DIM
    expand = (jnp.arange(SSD_HEADS)[:, None] == hh[None, :]).astype(BF16)
    rr = jnp.arange(rows)
    same_chunk = (rr[:, None] // CHUNK) == (rr[None, :] // CHUNK)
    bdtril = (same_chunk & (rr[None, :] <= rr[:, None])).astype(BF16)
    gw = SSD_WIDTH // SSD_GROUPS
    tile8 = (jnp.arange(CHUNK)[:, None] == (jnp.arange(gw)[None, :] % CHUNK)).astype(BF16)

    def row_map(b, r):
        return b * blocks_per_seq + r

    def col_spec(width, off):
        assert off % width == 0
        return pl.BlockSpec((rows, width), lambda b, r: (row_map(b, r), off // width))

    const = lambda shape: pl.BlockSpec(shape, lambda b, r: (0, 0))
    return pl.pallas_call(
        _ssd_kernel,
        grid=(batch, blocks_per_seq),
        in_specs=[
            col_spec(SSD_WIDTH, CV_XS),
            col_spec(BC_WIDTH, CV_B),
            col_spec(BC_WIDTH, CV_C),
            col_spec(SSD_WIDTH, GA_SZ),
            pl.BlockSpec((rows, SSD_HEADS), lambda b, r: (row_map(b, r), 0)),
            const((1, SSD_HEADS)),
            const((1, SSD_WIDTH)),
            const((1, SSD_WIDTH)),
            const((SSD_HEADS, SSD_WIDTH)),
            const((rows, rows)),
            const((CHUNK, gw)),
        ],
        out_specs=pl.BlockSpec((rows, SSD_WIDTH), lambda b, r: (row_map(b, r), 0)),
        out_shape=jax.ShapeDtypeStruct((batch * seq, SSD_WIDTH), BF16),
        scratch_shapes=[
            pltpu.VMEM((rows, SSD_WIDTH), F32),
            pltpu.VMEM((rows, SSD_WIDTH), F32),
            pltpu.VMEM((SSD_STATE, SSD_WIDTH), F32),
        ],
        compiler_params=pltpu.CompilerParams(
            dimension_semantics=("arbitrary", "arbitrary"),
            vmem_limit_bytes=VMEM_LIMIT_BYTES),
        name="ssd",
    )(conv_act, conv_act, conv_act, gate_act, dt, a_row, dskip_row, normw_row, expand, bdtril, tile8)


def _attn_kernel(q_ref, k_ref, v_ref, gb_ref, lq1_ref, lk1_ref, lq2_ref, lk2_ref, subw_ref,
                 o_ref, vaug_s, m_s, acc_s, sa_s, sb_s, *, lambda_init):
    qi = pl.program_id(2)
    tq = o_ref.shape[0]
    tk = ATT_TK
    seq = k_ref.shape[0]
    q_tiles = seq // tq

    @pl.when(qi == 0)
    def _():
        vaug_s[:, :DA_VDIM] = v_ref[...].astype(vaug_s.dtype)
        vaug_s[:, DA_VDIM:] = jnp.ones((seq, DA_VDIM), vaug_s.dtype)

    assert tq == tk
    band = ATT_BAND
    n_bands = tq // band
    half_lane = lax.broadcasted_iota(jnp.int32, (band, DA_VDIM), 1) & (LANES // 2 - 1)
    in_map0 = (half_lane < ROT_HALF) | ((half_lane >= ROT_DIM) & (half_lane < ROT_DIM + ROT_PASS))
    zero = jnp.zeros((), q_ref.dtype)

    def stacked_q(tile):
        out = []
        for c in range(n_bands):
            r0 = pl.multiple_of(tile * tq + c * band, band)
            q = q_ref[pl.ds(r0, band), :]
            out.append(jnp.concatenate([jnp.where(in_map0, q, zero), jnp.where(in_map0, zero, q)], axis=0))
        return out

    qq = stacked_q(qi)
    m_s[...] = jnp.full_like(m_s, NEG_BIG)
    acc_s[...] = jnp.zeros_like(acc_s)

    def scores(dst, kt, q_bands=None):
        q_bands = qq if q_bands is None else q_bands
        k0 = pl.multiple_of(kt * tk, tk)
        k = k_ref[pl.ds(k0, tk), :]
        for c in range(n_bands):
            dst[c] = lax.dot_general(q_bands[c], k, (((1,), (1,)), ((), ())), preferred_element_type=F32)

    def softmax_pv(c, s, vt, mask):
        rows = slice(c * 2 * band, (c + 1) * 2 * band)
        if mask is not None:
            s = jnp.where(mask, s, NEG_BIG)
        m_old = m_s[rows, :]
        m_new = jnp.maximum(m_old, jnp.max(s, axis=-1, keepdims=True))
        alpha = jnp.exp2(m_old - m_new)
        p = jnp.exp2(s - m_new[:, 0:1]).astype(vaug_s.dtype)
        pv = jnp.dot(p, vt, preferred_element_type=F32)
        acc_s[rows, :] = jnp.concatenate([alpha, alpha], axis=1) * acc_s[rows, :] + pv
        m_s[rows, :] = m_new

    def step(src, dst, kt):
        scores(dst, kt + 1)
        k0 = pl.multiple_of(kt * tk, tk)
        vt = vaug_s[pl.ds(k0, tk), :]
        for c in range(n_bands):
            softmax_pv(c, src[c], vt, None)

    def diagonal(src):
        k0 = pl.multiple_of(qi * tk, tk)
        for c in range(n_bands):
            nk = (c + 1) * band
            r_chunk = (c * band + lax.broadcasted_iota(jnp.int32, (band, nk), 0)) // CHUNK
            k_chunk = lax.broadcasted_iota(jnp.int32, (band, nk), 1) // CHUNK
            ok = k_chunk <= r_chunk
            softmax_pv(c, src[c, :, :nk], vaug_s[pl.ds(k0, nk), :], jnp.concatenate([ok, ok], axis=0))

    @pl.when(qi == 0)
    def _():
        scores(sa_s, 0)

    def pair(i, carry):
        step(sa_s, sb_s, 2 * i)
        step(sb_s, sa_s, 2 * i + 1)
        return carry

    lax.fori_loop(0, lax.shift_right_logical(qi, 1), pair, 0)
    odd = (qi & 1) == 1
    next_tile = jnp.minimum(qi + 1, q_tiles - 1)

    @pl.when(odd)
    def _():
        step(sa_s, sb_s, qi - 1)
        scores(sa_s, 0, stacked_q(next_tile))
        diagonal(sb_s)

    @pl.when(jnp.logical_not(odd))
    def _():
        diagonal(sa_s)
        scores(sa_s, 0, stacked_q(next_tile))

    lam = (jnp.exp(jnp.sum(lq1_ref[...] * lk1_ref[...], axis=-1, keepdims=True))
           - jnp.exp(jnp.sum(lq2_ref[...] * lk2_ref[...], axis=-1, keepdims=True)) + lambda_init)
    scale = subw_ref[...] * (1.0 - lambda_init)
    for c in range(n_bands):
        acc = acc_s[c * 2 * band:(c + 1) * 2 * band, :]
        o_all = acc[:, :DA_VDIM] / acc[:, DA_VDIM:]
        o = o_all[:band] - lam * o_all[band:]
        ms = jnp.mean(o * o, axis=-1, keepdims=True)
        o = o * lax.rsqrt(ms + EPS) * scale
        rows = slice(c * band, (c + 1) * band)
        o_ref[rows, :] = (o * gb_ref[rows, :].astype(F32)).astype(o_ref.dtype)


def _attention(attn_act, gate_act, lq1, lk1, lq2, lk2, subw, batch, seq, lambda_init):
    tq = ATT_TQ
    assert seq % ATT_TK == 0 and ATT_TK == tq and tq % ATT_BAND == 0 and ATT_BAND % CHUNK == 0
    q_tiles = seq // tq

    def head_spec(rows_blk, off):
        assert off % DA_VDIM == 0
        if rows_blk == seq:
            return pl.BlockSpec((seq, DA_VDIM), lambda b, hd, qi: (b, off // DA_VDIM + hd))
        return pl.BlockSpec((rows_blk, DA_VDIM), lambda b, hd, qi: (b * q_tiles + qi, off // DA_VDIM + hd))

    vec = lambda n: pl.BlockSpec((1, n), lambda b, hd, qi: (0, 0))
    kern = functools.partial(_attn_kernel, lambda_init=lambda_init)
    return pl.pallas_call(
        kern,
        grid=(batch, DA_HEADS, q_tiles),
        in_specs=[
            head_spec(seq, AT_Q),
            head_spec(seq, AT_K),
            head_spec(seq, AT_V),
            head_spec(tq, GA_GB),
            vec(DA_HEAD_DIM), vec(DA_HEAD_DIM), vec(DA_HEAD_DIM), vec(DA_HEAD_DIM),
            vec(DA_VDIM),
        ],
        out_specs=pl.BlockSpec((tq, DA_VDIM), lambda b, hd, qi: (b * q_tiles + qi, hd)),
        out_shape=jax.ShapeDtypeStruct((batch * seq, DA_WIDTH), BF16),
        scratch_shapes=[
            pltpu.VMEM((seq, 2 * DA_VDIM), BF16),
            pltpu.VMEM((2 * tq, LANES), F32),
            pltpu.VMEM((2 * tq, 2 * DA_VDIM), F32),
            pltpu.VMEM((tq // ATT_BAND, 2 * ATT_BAND, ATT_TK), F32),
            pltpu.VMEM((tq // ATT_BAND, 2 * ATT_BAND, ATT_TK), F32),
        ],
        compiler_params=pltpu.CompilerParams(
            dimension_semantics=("arbitrary", "arbitrary", "arbitrary"),
            vmem_limit_bytes=VMEM_LIMIT_BYTES),
        name="diffattn",
    )(attn_act, attn_act, attn_act, gate_act, lq1, lk1, lq2, lk2, subw)


def _out_kernel(x_ref, ya_ref, ob_ref, g0_ref, g1_ref, wa_ref, wb_ref, wo_ref, lng_ref, lnb_ref,
                out_ref, *, alpha):
    d = functools.partial(jnp.dot, preferred_element_type=F32)
    branch_a = d(ya_ref[...], wa_ref[...])
    branch_b = d(ob_ref[...], wb_ref[...])
    merged = g0_ref[...].astype(F32) * branch_a + g1_ref[...].astype(F32) * branch_b
    y = d(merged.astype(BF16), wo_ref[...])
    r = alpha * x_ref[...] + y
    mu = jnp.mean(r, axis=-1, keepdims=True)
    rc = r - mu
    var = jnp.mean(rc * rc, axis=-1, keepdims=True)
    out_ref[...] = rc * lax.rsqrt(var + EPS) * lng_ref[...] + lnb_ref[...]


def _output_stage(x2d, y_ssd, o_att, gate_act, w_a, w_b, w_o, ln_g, ln_b, alpha):
    t_rows = x2d.shape[0]
    tm = OUT_TM
    assert t_rows % tm == 0 and GA_GM % D_MODEL == 0
    row = lambda width: pl.BlockSpec((tm, width), lambda i: (i, 0))
    full = lambda shape: pl.BlockSpec(shape, lambda i: (0, 0))
    gate_spec = lambda k: pl.BlockSpec((tm, D_MODEL), lambda i: (i, GA_GM // D_MODEL + k))
    return pl.pallas_call(
        functools.partial(_out_kernel, alpha=alpha),
        grid=(t_rows // tm,),
        in_specs=[row(D_MODEL), row(SSD_WIDTH), row(DA_WIDTH), gate_spec(0), gate_spec(1),
                  full((SSD_WIDTH, D_MODEL)), full((DA_WIDTH, D_MODEL)), full((D_MODEL, D_MODEL)),
                  full((1, D_MODEL)), full((1, D_MODEL))],
        out_specs=row(D_MODEL),
        out_shape=jax.ShapeDtypeStruct((t_rows, D_MODEL), F32),
        compiler_params=pltpu.CompilerParams(
            dimension_semantics=("arbitrary",),
            vmem_limit_bytes=VMEM_LIMIT_BYTES),
        name="outproj",
    )(x2d, y_ssd, o_att, gate_act, gate_act, w_a, w_b, w_o, ln_g, ln_b)


def _rope_tables(seq):
    pos = jnp.arange(seq, dtype=F32)
    inv_freq = ROPE_THETA ** (-jnp.arange(0, ROT_DIM, 2, dtype=F32) / ROT_DIM)
    ang = pos[:, None] * inv_freq[None, :]
    cos, sin = jnp.cos(ang), jnp.sin(ang)
    half = jnp.arange(LANES) % (LANES // 2)
    f = half % ROT_HALF
    rotary = half < ROT_DIM
    first_half = jnp.arange(LANES) < LANES // 2
    cos_t = jnp.where(rotary[None, :], cos[:, f], 1.0)
    sin_t = jnp.where(rotary[None, :], jnp.where(first_half[None, :], -sin[:, f], sin[:, f]), 0.0)
    return cos_t, sin_t


def _permute_head_lanes(w):
    d_in = w.shape[0]
    w4 = w.reshape(d_in, DA_HEADS, 2, DA_HEAD_DIM)
    parts = [w4[..., :ROT_HALF], w4[..., ROT_DIM:ROT_DIM + ROT_PASS],
             w4[..., ROT_HALF:ROT_DIM], w4[..., ROT_DIM + ROT_PASS:]]
    parts = [p.reshape(d_in, DA_HEADS, -1) for p in parts]
    return jnp.concatenate(parts, axis=-1).reshape(d_in, DA_WIDTH)


def _layer(x2d, batch, seq, depth_total, l, w_in, b_gate, conv_w, conv_b, dt_bias, a_log, d_skip,
           ssd_norm_w, lambda_q1, lambda_k1, lambda_q2, lambda_k2, subln_w, w_a, w_b, w_o, ln_g, ln_b,
           tables):
    alpha = (2.0 * depth_total) ** 0.25
    lambda_init = 0.8 - 0.6 * math.exp(-0.3 * l)
    offs = [0]
    for s in IN_SIZES:
        offs.append(offs[-1] + s)
    z_w, xbc_w, dt_w, q_w, k_w, v_w, gb_w, gm_w = [w_in[:, offs[n]:offs[n + 1]] for n in range(len(IN_SIZES))]

    w_gate = jnp.concatenate([z_w, gb_w, gm_w], axis=1).astype(BF16)
    w_attn = jnp.concatenate([_permute_head_lanes(q_w), _permute_head_lanes(k_w), v_w], axis=1).astype(BF16)
    p_conv = jnp.zeros((SUBLANES, CONV_CH), F32).at[:CONV_WIDTH].set(conv_w).at[CONV_WIDTH].set(conv_b)

    gate_act = _proj_gate(x2d, w_gate, b_gate[None, :].astype(F32))
    attn_act = _proj_attn(x2d, w_attn, *tables, seq)
    conv_act, dt = _proj_conv(x2d, xbc_w.astype(BF16), p_conv, dt_w.astype(BF16),
                              dt_bias[None, :].astype(F32), seq)

    a_row = -jnp.exp(a_log.astype(F32))[None, :]
    dskip_row = jnp.repeat(d_skip.astype(F32), SSD_HEAD_DIM)[None, :]
    y_ssd = _ssd(conv_act, gate_act, dt, a_row, dskip_row, ssd_norm_w[None, :].astype(F32), batch, seq)

    o_att = _attention(attn_act, gate_act, lambda_q1[None, :], lambda_k1[None, :], lambda_q2[None, :],
                       lambda_k2[None, :], subln_w[None, :], batch, seq, lambda_init)

    return _output_stage(x2d, y_ssd, o_att, gate_act, w_a.astype(BF16), w_b.astype(BF16), w_o.astype(BF16),
                         ln_g[None, :], ln_b[None, :], alpha)


def kernel(x, w_in, b_gate, conv_w, conv_b, dt_bias, a_log, d_skip, ssd_norm_w, lambda_q1, lambda_k1,
           lambda_q2, lambda_k2, subln_w, w_a, w_b, w_o, ln_g, ln_b):
    batch, seq, _ = x.shape
    depth = w_in.shape[0]
    tables = _rope_tables(seq)
    x2d = x.reshape(batch * seq, D_MODEL)
    for l in range(depth):
        x2d = _layer(x2d, batch, seq, depth, l, w_in[l], b_gate[l], conv_w[l], conv_b[l], dt_bias[l],
                     a_log[l], d_skip[l], ssd_norm_w[l], lambda_q1[l], lambda_k1[l], lambda_q2[l],
                     lambda_k2[l], subln_w[l], w_a[l], w_b[l], w_o[l], ln_g[l], ln_b[l], tables)
    return x2d.reshape(batch, seq, D_MODEL)
```

```python
import functools
import math

import jax
import jax.numpy as jnp
from jax import lax
from jax.experimental import pallas as pl
from jax.experimental.pallas import tpu as pltpu

F32 = jnp.float32
BF16 = jnp.bfloat16

D_MODEL = 1024
CHUNK = 64
SSD_HEADS = 16
SSD_HEAD_DIM = 64
SSD_WIDTH = SSD_HEADS * SSD_HEAD_DIM
SSD_GROUPS = 2
SSD_STATE = 128
CONV_WIDTH = 4
BC_WIDTH = SSD_GROUPS * SSD_STATE
CONV_CH = SSD_WIDTH + 2 * BC_WIDTH
DA_HEADS = 8
DA_HEAD_DIM = 64
DA_VDIM = 2 * DA_HEAD_DIM
DA_WIDTH = DA_HEADS * DA_VDIM
ROPE_THETA = 500000.0
ROT_DIM = DA_HEAD_DIM // 4
ROT_HALF = ROT_DIM // 2
N_BRANCH = 2
EPS = 1e-5
IN_SIZES = (SSD_WIDTH, CONV_CH, SSD_HEADS, DA_WIDTH, DA_WIDTH, DA_WIDTH, DA_WIDTH, N_BRANCH * D_MODEL)

GA_SZ, GA_GB, GA_GM = 0, SSD_WIDTH, SSD_WIDTH + DA_WIDTH
GA_COLS = GA_GM + N_BRANCH * D_MODEL
AT_Q, AT_K, AT_V = 0, DA_WIDTH, 2 * DA_WIDTH
AT_COLS = 3 * DA_WIDTH
CV_XS, CV_B, CV_C = 0, SSD_WIDTH, SSD_WIDTH + BC_WIDTH

ROT_PASS = (DA_HEAD_DIM - ROT_DIM) // 2

LANES = 128
SUBLANES = 8
BF16_ROWS = 16
VMEM_LIMIT_BYTES = 56 * 1024 * 1024

IN_TM = 512
SSD_ROWS = 512
ATT_TQ = 1024
ATT_TK = 1024
ATT_BAND = 256
ATT_SAFE_LOG2 = 64.0
OUT_TM = 512

NEG_BIG = -1e30
LOG2E = 1.4426950408889634


def _sigmoid(t):
    return 1.0 / (1.0 + jnp.exp2(t * (-LOG2E)))


def _split2(v):
    hi = v.astype(BF16)
    mid = (v - hi.astype(F32)).astype(BF16)
    return hi, mid


def _dot01_right(v, m01):
    hi, mid = _split2(v)
    d = functools.partial(jnp.dot, preferred_element_type=F32)
    return d(hi, m01) + d(mid, m01)


def _dot01_left(m01, v):
    hi, mid = _split2(v)
    d = functools.partial(jnp.dot, preferred_element_type=F32)
    return d(m01, hi) + d(m01, mid)


def _proj_params():
    return pltpu.CompilerParams(dimension_semantics=("arbitrary",),
                                vmem_limit_bytes=VMEM_LIMIT_BYTES)


def _proj_gate_kernel(x_ref, w_ref, bias_ref, o_ref):
    acc = jnp.dot(x_ref[...].astype(BF16), w_ref[...], preferred_element_type=F32)
    t = acc[:, :GA_GM]
    o_ref[:, :GA_GM] = (t * _sigmoid(t)).astype(o_ref.dtype)
    o_ref[:, GA_GM:] = _sigmoid(acc[:, GA_GM:] + bias_ref[...]).astype(o_ref.dtype)


def _proj_gate(x2d, w, b_gate_row):
    t_rows = x2d.shape[0]
    tm = IN_TM
    assert t_rows % tm == 0
    full = lambda shape: pl.BlockSpec(shape, lambda i: (0, 0))
    return pl.pallas_call(
        _proj_gate_kernel,
        grid=(t_rows // tm,),
        in_specs=[
            pl.BlockSpec((tm, D_MODEL), lambda i: (i, 0)),
            full((D_MODEL, GA_COLS)),
            full((1, N_BRANCH * D_MODEL)),
        ],
        out_specs=pl.BlockSpec((tm, GA_COLS), lambda i: (i, 0)),
        out_shape=jax.ShapeDtypeStruct((t_rows, GA_COLS), BF16),
        compiler_params=_proj_params(),
        name="proj_gate",
    )(x2d, w, b_gate_row)


def _proj_attn_kernel(x_ref, w_ref, cos_ref, sin_ref, o_ref, *, q_scale):
    acc = jnp.dot(x_ref[...].astype(BF16), w_ref[...], preferred_element_type=F32)
    cos = cos_ref[...]
    sin = sin_ref[...]
    cos_q = cos * q_scale
    sin_q = sin * q_scale
    for hd in range(AT_V // LANES):
        sl = slice(hd * LANES, (hd + 1) * LANES)
        a = acc[:, sl]
        partner = pltpu.roll(a, LANES // 2, axis=1)
        c, s = (cos_q, sin_q) if hd < AT_K // LANES else (cos, sin)
        o_ref[:, sl] = (a * c + partner * s).astype(o_ref.dtype)
    o_ref[:, AT_V:] = acc[:, AT_V:].astype(o_ref.dtype)


def _proj_attn(x2d, w, cos_t, sin_t, seq):
    t_rows = x2d.shape[0]
    tm = IN_TM
    assert t_rows % tm == 0 and seq % tm == 0
    tiles_per_seq = seq // tm
    table = lambda: pl.BlockSpec((tm, LANES), lambda i: (i % tiles_per_seq, 0))
    return pl.pallas_call(
        functools.partial(_proj_attn_kernel, q_scale=DA_HEAD_DIM ** -0.5 * LOG2E),
        grid=(t_rows // tm,),
        in_specs=[
            pl.BlockSpec((tm, D_MODEL), lambda i: (i, 0)),
            pl.BlockSpec((D_MODEL, AT_COLS), lambda i: (0, 0)),
            table(), table(),
        ],
        out_specs=pl.BlockSpec((tm, AT_COLS), lambda i: (i, 0)),
        out_shape=jax.ShapeDtypeStruct((t_rows, AT_COLS), BF16),
        compiler_params=_proj_params(),
        name="proj_attn",
    )(x2d, w, cos_t, sin_t)


def _proj_conv_kernel(x_ref, xh_ref, w_ref, p_ref, wdt_ref, dtb_ref, o_ref, dt_ref, *, tiles_per_seq):
    i = pl.program_id(0)
    tm = o_ref.shape[0]
    xb = x_ref[...].astype(BF16)
    w = w_ref[...]
    acc = jnp.dot(xb, w, preferred_element_type=F32)
    halo = jnp.dot(xh_ref[...].astype(BF16), w, preferred_element_type=F32)
    halo = jnp.where(i % tiles_per_seq == 0, 0.0, halo)

    bias = p_ref[CONV_WIDTH:CONV_WIDTH + 1, :]
    taps = [p_ref[k:k + 1, :] for k in range(CONV_WIDTH)]

    def conv(rows_val):
        out = taps[0] * rows_val
        for k in range(1, CONV_WIDTH):
            out = pltpu.roll(out, 1, axis=0) + taps[k] * rows_val
        out = out + bias
        return out * _sigmoid(out)

    head = BF16_ROWS
    ext = jnp.concatenate([halo, acc[:head]], axis=0)
    o_ref[:head, :] = conv(ext)[SUBLANES:, :].astype(o_ref.dtype)
    o_ref[head:, :] = conv(acc)[head:, :].astype(o_ref.dtype)

    raw = jnp.dot(xb, wdt_ref[...], preferred_element_type=F32) + dtb_ref[...]
    dt_ref[...] = jnp.maximum(raw, 0.0) + jnp.log(1.0 + jnp.exp(-jnp.abs(raw)))


def _proj_conv(x2d, w, params, w_dt, dt_bias, seq):
    t_rows = x2d.shape[0]
    tm = IN_TM
    assert t_rows % tm == 0 and seq % tm == 0
    tiles_per_seq = seq // tm
    halo_blocks = tm // SUBLANES
    full = lambda shape: pl.BlockSpec(shape, lambda i: (0, 0))
    return pl.pallas_call(
        functools.partial(_proj_conv_kernel, tiles_per_seq=tiles_per_seq),
        grid=(t_rows // tm,),
        in_specs=[
            pl.BlockSpec((tm, D_MODEL), lambda i: (i, 0)),
            pl.BlockSpec((SUBLANES, D_MODEL), lambda i: (jnp.maximum(i * halo_blocks - 1, 0), 0)),
            full((D_MODEL, CONV_CH)),
            full((SUBLANES, CONV_CH)),
            full((D_MODEL, SSD_HEADS)),
            full((1, SSD_HEADS)),
        ],
        out_specs=[
            pl.BlockSpec((tm, CONV_CH), lambda i: (i, 0)),
            pl.BlockSpec((tm, SSD_HEADS), lambda i: (i, 0)),
        ],
        out_shape=[
            jax.ShapeDtypeStruct((t_rows, CONV_CH), BF16),
            jax.ShapeDtypeStruct((t_rows, SSD_HEADS), F32),
        ],
        compiler_params=pltpu.CompilerParams(dimension_semantics=("arbitrary",),
                                             vmem_limit_bytes=VMEM_LIMIT_BYTES),
        name="proj_conv",
    )(x2d, x2d, w, params, w_dt, dt_bias)


def _ssd_kernel(xs_ref, b_ref, c_ref, sz_ref, dt_ref, a_ref, dskip_ref, normw_ref,
                expand_ref, bdtril_ref, tile8_ref,
                y_ref, acol_s, dtx_s, ht_s):
    rows = xs_ref.shape[0]
    n_chunks = rows // CHUNK
    gw = SSD_WIDTH // SSD_GROUPS

    @pl.when(pl.program_id(1) == 0)
    def _():
        ht_s[...] = jnp.zeros_like(ht_s)

    dt = dt_ref[...]
    a_cs = _dot01_left(bdtril_ref[...], dt * a_ref[...])
    expand = expand_ref[...]
    acol_s[...] = _dot01_right(a_cs, expand)
    dtx_s[...] = _dot01_right(dt, expand)

    lane = lax.broadcasted_iota(jnp.int32, (CHUNK, SSD_WIDTH), 1)
    row = lax.broadcasted_iota(jnp.int32, (CHUNK, SSD_WIDTH), 0)
    s_of_lane = lane & (CHUNK - 1)
    diag_mask = s_of_lane == row
    causal_mask = s_of_lane <= row
    bd_r = lax.broadcasted_iota(jnp.int32, (4 * CHUNK, 4 * SSD_HEAD_DIM), 0) // CHUNK
    bd_c = lax.broadcasted_iota(jnp.int32, (4 * CHUNK, 4 * SSD_HEAD_DIM), 1) // SSD_HEAD_DIM
    bd_mask = bd_r == bd_c
    tile8 = tile8_ref[...]
    dskip = dskip_ref[...]
    normw = normw_ref[...]

    def chunk_body(c, carry):
        r0 = pl.multiple_of(c * CHUNK, CHUNK)
        acol = acol_s[pl.ds(r0, CHUNK), :]
        dtx = dtx_s[pl.ds(r0, CHUNK), :]
        xs = xs_ref[pl.ds(r0, CHUNK), :].astype(F32)
        bc = b_ref[pl.ds(r0, CHUNK), :]
        cc = c_ref[pl.ds(r0, CHUNK), :]
        a_last = acol[CHUNK - 1:CHUNK, :]

        arow = jnp.sum(jnp.where(diag_mask, acol, 0.0), axis=0, keepdims=True)
        decay_ls = jnp.exp(jnp.where(causal_mask, acol - arow, NEG_BIG))

        cb = []
        for g in range(SSD_GROUPS):
            cg = cc[:, g * SSD_STATE:(g + 1) * SSD_STATE]
            bg = bc[:, g * SSD_STATE:(g + 1) * SSD_STATE]
            cb_g = lax.dot_general(cg, bg, (((1,), (1,)), ((), ())), preferred_element_type=F32)
            cb.append(_dot01_right(cb_g, tile8))
        w_ls = (jnp.concatenate(cb, axis=1) * decay_ls).astype(BF16)

        xdt = (xs * dtx).astype(BF16)
        y_parts = []
        for qd in range(SSD_WIDTH // (4 * SSD_HEAD_DIM)):
            sl = slice(qd * 4 * SSD_HEAD_DIM, (qd + 1) * 4 * SSD_HEAD_DIM)
            xq = xdt[:, sl]
            bd = jnp.where(bd_mask, jnp.concatenate([xq] * 4, axis=0), jnp.zeros((), BF16))
            y_parts.append(jnp.dot(w_ls[:, sl], bd, preferred_element_type=F32))
        y_diag = jnp.concatenate(y_parts, axis=1)

        ht = ht_s[...]
        ht_b = ht.astype(BF16)
        y_off = jnp.concatenate(
            [jnp.dot(cc[:, g * SSD_STATE:(g + 1) * SSD_STATE], ht_b[:, g * gw:(g + 1) * gw],
                     preferred_element_type=F32) for g in range(SSD_GROUPS)], axis=1)
        y = y_diag + y_off * jnp.exp(acol) + dskip * xs

        gated = y * sz_ref[pl.ds(r0, CHUNK), :].astype(F32)
        normed = []
        for g in range(SSD_GROUPS):
            gg = gated[:, g * gw:(g + 1) * gw]
            ms = jnp.mean(gg * gg, axis=-1, keepdims=True)
            normed.append(gg * lax.rsqrt(ms + EPS))
        y_ref[pl.ds(r0, CHUNK), :] = (jnp.concatenate(normed, axis=1) * normw).astype(y_ref.dtype)

        xd = (xs * (jnp.exp(a_last - acol) * dtx)).astype(BF16)
        st = jnp.concatenate(
            [lax.dot_general(bc[:, g * SSD_STATE:(g + 1) * SSD_STATE], xd[:, g * gw:(g + 1) * gw],
                             (((0,), (0,)), ((), ())), preferred_element_type=F32)
             for g in range(SSD_GROUPS)], axis=1)
        ht_s[...] = ht * jnp.exp(a_last) + st
        return carry

    lax.fori_loop(0, n_chunks, chunk_body, 0, unroll=2)


def _ssd(conv_act, gate_act, dt, a_row, dskip_row, normw_row, batch, seq):
    rows = SSD_ROWS
    assert seq % rows == 0 and rows % (2 * CHUNK) == 0
    blocks_per_seq = seq // rows
    hh = jnp.arange(SSD_WIDTH) // SSD_HEAD_DIM
    expand = (jnp.arange(SSD_HEADS)[:, None] == hh[None, :]).astype(BF16)
    rr = jnp.arange(rows)
    same_chunk = (rr[:, None] // CHUNK) == (rr[None, :] // CHUNK)
    bdtril = (same_chunk & (rr[None, :] <= rr[:, None])).astype(BF16)
    gw = SSD_WIDTH // SSD_GROUPS
    tile8 = (jnp.arange(CHUNK)[:, None] == (jnp.arange(gw)[None, :] % CHUNK)).astype(BF16)

    def row_map(b, r):
        return b * blocks_per_seq + r

    def col_spec(width, off):
        assert off % width == 0
        return pl.BlockSpec((rows, width), lambda b, r: (row_map(b, r), off // width))

    const = lambda shape: pl.BlockSpec(shape, lambda b, r: (0, 0))
    return pl.pallas_call(
        _ssd_kernel,
        grid=(batch, blocks_per_seq),
        in_specs=[
            col_spec(SSD_WIDTH, CV_XS),
            col_spec(BC_WIDTH, CV_B),
            col_spec(BC_WIDTH, CV_C),
            col_spec(SSD_WIDTH, GA_SZ),
            pl.BlockSpec((rows, SSD_HEADS), lambda b, r: (row_map(b, r), 0)),
            const((1, SSD_HEADS)),
            const((1, SSD_WIDTH)),
            const((1, SSD_WIDTH)),
            const((SSD_HEADS, SSD_WIDTH)),
            const((rows, rows)),
            const((CHUNK, gw)),
        ],
        out_specs=pl.BlockSpec((rows, SSD_WIDTH), lambda b, r: (row_map(b, r), 0)),
        out_shape=jax.ShapeDtypeStruct((batch * seq, SSD_WIDTH), BF16),
        scratch_shapes=[
            pltpu.VMEM((rows, SSD_WIDTH), F32),
            pltpu.VMEM((rows, SSD_WIDTH), F32),
            pltpu.VMEM((SSD_STATE, SSD_WIDTH), F32),
        ],
        compiler_params=pltpu.CompilerParams(
            dimension_semantics=("arbitrary", "arbitrary"),
            vmem_limit_bytes=VMEM_LIMIT_BYTES),
        name="ssd",
    )(conv_act, conv_act, conv_act, gate_act, dt, a_row, dskip_row, normw_row, expand, bdtril, tile8)


def _attn_kernel(q_ref, k_ref, v_ref, gb_ref, lq1_ref, lk1_ref, lq2_ref, lk2_ref, subw_ref,
                 o_ref, vaug_s, m_s, acc_s, knorm_s, *, lambda_init):
    qi = pl.program_id(2)
    tq = q_ref.shape[0]
    tk = ATT_TK
    seq = k_ref.shape[0]

    @pl.when(qi == 0)
    def _():
        vaug_s[:, :DA_VDIM] = v_ref[...].astype(vaug_s.dtype)
        vaug_s[:, DA_VDIM:] = jnp.ones((seq, DA_VDIM), vaug_s.dtype)
        kf = k_ref[...].astype(F32)
        knorm_s[0] = jnp.max(jnp.sum(kf * kf, axis=-1, keepdims=True))

    assert tq == tk
    band = ATT_BAND
    n_bands = tq // band
    half_lane = lax.broadcasted_iota(jnp.int32, (band, DA_VDIM), 1) & (LANES // 2 - 1)
    in_map0 = (half_lane < ROT_HALF) | ((half_lane >= ROT_DIM) & (half_lane < ROT_DIM + ROT_PASS))
    zero = jnp.zeros((), q_ref.dtype)

    qq = []
    for c in range(n_bands):
        q = q_ref[c * band:(c + 1) * band, :]
        qq.append(jnp.concatenate([jnp.where(in_map0, q, zero), jnp.where(in_map0, zero, q)], axis=0))

    acc_s[...] = jnp.zeros_like(acc_s)

    def band_scores(c, k):
        return lax.dot_general(qq[c], k, (((1,), (1,)), ((), ())), preferred_element_type=F32)

    def band_rows(c):
        return slice(c * 2 * band, (c + 1) * 2 * band)

    def diagonal_operands(c):
        nk = (c + 1) * band
        k0 = pl.multiple_of(qi * tk, tk)
        r_chunk = (c * band + lax.broadcasted_iota(jnp.int32, (band, nk), 0)) // CHUNK
        k_chunk = lax.broadcasted_iota(jnp.int32, (band, nk), 1) // CHUNK
        ok = k_chunk <= r_chunk
        return k_ref[pl.ds(k0, nk), :], vaug_s[pl.ds(k0, nk), :], jnp.concatenate([ok, ok], axis=0)

    def full_operands(kt):
        k0 = pl.multiple_of(kt * tk, tk)
        return k_ref[pl.ds(k0, tk), :], vaug_s[pl.ds(k0, tk), :]

    qf = q_ref[...].astype(F32)
    q_norm2 = jnp.max(jnp.sum(qf * qf, axis=-1, keepdims=True))
    unshifted_ok = q_norm2 * knorm_s[0] <= ATT_SAFE_LOG2 * ATT_SAFE_LOG2

    @pl.when(unshifted_ok)
    def _():
        def tile(kt):
            k, vt = full_operands(kt)
            for c in range(n_bands):
                p = jnp.exp2(band_scores(c, k)).astype(vaug_s.dtype)
                acc_s[band_rows(c), :] += jnp.dot(p, vt, preferred_element_type=F32)

        def pair(i, carry):
            tile(2 * i)
            tile(2 * i + 1)
            return carry

        lax.fori_loop(0, lax.shift_right_logical(qi, 1), pair, 0)

        @pl.when((qi & 1) == 1)
        def _():
            tile(qi - 1)

        for c in range(n_bands):
            k, vt, ok = diagonal_operands(c)
            p = jnp.exp2(jnp.where(ok, band_scores(c, k), NEG_BIG)).astype(vaug_s.dtype)
            acc_s[band_rows(c), :] += jnp.dot(p, vt, preferred_element_type=F32)

    @pl.when(jnp.logical_not(unshifted_ok))
    def _():
        m_s[...] = jnp.full_like(m_s, NEG_BIG)

        def softmax_pv(c, s, vt):
            rows = band_rows(c)
            m_old = m_s[rows, :]
            m_new = jnp.maximum(m_old, jnp.max(s, axis=-1, keepdims=True))
            alpha = jnp.exp2(m_old - m_new)
            p = jnp.exp2(s - m_new[:, 0:1]).astype(vaug_s.dtype)
            pv = jnp.dot(p, vt, preferred_element_type=F32)
            acc_s[rows, :] = jnp.concatenate([alpha, alpha], axis=1) * acc_s[rows, :] + pv
            m_s[rows, :] = m_new

        def tile(kt, carry):
            k, vt = full_operands(kt)
            for c in range(n_bands):
                softmax_pv(c, band_scores(c, k), vt)
            return carry

        lax.fori_loop(0, qi, tile, 0)
        for c in range(n_bands):
            k, vt, ok = diagonal_operands(c)
            softmax_pv(c, jnp.where(ok, band_scores(c, k), NEG_BIG), vt)

    lam = (jnp.exp(jnp.sum(lq1_ref[...] * lk1_ref[...], axis=-1, keepdims=True))
           - jnp.exp(jnp.sum(lq2_ref[...] * lk2_ref[...], axis=-1, keepdims=True)) + lambda_init)
    scale = subw_ref[...] * (1.0 - lambda_init)
    for c in range(n_bands):
        acc = acc_s[c * 2 * band:(c + 1) * 2 * band, :]
        o_all = acc[:, :DA_VDIM] / acc[:, DA_VDIM:]
        o = o_all[:band] - lam * o_all[band:]
        ms = jnp.mean(o * o, axis=-1, keepdims=True)
        o = o * lax.rsqrt(ms + EPS) * scale
        rows = slice(c * band, (c + 1) * band)
        o_ref[rows, :] = (o * gb_ref[rows, :].astype(F32)).astype(o_ref.dtype)


def _attention(attn_act, gate_act, lq1, lk1, lq2, lk2, subw, batch, seq, lambda_init):
    tq = ATT_TQ
    assert seq % ATT_TK == 0 and ATT_TK == tq and tq % ATT_BAND == 0 and ATT_BAND % CHUNK == 0
    q_tiles = seq // tq

    def head_spec(rows_blk, off):
        assert off % DA_VDIM == 0
        if rows_blk == seq:
            return pl.BlockSpec((seq, DA_VDIM), lambda b, hd, qi: (b, off // DA_VDIM + hd))
        return pl.BlockSpec((rows_blk, DA_VDIM), lambda b, hd, qi: (b * q_tiles + qi, off // DA_VDIM + hd))

    vec = lambda n: pl.BlockSpec((1, n), lambda b, hd, qi: (0, 0))
    kern = functools.partial(_attn_kernel, lambda_init=lambda_init)
    return pl.pallas_call(
        kern,
        grid=(batch, DA_HEADS, q_tiles),
        in_specs=[
            head_spec(tq, AT_Q),
            head_spec(seq, AT_K),
            head_spec(seq, AT_V),
            head_spec(tq, GA_GB),
            vec(DA_HEAD_DIM), vec(DA_HEAD_DIM), vec(DA_HEAD_DIM), vec(DA_HEAD_DIM),
            vec(DA_VDIM),
        ],
        out_specs=pl.BlockSpec((tq, DA_VDIM), lambda b, hd, qi: (b * q_tiles + qi, hd)),
        out_shape=jax.ShapeDtypeStruct((batch * seq, DA_WIDTH), BF16),
        scratch_shapes=[
            pltpu.VMEM((seq, 2 * DA_VDIM), BF16),
            pltpu.VMEM((2 * tq, LANES), F32),
            pltpu.VMEM((2 * tq, 2 * DA_VDIM), F32),
            pltpu.SMEM((1,), F32),
        ],
        compiler_params=pltpu.CompilerParams(
            dimension_semantics=("arbitrary", "arbitrary", "arbitrary"),
            vmem_limit_bytes=VMEM_LIMIT_BYTES),
        name="diffattn",
    )(attn_act, attn_act, attn_act, gate_act, lq1, lk1, lq2, lk2, subw)


def _out_kernel(x_ref, ya_ref, ob_ref, g0_ref, g1_ref, wa_ref, wb_ref, wo_ref, lng_ref, lnb_ref,
                out_ref, *, alpha):
    d = functools.partial(jnp.dot, preferred_element_type=F32)
    branch_a = d(ya_ref[...], wa_ref[...])
    branch_b = d(ob_ref[...], wb_ref[...])
    merged = g0_ref[...].astype(F32) * branch_a + g1_ref[...].astype(F32) * branch_b
    y = d(merged.astype(BF16), wo_ref[...])
    r = alpha * x_ref[...] + y
    mu = jnp.mean(r, axis=-1, keepdims=True)
    rc = r - mu
    var = jnp.mean(rc * rc, axis=-1, keepdims=True)
    out_ref[...] = rc * lax.rsqrt(var + EPS) * lng_ref[...] + lnb_ref[...]


def _output_stage(x2d, y_ssd, o_att, gate_act, w_a, w_b, w_o, ln_g, ln_b, alpha):
    t_rows = x2d.shape[0]
    tm = OUT_TM
    assert t_rows % tm == 0 and GA_GM % D_MODEL == 0
    row = lambda width: pl.BlockSpec((tm, width), lambda i: (i, 0))
    full = lambda shape: pl.BlockSpec(shape, lambda i: (0, 0))
    gate_spec = lambda k: pl.BlockSpec((tm, D_MODEL), lambda i: (i, GA_GM // D_MODEL + k))
    return pl.pallas_call(
        functools.partial(_out_kernel, alpha=alpha),
        grid=(t_rows // tm,),
        in_specs=[row(D_MODEL), row(SSD_WIDTH), row(DA_WIDTH), gate_spec(0), gate_spec(1),
                  full((SSD_WIDTH, D_MODEL)), full((DA_WIDTH, D_MODEL)), full((D_MODEL, D_MODEL)),
                  full((1, D_MODEL)), full((1, D_MODEL))],
        out_specs=row(D_MODEL),
        out_shape=jax.ShapeDtypeStruct((t_rows, D_MODEL), F32),
        compiler_params=pltpu.CompilerParams(
            dimension_semantics=("arbitrary",),
            vmem_limit_bytes=VMEM_LIMIT_BYTES),
        name="outproj",
    )(x2d, y_ssd, o_att, gate_act, gate_act, w_a, w_b, w_o, ln_g, ln_b)


def _rope_tables(seq):
    pos = jnp.arange(seq, dtype=F32)
    inv_freq = ROPE_THETA ** (-jnp.arange(0, ROT_DIM, 2, dtype=F32) / ROT_DIM)
    ang = pos[:, None] * inv_freq[None, :]
    cos, sin = jnp.cos(ang), jnp.sin(ang)
    half = jnp.arange(LANES) % (LANES // 2)
    f = half % ROT_HALF
    rotary = half < ROT_DIM
    first_half = jnp.arange(LANES) < LANES // 2
    cos_t = jnp.where(rotary[None, :], cos[:, f], 1.0)
    sin_t = jnp.where(rotary[None, :], jnp.where(first_half[None, :], -sin[:, f], sin[:, f]), 0.0)
    return cos_t, sin_t


def _permute_head_lanes(w):
    d_in = w.shape[0]
    w4 = w.reshape(d_in, DA_HEADS, 2, DA_HEAD_DIM)
    parts = [w4[..., :ROT_HALF], w4[..., ROT_DIM:ROT_DIM + ROT_PASS],
             w4[..., ROT_HALF:ROT_DIM], w4[..., ROT_DIM + ROT_PASS:]]
    parts = [p.reshape(d_in, DA_HEADS, -1) for p in parts]
    return jnp.concatenate(parts, axis=-1).reshape(d_in, DA_WIDTH)


def _layer(x2d, batch, seq, depth_total, l, w_in, b_gate, conv_w, conv_b, dt_bias, a_log, d_skip,
           ssd_norm_w, lambda_q1, lambda_k1, lambda_q2, lambda_k2, subln_w, w_a, w_b, w_o, ln_g, ln_b,
           tables):
    alpha = (2.0 * depth_total) ** 0.25
    lambda_init = 0.8 - 0.6 * math.exp(-0.3 * l)
    offs = [0]
    for s in IN_SIZES:
        offs.append(offs[-1] + s)
    z_w, xbc_w, dt_w, q_w, k_w, v_w, gb_w, gm_w = [w_in[:, offs[n]:offs[n + 1]] for n in range(len(IN_SIZES))]

    w_gate = jnp.concatenate([z_w, gb_w, gm_w], axis=1).astype(BF16)
    w_attn = jnp.concatenate([_permute_head_lanes(q_w), _permute_head_lanes(k_w), v_w], axis=1).astype(BF16)
    p_conv = jnp.zeros((SUBLANES, CONV_CH), F32).at[:CONV_WIDTH].set(conv_w).at[CONV_WIDTH].set(conv_b)

    gate_act = _proj_gate(x2d, w_gate, b_gate[None, :].astype(F32))
    attn_act = _proj_attn(x2d, w_attn, *tables, seq)
    conv_act, dt = _proj_conv(x2d, xbc_w.astype(BF16), p_conv, dt_w.astype(BF16),
                              dt_bias[None, :].astype(F32), seq)

    a_row = -jnp.exp(a_log.astype(F32))[None, :]
    dskip_row = jnp.repeat(d_skip.astype(F32), SSD_HEAD_DIM)[None, :]
    y_ssd = _ssd(conv_act, gate_act, dt, a_row, dskip_row, ssd_norm_w[None, :].astype(F32), batch, seq)

    o_att = _attention(attn_act, gate_act, lambda_q1[None, :], lambda_k1[None, :], lambda_q2[None, :],
                       lambda_k2[None, :], subln_w[None, :], batch, seq, lambda_init)

    return _output_stage(x2d, y_ssd, o_att, gate_act, w_a.astype(BF16), w_b.astype(BF16), w_o.astype(BF16),
                         ln_g[None, :], ln_b[None, :], alpha)


def kernel(x, w_in, b_gate, conv_w, conv_b, dt_bias, a_log, d_skip, ssd_norm_w, lambda_q1, lambda_k1,
           lambda_q2, lambda_k2, subln_w, w_a, w_b, w_o, ln_g, ln_b):
    batch, seq, _ = x.shape
    depth = w_in.shape[0]
    tables = _rope_tables(seq)
    x2d = x.reshape(batch * seq, D_MODEL)
    for l in range(depth):
        x2d = _layer(x2d, batch, seq, depth, l, w_in[l], b_gate[l], conv_w[l], conv_b[l], dt_bias[l],
                     a_log[l], d_skip[l], ssd_norm_w[l], lambda_q1[l], lambda_k1[l], lambda_q2[l],
                     lambda_k2[l], subln_w[l], w_a[l], w_b[l], w_o[l], ln_g[l], ln_b[l], tables)
    return x2d.reshape(batch, seq, D_MODEL)
```

```python
import functools
import math

import jax
import jax.numpy as jnp
import numpy as np
from jax import lax
from jax.experimental import pallas as pl
from jax.experimental.pallas import tpu as pltpu

F32 = jnp.float32
BF16 = jnp.bfloat16

D_MODEL = 1024
CHUNK = 64
SSD_HEADS = 16
SSD_HEAD_DIM = 64
SSD_WIDTH = SSD_HEADS * SSD_HEAD_DIM
SSD_GROUPS = 2
SSD_STATE = 128
CONV_WIDTH = 4
BC_WIDTH = SSD_GROUPS * SSD_STATE
CONV_CH = SSD_WIDTH + 2 * BC_WIDTH
DA_HEADS = 8
DA_HEAD_DIM = 64
DA_VDIM = 2 * DA_HEAD_DIM
DA_WIDTH = DA_HEADS * DA_VDIM
ROPE_THETA = 500000.0
ROT_DIM = DA_HEAD_DIM // 4
ROT_HALF = ROT_DIM // 2
N_BRANCH = 2
EPS = 1e-5
IN_SIZES = (SSD_WIDTH, CONV_CH, SSD_HEADS, DA_WIDTH, DA_WIDTH, DA_WIDTH, DA_WIDTH, N_BRANCH * D_MODEL)

GA_SZ, GA_GB, GA_GM = 0, SSD_WIDTH, SSD_WIDTH + DA_WIDTH
GA_COLS = GA_GM + N_BRANCH * D_MODEL
AT_Q, AT_K, AT_V = 0, DA_WIDTH, 2 * DA_WIDTH
AT_COLS = 3 * DA_WIDTH
CV_XS, CV_B, CV_C = 0, SSD_WIDTH, SSD_WIDTH + BC_WIDTH

ROT_PASS = (DA_HEAD_DIM - ROT_DIM) // 2

LANES = 128
SUBLANES = 8
BF16_ROWS = 16
VMEM_LIMIT_BYTES = 56 * 1024 * 1024

IN_TM = 512
SSD_ROWS = 512
ATT_TQ = 1024
ATT_TK = 1024
ATT_BAND = 256
ATT_SAFE_LOG2 = 64.0
OUT_TM = 512

NEG_BIG = -1e30
LOG2E = 1.4426950408889634


def _sigmoid(t):
    return 1.0 / (1.0 + jnp.exp2(t * (-LOG2E)))


def _split2(v):
    hi = v.astype(BF16)
    mid = (v - hi.astype(F32)).astype(BF16)
    return hi, mid


def _dot01_right(v, m01):
    hi, mid = _split2(v)
    d = functools.partial(jnp.dot, preferred_element_type=F32)
    return d(hi, m01) + d(mid, m01)


def _dot01_left(m01, v):
    hi, mid = _split2(v)
    d = functools.partial(jnp.dot, preferred_element_type=F32)
    return d(m01, hi) + d(m01, mid)


def _proj_params():
    return pltpu.CompilerParams(dimension_semantics=("arbitrary",),
                                vmem_limit_bytes=VMEM_LIMIT_BYTES)


def _proj_gate_kernel(x_ref, w_ref, bias_ref, o_ref):
    acc = jnp.dot(x_ref[...].astype(BF16), w_ref[...], preferred_element_type=F32)
    t = acc[:, :GA_GM]
    o_ref[:, :GA_GM] = (t * _sigmoid(t)).astype(o_ref.dtype)
    o_ref[:, GA_GM:] = _sigmoid(acc[:, GA_GM:] + bias_ref[...]).astype(o_ref.dtype)


def _proj_gate(x2d, w, b_gate_row):
    t_rows = x2d.shape[0]
    tm = IN_TM
    assert t_rows % tm == 0
    full = lambda shape: pl.BlockSpec(shape, lambda i: (0, 0))
    return pl.pallas_call(
        _proj_gate_kernel,
        grid=(t_rows // tm,),
        in_specs=[
            pl.BlockSpec((tm, D_MODEL), lambda i: (i, 0)),
            full((D_MODEL, GA_COLS)),
            full((1, N_BRANCH * D_MODEL)),
        ],
        out_specs=pl.BlockSpec((tm, GA_COLS), lambda i: (i, 0)),
        out_shape=jax.ShapeDtypeStruct((t_rows, GA_COLS), BF16),
        compiler_params=_proj_params(),
        name="proj_gate",
    )(x2d, w, b_gate_row)


def _proj_attn_kernel(x_ref, w_ref, cos_ref, sin_ref, o_ref, *, q_scale):
    acc = jnp.dot(x_ref[...].astype(BF16), w_ref[...], preferred_element_type=F32)
    cos = cos_ref[...]
    sin = sin_ref[...]
    cos_q = cos * q_scale
    sin_q = sin * q_scale
    for hd in range(AT_V // LANES):
        sl = slice(hd * LANES, (hd + 1) * LANES)
        a = acc[:, sl]
        partner = pltpu.roll(a, LANES // 2, axis=1)
        c, s = (cos_q, sin_q) if hd < AT_K // LANES else (cos, sin)
        o_ref[:, sl] = (a * c + partner * s).astype(o_ref.dtype)
    o_ref[:, AT_V:] = acc[:, AT_V:].astype(o_ref.dtype)


def _proj_attn(x2d, w, cos_t, sin_t, seq):
    t_rows = x2d.shape[0]
    tm = IN_TM
    assert t_rows % tm == 0 and seq % tm == 0
    tiles_per_seq = seq // tm
    table = lambda: pl.BlockSpec((tm, LANES), lambda i: (i % tiles_per_seq, 0))
    return pl.pallas_call(
        functools.partial(_proj_attn_kernel, q_scale=DA_HEAD_DIM ** -0.5 * LOG2E),
        grid=(t_rows // tm,),
        in_specs=[
            pl.BlockSpec((tm, D_MODEL), lambda i: (i, 0)),
            pl.BlockSpec((D_MODEL, AT_COLS), lambda i: (0, 0)),
            table(), table(),
        ],
        out_specs=pl.BlockSpec((tm, AT_COLS), lambda i: (i, 0)),
        out_shape=jax.ShapeDtypeStruct((t_rows, AT_COLS), BF16),
        compiler_params=_proj_params(),
        name="proj_attn",
    )(x2d, w, cos_t, sin_t)


def _proj_conv_kernel(x_ref, xh_ref, w_ref, p_ref, wdt_ref, dtb_ref, o_ref, dt_ref, *, tiles_per_seq):
    i = pl.program_id(0)
    tm = o_ref.shape[0]
    xb = x_ref[...].astype(BF16)
    w = w_ref[...]
    acc = jnp.dot(xb, w, preferred_element_type=F32)
    halo = jnp.dot(xh_ref[...].astype(BF16), w, preferred_element_type=F32)
    halo = jnp.where(i % tiles_per_seq == 0, 0.0, halo)

    bias = p_ref[CONV_WIDTH:CONV_WIDTH + 1, :]
    taps = [p_ref[k:k + 1, :] for k in range(CONV_WIDTH)]

    def conv(rows_val):
        out = taps[0] * rows_val
        for k in range(1, CONV_WIDTH):
            out = pltpu.roll(out, 1, axis=0) + taps[k] * rows_val
        out = out + bias
        return out * _sigmoid(out)

    head = BF16_ROWS
    ext = jnp.concatenate([halo, acc[:head]], axis=0)
    o_ref[:head, :] = conv(ext)[SUBLANES:, :].astype(o_ref.dtype)
    o_ref[head:, :] = conv(acc)[head:, :].astype(o_ref.dtype)

    raw = jnp.dot(xb, wdt_ref[...], preferred_element_type=F32) + dtb_ref[...]
    dt_ref[...] = jnp.maximum(raw, 0.0) + jnp.log(1.0 + jnp.exp(-jnp.abs(raw)))


def _proj_conv(x2d, w, params, w_dt, dt_bias, seq):
    t_rows = x2d.shape[0]
    tm = IN_TM
    assert t_rows % tm == 0 and seq % tm == 0
    tiles_per_seq = seq // tm
    halo_blocks = tm // SUBLANES
    full = lambda shape: pl.BlockSpec(shape, lambda i: (0, 0))
    return pl.pallas_call(
        functools.partial(_proj_conv_kernel, tiles_per_seq=tiles_per_seq),
        grid=(t_rows // tm,),
        in_specs=[
            pl.BlockSpec((tm, D_MODEL), lambda i: (i, 0)),
            pl.BlockSpec((SUBLANES, D_MODEL), lambda i: (jnp.maximum(i * halo_blocks - 1, 0), 0)),
            full((D_MODEL, CONV_CH)),
            full((SUBLANES, CONV_CH)),
            full((D_MODEL, SSD_HEADS)),
            full((1, SSD_HEADS)),
        ],
        out_specs=[
            pl.BlockSpec((tm, CONV_CH), lambda i: (i, 0)),
            pl.BlockSpec((tm, SSD_HEADS), lambda i: (i, 0)),
        ],
        out_shape=[
            jax.ShapeDtypeStruct((t_rows, CONV_CH), BF16),
            jax.ShapeDtypeStruct((t_rows, SSD_HEADS), F32),
        ],
        compiler_params=pltpu.CompilerParams(dimension_semantics=("arbitrary",),
                                             vmem_limit_bytes=VMEM_LIMIT_BYTES),
        name="proj_conv",
    )(x2d, x2d, w, params, w_dt, dt_bias)


def _ssd_kernel(xs_ref, b_ref, c_ref, sz_ref, dt_ref, a_ref, dskip_ref, normw_ref,
                expand_ref, bdtril_ref, tile8_ref,
                y_ref, acol_s, dtx_s, ht_s):
    rows = xs_ref.shape[0]
    n_chunks = rows // CHUNK
    gw = SSD_WIDTH // SSD_GROUPS

    @pl.when(pl.program_id(1) == 0)
    def _():
        ht_s[...] = jnp.zeros_like(ht_s)

    dt = dt_ref[...]
    a_cs = _dot01_left(bdtril_ref[...], dt * (a_ref[...] * LOG2E))
    expand = expand_ref[...]
    acol_s[...] = _dot01_right(a_cs, expand)
    dtx_s[...] = _dot01_right(dt, expand)

    qw = 4 * SSD_HEAD_DIM
    quads_per_group = gw // qw
    lane = lax.broadcasted_iota(jnp.int32, (CHUNK, qw), 1)
    row = lax.broadcasted_iota(jnp.int32, (CHUNK, qw), 0)
    s_of_lane = lane & (CHUNK - 1)
    diag_mask = s_of_lane == row
    causal_mask = s_of_lane <= row
    bd_r = lax.broadcasted_iota(jnp.int32, (4 * CHUNK, qw), 0) // CHUNK
    bd_c = lax.broadcasted_iota(jnp.int32, (4 * CHUNK, qw), 1) // SSD_HEAD_DIM
    bd_mask = bd_r == bd_c
    tile4 = tile8_ref[:, :qw]

    def chunk_body(c, carry):
        r0 = pl.multiple_of(c * CHUNK, CHUNK)
        rows_c = pl.ds(r0, CHUNK)
        bc = b_ref[rows_c, :]
        cc = c_ref[rows_c, :]
        for g in range(SSD_GROUPS):
            cg = cc[:, g * SSD_STATE:(g + 1) * SSD_STATE]
            bg = bc[:, g * SSD_STATE:(g + 1) * SSD_STATE]
            cb_g = lax.dot_general(cg, bg, (((1,), (1,)), ((), ())), preferred_element_type=F32)
            cb_rep = _dot01_right(cb_g, tile4)
            gated = []
            for qd in range(quads_per_group):
                sl = slice(g * gw + qd * qw, g * gw + (qd + 1) * qw)
                acol = acol_s[rows_c, sl]
                a_last = acol[CHUNK - 1:CHUNK, :]
                xs = xs_ref[rows_c, sl].astype(F32)
                xdt_f = xs * dtx_s[rows_c, sl]

                arow = jnp.sum(jnp.where(diag_mask, acol, 0.0), axis=0, keepdims=True)
                decay_ls = jnp.exp2(jnp.where(causal_mask, acol - arow, NEG_BIG))
                w_ls = (cb_rep * decay_ls).astype(BF16)
                xq = xdt_f.astype(BF16)
                bd = jnp.where(bd_mask, jnp.concatenate([xq] * 4, axis=0), jnp.zeros((), BF16))
                y_diag = jnp.dot(w_ls, bd, preferred_element_type=F32)

                ht = ht_s[:, sl]
                y_off = jnp.dot(cg, ht.astype(BF16), preferred_element_type=F32)
                y = y_diag + y_off * jnp.exp2(acol) + dskip_ref[:, sl] * xs
                gated.append(y * sz_ref[rows_c, sl].astype(F32))

                xd = (xdt_f * jnp.exp2(a_last - acol)).astype(BF16)
                st = lax.dot_general(bg, xd, (((0,), (0,)), ((), ())), preferred_element_type=F32)
                ht_s[:, sl] = ht * jnp.exp2(a_last) + st

            ssq = sum(jnp.sum(t * t, axis=-1, keepdims=True) for t in gated)
            inv = lax.rsqrt(ssq * (1.0 / gw) + EPS)
            for qd in range(quads_per_group):
                sl = slice(g * gw + qd * qw, g * gw + (qd + 1) * qw)
                y_ref[rows_c, sl] = (gated[qd] * inv * normw_ref[:, sl]).astype(y_ref.dtype)
        return carry

    lax.fori_loop(0, n_chunks, chunk_body, 0, unroll=4)


def _ssd(conv_act, gate_act, dt, a_row, dskip_row, normw_row, batch, seq):
    rows = SSD_ROWS
    assert seq % rows == 0 and rows % (2 * CHUNK) == 0
    blocks_per_seq = seq // rows
    hh = jnp.arange(SSD_WIDTH) // SSD_HEAD_DIM
    expand = (jnp.arange(SSD_HEADS)[:, None] == hh[None, :]).astype(BF16)
    rr = jnp.arange(rows)
    same_chunk = (rr[:, None] // CHUNK) == (rr[None, :] // CHUNK)
    bdtril = (same_chunk & (rr[None, :] <= rr[:, None])).astype(BF16)
    gw = SSD_WIDTH // SSD_GROUPS
    tile8 = (jnp.arange(CHUNK)[:, None] == (jnp.arange(gw)[None, :] % CHUNK)).astype(BF16)

    def row_map(b, r):
        return b * blocks_per_seq + r

    def col_spec(width, off):
        assert off % width == 0
        return pl.BlockSpec((rows, width), lambda b, r: (row_map(b, r), off // width))

    const = lambda shape: pl.BlockSpec(shape, lambda b, r: (0, 0))
    return pl.pallas_call(
        _ssd_kernel,
        grid=(batch, blocks_per_seq),
        in_specs=[
            col_spec(SSD_WIDTH, CV_XS),
            col_spec(BC_WIDTH, CV_B),
            col_spec(BC_WIDTH, CV_C),
            col_spec(SSD_WIDTH, GA_SZ),
            pl.BlockSpec((rows, SSD_HEADS), lambda b, r: (row_map(b, r), 0)),
            const((1, SSD_HEADS)),
            const((1, SSD_WIDTH)),
            const((1, SSD_WIDTH)),
            const((SSD_HEADS, SSD_WIDTH)),
            const((rows, rows)),
            const((CHUNK, gw)),
        ],
        out_specs=pl.BlockSpec((rows, SSD_WIDTH), lambda b, r: (row_map(b, r), 0)),
        out_shape=jax.ShapeDtypeStruct((batch * seq, SSD_WIDTH), BF16),
        scratch_shapes=[
            pltpu.VMEM((rows, SSD_WIDTH), F32),
            pltpu.VMEM((rows, SSD_WIDTH), F32),
            pltpu.VMEM((SSD_STATE, SSD_WIDTH), F32),
        ],
        compiler_params=pltpu.CompilerParams(
            dimension_semantics=("arbitrary", "arbitrary"),
            vmem_limit_bytes=VMEM_LIMIT_BYTES),
        name="ssd",
    )(conv_act, conv_act, conv_act, gate_act, dt, a_row, dskip_row, normw_row, expand, bdtril, tile8)


def _attn_kernel(q_ref, k_ref, v_ref, gb_ref, lq1_ref, lk1_ref, lq2_ref, lk2_ref, subw_ref,
                 o_ref, vaug_s, m_s, acc_s, knorm_s, *, lambda_init):
    qi = pl.program_id(2)
    tq = q_ref.shape[0]
    tk = ATT_TK
    seq = k_ref.shape[0]

    @pl.when(qi == 0)
    def _():
        vaug_s[:, :DA_VDIM] = v_ref[...].astype(vaug_s.dtype)
        vaug_s[:, DA_VDIM:] = jnp.ones((seq, DA_VDIM), vaug_s.dtype)
        kf = k_ref[...].astype(F32)
        knorm_s[0] = jnp.max(jnp.sum(kf * kf, axis=-1, keepdims=True))

    assert tq == tk
    band = ATT_BAND
    n_bands = tq // band
    half_lane = lax.broadcasted_iota(jnp.int32, (band, DA_VDIM), 1) & (LANES // 2 - 1)
    in_map0 = (half_lane < ROT_HALF) | ((half_lane >= ROT_DIM) & (half_lane < ROT_DIM + ROT_PASS))
    zero = jnp.zeros((), q_ref.dtype)

    qq = []
    for c in range(n_bands):
        q = q_ref[c * band:(c + 1) * band, :]
        qq.append(jnp.concatenate([jnp.where(in_map0, q, zero), jnp.where(in_map0, zero, q)], axis=0))

    acc_s[...] = jnp.zeros_like(acc_s)

    def band_scores(c, k):
        return lax.dot_general(qq[c], k, (((1,), (1,)), ((), ())), preferred_element_type=F32)

    def band_rows(c):
        return slice(c * 2 * band, (c + 1) * 2 * band)

    def diagonal_operands(c):
        nk = (c + 1) * band
        k0 = pl.multiple_of(qi * tk, tk)
        r_chunk = (c * band + lax.broadcasted_iota(jnp.int32, (band, nk), 0)) // CHUNK
        k_chunk = lax.broadcasted_iota(jnp.int32, (band, nk), 1) // CHUNK
        ok = k_chunk <= r_chunk
        return k_ref[pl.ds(k0, nk), :], vaug_s[pl.ds(k0, nk), :], jnp.concatenate([ok, ok], axis=0)

    def full_operands(kt):
        k0 = pl.multiple_of(kt * tk, tk)
        return k_ref[pl.ds(k0, tk), :], vaug_s[pl.ds(k0, tk), :]

    qf = q_ref[...].astype(F32)
    q_norm2 = jnp.max(jnp.sum(qf * qf, axis=-1, keepdims=True))
    unshifted_ok = q_norm2 * knorm_s[0] <= ATT_SAFE_LOG2 * ATT_SAFE_LOG2

    @pl.when(unshifted_ok)
    def _():
        def tile(kt):
            k, vt = full_operands(kt)
            for c in range(n_bands):
                p = jnp.exp2(band_scores(c, k)).astype(vaug_s.dtype)
                acc_s[band_rows(c), :] += jnp.dot(p, vt, preferred_element_type=F32)

        def pair(i, carry):
            tile(2 * i)
            tile(2 * i + 1)
            return carry

        lax.fori_loop(0, lax.shift_right_logical(qi, 1), pair, 0)

        @pl.when((qi & 1) == 1)
        def _():
            tile(qi - 1)

        for c in range(n_bands):
            k, vt, ok = diagonal_operands(c)
            p = jnp.exp2(jnp.where(ok, band_scores(c, k), NEG_BIG)).astype(vaug_s.dtype)
            acc_s[band_rows(c), :] += jnp.dot(p, vt, preferred_element_type=F32)

    @pl.when(jnp.logical_not(unshifted_ok))
    def _():
        m_s[...] = jnp.full_like(m_s, NEG_BIG)

        def softmax_pv(c, s, vt):
            rows = band_rows(c)
            m_old = m_s[rows, :]
            m_new = jnp.maximum(m_old, jnp.max(s, axis=-1, keepdims=True))
            alpha = jnp.exp2(m_old - m_new)
            p = jnp.exp2(s - m_new[:, 0:1]).astype(vaug_s.dtype)
            pv = jnp.dot(p, vt, preferred_element_type=F32)
            acc_s[rows, :] = jnp.concatenate([alpha, alpha], axis=1) * acc_s[rows, :] + pv
            m_s[rows, :] = m_new

        def tile(kt, carry):
            k, vt = full_operands(kt)
            for c in range(n_bands):
                softmax_pv(c, band_scores(c, k), vt)
            return carry

        lax.fori_loop(0, qi, tile, 0)
        for c in range(n_bands):
            k, vt, ok = diagonal_operands(c)
            softmax_pv(c, jnp.where(ok, band_scores(c, k), NEG_BIG), vt)

    lam = (jnp.exp(jnp.sum(lq1_ref[...] * lk1_ref[...], axis=-1, keepdims=True))
           - jnp.exp(jnp.sum(lq2_ref[...] * lk2_ref[...], axis=-1, keepdims=True)) + lambda_init)
    scale = subw_ref[...] * (1.0 - lambda_init)
    for c in range(n_bands):
        acc = acc_s[c * 2 * band:(c + 1) * 2 * band, :]
        o_all = acc[:, :DA_VDIM] / acc[:, DA_VDIM:]
        o = o_all[:band] - lam * o_all[band:]
        ms = jnp.mean(o * o, axis=-1, keepdims=True)
        o = o * lax.rsqrt(ms + EPS) * scale
        rows = slice(c * band, (c + 1) * band)
        o_ref[rows, :] = (o * gb_ref[rows, :].astype(F32)).astype(o_ref.dtype)


def _attention(attn_act, gate_act, lq1, lk1, lq2, lk2, subw, batch, seq, lambda_init):
    tq = ATT_TQ
    assert seq % ATT_TK == 0 and ATT_TK == tq and tq % ATT_BAND == 0 and ATT_BAND % CHUNK == 0
    q_tiles = seq // tq

    def head_spec(rows_blk, off):
        assert off % DA_VDIM == 0
        if rows_blk == seq:
            return pl.BlockSpec((seq, DA_VDIM), lambda b, hd, qi: (b, off // DA_VDIM + hd))
        return pl.BlockSpec((rows_blk, DA_VDIM), lambda b, hd, qi: (b * q_tiles + qi, off // DA_VDIM + hd))

    vec = lambda n: pl.BlockSpec((1, n), lambda b, hd, qi: (0, 0))
    kern = functools.partial(_attn_kernel, lambda_init=lambda_init)
    return pl.pallas_call(
        kern,
        grid=(batch, DA_HEADS, q_tiles),
        in_specs=[
            head_spec(tq, AT_Q),
            head_spec(seq, AT_K),
            head_spec(seq, AT_V),
            head_spec(tq, GA_GB),
            vec(DA_HEAD_DIM), vec(DA_HEAD_DIM), vec(DA_HEAD_DIM), vec(DA_HEAD_DIM),
            vec(DA_VDIM),
        ],
        out_specs=pl.BlockSpec((tq, DA_VDIM), lambda b, hd, qi: (b * q_tiles + qi, hd)),
        out_shape=jax.ShapeDtypeStruct((batch * seq, DA_WIDTH), BF16),
        scratch_shapes=[
            pltpu.VMEM((seq, 2 * DA_VDIM), BF16),
            pltpu.VMEM((2 * tq, LANES), F32),
            pltpu.VMEM((2 * tq, 2 * DA_VDIM), F32),
            pltpu.SMEM((1,), F32),
        ],
        compiler_params=pltpu.CompilerParams(
            dimension_semantics=("arbitrary", "arbitrary", "arbitrary"),
            vmem_limit_bytes=VMEM_LIMIT_BYTES),
        name="diffattn",
    )(attn_act, attn_act, attn_act, gate_act, lq1, lk1, lq2, lk2, subw)


def _out_kernel(x_ref, ya_ref, ob_ref, g0_ref, g1_ref, wa_ref, wb_ref, wo_ref, lng_ref, lnb_ref,
                out_ref, *, alpha):
    d = functools.partial(jnp.dot, preferred_element_type=F32)
    branch_a = d(ya_ref[...], wa_ref[...])
    branch_b = d(ob_ref[...], wb_ref[...])
    merged = g0_ref[...].astype(F32) * branch_a + g1_ref[...].astype(F32) * branch_b
    y = d(merged.astype(BF16), wo_ref[...])
    r = alpha * x_ref[...] + y
    mu = jnp.mean(r, axis=-1, keepdims=True)
    rc = r - mu
    var = jnp.mean(rc * rc, axis=-1, keepdims=True)
    out_ref[...] = rc * lax.rsqrt(var + EPS) * lng_ref[...] + lnb_ref[...]


def _output_stage(x2d, y_ssd, o_att, gate_act, w_a, w_b, w_o, ln_g, ln_b, alpha):
    t_rows = x2d.shape[0]
    tm = OUT_TM
    assert t_rows % tm == 0 and GA_GM % D_MODEL == 0
    row = lambda width: pl.BlockSpec((tm, width), lambda i: (i, 0))
    full = lambda shape: pl.BlockSpec(shape, lambda i: (0, 0))
    gate_spec = lambda k: pl.BlockSpec((tm, D_MODEL), lambda i: (i, GA_GM // D_MODEL + k))
    return pl.pallas_call(
        functools.partial(_out_kernel, alpha=alpha),
        grid=(t_rows // tm,),
        in_specs=[row(D_MODEL), row(SSD_WIDTH), row(DA_WIDTH), gate_spec(0), gate_spec(1),
                  full((SSD_WIDTH, D_MODEL)), full((DA_WIDTH, D_MODEL)), full((D_MODEL, D_MODEL)),
                  full((1, D_MODEL)), full((1, D_MODEL))],
        out_specs=row(D_MODEL),
        out_shape=jax.ShapeDtypeStruct((t_rows, D_MODEL), F32),
        compiler_params=pltpu.CompilerParams(
            dimension_semantics=("arbitrary",),
            vmem_limit_bytes=VMEM_LIMIT_BYTES),
        name="outproj",
    )(x2d, y_ssd, o_att, gate_act, gate_act, w_a, w_b, w_o, ln_g, ln_b)


def _rope_tables(seq):
    pos = jnp.arange(seq, dtype=F32)
    inv_freq = ROPE_THETA ** (-jnp.arange(0, ROT_DIM, 2, dtype=F32) / ROT_DIM)
    ang = pos[:, None] * inv_freq[None, :]
    cos, sin = jnp.cos(ang), jnp.sin(ang)
    ones = jnp.ones((seq, LANES // 2 - ROT_DIM), F32)
    zeros = jnp.zeros_like(ones)
    cos_t = jnp.concatenate([cos, cos, ones, cos, cos, ones], axis=1)
    sin_t = jnp.concatenate([-sin, -sin, zeros, sin, sin, zeros], axis=1)
    return cos_t, sin_t


def _head_lane_permutation():
    src = np.zeros(LANES, np.int64)
    for m in range(2):
        for d in range(DA_HEAD_DIM):
            if d < ROT_HALF:
                lane = m * ROT_HALF + d
            elif d < ROT_DIM:
                lane = LANES // 2 + m * ROT_HALF + (d - ROT_HALF)
            elif d < ROT_DIM + ROT_PASS:
                lane = ROT_DIM + m * ROT_PASS + (d - ROT_DIM)
            else:
                lane = LANES // 2 + ROT_DIM + m * ROT_PASS + (d - ROT_DIM - ROT_PASS)
            src[lane] = m * DA_HEAD_DIM + d
    perm = np.zeros((DA_WIDTH, DA_WIDTH), np.float32)
    for hd in range(DA_HEADS):
        perm[hd * LANES + src, hd * LANES + np.arange(LANES)] = 1.0
    return jnp.asarray(perm, BF16)


def _permute_head_lanes(w_bf16, perm):
    return jnp.dot(w_bf16, perm, preferred_element_type=BF16)


def _layer(x2d, batch, seq, depth_total, l, w_in, b_gate, conv_w, conv_b, dt_bias, a_log, d_skip,
           ssd_norm_w, lambda_q1, lambda_k1, lambda_q2, lambda_k2, subln_w, w_a, w_b, w_o, ln_g, ln_b,
           tables):
    alpha = (2.0 * depth_total) ** 0.25
    lambda_init = 0.8 - 0.6 * math.exp(-0.3 * l)
    offs = [0]
    for s in IN_SIZES:
        offs.append(offs[-1] + s)
    z_w, xbc_w, dt_w, q_w, k_w, v_w, gb_w, gm_w = [w_in[:, offs[n]:offs[n + 1]] for n in range(len(IN_SIZES))]

    w_gate = jnp.concatenate([z_w, gb_w, gm_w], axis=1).astype(BF16)
    perm = _head_lane_permutation()
    w_attn = jnp.concatenate([_permute_head_lanes(q_w.astype(BF16), perm),
                              _permute_head_lanes(k_w.astype(BF16), perm), v_w.astype(BF16)], axis=1)
    p_conv = jnp.zeros((SUBLANES, CONV_CH), F32).at[:CONV_WIDTH].set(conv_w).at[CONV_WIDTH].set(conv_b)

    gate_act = _proj_gate(x2d, w_gate, b_gate[None, :].astype(F32))
    attn_act = _proj_attn(x2d, w_attn, *tables, seq)
    conv_act, dt = _proj_conv(x2d, xbc_w.astype(BF16), p_conv, dt_w.astype(BF16),
                              dt_bias[None, :].astype(F32), seq)

    a_row = -jnp.exp(a_log.astype(F32))[None, :]
    dskip_row = jnp.repeat(d_skip.astype(F32), SSD_HEAD_DIM)[None, :]
    y_ssd = _ssd(conv_act, gate_act, dt, a_row, dskip_row, ssd_norm_w[None, :].astype(F32), batch, seq)

    o_att = _attention(attn_act, gate_act, lambda_q1[None, :], lambda_k1[None, :], lambda_q2[None, :],
                       lambda_k2[None, :], subln_w[None, :], batch, seq, lambda_init)

    return _output_stage(x2d, y_ssd, o_att, gate_act, w_a.astype(BF16), w_b.astype(BF16), w_o.astype(BF16),
                         ln_g[None, :], ln_b[None, :], alpha)


def kernel(x, w_in, b_gate, conv_w, conv_b, dt_bias, a_log, d_skip, ssd_norm_w, lambda_q1, lambda_k1,
           lambda_q2, lambda_k2, subln_w, w_a, w_b, w_o, ln_g, ln_b):
    batch, seq, _ = x.shape
    depth = w_in.shape[0]
    tables = _rope_tables(seq)
    x2d = x.reshape(batch * seq, D_MODEL)
    for l in range(depth):
        x2d = _layer(x2d, batch, seq, depth, l, w_in[l], b_gate[l], conv_w[l], conv_b[l], dt_bias[l],
                     a_log[l], d_skip[l], ssd_norm_w[l], lambda_q1[l], lambda_k1[l], lambda_q2[l],
                     lambda_k2[l], subln_w[l], w_a[l], w_b[l], w_o[l], ln_g[l], ln_b[l], tables)
    return x2d.reshape(batch, seq, D_MODEL)
```

```python
import functools
import math

import jax
import jax.numpy as jnp
import numpy as np
from jax import lax
from jax.experimental import pallas as pl
from jax.experimental.pallas import tpu as pltpu

F32 = jnp.float32
BF16 = jnp.bfloat16

D_MODEL = 1024
CHUNK = 64
SSD_HEADS = 16
SSD_HEAD_DIM = 64
SSD_WIDTH = SSD_HEADS * SSD_HEAD_DIM
SSD_GROUPS = 2
SSD_STATE = 128
CONV_WIDTH = 4
BC_WIDTH = SSD_GROUPS * SSD_STATE
CONV_CH = SSD_WIDTH + 2 * BC_WIDTH
DA_HEADS = 8
DA_HEAD_DIM = 64
DA_VDIM = 2 * DA_HEAD_DIM
DA_WIDTH = DA_HEADS * DA_VDIM
ROPE_THETA = 500000.0
ROT_DIM = DA_HEAD_DIM // 4
ROT_HALF = ROT_DIM // 2
N_BRANCH = 2
EPS = 1e-5
IN_SIZES = (SSD_WIDTH, CONV_CH, SSD_HEADS, DA_WIDTH, DA_WIDTH, DA_WIDTH, DA_WIDTH, N_BRANCH * D_MODEL)

GA_SZ, GA_GB, GA_GM = 0, SSD_WIDTH, SSD_WIDTH + DA_WIDTH
GA_COLS = GA_GM + N_BRANCH * D_MODEL
AT_Q, AT_K, AT_V = 0, DA_WIDTH, 2 * DA_WIDTH
AT_COLS = 3 * DA_WIDTH
CV_XS, CV_B, CV_C = 0, SSD_WIDTH, SSD_WIDTH + BC_WIDTH

ROT_PASS = (DA_HEAD_DIM - ROT_DIM) // 2

LANES = 128
SUBLANES = 8
BF16_ROWS = 16
VMEM_LIMIT_BYTES = 56 * 1024 * 1024

IN_TM = 512
SSD_ROWS = 512
ATT_TQ = 1024
ATT_TK = 1024
ATT_BAND = 256
ATT_SAFE_LOG2 = 64.0
OUT_TM = 512

NEG_BIG = -1e30
LOG2E = 1.4426950408889634


def _sigmoid(t):
    return 1.0 / (1.0 + jnp.exp2(t * (-LOG2E)))


def _split2(v):
    hi = v.astype(BF16)
    mid = (v - hi.astype(F32)).astype(BF16)
    return hi, mid


def _dot01_right(v, m01):
    hi, mid = _split2(v)
    d = functools.partial(jnp.dot, preferred_element_type=F32)
    return d(hi, m01) + d(mid, m01)


def _dot01_left(m01, v):
    hi, mid = _split2(v)
    d = functools.partial(jnp.dot, preferred_element_type=F32)
    return d(m01, hi) + d(m01, mid)


def _proj_params():
    return pltpu.CompilerParams(dimension_semantics=("arbitrary",),
                                vmem_limit_bytes=VMEM_LIMIT_BYTES)


def _proj_gate_kernel(x_ref, w_ref, bias_ref, o_ref):
    acc = jnp.dot(x_ref[...].astype(BF16), w_ref[...], preferred_element_type=F32)
    t = acc[:, :GA_GM]
    o_ref[:, :GA_GM] = (t * _sigmoid(t)).astype(o_ref.dtype)
    o_ref[:, GA_GM:] = _sigmoid(acc[:, GA_GM:] + bias_ref[...]).astype(o_ref.dtype)


def _proj_gate(x2d, w, b_gate_row):
    t_rows = x2d.shape[0]
    tm = IN_TM
    assert t_rows % tm == 0
    full = lambda shape: pl.BlockSpec(shape, lambda i: (0, 0))
    return pl.pallas_call(
        _proj_gate_kernel,
        grid=(t_rows // tm,),
        in_specs=[
            pl.BlockSpec((tm, D_MODEL), lambda i: (i, 0)),
            full((D_MODEL, GA_COLS)),
            full((1, N_BRANCH * D_MODEL)),
        ],
        out_specs=pl.BlockSpec((tm, GA_COLS), lambda i: (i, 0)),
        out_shape=jax.ShapeDtypeStruct((t_rows, GA_COLS), BF16),
        compiler_params=_proj_params(),
        name="proj_gate",
    )(x2d, w, b_gate_row)


def _proj_attn_kernel(x_ref, w_ref, cos_ref, sin_ref, o_ref, *, q_scale):
    acc = jnp.dot(x_ref[...].astype(BF16), w_ref[...], preferred_element_type=F32)
    cos = cos_ref[...]
    sin = sin_ref[...]
    cos_q = cos * q_scale
    sin_q = sin * q_scale
    for hd in range(AT_V // LANES):
        sl = slice(hd * LANES, (hd + 1) * LANES)
        a = acc[:, sl]
        partner = pltpu.roll(a, LANES // 2, axis=1)
        c, s = (cos_q, sin_q) if hd < AT_K // LANES else (cos, sin)
        o_ref[:, sl] = (a * c + partner * s).astype(o_ref.dtype)
    o_ref[:, AT_V:] = acc[:, AT_V:].astype(o_ref.dtype)


def _proj_attn(x2d, w, cos_t, sin_t, seq):
    t_rows = x2d.shape[0]
    tm = IN_TM
    assert t_rows % tm == 0 and seq % tm == 0
    tiles_per_seq = seq // tm
    table = lambda: pl.BlockSpec((tm, LANES), lambda i: (i % tiles_per_seq, 0))
    return pl.pallas_call(
        functools.partial(_proj_attn_kernel, q_scale=DA_HEAD_DIM ** -0.5 * LOG2E),
        grid=(t_rows // tm,),
        in_specs=[
            pl.BlockSpec((tm, D_MODEL), lambda i: (i, 0)),
            pl.BlockSpec((D_MODEL, AT_COLS), lambda i: (0, 0)),
            table(), table(),
        ],
        out_specs=pl.BlockSpec((tm, AT_COLS), lambda i: (i, 0)),
        out_shape=jax.ShapeDtypeStruct((t_rows, AT_COLS), BF16),
        compiler_params=_proj_params(),
        name="proj_attn",
    )(x2d, w, cos_t, sin_t)


def _proj_conv_kernel(x_ref, xh_ref, w_ref, p_ref, wdt_ref, dtb_ref, o_ref, dt_ref, *, tiles_per_seq):
    i = pl.program_id(0)
    tm = o_ref.shape[0]
    xb = x_ref[...].astype(BF16)
    w = w_ref[...]
    acc = jnp.dot(xb, w, preferred_element_type=F32)
    halo = jnp.dot(xh_ref[...].astype(BF16), w, preferred_element_type=F32)
    halo = jnp.where(i % tiles_per_seq == 0, 0.0, halo)

    bias = p_ref[CONV_WIDTH:CONV_WIDTH + 1, :]
    taps = [p_ref[k:k + 1, :] for k in range(CONV_WIDTH)]

    def conv(rows_val):
        out = taps[0] * rows_val
        for k in range(1, CONV_WIDTH):
            out = pltpu.roll(out, 1, axis=0) + taps[k] * rows_val
        out = out + bias
        return out * _sigmoid(out)

    head = BF16_ROWS
    ext = jnp.concatenate([halo, acc[:head]], axis=0)
    o_ref[:head, :] = conv(ext)[SUBLANES:, :].astype(o_ref.dtype)
    o_ref[head:, :] = conv(acc)[head:, :].astype(o_ref.dtype)

    raw = jnp.dot(xb, wdt_ref[...], preferred_element_type=F32) + dtb_ref[...]
    dt_ref[...] = jnp.maximum(raw, 0.0) + jnp.log(1.0 + jnp.exp(-jnp.abs(raw)))


def _proj_conv(x2d, w, params, w_dt, dt_bias, seq):
    t_rows = x2d.shape[0]
    tm = IN_TM
    assert t_rows % tm == 0 and seq % tm == 0
    tiles_per_seq = seq // tm
    halo_blocks = tm // SUBLANES
    full = lambda shape: pl.BlockSpec(shape, lambda i: (0, 0))
    return pl.pallas_call(
        functools.partial(_proj_conv_kernel, tiles_per_seq=tiles_per_seq),
        grid=(t_rows // tm,),
        in_specs=[
            pl.BlockSpec((tm, D_MODEL), lambda i: (i, 0)),
            pl.BlockSpec((SUBLANES, D_MODEL), lambda i: (jnp.maximum(i * halo_blocks - 1, 0), 0)),
            full((D_MODEL, CONV_CH)),
            full((SUBLANES, CONV_CH)),
            full((D_MODEL, SSD_HEADS)),
            full((1, SSD_HEADS)),
        ],
        out_specs=[
            pl.BlockSpec((tm, CONV_CH), lambda i: (i, 0)),
            pl.BlockSpec((tm, SSD_HEADS), lambda i: (i, 0)),
        ],
        out_shape=[
            jax.ShapeDtypeStruct((t_rows, CONV_CH), BF16),
            jax.ShapeDtypeStruct((t_rows, SSD_HEADS), F32),
        ],
        compiler_params=pltpu.CompilerParams(dimension_semantics=("arbitrary",),
                                             vmem_limit_bytes=VMEM_LIMIT_BYTES),
        name="proj_conv",
    )(x2d, x2d, w, params, w_dt, dt_bias)


def _ssd_kernel(xs_ref, b_ref, c_ref, sz_ref, dt_ref, a_ref, dskip_ref, normw_ref,
                expand_ref, bdtril_ref, tile8_ref,
                y_ref, acol_s, dtx_s, ht_s):
    rows = xs_ref.shape[0]
    n_chunks = rows // CHUNK
    gw = SSD_WIDTH // SSD_GROUPS

    @pl.when(pl.program_id(1) == 0)
    def _():
        ht_s[...] = jnp.zeros_like(ht_s)

    dt = dt_ref[...]
    a_cs = _dot01_left(bdtril_ref[...], dt * (a_ref[...] * LOG2E))
    expand = expand_ref[...]
    acol_s[...] = _dot01_right(a_cs, expand)
    dtx_s[...] = _dot01_right(dt, expand)

    qw = 4 * SSD_HEAD_DIM
    quads_per_group = gw // qw
    lane = lax.broadcasted_iota(jnp.int32, (CHUNK, qw), 1)
    row = lax.broadcasted_iota(jnp.int32, (CHUNK, qw), 0)
    s_of_lane = lane & (CHUNK - 1)
    diag_mask = s_of_lane == row
    causal_mask = s_of_lane <= row
    bd_r = lax.broadcasted_iota(jnp.int32, (4 * CHUNK, qw), 0) // CHUNK
    bd_c = lax.broadcasted_iota(jnp.int32, (4 * CHUNK, qw), 1) // SSD_HEAD_DIM
    bd_mask = bd_r == bd_c
    tile4 = tile8_ref[:, :qw]

    def chunk_body(c, carry):
        r0 = pl.multiple_of(c * CHUNK, CHUNK)
        rows_c = pl.ds(r0, CHUNK)
        bc = b_ref[rows_c, :]
        cc = c_ref[rows_c, :]
        for g in range(SSD_GROUPS):
            cg = cc[:, g * SSD_STATE:(g + 1) * SSD_STATE]
            bg = bc[:, g * SSD_STATE:(g + 1) * SSD_STATE]
            cb_g = lax.dot_general(cg, bg, (((1,), (1,)), ((), ())), preferred_element_type=F32)
            cb_rep = _dot01_right(cb_g, tile4)
            gated = []
            for qd in range(quads_per_group):
                sl = slice(g * gw + qd * qw, g * gw + (qd + 1) * qw)
                acol = acol_s[rows_c, sl]
                a_last = acol[CHUNK - 1:CHUNK, :]
                xs = xs_ref[rows_c, sl].astype(F32)
                xdt_f = xs * dtx_s[rows_c, sl]

                arow = jnp.sum(jnp.where(diag_mask, acol, 0.0), axis=0, keepdims=True)
                decay_ls = jnp.exp2(jnp.where(causal_mask, acol - arow, NEG_BIG))
                w_ls = (cb_rep * decay_ls).astype(BF16)
                xq = xdt_f.astype(BF16)
                bd = jnp.where(bd_mask, jnp.concatenate([xq] * 4, axis=0), jnp.zeros((), BF16))
                y_diag = jnp.dot(w_ls, bd, preferred_element_type=F32)

                ht = ht_s[:, sl]
                y_off = jnp.dot(cg, ht.astype(BF16), preferred_element_type=F32)
                y = y_diag + y_off * jnp.exp2(acol) + dskip_ref[:, sl] * xs
                gated.append(y * sz_ref[rows_c, sl].astype(F32))

                xd = (xdt_f * jnp.exp2(a_last - acol)).astype(BF16)
                st = lax.dot_general(bg, xd, (((0,), (0,)), ((), ())), preferred_element_type=F32)
                ht_s[:, sl] = ht * jnp.exp2(a_last) + st

            ssq = sum(jnp.sum(t * t, axis=-1, keepdims=True) for t in gated)
            inv = lax.rsqrt(ssq * (1.0 / gw) + EPS)
            for qd in range(quads_per_group):
                sl = slice(g * gw + qd * qw, g * gw + (qd + 1) * qw)
                y_ref[rows_c, sl] = (gated[qd] * inv * normw_ref[:, sl]).astype(y_ref.dtype)
        return carry

    lax.fori_loop(0, n_chunks, chunk_body, 0, unroll=4)


def _ssd(conv_act, gate_act, dt, a_row, dskip_row, normw_row, batch, seq):
    rows = SSD_ROWS
    assert seq % rows == 0 and rows % (2 * CHUNK) == 0
    blocks_per_seq = seq // rows
    hh = np.arange(SSD_WIDTH) // SSD_HEAD_DIM
    expand = jnp.asarray(np.arange(SSD_HEADS)[:, None] == hh[None, :], BF16)
    rr = np.arange(rows)
    same_chunk = (rr[:, None] // CHUNK) == (rr[None, :] // CHUNK)
    bdtril = jnp.asarray(same_chunk & (rr[None, :] <= rr[:, None]), BF16)
    gw = SSD_WIDTH // SSD_GROUPS
    tile8 = jnp.asarray(np.arange(CHUNK)[:, None] == (np.arange(gw)[None, :] % CHUNK), BF16)

    def row_map(b, r):
        return b * blocks_per_seq + r

    def col_spec(width, off):
        assert off % width == 0
        return pl.BlockSpec((rows, width), lambda b, r: (row_map(b, r), off // width))

    const = lambda shape: pl.BlockSpec(shape, lambda b, r: (0, 0))
    return pl.pallas_call(
        _ssd_kernel,
        grid=(batch, blocks_per_seq),
        in_specs=[
            col_spec(SSD_WIDTH, CV_XS),
            col_spec(BC_WIDTH, CV_B),
            col_spec(BC_WIDTH, CV_C),
            col_spec(SSD_WIDTH, GA_SZ),
            pl.BlockSpec((rows, SSD_HEADS), lambda b, r: (row_map(b, r), 0)),
            const((1, SSD_HEADS)),
            const((1, SSD_WIDTH)),
            const((1, SSD_WIDTH)),
            const((SSD_HEADS, SSD_WIDTH)),
            const((rows, rows)),
            const((CHUNK, gw)),
        ],
        out_specs=pl.BlockSpec((rows, SSD_WIDTH), lambda b, r: (row_map(b, r), 0)),
        out_shape=jax.ShapeDtypeStruct((batch * seq, SSD_WIDTH), BF16),
        scratch_shapes=[
            pltpu.VMEM((rows, SSD_WIDTH), F32),
            pltpu.VMEM((rows, SSD_WIDTH), F32),
            pltpu.VMEM((SSD_STATE, SSD_WIDTH), F32),
        ],
        compiler_params=pltpu.CompilerParams(
            dimension_semantics=("arbitrary", "arbitrary"),
            vmem_limit_bytes=VMEM_LIMIT_BYTES),
        name="ssd",
    )(conv_act, conv_act, conv_act, gate_act, dt, a_row, dskip_row, normw_row, expand, bdtril, tile8)


def _attn_kernel(q_ref, k_ref, v_ref, gb_ref, lq1_ref, lk1_ref, lq2_ref, lk2_ref, subw_ref,
                 o_ref, vaug_s, m_s, acc_s, knorm_s, *, lambda_init):
    qi = pl.program_id(2)
    tq = q_ref.shape[0]
    tk = ATT_TK
    seq = k_ref.shape[0]

    @pl.when(qi == 0)
    def _():
        vaug_s[:, :DA_VDIM] = v_ref[...].astype(vaug_s.dtype)
        vaug_s[:, DA_VDIM:] = jnp.ones((seq, DA_VDIM), vaug_s.dtype)
        kf = k_ref[...].astype(F32)
        knorm_s[0] = jnp.max(jnp.sum(kf * kf, axis=-1, keepdims=True))

    assert tq == tk
    band = ATT_BAND
    n_bands = tq // band
    half_lane = lax.broadcasted_iota(jnp.int32, (band, DA_VDIM), 1) & (LANES // 2 - 1)
    in_map0 = (half_lane < ROT_HALF) | ((half_lane >= ROT_DIM) & (half_lane < ROT_DIM + ROT_PASS))
    zero = jnp.zeros((), q_ref.dtype)

    qq = []
    for c in range(n_bands):
        q = q_ref[c * band:(c + 1) * band, :]
        qq.append(jnp.concatenate([jnp.where(in_map0, q, zero), jnp.where(in_map0, zero, q)], axis=0))

    def band_scores(c, k):
        return lax.dot_general(qq[c], k, (((1,), (1,)), ((), ())), preferred_element_type=F32)

    def band_rows(c):
        return slice(c * 2 * band, (c + 1) * 2 * band)

    def diagonal_operands(c):
        nk = (c + 1) * band
        k0 = pl.multiple_of(qi * tk, tk)
        r_chunk = (c * band + lax.broadcasted_iota(jnp.int32, (band, nk), 0)) // CHUNK
        k_chunk = lax.broadcasted_iota(jnp.int32, (band, nk), 1) // CHUNK
        ok = k_chunk <= r_chunk
        return k_ref[pl.ds(k0, nk), :], vaug_s[pl.ds(k0, nk), :], jnp.concatenate([ok, ok], axis=0)

    def full_operands(kt):
        k0 = pl.multiple_of(kt * tk, tk)
        return k_ref[pl.ds(k0, tk), :], vaug_s[pl.ds(k0, tk), :]

    qf = q_ref[...].astype(F32)
    q_norm2 = jnp.max(jnp.sum(qf * qf, axis=-1, keepdims=True))
    unshifted_ok = q_norm2 * knorm_s[0] <= ATT_SAFE_LOG2 * ATT_SAFE_LOG2

    @pl.when(unshifted_ok)
    def _():
        for c in range(n_bands):
            k, vt, ok = diagonal_operands(c)
            p = jnp.exp2(jnp.where(ok, band_scores(c, k), NEG_BIG)).astype(vaug_s.dtype)
            acc_s[band_rows(c), :] = jnp.dot(p, vt, preferred_element_type=F32)

        def tile(kt):
            k, vt = full_operands(kt)
            for c in range(n_bands):
                p = jnp.exp2(band_scores(c, k)).astype(vaug_s.dtype)
                acc_s[band_rows(c), :] += jnp.dot(p, vt, preferred_element_type=F32)

        def pair(i, carry):
            tile(2 * i)
            tile(2 * i + 1)
            return carry

        lax.fori_loop(0, lax.shift_right_logical(qi, 1), pair, 0)

        @pl.when((qi & 1) == 1)
        def _():
            tile(qi - 1)

    @pl.when(jnp.logical_not(unshifted_ok))
    def _():
        m_s[...] = jnp.full_like(m_s, NEG_BIG)
        acc_s[...] = jnp.zeros_like(acc_s)

        def softmax_pv(c, s, vt):
            rows = band_rows(c)
            m_old = m_s[rows, :]
            m_new = jnp.maximum(m_old, jnp.max(s, axis=-1, keepdims=True))
            alpha = jnp.exp2(m_old - m_new)
            p = jnp.exp2(s - m_new[:, 0:1]).astype(vaug_s.dtype)
            pv = jnp.dot(p, vt, preferred_element_type=F32)
            acc_s[rows, :] = jnp.concatenate([alpha, alpha], axis=1) * acc_s[rows, :] + pv
            m_s[rows, :] = m_new

        def tile(kt, carry):
            k, vt = full_operands(kt)
            for c in range(n_bands):
                softmax_pv(c, band_scores(c, k), vt)
            return carry

        lax.fori_loop(0, qi, tile, 0)
        for c in range(n_bands):
            k, vt, ok = diagonal_operands(c)
            softmax_pv(c, jnp.where(ok, band_scores(c, k), NEG_BIG), vt)

    lam = (jnp.exp(jnp.sum(lq1_ref[...] * lk1_ref[...], axis=-1, keepdims=True))
           - jnp.exp(jnp.sum(lq2_ref[...] * lk2_ref[...], axis=-1, keepdims=True)) + lambda_init)
    scale = subw_ref[...] * (1.0 - lambda_init)
    for c in range(n_bands):
        acc = acc_s[c * 2 * band:(c + 1) * 2 * band, :]
        o_all = acc[:, :DA_VDIM] / acc[:, DA_VDIM:]
        o = o_all[:band] - lam * o_all[band:]
        ms = jnp.mean(o * o, axis=-1, keepdims=True)
        o = o * lax.rsqrt(ms + EPS) * scale
        rows = slice(c * band, (c + 1) * band)
        o_ref[rows, :] = (o * gb_ref[rows, :].astype(F32)).astype(o_ref.dtype)


def _attention(attn_act, gate_act, lq1, lk1, lq2, lk2, subw, batch, seq, lambda_init):
    tq = ATT_TQ
    assert seq % ATT_TK == 0 and ATT_TK == tq and tq % ATT_BAND == 0 and ATT_BAND % CHUNK == 0
    q_tiles = seq // tq

    def head_spec(rows_blk, off):
        assert off % DA_VDIM == 0
        if rows_blk == seq:
            return pl.BlockSpec((seq, DA_VDIM), lambda b, hd, qi: (b, off // DA_VDIM + hd))
        return pl.BlockSpec((rows_blk, DA_VDIM), lambda b, hd, qi: (b * q_tiles + qi, off // DA_VDIM + hd))

    vec = lambda n: pl.BlockSpec((1, n), lambda b, hd, qi: (0, 0))
    kern = functools.partial(_attn_kernel, lambda_init=lambda_init)
    return pl.pallas_call(
        kern,
        grid=(batch, DA_HEADS, q_tiles),
        in_specs=[
            head_spec(tq, AT_Q),
            head_spec(seq, AT_K),
            head_spec(seq, AT_V),
            head_spec(tq, GA_GB),
            vec(DA_HEAD_DIM), vec(DA_HEAD_DIM), vec(DA_HEAD_DIM), vec(DA_HEAD_DIM),
            vec(DA_VDIM),
        ],
        out_specs=pl.BlockSpec((tq, DA_VDIM), lambda b, hd, qi: (b * q_tiles + qi, hd)),
        out_shape=jax.ShapeDtypeStruct((batch * seq, DA_WIDTH), BF16),
        scratch_shapes=[
            pltpu.VMEM((seq, 2 * DA_VDIM), BF16),
            pltpu.VMEM((2 * tq, LANES), F32),
            pltpu.VMEM((2 * tq, 2 * DA_VDIM), F32),
            pltpu.SMEM((1,), F32),
        ],
        compiler_params=pltpu.CompilerParams(
            dimension_semantics=("arbitrary", "arbitrary", "arbitrary"),
            vmem_limit_bytes=VMEM_LIMIT_BYTES),
        name="diffattn",
    )(attn_act, attn_act, attn_act, gate_act, lq1, lk1, lq2, lk2, subw)


def _out_kernel(x_ref, ya_ref, ob_ref, g0_ref, g1_ref, wa_ref, wb_ref, wo_ref, lng_ref, lnb_ref,
                out_ref, *, alpha):
    d = functools.partial(jnp.dot, preferred_element_type=F32)
    branch_a = d(ya_ref[...], wa_ref[...])
    branch_b = d(ob_ref[...], wb_ref[...])
    merged = g0_ref[...].astype(F32) * branch_a + g1_ref[...].astype(F32) * branch_b
    y = d(merged.astype(BF16), wo_ref[...])
    r = alpha * x_ref[...] + y
    mu = jnp.mean(r, axis=-1, keepdims=True)
    rc = r - mu
    var = jnp.mean(rc * rc, axis=-1, keepdims=True)
    out_ref[...] = rc * lax.rsqrt(var + EPS) * lng_ref[...] + lnb_ref[...]


def _output_stage(x2d, y_ssd, o_att, gate_act, w_a, w_b, w_o, ln_g, ln_b, alpha):
    t_rows = x2d.shape[0]
    tm = OUT_TM
    assert t_rows % tm == 0 and GA_GM % D_MODEL == 0
    row = lambda width: pl.BlockSpec((tm, width), lambda i: (i, 0))
    full = lambda shape: pl.BlockSpec(shape, lambda i: (0, 0))
    gate_spec = lambda k: pl.BlockSpec((tm, D_MODEL), lambda i: (i, GA_GM // D_MODEL + k))
    return pl.pallas_call(
        functools.partial(_out_kernel, alpha=alpha),
        grid=(t_rows // tm,),
        in_specs=[row(D_MODEL), row(SSD_WIDTH), row(DA_WIDTH), gate_spec(0), gate_spec(1),
                  full((SSD_WIDTH, D_MODEL)), full((DA_WIDTH, D_MODEL)), full((D_MODEL, D_MODEL)),
                  full((1, D_MODEL)), full((1, D_MODEL))],
        out_specs=row(D_MODEL),
        out_shape=jax.ShapeDtypeStruct((t_rows, D_MODEL), F32),
        compiler_params=pltpu.CompilerParams(
            dimension_semantics=("arbitrary",),
            vmem_limit_bytes=VMEM_LIMIT_BYTES),
        name="outproj",
    )(x2d, y_ssd, o_att, gate_act, gate_act, w_a, w_b, w_o, ln_g, ln_b)


def _rope_tables(seq):
    pos = np.arange(seq, dtype=np.float64)
    inv_freq = ROPE_THETA ** (-np.arange(0, ROT_DIM, 2, dtype=np.float64) / ROT_DIM)
    ang = pos[:, None] * inv_freq[None, :]
    cos, sin = np.cos(ang).astype(np.float32), np.sin(ang).astype(np.float32)
    ones = np.ones((seq, LANES // 2 - ROT_DIM), np.float32)
    zeros = np.zeros_like(ones)
    cos_t = np.concatenate([cos, cos, ones, cos, cos, ones], axis=1)
    sin_t = np.concatenate([-sin, -sin, zeros, sin, sin, zeros], axis=1)
    return jnp.asarray(cos_t), jnp.asarray(sin_t)


def _head_lane_permutation():
    src = np.zeros(LANES, np.int64)
    for m in range(2):
        for d in range(DA_HEAD_DIM):
            if d < ROT_HALF:
                lane = m * ROT_HALF + d
            elif d < ROT_DIM:
                lane = LANES // 2 + m * ROT_HALF + (d - ROT_HALF)
            elif d < ROT_DIM + ROT_PASS:
                lane = ROT_DIM + m * ROT_PASS + (d - ROT_DIM)
            else:
                lane = LANES // 2 + ROT_DIM + m * ROT_PASS + (d - ROT_DIM - ROT_PASS)
            src[lane] = m * DA_HEAD_DIM + d
    perm = np.zeros((DA_WIDTH, DA_WIDTH), np.float32)
    for hd in range(DA_HEADS):
        perm[hd * LANES + src, hd * LANES + np.arange(LANES)] = 1.0
    return jnp.asarray(perm, BF16)


def _permute_head_lanes(w_bf16, perm):
    return jnp.dot(w_bf16, perm, preferred_element_type=BF16)


def _layer(x2d, batch, seq, depth_total, l, w_in, b_gate, conv_w, conv_b, dt_bias, a_log, d_skip,
           ssd_norm_w, lambda_q1, lambda_k1, lambda_q2, lambda_k2, subln_w, w_a, w_b, w_o, ln_g, ln_b,
           tables):
    alpha = (2.0 * depth_total) ** 0.25
    lambda_init = 0.8 - 0.6 * math.exp(-0.3 * l)
    offs = [0]
    for s in IN_SIZES:
        offs.append(offs[-1] + s)
    z_w, xbc_w, dt_w, q_w, k_w, v_w, gb_w, gm_w = [w_in[:, offs[n]:offs[n + 1]] for n in range(len(IN_SIZES))]

    w_gate = jnp.concatenate([z_w, gb_w, gm_w], axis=1).astype(BF16)
    perm = _head_lane_permutation()
    w_attn = jnp.concatenate([_permute_head_lanes(q_w.astype(BF16), perm),
                              _permute_head_lanes(k_w.astype(BF16), perm), v_w.astype(BF16)], axis=1)
    p_conv = jnp.concatenate([conv_w.astype(F32), conv_b[None, :].astype(F32),
                              jnp.zeros((SUBLANES - CONV_WIDTH - 1, CONV_CH), F32)], axis=0)

    gate_act = _proj_gate(x2d, w_gate, b_gate[None, :].astype(F32))
    attn_act = _proj_attn(x2d, w_attn, *tables, seq)
    conv_act, dt = _proj_conv(x2d, xbc_w.astype(BF16), p_conv, dt_w.astype(BF16),
                              dt_bias[None, :].astype(F32), seq)

    a_row = -jnp.exp(a_log.astype(F32))[None, :]
    dskip_row = jnp.repeat(d_skip.astype(F32), SSD_HEAD_DIM)[None, :]
    y_ssd = _ssd(conv_act, gate_act, dt, a_row, dskip_row, ssd_norm_w[None, :].astype(F32), batch, seq)

    o_att = _attention(attn_act, gate_act, lambda_q1[None, :], lambda_k1[None, :], lambda_q2[None, :],
                       lambda_k2[None, :], subln_w[None, :], batch, seq, lambda_init)

    return _output_stage(x2d, y_ssd, o_att, gate_act, w_a.astype(BF16), w_b.astype(BF16), w_o.astype(BF16),
                         ln_g[None, :], ln_b[None, :], alpha)


def kernel(x, w_in, b_gate, conv_w, conv_b, dt_bias, a_log, d_skip, ssd_norm_w, lambda_q1, lambda_k1,
           lambda_q2, lambda_k2, subln_w, w_a, w_b, w_o, ln_g, ln_b):
    batch, seq, _ = x.shape
    depth = w_in.shape[0]
    tables = _rope_tables(seq)
    x2d = x.reshape(batch * seq, D_MODEL)
    for l in range(depth):
        x2d = _layer(x2d, batch, seq, depth, l, w_in[l], b_gate[l], conv_w[l], conv_b[l], dt_bias[l],
                     a_log[l], d_skip[l], ssd_norm_w[l], lambda_q1[l], lambda_k1[l], lambda_q2[l],
                     lambda_k2[l], subln_w[l], w_a[l], w_b[l], w_o[l], ln_g[l], ln_b[l], tables)
    return x2d.reshape(batch, seq, D_MODEL)
```

```python
import functools
import math

import jax
import jax.numpy as jnp
import numpy as np
from jax import lax
from jax.experimental import pallas as pl
from jax.experimental.pallas import tpu as pltpu

F32 = jnp.float32
BF16 = jnp.bfloat16

D_MODEL = 1024
CHUNK = 64
SSD_HEADS = 16
SSD_HEAD_DIM = 64
SSD_WIDTH = SSD_HEADS * SSD_HEAD_DIM
SSD_GROUPS = 2
SSD_STATE = 128
CONV_WIDTH = 4
BC_WIDTH = SSD_GROUPS * SSD_STATE
CONV_CH = SSD_WIDTH + 2 * BC_WIDTH
DA_HEADS = 8
DA_HEAD_DIM = 64
DA_VDIM = 2 * DA_HEAD_DIM
DA_WIDTH = DA_HEADS * DA_VDIM
ROPE_THETA = 500000.0
ROT_DIM = DA_HEAD_DIM // 4
ROT_HALF = ROT_DIM // 2
N_BRANCH = 2
EPS = 1e-5
IN_SIZES = (SSD_WIDTH, CONV_CH, SSD_HEADS, DA_WIDTH, DA_WIDTH, DA_WIDTH, DA_WIDTH, N_BRANCH * D_MODEL)

GA_SZ, GA_GB, GA_GM = 0, SSD_WIDTH, SSD_WIDTH + DA_WIDTH
GA_COLS = GA_GM + N_BRANCH * D_MODEL
AT_Q, AT_K, AT_V = 0, DA_WIDTH, 2 * DA_WIDTH
AT_COLS = 3 * DA_WIDTH
CV_XS, CV_B, CV_C = 0, SSD_WIDTH, SSD_WIDTH + BC_WIDTH

ROT_PASS = (DA_HEAD_DIM - ROT_DIM) // 2

LANES = 128
SUBLANES = 8
BF16_ROWS = 16
VMEM_LIMIT_BYTES = 56 * 1024 * 1024

IN_TM = 512
SSD_ROWS = 512
ATT_TQ = 1024
ATT_TK = 1024
ATT_BAND = 256
ATT_SAFE_LOG2 = 64.0
OUT_TM = 512

NEG_BIG = -1e30
LOG2E = 1.4426950408889634


def _sigmoid(t):
    return 0.5 * jnp.tanh(0.5 * t) + 0.5


def _split2(v):
    hi = v.astype(BF16)
    mid = (v - hi.astype(F32)).astype(BF16)
    return hi, mid


def _dot01_right(v, m01):
    hi, mid = _split2(v)
    d = functools.partial(jnp.dot, preferred_element_type=F32)
    return d(hi, m01) + d(mid, m01)


def _dot01_left(m01, v):
    hi, mid = _split2(v)
    d = functools.partial(jnp.dot, preferred_element_type=F32)
    return d(m01, hi) + d(m01, mid)


def _proj_params():
    return pltpu.CompilerParams(dimension_semantics=("arbitrary",),
                                vmem_limit_bytes=VMEM_LIMIT_BYTES)


def _proj_gate_kernel(x_ref, w_ref, bias_ref, o_ref):
    acc = jnp.dot(x_ref[...].astype(BF16), w_ref[...], preferred_element_type=F32)
    t = acc[:, :GA_GM]
    o_ref[:, :GA_GM] = (t * _sigmoid(t)).astype(o_ref.dtype)
    o_ref[:, GA_GM:] = _sigmoid(acc[:, GA_GM:] + bias_ref[...]).astype(o_ref.dtype)


def _proj_gate(x2d, w, b_gate_row):
    t_rows = x2d.shape[0]
    tm = IN_TM
    assert t_rows % tm == 0
    full = lambda shape: pl.BlockSpec(shape, lambda i: (0, 0))
    return pl.pallas_call(
        _proj_gate_kernel,
        grid=(t_rows // tm,),
        in_specs=[
            pl.BlockSpec((tm, D_MODEL), lambda i: (i, 0)),
            full((D_MODEL, GA_COLS)),
            full((1, N_BRANCH * D_MODEL)),
        ],
        out_specs=pl.BlockSpec((tm, GA_COLS), lambda i: (i, 0)),
        out_shape=jax.ShapeDtypeStruct((t_rows, GA_COLS), BF16),
        compiler_params=_proj_params(),
        name="proj_gate",
    )(x2d, w, b_gate_row)


def _proj_attn_kernel(x_ref, w_ref, cos_ref, sin_ref, o_ref, *, q_scale):
    acc = jnp.dot(x_ref[...].astype(BF16), w_ref[...], preferred_element_type=F32)
    cos = cos_ref[...]
    sin = sin_ref[...]
    cos_q = cos * q_scale
    sin_q = sin * q_scale
    for hd in range(AT_V // LANES):
        sl = slice(hd * LANES, (hd + 1) * LANES)
        a = acc[:, sl]
        partner = pltpu.roll(a, LANES // 2, axis=1)
        c, s = (cos_q, sin_q) if hd < AT_K // LANES else (cos, sin)
        o_ref[:, sl] = (a * c + partner * s).astype(o_ref.dtype)
    o_ref[:, AT_V:] = acc[:, AT_V:].astype(o_ref.dtype)


def _proj_attn(x2d, w, cos_t, sin_t, seq):
    t_rows = x2d.shape[0]
    tm = IN_TM
    assert t_rows % tm == 0 and seq % tm == 0
    tiles_per_seq = seq // tm
    table = lambda: pl.BlockSpec((tm, LANES), lambda i: (i % tiles_per_seq, 0))
    return pl.pallas_call(
        functools.partial(_proj_attn_kernel, q_scale=DA_HEAD_DIM ** -0.5 * LOG2E),
        grid=(t_rows // tm,),
        in_specs=[
            pl.BlockSpec((tm, D_MODEL), lambda i: (i, 0)),
            pl.BlockSpec((D_MODEL, AT_COLS), lambda i: (0, 0)),
            table(), table(),
        ],
        out_specs=pl.BlockSpec((tm, AT_COLS), lambda i: (i, 0)),
        out_shape=jax.ShapeDtypeStruct((t_rows, AT_COLS), BF16),
        compiler_params=_proj_params(),
        name="proj_attn",
    )(x2d, w, cos_t, sin_t)


def _proj_conv_kernel(x_ref, xh_ref, w_ref, p_ref, wdt_ref, dtb_ref, o_ref, dt_ref, *, tiles_per_seq):
    i = pl.program_id(0)
    tm = o_ref.shape[0]
    xb = x_ref[...].astype(BF16)
    w = w_ref[...]
    acc = jnp.dot(xb, w, preferred_element_type=F32)
    halo = jnp.dot(xh_ref[...].astype(BF16), w, preferred_element_type=F32)
    halo = jnp.where(i % tiles_per_seq == 0, 0.0, halo)

    bias = p_ref[CONV_WIDTH:CONV_WIDTH + 1, :]
    taps = [p_ref[k:k + 1, :] for k in range(CONV_WIDTH)]

    def conv(rows_val):
        out = taps[0] * rows_val
        for k in range(1, CONV_WIDTH):
            out = pltpu.roll(out, 1, axis=0) + taps[k] * rows_val
        out = out + bias
        return out * _sigmoid(out)

    head = BF16_ROWS
    ext = jnp.concatenate([halo, acc[:head]], axis=0)
    o_ref[:head, :] = conv(ext)[SUBLANES:, :].astype(o_ref.dtype)
    o_ref[head:, :] = conv(acc)[head:, :].astype(o_ref.dtype)

    raw = jnp.dot(xb, wdt_ref[...], preferred_element_type=F32) + dtb_ref[...]
    dt_ref[...] = jnp.maximum(raw, 0.0) + jnp.log(1.0 + jnp.exp(-jnp.abs(raw)))


def _proj_conv(x2d, w, params, w_dt, dt_bias, seq):
    t_rows = x2d.shape[0]
    tm = IN_TM
    assert t_rows % tm == 0 and seq % tm == 0
    tiles_per_seq = seq // tm
    halo_blocks = tm // SUBLANES
    full = lambda shape: pl.BlockSpec(shape, lambda i: (0, 0))
    return pl.pallas_call(
        functools.partial(_proj_conv_kernel, tiles_per_seq=tiles_per_seq),
        grid=(t_rows // tm,),
        in_specs=[
            pl.BlockSpec((tm, D_MODEL), lambda i: (i, 0)),
            pl.BlockSpec((SUBLANES, D_MODEL), lambda i: (jnp.maximum(i * halo_blocks - 1, 0), 0)),
            full((D_MODEL, CONV_CH)),
            full((SUBLANES, CONV_CH)),
            full((D_MODEL, SSD_HEADS)),
            full((1, SSD_HEADS)),
        ],
        out_specs=[
            pl.BlockSpec((tm, CONV_CH), lambda i: (i, 0)),
            pl.BlockSpec((tm, SSD_HEADS), lambda i: (i, 0)),
        ],
        out_shape=[
            jax.ShapeDtypeStruct((t_rows, CONV_CH), BF16),
            jax.ShapeDtypeStruct((t_rows, SSD_HEADS), F32),
        ],
        compiler_params=pltpu.CompilerParams(dimension_semantics=("arbitrary",),
                                             vmem_limit_bytes=VMEM_LIMIT_BYTES),
        name="proj_conv",
    )(x2d, x2d, w, params, w_dt, dt_bias)


def _ssd_kernel(xs_ref, b_ref, c_ref, sz_ref, dt_ref, a_ref, dskip_ref, normw_ref,
                expand_ref, bdtril_ref, tile8_ref,
                y_ref, acol_s, dtx_s, ht_s):
    rows = xs_ref.shape[0]
    n_chunks = rows // CHUNK
    gw = SSD_WIDTH // SSD_GROUPS

    @pl.when(pl.program_id(1) == 0)
    def _():
        ht_s[...] = jnp.zeros_like(ht_s)

    dt = dt_ref[...]
    a_cs = _dot01_left(bdtril_ref[...], dt * (a_ref[...] * LOG2E))
    expand = expand_ref[...]
    acol_s[...] = _dot01_right(a_cs, expand)
    dtx_s[...] = _dot01_right(dt, expand)

    qw = 4 * SSD_HEAD_DIM
    quads_per_group = gw // qw
    lane = lax.broadcasted_iota(jnp.int32, (CHUNK, qw), 1)
    row = lax.broadcasted_iota(jnp.int32, (CHUNK, qw), 0)
    s_of_lane = lane & (CHUNK - 1)
    diag_mask = s_of_lane == row
    causal_mask = s_of_lane <= row
    bd_r = lax.broadcasted_iota(jnp.int32, (4 * CHUNK, qw), 0) // CHUNK
    bd_c = lax.broadcasted_iota(jnp.int32, (4 * CHUNK, qw), 1) // SSD_HEAD_DIM
    bd_mask = bd_r == bd_c
    tile4 = tile8_ref[:, :qw]

    def chunk_body(c, carry):
        r0 = pl.multiple_of(c * CHUNK, CHUNK)
        rows_c = pl.ds(r0, CHUNK)
        bc = b_ref[rows_c, :]
        cc = c_ref[rows_c, :]
        for g in range(SSD_GROUPS):
            cg = cc[:, g * SSD_STATE:(g + 1) * SSD_STATE]
            bg = bc[:, g * SSD_STATE:(g + 1) * SSD_STATE]
            cb_g = lax.dot_general(cg, bg, (((1,), (1,)), ((), ())), preferred_element_type=F32)
            cb_rep = _dot01_right(cb_g, tile4)
            gated = []
            for qd in range(quads_per_group):
                sl = slice(g * gw + qd * qw, g * gw + (qd + 1) * qw)
                acol = acol_s[rows_c, sl]
                a_last = acol[CHUNK - 1:CHUNK, :]
                xs = xs_ref[rows_c, sl].astype(F32)
                xdt_f = xs * dtx_s[rows_c, sl]

                arow = jnp.sum(jnp.where(diag_mask, acol, 0.0), axis=0, keepdims=True)
                decay_ls = jnp.exp2(jnp.where(causal_mask, acol - arow, NEG_BIG))
                w_ls = (cb_rep * decay_ls).astype(BF16)
                xq = xdt_f.astype(BF16)
                bd = jnp.where(bd_mask, jnp.concatenate([xq] * 4, axis=0), jnp.zeros((), BF16))
                y_diag = jnp.dot(w_ls, bd, preferred_element_type=F32)

                ht = ht_s[:, sl]
                y_off = jnp.dot(cg, ht.astype(BF16), preferred_element_type=F32)
                y = y_diag + y_off * jnp.exp2(acol) + dskip_ref[:, sl] * xs
                gated.append(y * sz_ref[rows_c, sl].astype(F32))

                xd = (xdt_f * jnp.exp2(a_last - acol)).astype(BF16)
                st = lax.dot_general(bg, xd, (((0,), (0,)), ((), ())), preferred_element_type=F32)
                ht_s[:, sl] = ht * jnp.exp2(a_last) + st

            ssq = sum(jnp.sum(t * t, axis=-1, keepdims=True) for t in gated)
            inv = lax.rsqrt(ssq * (1.0 / gw) + EPS)
            for qd in range(quads_per_group):
                sl = slice(g * gw + qd * qw, g * gw + (qd + 1) * qw)
                y_ref[rows_c, sl] = (gated[qd] * inv * normw_ref[:, sl]).astype(y_ref.dtype)
        return carry

    lax.fori_loop(0, n_chunks, chunk_body, 0, unroll=4)


def _ssd(conv_act, gate_act, dt, a_row, dskip_row, normw_row, batch, seq):
    rows = SSD_ROWS
    assert seq % rows == 0 and rows % (4 * CHUNK) == 0
    blocks_per_seq = seq // rows
    hh = np.arange(SSD_WIDTH) // SSD_HEAD_DIM
    expand = jnp.asarray(np.arange(SSD_HEADS)[:, None] == hh[None, :], BF16)
    rr = np.arange(rows)
    same_chunk = (rr[:, None] // CHUNK) == (rr[None, :] // CHUNK)
    bdtril = jnp.asarray(same_chunk & (rr[None, :] <= rr[:, None]), BF16)
    gw = SSD_WIDTH // SSD_GROUPS
    tile8 = jnp.asarray(np.arange(CHUNK)[:, None] == (np.arange(gw)[None, :] % CHUNK), BF16)

    def row_map(b, r):
        return b * blocks_per_seq + r

    def col_spec(width, off):
        assert off % width == 0
        return pl.BlockSpec((rows, width), lambda b, r: (row_map(b, r), off // width))

    const = lambda shape: pl.BlockSpec(shape, lambda b, r: (0, 0))
    return pl.pallas_call(
        _ssd_kernel,
        grid=(batch, blocks_per_seq),
        in_specs=[
            col_spec(SSD_WIDTH, CV_XS),
            col_spec(BC_WIDTH, CV_B),
            col_spec(BC_WIDTH, CV_C),
            col_spec(SSD_WIDTH, GA_SZ),
            pl.BlockSpec((rows, SSD_HEADS), lambda b, r: (row_map(b, r), 0)),
            const((1, SSD_HEADS)),
            const((1, SSD_WIDTH)),
            const((1, SSD_WIDTH)),
            const((SSD_HEADS, SSD_WIDTH)),
            const((rows, rows)),
            const((CHUNK, gw)),
        ],
        out_specs=pl.BlockSpec((rows, SSD_WIDTH), lambda b, r: (row_map(b, r), 0)),
        out_shape=jax.ShapeDtypeStruct((batch * seq, SSD_WIDTH), BF16),
        scratch_shapes=[
            pltpu.VMEM((rows, SSD_WIDTH), F32),
            pltpu.VMEM((rows, SSD_WIDTH), F32),
            pltpu.VMEM((SSD_STATE, SSD_WIDTH), F32),
        ],
        compiler_params=pltpu.CompilerParams(
            dimension_semantics=("arbitrary", "arbitrary"),
            vmem_limit_bytes=VMEM_LIMIT_BYTES),
        name="ssd",
    )(conv_act, conv_act, conv_act, gate_act, dt, a_row, dskip_row, normw_row, expand, bdtril, tile8)


def _attn_kernel(q_ref, k_ref, v_ref, gb_ref, lq1_ref, lk1_ref, lq2_ref, lk2_ref, subw_ref,
                 o_ref, vaug_s, m_s, acc_s, knorm_s, *, lambda_init):
    qi = pl.program_id(2)
    tq = q_ref.shape[0]
    tk = ATT_TK
    seq = k_ref.shape[0]

    @pl.when(qi == 0)
    def _():
        vaug_s[:, :DA_VDIM] = v_ref[...].astype(vaug_s.dtype)
        vaug_s[:, DA_VDIM:] = jnp.ones((seq, DA_VDIM), vaug_s.dtype)
        kf = k_ref[...].astype(F32)
        knorm_s[0] = jnp.max(jnp.sum(kf * kf, axis=-1, keepdims=True))

    assert tq == tk
    band = ATT_BAND
    n_bands = tq // band
    half_lane = lax.broadcasted_iota(jnp.int32, (band, DA_VDIM), 1) & (LANES // 2 - 1)
    in_map0 = (half_lane < ROT_HALF) | ((half_lane >= ROT_DIM) & (half_lane < ROT_DIM + ROT_PASS))
    zero = jnp.zeros((), q_ref.dtype)

    qq = []
    for c in range(n_bands):
        q = q_ref[c * band:(c + 1) * band, :]
        qq.append(jnp.concatenate([jnp.where(in_map0, q, zero), jnp.where(in_map0, zero, q)], axis=0))

    def band_scores(c, k):
        return lax.dot_general(qq[c], k, (((1,), (1,)), ((), ())), preferred_element_type=F32)

    def band_rows(c):
        return slice(c * 2 * band, (c + 1) * 2 * band)

    def diagonal_operands(c):
        nk = (c + 1) * band
        k0 = pl.multiple_of(qi * tk, tk)
        r_chunk = (c * band + lax.broadcasted_iota(jnp.int32, (band, nk), 0)) // CHUNK
        k_chunk = lax.broadcasted_iota(jnp.int32, (band, nk), 1) // CHUNK
        ok = k_chunk <= r_chunk
        return k_ref[pl.ds(k0, nk), :], vaug_s[pl.ds(k0, nk), :], jnp.concatenate([ok, ok], axis=0)

    def full_operands(kt):
        k0 = pl.multiple_of(kt * tk, tk)
        return k_ref[pl.ds(k0, tk), :], vaug_s[pl.ds(k0, tk), :]

    qf = q_ref[...].astype(F32)
    q_norm2 = jnp.max(jnp.sum(qf * qf, axis=-1, keepdims=True))
    unshifted_ok = q_norm2 * knorm_s[0] <= ATT_SAFE_LOG2 * ATT_SAFE_LOG2

    @pl.when(unshifted_ok)
    def _():
        def diagonal_init():
            for c in range(n_bands):
                k, vt, ok = diagonal_operands(c)
                p = jnp.exp2(jnp.where(ok, band_scores(c, k), NEG_BIG)).astype(vaug_s.dtype)
                acc_s[band_rows(c), :] = jnp.dot(p, vt, preferred_element_type=F32)

        def tile(kt):
            k, vt = full_operands(kt)
            for c in range(n_bands):
                p = jnp.exp2(band_scores(c, k)).astype(vaug_s.dtype)
                acc_s[band_rows(c), :] += jnp.dot(p, vt, preferred_element_type=F32)

        lone = qi & 1

        @pl.when(lone == 1)
        def _():
            diagonal_init()
            tile(0)

        @pl.when(lone == 0)
        def _():
            diagonal_init()

        def pair(i, carry):
            tile(lone + 2 * i)
            tile(lone + 2 * i + 1)
            return carry

        lax.fori_loop(0, lax.shift_right_logical(qi, 1), pair, 0)

    @pl.when(jnp.logical_not(unshifted_ok))
    def _():
        m_s[...] = jnp.full_like(m_s, NEG_BIG)
        acc_s[...] = jnp.zeros_like(acc_s)

        def softmax_pv(c, s, vt):
            rows = band_rows(c)
            m_old = m_s[rows, :]
            m_new = jnp.maximum(m_old, jnp.max(s, axis=-1, keepdims=True))
            alpha = jnp.exp2(m_old - m_new)
            p = jnp.exp2(s - m_new[:, 0:1]).astype(vaug_s.dtype)
            pv = jnp.dot(p, vt, preferred_element_type=F32)
            acc_s[rows, :] = jnp.concatenate([alpha, alpha], axis=1) * acc_s[rows, :] + pv
            m_s[rows, :] = m_new

        def tile(kt, carry):
            k, vt = full_operands(kt)
            for c in range(n_bands):
                softmax_pv(c, band_scores(c, k), vt)
            return carry

        lax.fori_loop(0, qi, tile, 0)
        for c in range(n_bands):
            k, vt, ok = diagonal_operands(c)
            softmax_pv(c, jnp.where(ok, band_scores(c, k), NEG_BIG), vt)

    lam = (jnp.exp(jnp.sum(lq1_ref[...] * lk1_ref[...], axis=-1, keepdims=True))
           - jnp.exp(jnp.sum(lq2_ref[...] * lk2_ref[...], axis=-1, keepdims=True)) + lambda_init)
    scale = subw_ref[...] * (1.0 - lambda_init)
    for c in range(n_bands):
        acc = acc_s[c * 2 * band:(c + 1) * 2 * band, :]
        o_all = acc[:, :DA_VDIM] / acc[:, DA_VDIM:]
        o = o_all[:band] - lam * o_all[band:]
        ms = jnp.mean(o * o, axis=-1, keepdims=True)
        o = o * lax.rsqrt(ms + EPS) * scale
        rows = slice(c * band, (c + 1) * band)
        o_ref[rows, :] = (o * gb_ref[rows, :].astype(F32)).astype(o_ref.dtype)


def _attention(attn_act, gate_act, lq1, lk1, lq2, lk2, subw, batch, seq, lambda_init):
    tq = ATT_TQ
    assert seq % ATT_TK == 0 and ATT_TK == tq and tq % ATT_BAND == 0 and ATT_BAND % CHUNK == 0
    q_tiles = seq // tq

    def head_spec(rows_blk, off):
        assert off % DA_VDIM == 0
        if rows_blk == seq:
            return pl.BlockSpec((seq, DA_VDIM), lambda b, hd, qi: (b, off // DA_VDIM + hd))
        return pl.BlockSpec((rows_blk, DA_VDIM), lambda b, hd, qi: (b * q_tiles + qi, off // DA_VDIM + hd))

    vec = lambda n: pl.BlockSpec((1, n), lambda b, hd, qi: (0, 0))
    kern = functools.partial(_attn_kernel, lambda_init=lambda_init)
    return pl.pallas_call(
        kern,
        grid=(batch, DA_HEADS, q_tiles),
        in_specs=[
            head_spec(tq, AT_Q),
            head_spec(seq, AT_K),
            head_spec(seq, AT_V),
            head_spec(tq, GA_GB),
            vec(DA_HEAD_DIM), vec(DA_HEAD_DIM), vec(DA_HEAD_DIM), vec(DA_HEAD_DIM),
            vec(DA_VDIM),
        ],
        out_specs=pl.BlockSpec((tq, DA_VDIM), lambda b, hd, qi: (b * q_tiles + qi, hd)),
        out_shape=jax.ShapeDtypeStruct((batch * seq, DA_WIDTH), BF16),
        scratch_shapes=[
            pltpu.VMEM((seq, 2 * DA_VDIM), BF16),
            pltpu.VMEM((2 * tq, LANES), F32),
            pltpu.VMEM((2 * tq, 2 * DA_VDIM), F32),
            pltpu.SMEM((1,), F32),
        ],
        compiler_params=pltpu.CompilerParams(
            dimension_semantics=("arbitrary", "arbitrary", "arbitrary"),
            vmem_limit_bytes=VMEM_LIMIT_BYTES),
        name="diffattn",
    )(attn_act, attn_act, attn_act, gate_act, lq1, lk1, lq2, lk2, subw)


def _out_kernel(x_ref, ya_ref, ob_ref, g0_ref, g1_ref, wa_ref, wb_ref, wo_ref, lng_ref, lnb_ref,
                out_ref, *, alpha):
    d = functools.partial(jnp.dot, preferred_element_type=F32)
    branch_a = d(ya_ref[...], wa_ref[...])
    branch_b = d(ob_ref[...], wb_ref[...])
    merged = g0_ref[...].astype(F32) * branch_a + g1_ref[...].astype(F32) * branch_b
    y = d(merged.astype(BF16), wo_ref[...])
    r = alpha * x_ref[...] + y
    mu = jnp.mean(r, axis=-1, keepdims=True)
    rc = r - mu
    var = jnp.mean(rc * rc, axis=-1, keepdims=True)
    out_ref[...] = rc * lax.rsqrt(var + EPS) * lng_ref[...] + lnb_ref[...]


def _output_stage(x2d, y_ssd, o_att, gate_act, w_a, w_b, w_o, ln_g, ln_b, alpha):
    t_rows = x2d.shape[0]
    tm = OUT_TM
    assert t_rows % tm == 0 and GA_GM % D_MODEL == 0
    row = lambda width: pl.BlockSpec((tm, width), lambda i: (i, 0))
    full = lambda shape: pl.BlockSpec(shape, lambda i: (0, 0))
    gate_spec = lambda k: pl.BlockSpec((tm, D_MODEL), lambda i: (i, GA_GM // D_MODEL + k))
    return pl.pallas_call(
        functools.partial(_out_kernel, alpha=alpha),
        grid=(t_rows // tm,),
        in_specs=[row(D_MODEL), row(SSD_WIDTH), row(DA_WIDTH), gate_spec(0), gate_spec(1),
                  full((SSD_WIDTH, D_MODEL)), full((DA_WIDTH, D_MODEL)), full((D_MODEL, D_MODEL)),
                  full((1, D_MODEL)), full((1, D_MODEL))],
        out_specs=row(D_MODEL),
        out_shape=jax.ShapeDtypeStruct((t_rows, D_MODEL), F32),
        compiler_params=pltpu.CompilerParams(
            dimension_semantics=("arbitrary",),
            vmem_limit_bytes=VMEM_LIMIT_BYTES),
        name="outproj",
    )(x2d, y_ssd, o_att, gate_act, gate_act, w_a, w_b, w_o, ln_g, ln_b)


def _rope_tables(seq):
    pos = np.arange(seq, dtype=np.float64)
    inv_freq = ROPE_THETA ** (-np.arange(0, ROT_DIM, 2, dtype=np.float64) / ROT_DIM)
    ang = pos[:, None] * inv_freq[None, :]
    cos, sin = np.cos(ang).astype(np.float32), np.sin(ang).astype(np.float32)
    ones = np.ones((seq, LANES // 2 - ROT_DIM), np.float32)
    zeros = np.zeros_like(ones)
    cos_t = np.concatenate([cos, cos, ones, cos, cos, ones], axis=1)
    sin_t = np.concatenate([-sin, -sin, zeros, sin, sin, zeros], axis=1)
    return jnp.asarray(cos_t), jnp.asarray(sin_t)


def _head_lane_permutation():
    src = np.zeros(LANES, np.int64)
    for m in range(2):
        for d in range(DA_HEAD_DIM):
            if d < ROT_HALF:
                lane = m * ROT_HALF + d
            elif d < ROT_DIM:
                lane = LANES // 2 + m * ROT_HALF + (d - ROT_HALF)
            elif d < ROT_DIM + ROT_PASS:
                lane = ROT_DIM + m * ROT_PASS + (d - ROT_DIM)
            else:
                lane = LANES // 2 + ROT_DIM + m * ROT_PASS + (d - ROT_DIM - ROT_PASS)
            src[lane] = m * DA_HEAD_DIM + d
    perm = np.zeros((DA_WIDTH, DA_WIDTH), np.float32)
    for hd in range(DA_HEADS):
        perm[hd * LANES + src, hd * LANES + np.arange(LANES)] = 1.0
    return jnp.asarray(perm, BF16)


def _permute_head_lanes(w_bf16, perm):
    return jnp.dot(w_bf16, perm, preferred_element_type=BF16)


def _layer(x2d, batch, seq, depth_total, l, w_in, b_gate, conv_w, conv_b, dt_bias, a_log, d_skip,
           ssd_norm_w, lambda_q1, lambda_k1, lambda_q2, lambda_k2, subln_w, w_a, w_b, w_o, ln_g, ln_b,
           tables):
    alpha = (2.0 * depth_total) ** 0.25
    lambda_init = 0.8 - 0.6 * math.exp(-0.3 * l)
    offs = [0]
    for s in IN_SIZES:
        offs.append(offs[-1] + s)
    z_w, xbc_w, dt_w, q_w, k_w, v_w, gb_w, gm_w = [w_in[:, offs[n]:offs[n + 1]] for n in range(len(IN_SIZES))]

    w_gate = jnp.concatenate([z_w, gb_w, gm_w], axis=1).astype(BF16)
    perm = _head_lane_permutation()
    w_attn = jnp.concatenate([_permute_head_lanes(q_w.astype(BF16), perm),
                              _permute_head_lanes(k_w.astype(BF16), perm), v_w.astype(BF16)], axis=1)
    p_conv = jnp.concatenate([conv_w.astype(F32), conv_b[None, :].astype(F32),
                              jnp.zeros((SUBLANES - CONV_WIDTH - 1, CONV_CH), F32)], axis=0)

    gate_act = _proj_gate(x2d, w_gate, b_gate[None, :].astype(F32))
    attn_act = _proj_attn(x2d, w_attn, *tables, seq)
    conv_act, dt = _proj_conv(x2d, xbc_w.astype(BF16), p_conv, dt_w.astype(BF16),
                              dt_bias[None, :].astype(F32), seq)

    a_row = -jnp.exp(a_log.astype(F32))[None, :]
    dskip_row = jnp.repeat(d_skip.astype(F32), SSD_HEAD_DIM)[None, :]
    y_ssd = _ssd(conv_act, gate_act, dt, a_row, dskip_row, ssd_norm_w[None, :].astype(F32), batch, seq)

    o_att = _attention(attn_act, gate_act, lambda_q1[None, :], lambda_k1[None, :], lambda_q2[None, :],
                       lambda_k2[None, :], subln_w[None, :], batch, seq, lambda_init)

    return _output_stage(x2d, y_ssd, o_att, gate_act, w_a.astype(BF16), w_b.astype(BF16), w_o.astype(BF16),
                         ln_g[None, :], ln_b[None, :], alpha)


def kernel(x, w_in, b_gate, conv_w, conv_b, dt_bias, a_log, d_skip, ssd_norm_w, lambda_q1, lambda_k1,
           lambda_q2, lambda_k2, subln_w, w_a, w_b, w_o, ln_g, ln_b):
    batch, seq, _ = x.shape
    depth = w_in.shape[0]
    tables = _rope_tables(seq)
    x2d = x.reshape(batch * seq, D_MODEL)
    for l in range(depth):
        x2d = _layer(x2d, batch, seq, depth, l, w_in[l], b_gate[l], conv_w[l], conv_b[l], dt_bias[l],
                     a_log[l], d_skip[l], ssd_norm_w[l], lambda_q1[l], lambda_k1[l], lambda_q2[l],
                     lambda_k2[l], subln_w[l], w_a[l], w_b[l], w_o[l], ln_g[l], ln_b[l], tables)
    return x2d.reshape(batch, seq, D_MODEL)
```

```python
import functools
import math

import jax
import jax.numpy as jnp
import numpy as np
from jax import lax
from jax.experimental import pallas as pl
from jax.experimental.pallas import tpu as pltpu

F32 = jnp.float32
BF16 = jnp.bfloat16

D_MODEL = 1024
CHUNK = 64
SSD_HEADS = 16
SSD_HEAD_DIM = 64
SSD_WIDTH = SSD_HEADS * SSD_HEAD_DIM
SSD_GROUPS = 2
SSD_STATE = 128
CONV_WIDTH = 4
BC_WIDTH = SSD_GROUPS * SSD_STATE
CONV_CH = SSD_WIDTH + 2 * BC_WIDTH
DA_HEADS = 8
DA_HEAD_DIM = 64
DA_VDIM = 2 * DA_HEAD_DIM
DA_WIDTH = DA_HEADS * DA_VDIM
ROPE_THETA = 500000.0
ROT_DIM = DA_HEAD_DIM // 4
ROT_HALF = ROT_DIM // 2
N_BRANCH = 2
EPS = 1e-5
IN_SIZES = (SSD_WIDTH, CONV_CH, SSD_HEADS, DA_WIDTH, DA_WIDTH, DA_WIDTH, DA_WIDTH, N_BRANCH * D_MODEL)

GA_SZ, GA_GB, GA_GM = 0, SSD_WIDTH, SSD_WIDTH + DA_WIDTH
GA_COLS = GA_GM + N_BRANCH * D_MODEL
AT_Q, AT_K, AT_V = 0, DA_WIDTH, 2 * DA_WIDTH
AT_COLS = 3 * DA_WIDTH
CV_XS, CV_B, CV_C = 0, SSD_WIDTH, SSD_WIDTH + BC_WIDTH

ROT_PASS = (DA_HEAD_DIM - ROT_DIM) // 2

LANES = 128
SUBLANES = 8
BF16_ROWS = 16
VMEM_LIMIT_BYTES = 56 * 1024 * 1024

IN_TM = 512
SSD_ROWS = 512
ATT_TQ = 2048
ATT_TK = 1024
ATT_BAND = 256
ATT_SAFE_LOG2 = 64.0
OUT_TM = 512

NEG_BIG = -1e30
LOG2E = 1.4426950408889634


def _sigmoid(t):
    return 0.5 * jnp.tanh(0.5 * t) + 0.5


def _split2(v):
    hi = v.astype(BF16)
    mid = (v - hi.astype(F32)).astype(BF16)
    return hi, mid


def _dot01_right(v, m01):
    hi, mid = _split2(v)
    d = functools.partial(jnp.dot, preferred_element_type=F32)
    return d(hi, m01) + d(mid, m01)


def _dot01_left(m01, v):
    hi, mid = _split2(v)
    d = functools.partial(jnp.dot, preferred_element_type=F32)
    return d(m01, hi) + d(m01, mid)


def _proj_params():
    return pltpu.CompilerParams(dimension_semantics=("arbitrary",),
                                vmem_limit_bytes=VMEM_LIMIT_BYTES)


def _proj_gate_kernel(x_ref, w_ref, bias_ref, o_ref):
    acc = jnp.dot(x_ref[...].astype(BF16), w_ref[...], preferred_element_type=F32)
    t = acc[:, :GA_GM]
    o_ref[:, :GA_GM] = (t * _sigmoid(t)).astype(o_ref.dtype)
    o_ref[:, GA_GM:] = _sigmoid(acc[:, GA_GM:] + bias_ref[...]).astype(o_ref.dtype)


def _proj_gate(x2d, w, b_gate_row):
    t_rows = x2d.shape[0]
    tm = IN_TM
    assert t_rows % tm == 0
    full = lambda shape: pl.BlockSpec(shape, lambda i: (0, 0))
    return pl.pallas_call(
        _proj_gate_kernel,
        grid=(t_rows // tm,),
        in_specs=[
            pl.BlockSpec((tm, D_MODEL), lambda i: (i, 0)),
            full((D_MODEL, GA_COLS)),
            full((1, N_BRANCH * D_MODEL)),
        ],
        out_specs=pl.BlockSpec((tm, GA_COLS), lambda i: (i, 0)),
        out_shape=jax.ShapeDtypeStruct((t_rows, GA_COLS), BF16),
        compiler_params=_proj_params(),
        name="proj_gate",
    )(x2d, w, b_gate_row)


def _proj_attn_kernel(x_ref, w_ref, cos_ref, sin_ref, o_ref, *, q_scale):
    acc = jnp.dot(x_ref[...].astype(BF16), w_ref[...], preferred_element_type=F32)
    cos = cos_ref[...]
    sin = sin_ref[...]
    cos_q = cos * q_scale
    sin_q = sin * q_scale
    for hd in range(AT_V // LANES):
        sl = slice(hd * LANES, (hd + 1) * LANES)
        a = acc[:, sl]
        partner = pltpu.roll(a, LANES // 2, axis=1)
        c, s = (cos_q, sin_q) if hd < AT_K // LANES else (cos, sin)
        o_ref[:, sl] = (a * c + partner * s).astype(o_ref.dtype)
    o_ref[:, AT_V:] = acc[:, AT_V:].astype(o_ref.dtype)


def _proj_attn(x2d, w, cos_t, sin_t, seq):
    t_rows = x2d.shape[0]
    tm = IN_TM
    assert t_rows % tm == 0 and seq % tm == 0
    tiles_per_seq = seq // tm
    table = lambda: pl.BlockSpec((tm, LANES), lambda i: (i % tiles_per_seq, 0))
    return pl.pallas_call(
        functools.partial(_proj_attn_kernel, q_scale=DA_HEAD_DIM ** -0.5 * LOG2E),
        grid=(t_rows // tm,),
        in_specs=[
            pl.BlockSpec((tm, D_MODEL), lambda i: (i, 0)),
            pl.BlockSpec((D_MODEL, AT_COLS), lambda i: (0, 0)),
            table(), table(),
        ],
        out_specs=pl.BlockSpec((tm, AT_COLS), lambda i: (i, 0)),
        out_shape=jax.ShapeDtypeStruct((t_rows, AT_COLS), BF16),
        compiler_params=_proj_params(),
        name="proj_attn",
    )(x2d, w, cos_t, sin_t)


def _proj_conv_kernel(x_ref, xh_ref, w_ref, p_ref, wdt_ref, dtb_ref, o_ref, dt_ref, *, tiles_per_seq):
    i = pl.program_id(0)
    tm = o_ref.shape[0]
    xb = x_ref[...].astype(BF16)
    w = w_ref[...]
    acc = jnp.dot(xb, w, preferred_element_type=F32)
    halo = jnp.dot(xh_ref[...].astype(BF16), w, preferred_element_type=F32)
    halo = jnp.where(i % tiles_per_seq == 0, 0.0, halo)

    bias = p_ref[CONV_WIDTH:CONV_WIDTH + 1, :]
    taps = [p_ref[k:k + 1, :] for k in range(CONV_WIDTH)]

    def conv(rows_val):
        out = taps[0] * rows_val
        for k in range(1, CONV_WIDTH):
            out = pltpu.roll(out, 1, axis=0) + taps[k] * rows_val
        out = out + bias
        return out * _sigmoid(out)

    head = BF16_ROWS
    ext = jnp.concatenate([halo, acc[:head]], axis=0)
    o_ref[:head, :] = conv(ext)[SUBLANES:, :].astype(o_ref.dtype)
    o_ref[head:, :] = conv(acc)[head:, :].astype(o_ref.dtype)

    raw = jnp.dot(xb, wdt_ref[...], preferred_element_type=F32) + dtb_ref[...]
    dt_ref[...] = jnp.maximum(raw, 0.0) + jnp.log(1.0 + jnp.exp(-jnp.abs(raw)))


def _proj_conv(x2d, w, params, w_dt, dt_bias, seq):
    t_rows = x2d.shape[0]
    tm = IN_TM
    assert t_rows % tm == 0 and seq % tm == 0
    tiles_per_seq = seq // tm
    halo_blocks = tm // SUBLANES
    full = lambda shape: pl.BlockSpec(shape, lambda i: (0, 0))
    return pl.pallas_call(
        functools.partial(_proj_conv_kernel, tiles_per_seq=tiles_per_seq),
        grid=(t_rows // tm,),
        in_specs=[
            pl.BlockSpec((tm, D_MODEL), lambda i: (i, 0)),
            pl.BlockSpec((SUBLANES, D_MODEL), lambda i: (jnp.maximum(i * halo_blocks - 1, 0), 0)),
            full((D_MODEL, CONV_CH)),
            full((SUBLANES, CONV_CH)),
            full((D_MODEL, SSD_HEADS)),
            full((1, SSD_HEADS)),
        ],
        out_specs=[
            pl.BlockSpec((tm, CONV_CH), lambda i: (i, 0)),
            pl.BlockSpec((tm, SSD_HEADS), lambda i: (i, 0)),
        ],
        out_shape=[
            jax.ShapeDtypeStruct((t_rows, CONV_CH), BF16),
            jax.ShapeDtypeStruct((t_rows, SSD_HEADS), F32),
        ],
        compiler_params=pltpu.CompilerParams(dimension_semantics=("arbitrary",),
                                             vmem_limit_bytes=VMEM_LIMIT_BYTES),
        name="proj_conv",
    )(x2d, x2d, w, params, w_dt, dt_bias)


def _ssd_kernel(xs_ref, b_ref, c_ref, sz_ref, dt_ref, a_ref, dskip_ref, normw_ref,
                expand_ref, bdtril_ref, tile8_ref,
                y_ref, acol_s, dtx_s, ht_s):
    rows = xs_ref.shape[0]
    n_chunks = rows // CHUNK
    gw = SSD_WIDTH // SSD_GROUPS

    @pl.when(pl.program_id(1) == 0)
    def _():
        ht_s[...] = jnp.zeros_like(ht_s)

    dt = dt_ref[...]
    a_cs = _dot01_left(bdtril_ref[...], dt * (a_ref[...] * LOG2E))
    expand = expand_ref[...]
    acol_s[...] = _dot01_right(a_cs, expand)
    dtx_s[...] = _dot01_right(dt, expand)

    qw = 4 * SSD_HEAD_DIM
    quads_per_group = gw // qw
    lane = lax.broadcasted_iota(jnp.int32, (CHUNK, qw), 1)
    row = lax.broadcasted_iota(jnp.int32, (CHUNK, qw), 0)
    s_of_lane = lane & (CHUNK - 1)
    diag_mask = s_of_lane == row
    causal_mask = s_of_lane <= row
    bd_r = lax.broadcasted_iota(jnp.int32, (4 * CHUNK, qw), 0) // CHUNK
    bd_c = lax.broadcasted_iota(jnp.int32, (4 * CHUNK, qw), 1) // SSD_HEAD_DIM
    bd_mask = bd_r == bd_c
    tile4 = tile8_ref[:, :qw]

    def chunk_body(c, carry):
        r0 = pl.multiple_of(c * CHUNK, CHUNK)
        rows_c = pl.ds(r0, CHUNK)
        bc = b_ref[rows_c, :]
        cc = c_ref[rows_c, :]
        for g in range(SSD_GROUPS):
            cg = cc[:, g * SSD_STATE:(g + 1) * SSD_STATE]
            bg = bc[:, g * SSD_STATE:(g + 1) * SSD_STATE]
            cb_g = lax.dot_general(cg, bg, (((1,), (1,)), ((), ())), preferred_element_type=F32)
            cb_rep = _dot01_right(cb_g, tile4)
            gated = []
            for qd in range(quads_per_group):
                sl = slice(g * gw + qd * qw, g * gw + (qd + 1) * qw)
                acol = acol_s[rows_c, sl]
                a_last = acol[CHUNK - 1:CHUNK, :]
                xs = xs_ref[rows_c, sl].astype(F32)
                xdt_f = xs * dtx_s[rows_c, sl]

                arow = jnp.sum(jnp.where(diag_mask, acol, 0.0), axis=0, keepdims=True)
                decay_ls = jnp.exp2(jnp.where(causal_mask, acol - arow, NEG_BIG))
                w_ls = (cb_rep * decay_ls).astype(BF16)
                xq = xdt_f.astype(BF16)
                bd = jnp.where(bd_mask, jnp.concatenate([xq] * 4, axis=0), jnp.zeros((), BF16))
                y_diag = jnp.dot(w_ls, bd, preferred_element_type=F32)

                ht = ht_s[:, sl]
                y_off = jnp.dot(cg, ht.astype(BF16), preferred_element_type=F32)
                y = y_diag + y_off * jnp.exp2(acol) + dskip_ref[:, sl] * xs
                gated.append(y * sz_ref[rows_c, sl].astype(F32))

                xd = (xdt_f * jnp.exp2(a_last - acol)).astype(BF16)
                st = lax.dot_general(bg, xd, (((0,), (0,)), ((), ())), preferred_element_type=F32)
                ht_s[:, sl] = ht * jnp.exp2(a_last) + st

            ssq = sum(jnp.sum(t * t, axis=-1, keepdims=True) for t in gated)
            inv = lax.rsqrt(ssq * (1.0 / gw) + EPS)
            for qd in range(quads_per_group):
                sl = slice(g * gw + qd * qw, g * gw + (qd + 1) * qw)
                y_ref[rows_c, sl] = (gated[qd] * inv * normw_ref[:, sl]).astype(y_ref.dtype)
        return carry

    lax.fori_loop(0, n_chunks, chunk_body, 0, unroll=4)


def _ssd(conv_act, gate_act, dt, a_row, dskip_row, normw_row, batch, seq):
    rows = SSD_ROWS
    assert seq % rows == 0 and rows % (4 * CHUNK) == 0
    blocks_per_seq = seq // rows
    hh = np.arange(SSD_WIDTH) // SSD_HEAD_DIM
    expand = jnp.asarray(np.arange(SSD_HEADS)[:, None] == hh[None, :], BF16)
    rr = np.arange(rows)
    same_chunk = (rr[:, None] // CHUNK) == (rr[None, :] // CHUNK)
    bdtril = jnp.asarray(same_chunk & (rr[None, :] <= rr[:, None]), BF16)
    gw = SSD_WIDTH // SSD_GROUPS
    tile8 = jnp.asarray(np.arange(CHUNK)[:, None] == (np.arange(gw)[None, :] % CHUNK), BF16)

    def row_map(b, r):
        return b * blocks_per_seq + r

    def col_spec(width, off):
        assert off % width == 0
        return pl.BlockSpec((rows, width), lambda b, r: (row_map(b, r), off // width))

    const = lambda shape: pl.BlockSpec(shape, lambda b, r: (0, 0))
    return pl.pallas_call(
        _ssd_kernel,
        grid=(batch, blocks_per_seq),
        in_specs=[
            col_spec(SSD_WIDTH, CV_XS),
            col_spec(BC_WIDTH, CV_B),
            col_spec(BC_WIDTH, CV_C),
            col_spec(SSD_WIDTH, GA_SZ),
            pl.BlockSpec((rows, SSD_HEADS), lambda b, r: (row_map(b, r), 0)),
            const((1, SSD_HEADS)),
            const((1, SSD_WIDTH)),
            const((1, SSD_WIDTH)),
            const((SSD_HEADS, SSD_WIDTH)),
            const((rows, rows)),
            const((CHUNK, gw)),
        ],
        out_specs=pl.BlockSpec((rows, SSD_WIDTH), lambda b, r: (row_map(b, r), 0)),
        out_shape=jax.ShapeDtypeStruct((batch * seq, SSD_WIDTH), BF16),
        scratch_shapes=[
            pltpu.VMEM((rows, SSD_WIDTH), F32),
            pltpu.VMEM((rows, SSD_WIDTH), F32),
            pltpu.VMEM((SSD_STATE, SSD_WIDTH), F32),
        ],
        compiler_params=pltpu.CompilerParams(
            dimension_semantics=("arbitrary", "arbitrary"),
            vmem_limit_bytes=VMEM_LIMIT_BYTES),
        name="ssd",
    )(conv_act, conv_act, conv_act, gate_act, dt, a_row, dskip_row, normw_row, expand, bdtril, tile8)


def _attn_kernel(q_ref, k_ref, v_ref, gb_ref, lq1_ref, lk1_ref, lq2_ref, lk2_ref, subw_ref,
                 o_ref, vaug_s, m_s, acc_s, knorm_s, *, lambda_init):
    qi = pl.program_id(2)
    tq = q_ref.shape[0]
    tk = ATT_TK
    seq = k_ref.shape[0]

    @pl.when(qi == 0)
    def _():
        vaug_s[:, :DA_VDIM] = v_ref[...].astype(vaug_s.dtype)
        vaug_s[:, DA_VDIM:] = jnp.ones((seq, DA_VDIM), vaug_s.dtype)
        kf = k_ref[...].astype(F32)
        knorm_s[0] = jnp.max(jnp.sum(kf * kf, axis=-1, keepdims=True))

    assert tq % tk == 0
    band = ATT_BAND
    n_bands = tq // band
    n_full = qi * (tq // tk)
    half_lane = lax.broadcasted_iota(jnp.int32, (band, DA_VDIM), 1) & (LANES // 2 - 1)
    in_map0 = (half_lane < ROT_HALF) | ((half_lane >= ROT_DIM) & (half_lane < ROT_DIM + ROT_PASS))
    zero = jnp.zeros((), q_ref.dtype)

    qq = []
    for c in range(n_bands):
        q = q_ref[c * band:(c + 1) * band, :]
        qq.append(jnp.concatenate([jnp.where(in_map0, q, zero), jnp.where(in_map0, zero, q)], axis=0))

    def band_scores(c, k):
        return lax.dot_general(qq[c], k, (((1,), (1,)), ((), ())), preferred_element_type=F32)

    def band_rows(c):
        return slice(c * 2 * band, (c + 1) * 2 * band)

    def diagonal_operands(c):
        nk = (c + 1) * band
        k0 = pl.multiple_of(qi * tq, tq)
        r_chunk = (c * band + lax.broadcasted_iota(jnp.int32, (band, nk), 0)) // CHUNK
        k_chunk = lax.broadcasted_iota(jnp.int32, (band, nk), 1) // CHUNK
        ok = k_chunk <= r_chunk
        return k_ref[pl.ds(k0, nk), :], vaug_s[pl.ds(k0, nk), :], jnp.concatenate([ok, ok], axis=0)

    def full_operands(kt):
        k0 = pl.multiple_of(kt * tk, tk)
        return k_ref[pl.ds(k0, tk), :], vaug_s[pl.ds(k0, tk), :]

    qf = q_ref[...].astype(F32)
    q_norm2 = jnp.max(jnp.sum(qf * qf, axis=-1, keepdims=True))
    unshifted_ok = q_norm2 * knorm_s[0] <= ATT_SAFE_LOG2 * ATT_SAFE_LOG2

    @pl.when(unshifted_ok)
    def _():
        for c in range(n_bands):
            k, vt, ok = diagonal_operands(c)
            p = jnp.exp2(jnp.where(ok, band_scores(c, k), NEG_BIG)).astype(vaug_s.dtype)
            acc_s[band_rows(c), :] = jnp.dot(p, vt, preferred_element_type=F32)

        def tile(kt, carry):
            k, vt = full_operands(kt)
            for c in range(n_bands):
                p = jnp.exp2(band_scores(c, k)).astype(vaug_s.dtype)
                acc_s[band_rows(c), :] += jnp.dot(p, vt, preferred_element_type=F32)
            return carry

        lax.fori_loop(0, n_full, tile, 0)

    @pl.when(jnp.logical_not(unshifted_ok))
    def _():
        m_s[...] = jnp.full_like(m_s, NEG_BIG)
        acc_s[...] = jnp.zeros_like(acc_s)

        def softmax_pv(c, s, vt):
            rows = band_rows(c)
            m_old = m_s[rows, :]
            m_new = jnp.maximum(m_old, jnp.max(s, axis=-1, keepdims=True))
            alpha = jnp.exp2(m_old - m_new)
            p = jnp.exp2(s - m_new[:, 0:1]).astype(vaug_s.dtype)
            pv = jnp.dot(p, vt, preferred_element_type=F32)
            acc_s[rows, :] = jnp.concatenate([alpha, alpha], axis=1) * acc_s[rows, :] + pv
            m_s[rows, :] = m_new

        def tile(kt, carry):
            k, vt = full_operands(kt)
            for c in range(n_bands):
                softmax_pv(c, band_scores(c, k), vt)
            return carry

        lax.fori_loop(0, n_full, tile, 0)
        for c in range(n_bands):
            k, vt, ok = diagonal_operands(c)
            softmax_pv(c, jnp.where(ok, band_scores(c, k), NEG_BIG), vt)

    lam = (jnp.exp(jnp.sum(lq1_ref[...] * lk1_ref[...], axis=-1, keepdims=True))
           - jnp.exp(jnp.sum(lq2_ref[...] * lk2_ref[...], axis=-1, keepdims=True)) + lambda_init)
    scale = subw_ref[...] * (1.0 - lambda_init)
    for c in range(n_bands):
        acc = acc_s[c * 2 * band:(c + 1) * 2 * band, :]
        o_all = acc[:, :DA_VDIM] / acc[:, DA_VDIM:]
        o = o_all[:band] - lam * o_all[band:]
        ms = jnp.mean(o * o, axis=-1, keepdims=True)
        o = o * lax.rsqrt(ms + EPS) * scale
        rows = slice(c * band, (c + 1) * band)
        o_ref[rows, :] = (o * gb_ref[rows, :].astype(F32)).astype(o_ref.dtype)


def _attention(attn_act, gate_act, lq1, lk1, lq2, lk2, subw, batch, seq, lambda_init):
    tq = ATT_TQ
    assert seq % tq == 0 and tq % ATT_TK == 0 and tq % ATT_BAND == 0 and ATT_BAND % CHUNK == 0
    q_tiles = seq // tq

    def head_spec(rows_blk, off):
        assert off % DA_VDIM == 0
        if rows_blk == seq:
            return pl.BlockSpec((seq, DA_VDIM), lambda b, hd, qi: (b, off // DA_VDIM + hd))
        return pl.BlockSpec((rows_blk, DA_VDIM), lambda b, hd, qi: (b * q_tiles + qi, off // DA_VDIM + hd))

    vec = lambda n: pl.BlockSpec((1, n), lambda b, hd, qi: (0, 0))
    kern = functools.partial(_attn_kernel, lambda_init=lambda_init)
    return pl.pallas_call(
        kern,
        grid=(batch, DA_HEADS, q_tiles),
        in_specs=[
            head_spec(tq, AT_Q),
            head_spec(seq, AT_K),
            head_spec(seq, AT_V),
            head_spec(tq, GA_GB),
            vec(DA_HEAD_DIM), vec(DA_HEAD_DIM), vec(DA_HEAD_DIM), vec(DA_HEAD_DIM),
            vec(DA_VDIM),
        ],
        out_specs=pl.BlockSpec((tq, DA_VDIM), lambda b, hd, qi: (b * q_tiles + qi, hd)),
        out_shape=jax.ShapeDtypeStruct((batch * seq, DA_WIDTH), BF16),
        scratch_shapes=[
            pltpu.VMEM((seq, 2 * DA_VDIM), BF16),
            pltpu.VMEM((2 * tq, LANES), F32),
            pltpu.VMEM((2 * tq, 2 * DA_VDIM), F32),
            pltpu.SMEM((1,), F32),
        ],
        compiler_params=pltpu.CompilerParams(
            dimension_semantics=("arbitrary", "arbitrary", "arbitrary"),
            vmem_limit_bytes=VMEM_LIMIT_BYTES),
        name="diffattn",
    )(attn_act, attn_act, attn_act, gate_act, lq1, lk1, lq2, lk2, subw)


def _out_kernel(x_ref, ya_ref, ob_ref, g0_ref, g1_ref, wa_ref, wb_ref, wo_ref, lng_ref, lnb_ref,
                out_ref, *, alpha):
    d = functools.partial(jnp.dot, preferred_element_type=F32)
    branch_a = d(ya_ref[...], wa_ref[...])
    branch_b = d(ob_ref[...], wb_ref[...])
    merged = g0_ref[...].astype(F32) * branch_a + g1_ref[...].astype(F32) * branch_b
    y = d(merged.astype(BF16), wo_ref[...])
    r = alpha * x_ref[...] + y
    mu = jnp.mean(r, axis=-1, keepdims=True)
    rc = r - mu
    var = jnp.mean(rc * rc, axis=-1, keepdims=True)
    out_ref[...] = rc * lax.rsqrt(var + EPS) * lng_ref[...] + lnb_ref[...]


def _output_stage(x2d, y_ssd, o_att, gate_act, w_a, w_b, w_o, ln_g, ln_b, alpha):
    t_rows = x2d.shape[0]
    tm = OUT_TM
    assert t_rows % tm == 0 and GA_GM % D_MODEL == 0
    row = lambda width: pl.BlockSpec((tm, width), lambda i: (i, 0))
    full = lambda shape: pl.BlockSpec(shape, lambda i: (0, 0))
    gate_spec = lambda k: pl.BlockSpec((tm, D_MODEL), lambda i: (i, GA_GM // D_MODEL + k))
    return pl.pallas_call(
        functools.partial(_out_kernel, alpha=alpha),
        grid=(t_rows // tm,),
        in_specs=[row(D_MODEL), row(SSD_WIDTH), row(DA_WIDTH), gate_spec(0), gate_spec(1),
                  full((SSD_WIDTH, D_MODEL)), full((DA_WIDTH, D_MODEL)), full((D_MODEL, D_MODEL)),
                  full((1, D_MODEL)), full((1, D_MODEL))],
        out_specs=row(D_MODEL),
        out_shape=jax.ShapeDtypeStruct((t_rows, D_MODEL), F32),
        compiler_params=pltpu.CompilerParams(
            dimension_semantics=("arbitrary",),
            vmem_limit_bytes=VMEM_LIMIT_BYTES),
        name="outproj",
    )(x2d, y_ssd, o_att, gate_act, gate_act, w_a, w_b, w_o, ln_g, ln_b)


def _rope_tables(seq):
    pos = np.arange(seq, dtype=np.float64)
    inv_freq = ROPE_THETA ** (-np.arange(0, ROT_DIM, 2, dtype=np.float64) / ROT_DIM)
    ang = pos[:, None] * inv_freq[None, :]
    cos, sin = np.cos(ang).astype(np.float32), np.sin(ang).astype(np.float32)
    ones = np.ones((seq, LANES // 2 - ROT_DIM), np.float32)
    zeros = np.zeros_like(ones)
    cos_t = np.concatenate([cos, cos, ones, cos, cos, ones], axis=1)
    sin_t = np.concatenate([-sin, -sin, zeros, sin, sin, zeros], axis=1)
    return jnp.asarray(cos_t), jnp.asarray(sin_t)


def _head_lane_permutation():
    src = np.zeros(LANES, np.int64)
    for m in range(2):
        for d in range(DA_HEAD_DIM):
            if d < ROT_HALF:
                lane = m * ROT_HALF + d
            elif d < ROT_DIM:
                lane = LANES // 2 + m * ROT_HALF + (d - ROT_HALF)
            elif d < ROT_DIM + ROT_PASS:
                lane = ROT_DIM + m * ROT_PASS + (d - ROT_DIM)
            else:
                lane = LANES // 2 + ROT_DIM + m * ROT_PASS + (d - ROT_DIM - ROT_PASS)
            src[lane] = m * DA_HEAD_DIM + d
    perm = np.zeros((DA_WIDTH, DA_WIDTH), np.float32)
    for hd in range(DA_HEADS):
        perm[hd * LANES + src, hd * LANES + np.arange(LANES)] = 1.0
    return jnp.asarray(perm, BF16)


def _permute_head_lanes(w_bf16, perm):
    return jnp.dot(w_bf16, perm, preferred_element_type=BF16)


def _layer(x2d, batch, seq, depth_total, l, w_in, b_gate, conv_w, conv_b, dt_bias, a_log, d_skip,
           ssd_norm_w, lambda_q1, lambda_k1, lambda_q2, lambda_k2, subln_w, w_a, w_b, w_o, ln_g, ln_b,
           tables):
    alpha = (2.0 * depth_total) ** 0.25
    lambda_init = 0.8 - 0.6 * math.exp(-0.3 * l)
    offs = [0]
    for s in IN_SIZES:
        offs.append(offs[-1] + s)
    z_w, xbc_w, dt_w, q_w, k_w, v_w, gb_w, gm_w = [w_in[:, offs[n]:offs[n + 1]] for n in range(len(IN_SIZES))]

    w_gate = jnp.concatenate([z_w, gb_w, gm_w], axis=1).astype(BF16)
    perm = _head_lane_permutation()
    w_attn = jnp.concatenate([_permute_head_lanes(q_w.astype(BF16), perm),
                              _permute_head_lanes(k_w.astype(BF16), perm), v_w.astype(BF16)], axis=1)
    p_conv = jnp.concatenate([conv_w.astype(F32), conv_b[None, :].astype(F32),
                              jnp.zeros((SUBLANES - CONV_WIDTH - 1, CONV_CH), F32)], axis=0)

    gate_act = _proj_gate(x2d, w_gate, b_gate[None, :].astype(F32))
    attn_act = _proj_attn(x2d, w_attn, *tables, seq)
    conv_act, dt = _proj_conv(x2d, xbc_w.astype(BF16), p_conv, dt_w.astype(BF16),
                              dt_bias[None, :].astype(F32), seq)

    a_row = -jnp.exp(a_log.astype(F32))[None, :]
    dskip_row = jnp.repeat(d_skip.astype(F32), SSD_HEAD_DIM)[None, :]
    y_ssd = _ssd(conv_act, gate_act, dt, a_row, dskip_row, ssd_norm_w[None, :].astype(F32), batch, seq)

    o_att = _attention(attn_act, gate_act, lambda_q1[None, :], lambda_k1[None, :], lambda_q2[None, :],
                       lambda_k2[None, :], subln_w[None, :], batch, seq, lambda_init)

    return _output_stage(x2d, y_ssd, o_att, gate_act, w_a.astype(BF16), w_b.astype(BF16), w_o.astype(BF16),
                         ln_g[None, :], ln_b[None, :], alpha)


def kernel(x, w_in, b_gate, conv_w, conv_b, dt_bias, a_log, d_skip, ssd_norm_w, lambda_q1, lambda_k1,
           lambda_q2, lambda_k2, subln_w, w_a, w_b, w_o, ln_g, ln_b):
    batch, seq, _ = x.shape
    depth = w_in.shape[0]
    tables = _rope_tables(seq)
    x2d = x.reshape(batch * seq, D_MODEL)
    for l in range(depth):
        x2d = _layer(x2d, batch, seq, depth, l, w_in[l], b_gate[l], conv_w[l], conv_b[l], dt_bias[l],
                     a_log[l], d_skip[l], ssd_norm_w[l], lambda_q1[l], lambda_k1[l], lambda_q2[l],
                     lambda_k2[l], subln_w[l], w_a[l], w_b[l], w_o[l], ln_g[l], ln_b[l], tables)
    return x2d.reshape(batch, seq, D_MODEL)
```

```python
import functools
import math

import jax
import jax.numpy as jnp
import numpy as np
from jax import lax
from jax.experimental import pallas as pl
from jax.experimental.pallas import tpu as pltpu

F32 = jnp.float32
BF16 = jnp.bfloat16

D_MODEL = 1024
CHUNK = 64
SSD_HEADS = 16
SSD_HEAD_DIM = 64
SSD_WIDTH = SSD_HEADS * SSD_HEAD_DIM
SSD_GROUPS = 2
SSD_STATE = 128
CONV_WIDTH = 4
BC_WIDTH = SSD_GROUPS * SSD_STATE
CONV_CH = SSD_WIDTH + 2 * BC_WIDTH
DA_HEADS = 8
DA_HEAD_DIM = 64
DA_VDIM = 2 * DA_HEAD_DIM
DA_WIDTH = DA_HEADS * DA_VDIM
ROPE_THETA = 500000.0
ROT_DIM = DA_HEAD_DIM // 4
ROT_HALF = ROT_DIM // 2
N_BRANCH = 2
EPS = 1e-5
IN_SIZES = (SSD_WIDTH, CONV_CH, SSD_HEADS, DA_WIDTH, DA_WIDTH, DA_WIDTH, DA_WIDTH, N_BRANCH * D_MODEL)

GA_SZ, GA_GB, GA_GM = 0, SSD_WIDTH, SSD_WIDTH + DA_WIDTH
GA_COLS = GA_GM + N_BRANCH * D_MODEL
AT_Q, AT_K, AT_V = 0, DA_WIDTH, 2 * DA_WIDTH
AT_COLS = 3 * DA_WIDTH
CV_XS, CV_B, CV_C = 0, SSD_WIDTH, SSD_WIDTH + BC_WIDTH

ROT_PASS = (DA_HEAD_DIM - ROT_DIM) // 2

LANES = 128
SUBLANES = 8
BF16_ROWS = 16
VMEM_LIMIT_BYTES = 56 * 1024 * 1024

IN_TM = 512
IN_TM_WIDE = 1024
SSD_ROWS = 512
ATT_TQ = 2048
ATT_TK = 2048
ATT_BAND = 256
ATT_SAFE_LOG2 = 64.0
OUT_TM = 1024

NEG_BIG = -1e30
LOG2E = 1.4426950408889634


def _sigmoid(t):
    return 0.5 * jnp.tanh(0.5 * t) + 0.5


def _split2(v):
    hi = v.astype(BF16)
    mid = (v - hi.astype(F32)).astype(BF16)
    return hi, mid


def _dot01_right(v, m01):
    hi, mid = _split2(v)
    d = functools.partial(jnp.dot, preferred_element_type=F32)
    return d(hi, m01) + d(mid, m01)


def _dot01_left(m01, v):
    hi, mid = _split2(v)
    d = functools.partial(jnp.dot, preferred_element_type=F32)
    return d(m01, hi) + d(m01, mid)


def _proj_params():
    return pltpu.CompilerParams(dimension_semantics=("arbitrary",),
                                vmem_limit_bytes=VMEM_LIMIT_BYTES)


def _proj_gate_kernel(x_ref, w_ref, bias_ref, o_ref):
    acc = jnp.dot(x_ref[...].astype(BF16), w_ref[...], preferred_element_type=F32)
    t = acc[:, :GA_GM]
    o_ref[:, :GA_GM] = (t * _sigmoid(t)).astype(o_ref.dtype)
    o_ref[:, GA_GM:] = _sigmoid(acc[:, GA_GM:] + bias_ref[...]).astype(o_ref.dtype)


def _proj_gate(x2d, w, b_gate_row):
    t_rows = x2d.shape[0]
    tm = IN_TM_WIDE
    assert t_rows % tm == 0
    full = lambda shape: pl.BlockSpec(shape, lambda i: (0, 0))
    return pl.pallas_call(
        _proj_gate_kernel,
        grid=(t_rows // tm,),
        in_specs=[
            pl.BlockSpec((tm, D_MODEL), lambda i: (i, 0)),
            pl.BlockSpec((D_MODEL, GA_COLS), lambda i: (0, 0), pipeline_mode=pl.Buffered(1)),
            full((1, N_BRANCH * D_MODEL)),
        ],
        out_specs=pl.BlockSpec((tm, GA_COLS), lambda i: (i, 0)),
        out_shape=jax.ShapeDtypeStruct((t_rows, GA_COLS), BF16),
        compiler_params=_proj_params(),
        name="proj_gate",
    )(x2d, w, b_gate_row)


def _proj_attn_kernel(x_ref, w_ref, cos_ref, sin_ref, o_ref, *, q_scale):
    acc = jnp.dot(x_ref[...].astype(BF16), w_ref[...], preferred_element_type=F32)
    cos = cos_ref[...]
    sin = sin_ref[...]
    cos_q = cos * q_scale
    sin_q = sin * q_scale
    for hd in range(AT_V // LANES):
        sl = slice(hd * LANES, (hd + 1) * LANES)
        a = acc[:, sl]
        partner = pltpu.roll(a, LANES // 2, axis=1)
        c, s = (cos_q, sin_q) if hd < AT_K // LANES else (cos, sin)
        o_ref[:, sl] = (a * c + partner * s).astype(o_ref.dtype)
    o_ref[:, AT_V:] = acc[:, AT_V:].astype(o_ref.dtype)


def _proj_attn(x2d, w, cos_t, sin_t, seq):
    t_rows = x2d.shape[0]
    tm = IN_TM_WIDE
    assert t_rows % tm == 0 and seq % tm == 0
    tiles_per_seq = seq // tm
    table = lambda: pl.BlockSpec((tm, LANES), lambda i: (i % tiles_per_seq, 0))
    return pl.pallas_call(
        functools.partial(_proj_attn_kernel, q_scale=DA_HEAD_DIM ** -0.5 * LOG2E),
        grid=(t_rows // tm,),
        in_specs=[
            pl.BlockSpec((tm, D_MODEL), lambda i: (i, 0)),
            pl.BlockSpec((D_MODEL, AT_COLS), lambda i: (0, 0), pipeline_mode=pl.Buffered(1)),
            table(), table(),
        ],
        out_specs=pl.BlockSpec((tm, AT_COLS), lambda i: (i, 0)),
        out_shape=jax.ShapeDtypeStruct((t_rows, AT_COLS), BF16),
        compiler_params=_proj_params(),
        name="proj_attn",
    )(x2d, w, cos_t, sin_t)


def _proj_conv_kernel(x_ref, xh_ref, w_ref, p_ref, wdt_ref, dtb_ref, o_ref, dt_ref, *, tiles_per_seq):
    i = pl.program_id(0)
    tm = o_ref.shape[0]
    xb = x_ref[...].astype(BF16)
    w = w_ref[...]
    acc = jnp.dot(xb, w, preferred_element_type=F32)
    halo = jnp.dot(xh_ref[...].astype(BF16), w, preferred_element_type=F32)
    halo = jnp.where(i % tiles_per_seq == 0, 0.0, halo)

    bias = p_ref[CONV_WIDTH:CONV_WIDTH + 1, :]
    taps = [p_ref[k:k + 1, :] for k in range(CONV_WIDTH)]

    def conv(rows_val):
        out = taps[0] * rows_val
        for k in range(1, CONV_WIDTH):
            out = pltpu.roll(out, 1, axis=0) + taps[k] * rows_val
        out = out + bias
        return out * _sigmoid(out)

    head = BF16_ROWS
    ext = jnp.concatenate([halo, acc[:head]], axis=0)
    o_ref[:head, :] = conv(ext)[SUBLANES:, :].astype(o_ref.dtype)
    o_ref[head:, :] = conv(acc)[head:, :].astype(o_ref.dtype)

    raw = jnp.dot(xb, wdt_ref[...], preferred_element_type=F32) + dtb_ref[...]
    dt_ref[...] = jnp.maximum(raw, 0.0) + jnp.log(1.0 + jnp.exp(-jnp.abs(raw)))


def _proj_conv(x2d, w, params, w_dt, dt_bias, seq):
    t_rows = x2d.shape[0]
    tm = IN_TM
    assert t_rows % tm == 0 and seq % tm == 0
    tiles_per_seq = seq // tm
    halo_blocks = tm // SUBLANES
    full = lambda shape: pl.BlockSpec(shape, lambda i: (0, 0))
    return pl.pallas_call(
        functools.partial(_proj_conv_kernel, tiles_per_seq=tiles_per_seq),
        grid=(t_rows // tm,),
        in_specs=[
            pl.BlockSpec((tm, D_MODEL), lambda i: (i, 0)),
            pl.BlockSpec((SUBLANES, D_MODEL), lambda i: (jnp.maximum(i * halo_blocks - 1, 0), 0)),
            full((D_MODEL, CONV_CH)),
            full((SUBLANES, CONV_CH)),
            full((D_MODEL, SSD_HEADS)),
            full((1, SSD_HEADS)),
        ],
        out_specs=[
            pl.BlockSpec((tm, CONV_CH), lambda i: (i, 0)),
            pl.BlockSpec((tm, SSD_HEADS), lambda i: (i, 0)),
        ],
        out_shape=[
            jax.ShapeDtypeStruct((t_rows, CONV_CH), BF16),
            jax.ShapeDtypeStruct((t_rows, SSD_HEADS), F32),
        ],
        compiler_params=pltpu.CompilerParams(dimension_semantics=("arbitrary",),
                                             vmem_limit_bytes=VMEM_LIMIT_BYTES),
        name="proj_conv",
    )(x2d, x2d, w, params, w_dt, dt_bias)


def _ssd_kernel(xs_ref, b_ref, c_ref, sz_ref, dt_ref, a_ref, dskip_ref, normw_ref,
                expand_ref, bdtril_ref, tile8_ref,
                y_ref, acol_s, dtx_s, ht_s):
    rows = xs_ref.shape[0]
    n_chunks = rows // CHUNK
    gw = SSD_WIDTH // SSD_GROUPS

    @pl.when(pl.program_id(1) == 0)
    def _():
        ht_s[...] = jnp.zeros_like(ht_s)

    dt = dt_ref[...]
    a_cs = _dot01_left(bdtril_ref[...], dt * (a_ref[...] * LOG2E))
    expand = expand_ref[...]
    acol_s[...] = _dot01_right(a_cs, expand)
    dtx_s[...] = _dot01_right(dt, expand)

    qw = 4 * SSD_HEAD_DIM
    quads_per_group = gw // qw
    lane = lax.broadcasted_iota(jnp.int32, (CHUNK, qw), 1)
    row = lax.broadcasted_iota(jnp.int32, (CHUNK, qw), 0)
    s_of_lane = lane & (CHUNK - 1)
    diag_mask = s_of_lane == row
    causal_mask = s_of_lane <= row
    bd_r = lax.broadcasted_iota(jnp.int32, (4 * CHUNK, qw), 0) // CHUNK
    bd_c = lax.broadcasted_iota(jnp.int32, (4 * CHUNK, qw), 1) // SSD_HEAD_DIM
    bd_mask = bd_r == bd_c
    tile4 = tile8_ref[:, :qw]

    def chunk_body(c, carry):
        r0 = pl.multiple_of(c * CHUNK, CHUNK)
        rows_c = pl.ds(r0, CHUNK)
        bc = b_ref[rows_c, :]
        cc = c_ref[rows_c, :]
        for g in range(SSD_GROUPS):
            cg = cc[:, g * SSD_STATE:(g + 1) * SSD_STATE]
            bg = bc[:, g * SSD_STATE:(g + 1) * SSD_STATE]
            cb_g = lax.dot_general(cg, bg, (((1,), (1,)), ((), ())), preferred_element_type=F32)
            cb_rep = _dot01_right(cb_g, tile4)
            gated = []
            for qd in range(quads_per_group):
                sl = slice(g * gw + qd * qw, g * gw + (qd + 1) * qw)
                acol = acol_s[rows_c, sl]
                a_last = acol[CHUNK - 1:CHUNK, :]
                xs = xs_ref[rows_c, sl].astype(F32)
                xdt_f = xs * dtx_s[rows_c, sl]

                arow = jnp.sum(jnp.where(diag_mask, acol, 0.0), axis=0, keepdims=True)
                decay_ls = jnp.exp2(jnp.where(causal_mask, acol - arow, NEG_BIG))
                w_ls = (cb_rep * decay_ls).astype(BF16)
                xq = xdt_f.astype(BF16)
                bd = jnp.where(bd_mask, jnp.concatenate([xq] * 4, axis=0), jnp.zeros((), BF16))
                y_diag = jnp.dot(w_ls, bd, preferred_element_type=F32)

                ht = ht_s[:, sl]
                y_off = jnp.dot(cg, ht.astype(BF16), preferred_element_type=F32)
                y = y_diag + y_off * jnp.exp2(acol) + dskip_ref[:, sl] * xs
                gated.append(y * sz_ref[rows_c, sl].astype(F32))

                xd = (xdt_f * jnp.exp2(a_last - acol)).astype(BF16)
                st = lax.dot_general(bg, xd, (((0,), (0,)), ((), ())), preferred_element_type=F32)
                ht_s[:, sl] = ht * jnp.exp2(a_last) + st

            ssq = sum(jnp.sum(t * t, axis=-1, keepdims=True) for t in gated)
            inv = lax.rsqrt(ssq * (1.0 / gw) + EPS)
            for qd in range(quads_per_group):
                sl = slice(g * gw + qd * qw, g * gw + (qd + 1) * qw)
                y_ref[rows_c, sl] = (gated[qd] * inv * normw_ref[:, sl]).astype(y_ref.dtype)
        return carry

    lax.fori_loop(0, n_chunks, chunk_body, 0, unroll=4)


def _ssd(conv_act, gate_act, dt, a_row, dskip_row, normw_row, batch, seq):
    rows = SSD_ROWS
    assert seq % rows == 0 and rows % (4 * CHUNK) == 0
    blocks_per_seq = seq // rows
    hh = np.arange(SSD_WIDTH) // SSD_HEAD_DIM
    expand = jnp.asarray(np.arange(SSD_HEADS)[:, None] == hh[None, :], BF16)
    rr = np.arange(rows)
    same_chunk = (rr[:, None] // CHUNK) == (rr[None, :] // CHUNK)
    bdtril = jnp.asarray(same_chunk & (rr[None, :] <= rr[:, None]), BF16)
    gw = SSD_WIDTH // SSD_GROUPS
    tile8 = jnp.asarray(np.arange(CHUNK)[:, None] == (np.arange(gw)[None, :] % CHUNK), BF16)

    def row_map(b, r):
        return b * blocks_per_seq + r

    def col_spec(width, off):
        assert off % width == 0
        return pl.BlockSpec((rows, width), lambda b, r: (row_map(b, r), off // width))

    const = lambda shape: pl.BlockSpec(shape, lambda b, r: (0, 0))
    return pl.pallas_call(
        _ssd_kernel,
        grid=(batch, blocks_per_seq),
        in_specs=[
            col_spec(SSD_WIDTH, CV_XS),
            col_spec(BC_WIDTH, CV_B),
            col_spec(BC_WIDTH, CV_C),
            col_spec(SSD_WIDTH, GA_SZ),
            pl.BlockSpec((rows, SSD_HEADS), lambda b, r: (row_map(b, r), 0)),
            const((1, SSD_HEADS)),
            const((1, SSD_WIDTH)),
            const((1, SSD_WIDTH)),
            const((SSD_HEADS, SSD_WIDTH)),
            const((rows, rows)),
            const((CHUNK, gw)),
        ],
        out_specs=pl.BlockSpec((rows, SSD_WIDTH), lambda b, r: (row_map(b, r), 0)),
        out_shape=jax.ShapeDtypeStruct((batch * seq, SSD_WIDTH), BF16),
        scratch_shapes=[
            pltpu.VMEM((rows, SSD_WIDTH), F32),
            pltpu.VMEM((rows, SSD_WIDTH), F32),
            pltpu.VMEM((SSD_STATE, SSD_WIDTH), F32),
        ],
        compiler_params=pltpu.CompilerParams(
            dimension_semantics=("arbitrary", "arbitrary"),
            vmem_limit_bytes=VMEM_LIMIT_BYTES),
        name="ssd",
    )(conv_act, conv_act, conv_act, gate_act, dt, a_row, dskip_row, normw_row, expand, bdtril, tile8)


def _attn_kernel(q_ref, k_ref, v_ref, gb_ref, lq1_ref, lk1_ref, lq2_ref, lk2_ref, subw_ref,
                 o_ref, vaug_s, m_s, acc_s, knorm_s, *, lambda_init):
    qi = pl.program_id(2)
    tq = q_ref.shape[0]
    tk = ATT_TK
    seq = k_ref.shape[0]

    @pl.when(qi == 0)
    def _():
        vaug_s[:, :DA_VDIM] = v_ref[...].astype(vaug_s.dtype)
        vaug_s[:, DA_VDIM:] = jnp.ones((seq, DA_VDIM), vaug_s.dtype)
        kf = k_ref[...].astype(F32)
        knorm_s[0] = jnp.max(jnp.sum(kf * kf, axis=-1, keepdims=True))

    assert tq % tk == 0
    band = ATT_BAND
    n_bands = tq // band
    n_full = qi * (tq // tk)
    half_lane = lax.broadcasted_iota(jnp.int32, (band, DA_VDIM), 1) & (LANES // 2 - 1)
    in_map0 = (half_lane < ROT_HALF) | ((half_lane >= ROT_DIM) & (half_lane < ROT_DIM + ROT_PASS))
    zero = jnp.zeros((), q_ref.dtype)

    qq = []
    for c in range(n_bands):
        q = q_ref[c * band:(c + 1) * band, :]
        qq.append(jnp.concatenate([jnp.where(in_map0, q, zero), jnp.where(in_map0, zero, q)], axis=0))

    def band_scores(c, k):
        return lax.dot_general(qq[c], k, (((1,), (1,)), ((), ())), preferred_element_type=F32)

    def band_rows(c):
        return slice(c * 2 * band, (c + 1) * 2 * band)

    def diagonal_operands(c):
        nk = (c + 1) * band
        k0 = pl.multiple_of(qi * tq, tq)
        r_chunk = (c * band + lax.broadcasted_iota(jnp.int32, (band, nk), 0)) // CHUNK
        k_chunk = lax.broadcasted_iota(jnp.int32, (band, nk), 1) // CHUNK
        ok = k_chunk <= r_chunk
        return k_ref[pl.ds(k0, nk), :], vaug_s[pl.ds(k0, nk), :], jnp.concatenate([ok, ok], axis=0)

    def full_operands(kt):
        k0 = pl.multiple_of(kt * tk, tk)
        return k_ref[pl.ds(k0, tk), :], vaug_s[pl.ds(k0, tk), :]

    qf = q_ref[...].astype(F32)
    q_norm2 = jnp.max(jnp.sum(qf * qf, axis=-1, keepdims=True))
    unshifted_ok = q_norm2 * knorm_s[0] <= ATT_SAFE_LOG2 * ATT_SAFE_LOG2

    @pl.when(unshifted_ok)
    def _():
        for c in range(n_bands):
            k, vt, ok = diagonal_operands(c)
            p = jnp.exp2(jnp.where(ok, band_scores(c, k), NEG_BIG)).astype(vaug_s.dtype)
            acc_s[band_rows(c), :] = jnp.dot(p, vt, preferred_element_type=F32)

        def tile(kt, carry):
            k, vt = full_operands(kt)
            for c in range(n_bands):
                p = jnp.exp2(band_scores(c, k)).astype(vaug_s.dtype)
                acc_s[band_rows(c), :] += jnp.dot(p, vt, preferred_element_type=F32)
            return carry

        lax.fori_loop(0, n_full, tile, 0)

    @pl.when(jnp.logical_not(unshifted_ok))
    def _():
        m_s[...] = jnp.full_like(m_s, NEG_BIG)
        acc_s[...] = jnp.zeros_like(acc_s)

        def softmax_pv(c, s, vt):
            rows = band_rows(c)
            m_old = m_s[rows, :]
            m_new = jnp.maximum(m_old, jnp.max(s, axis=-1, keepdims=True))
            alpha = jnp.exp2(m_old - m_new)
            p = jnp.exp2(s - m_new[:, 0:1]).astype(vaug_s.dtype)
            pv = jnp.dot(p, vt, preferred_element_type=F32)
            acc_s[rows, :] = jnp.concatenate([alpha, alpha], axis=1) * acc_s[rows, :] + pv
            m_s[rows, :] = m_new

        def tile(kt, carry):
            k, vt = full_operands(kt)
            for c in range(n_bands):
                softmax_pv(c, band_scores(c, k), vt)
            return carry

        lax.fori_loop(0, n_full, tile, 0)
        for c in range(n_bands):
            k, vt, ok = diagonal_operands(c)
            softmax_pv(c, jnp.where(ok, band_scores(c, k), NEG_BIG), vt)

    lam = (jnp.exp(jnp.sum(lq1_ref[...] * lk1_ref[...], axis=-1, keepdims=True))
           - jnp.exp(jnp.sum(lq2_ref[...] * lk2_ref[...], axis=-1, keepdims=True)) + lambda_init)
    scale = subw_ref[...] * (1.0 - lambda_init)
    for c in range(n_bands):
        acc = acc_s[c * 2 * band:(c + 1) * 2 * band, :]
        o_all = acc[:, :DA_VDIM] / acc[:, DA_VDIM:]
        o = o_all[:band] - lam * o_all[band:]
        ms = jnp.mean(o * o, axis=-1, keepdims=True)
        o = o * lax.rsqrt(ms + EPS) * scale
        rows = slice(c * band, (c + 1) * band)
        o_ref[rows, :] = (o * gb_ref[rows, :].astype(F32)).astype(o_ref.dtype)


def _attention(attn_act, gate_act, lq1, lk1, lq2, lk2, subw, batch, seq, lambda_init):
    tq = ATT_TQ
    assert seq % tq == 0 and tq % ATT_TK == 0 and tq % ATT_BAND == 0 and ATT_BAND % CHUNK == 0
    q_tiles = seq // tq

    def head_spec(rows_blk, off):
        assert off % DA_VDIM == 0
        if rows_blk == seq:
            return pl.BlockSpec((seq, DA_VDIM), lambda b, hd, qi: (b, off // DA_VDIM + hd))
        return pl.BlockSpec((rows_blk, DA_VDIM), lambda b, hd, qi: (b * q_tiles + qi, off // DA_VDIM + hd))

    vec = lambda n: pl.BlockSpec((1, n), lambda b, hd, qi: (0, 0))
    kern = functools.partial(_attn_kernel, lambda_init=lambda_init)
    return pl.pallas_call(
        kern,
        grid=(batch, DA_HEADS, q_tiles),
        in_specs=[
            head_spec(tq, AT_Q),
            head_spec(seq, AT_K),
            head_spec(seq, AT_V),
            head_spec(tq, GA_GB),
            vec(DA_HEAD_DIM), vec(DA_HEAD_DIM), vec(DA_HEAD_DIM), vec(DA_HEAD_DIM),
            vec(DA_VDIM),
        ],
        out_specs=pl.BlockSpec((tq, DA_VDIM), lambda b, hd, qi: (b * q_tiles + qi, hd)),
        out_shape=jax.ShapeDtypeStruct((batch * seq, DA_WIDTH), BF16),
        scratch_shapes=[
            pltpu.VMEM((seq, 2 * DA_VDIM), BF16),
            pltpu.VMEM((2 * tq, LANES), F32),
            pltpu.VMEM((2 * tq, 2 * DA_VDIM), F32),
            pltpu.SMEM((1,), F32),
        ],
        compiler_params=pltpu.CompilerParams(
            dimension_semantics=("arbitrary", "arbitrary", "arbitrary"),
            vmem_limit_bytes=VMEM_LIMIT_BYTES),
        name="diffattn",
    )(attn_act, attn_act, attn_act, gate_act, lq1, lk1, lq2, lk2, subw)


def _out_kernel(x_ref, ya_ref, ob_ref, g0_ref, g1_ref, wa_ref, wb_ref, wo_ref, lng_ref, lnb_ref,
                out_ref, *, alpha):
    d = functools.partial(jnp.dot, preferred_element_type=F32)
    branch_a = d(ya_ref[...], wa_ref[...])
    branch_b = d(ob_ref[...], wb_ref[...])
    merged = g0_ref[...].astype(F32) * branch_a + g1_ref[...].astype(F32) * branch_b
    y = d(merged.astype(BF16), wo_ref[...])
    r = alpha * x_ref[...] + y
    mu = jnp.mean(r, axis=-1, keepdims=True)
    rc = r - mu
    var = jnp.mean(rc * rc, axis=-1, keepdims=True)
    out_ref[...] = rc * lax.rsqrt(var + EPS) * lng_ref[...] + lnb_ref[...]


def _output_stage(x2d, y_ssd, o_att, gate_act, w_a, w_b, w_o, ln_g, ln_b, alpha):
    t_rows = x2d.shape[0]
    tm = OUT_TM
    assert t_rows % tm == 0 and GA_GM % D_MODEL == 0
    row = lambda width: pl.BlockSpec((tm, width), lambda i: (i, 0))
    full = lambda shape: pl.BlockSpec(shape, lambda i: (0, 0))
    weight = lambda shape: pl.BlockSpec(shape, lambda i: (0, 0), pipeline_mode=pl.Buffered(1))
    gate_spec = lambda k: pl.BlockSpec((tm, D_MODEL), lambda i: (i, GA_GM // D_MODEL + k))
    return pl.pallas_call(
        functools.partial(_out_kernel, alpha=alpha),
        grid=(t_rows // tm,),
        in_specs=[row(D_MODEL), row(SSD_WIDTH), row(DA_WIDTH), gate_spec(0), gate_spec(1),
                  weight((SSD_WIDTH, D_MODEL)), weight((DA_WIDTH, D_MODEL)), weight((D_MODEL, D_MODEL)),
                  full((1, D_MODEL)), full((1, D_MODEL))],
        out_specs=row(D_MODEL),
        out_shape=jax.ShapeDtypeStruct((t_rows, D_MODEL), F32),
        compiler_params=pltpu.CompilerParams(
            dimension_semantics=("arbitrary",),
            vmem_limit_bytes=VMEM_LIMIT_BYTES),
        name="outproj",
    )(x2d, y_ssd, o_att, gate_act, gate_act, w_a, w_b, w_o, ln_g, ln_b)


def _rope_tables(seq):
    pos = np.arange(seq, dtype=np.float64)
    inv_freq = ROPE_THETA ** (-np.arange(0, ROT_DIM, 2, dtype=np.float64) / ROT_DIM)
    ang = pos[:, None] * inv_freq[None, :]
    cos, sin = np.cos(ang).astype(np.float32), np.sin(ang).astype(np.float32)
    ones = np.ones((seq, LANES // 2 - ROT_DIM), np.float32)
    zeros = np.zeros_like(ones)
    cos_t = np.concatenate([cos, cos, ones, cos, cos, ones], axis=1)
    sin_t = np.concatenate([-sin, -sin, zeros, sin, sin, zeros], axis=1)
    return jnp.asarray(cos_t), jnp.asarray(sin_t)


def _head_lane_permutation():
    src = np.zeros(LANES, np.int64)
    for m in range(2):
        for d in range(DA_HEAD_DIM):
            if d < ROT_HALF:
                lane = m * ROT_HALF + d
            elif d < ROT_DIM:
                lane = LANES // 2 + m * ROT_HALF + (d - ROT_HALF)
            elif d < ROT_DIM + ROT_PASS:
                lane = ROT_DIM + m * ROT_PASS + (d - ROT_DIM)
            else:
                lane = LANES // 2 + ROT_DIM + m * ROT_PASS + (d - ROT_DIM - ROT_PASS)
            src[lane] = m * DA_HEAD_DIM + d
    perm = np.zeros((DA_WIDTH, DA_WIDTH), np.float32)
    for hd in range(DA_HEADS):
        perm[hd * LANES + src, hd * LANES + np.arange(LANES)] = 1.0
    return jnp.asarray(perm, BF16)


def _permute_head_lanes(w_bf16, perm):
    return jnp.dot(w_bf16, perm, preferred_element_type=BF16)


def _layer(x2d, batch, seq, depth_total, l, w_in, b_gate, conv_w, conv_b, dt_bias, a_log, d_skip,
           ssd_norm_w, lambda_q1, lambda_k1, lambda_q2, lambda_k2, subln_w, w_a, w_b, w_o, ln_g, ln_b,
           tables):
    alpha = (2.0 * depth_total) ** 0.25
    lambda_init = 0.8 - 0.6 * math.exp(-0.3 * l)
    offs = [0]
    for s in IN_SIZES:
        offs.append(offs[-1] + s)
    z_w, xbc_w, dt_w, q_w, k_w, v_w, gb_w, gm_w = [w_in[:, offs[n]:offs[n + 1]] for n in range(len(IN_SIZES))]

    w_gate = jnp.concatenate([z_w, gb_w, gm_w], axis=1).astype(BF16)
    perm = _head_lane_permutation()
    w_attn = jnp.concatenate([_permute_head_lanes(q_w.astype(BF16), perm),
                              _permute_head_lanes(k_w.astype(BF16), perm), v_w.astype(BF16)], axis=1)
    p_conv = jnp.concatenate([conv_w.astype(F32), conv_b[None, :].astype(F32),
                              jnp.zeros((SUBLANES - CONV_WIDTH - 1, CONV_CH), F32)], axis=0)

    gate_act = _proj_gate(x2d, w_gate, b_gate[None, :].astype(F32))
    attn_act = _proj_attn(x2d, w_attn, *tables, seq)
    conv_act, dt = _proj_conv(x2d, xbc_w.astype(BF16), p_conv, dt_w.astype(BF16),
                              dt_bias[None, :].astype(F32), seq)

    a_row = -jnp.exp(a_log.astype(F32))[None, :]
    dskip_row = jnp.repeat(d_skip.astype(F32), SSD_HEAD_DIM)[None, :]
    y_ssd = _ssd(conv_act, gate_act, dt, a_row, dskip_row, ssd_norm_w[None, :].astype(F32), batch, seq)

    o_att = _attention(attn_act, gate_act, lambda_q1[None, :], lambda_k1[None, :], lambda_q2[None, :],
                       lambda_k2[None, :], subln_w[None, :], batch, seq, lambda_init)

    return _output_stage(x2d, y_ssd, o_att, gate_act, w_a.astype(BF16), w_b.astype(BF16), w_o.astype(BF16),
                         ln_g[None, :], ln_b[None, :], alpha)


def kernel(x, w_in, b_gate, conv_w, conv_b, dt_bias, a_log, d_skip, ssd_norm_w, lambda_q1, lambda_k1,
           lambda_q2, lambda_k2, subln_w, w_a, w_b, w_o, ln_g, ln_b):
    batch, seq, _ = x.shape
    depth = w_in.shape[0]
    tables = _rope_tables(seq)
    x2d = x.reshape(batch * seq, D_MODEL)
    for l in range(depth):
        x2d = _layer(x2d, batch, seq, depth, l, w_in[l], b_gate[l], conv_w[l], conv_b[l], dt_bias[l],
                     a_log[l], d_skip[l], ssd_norm_w[l], lambda_q1[l], lambda_k1[l], lambda_q2[l],
                     lambda_k2[l], subln_w[l], w_a[l], w_b[l], w_o[l], ln_g[l], ln_b[l], tables)
    return x2d.reshape(batch, seq, D_MODEL)
```

```python
import functools
import math

import jax
import jax.numpy as jnp
import numpy as np
from jax import lax
from jax.experimental import pallas as pl
from jax.experimental.pallas import tpu as pltpu

F32 = jnp.float32
BF16 = jnp.bfloat16

D_MODEL = 1024
CHUNK = 64
SSD_HEADS = 16
SSD_HEAD_DIM = 64
SSD_WIDTH = SSD_HEADS * SSD_HEAD_DIM
SSD_GROUPS = 2
SSD_STATE = 128
CONV_WIDTH = 4
BC_WIDTH = SSD_GROUPS * SSD_STATE
CONV_CH = SSD_WIDTH + 2 * BC_WIDTH
DA_HEADS = 8
DA_HEAD_DIM = 64
DA_VDIM = 2 * DA_HEAD_DIM
DA_WIDTH = DA_HEADS * DA_VDIM
ROPE_THETA = 500000.0
ROT_DIM = DA_HEAD_DIM // 4
ROT_HALF = ROT_DIM // 2
N_BRANCH = 2
EPS = 1e-5
IN_SIZES = (SSD_WIDTH, CONV_CH, SSD_HEADS, DA_WIDTH, DA_WIDTH, DA_WIDTH, DA_WIDTH, N_BRANCH * D_MODEL)

GA_SZ, GA_GB, GA_GM = 0, SSD_WIDTH, SSD_WIDTH + DA_WIDTH
GA_COLS = GA_GM + N_BRANCH * D_MODEL
AT_Q, AT_K, AT_V = 0, DA_WIDTH, 2 * DA_WIDTH
AT_COLS = 3 * DA_WIDTH
CV_XS, CV_B, CV_C = 0, SSD_WIDTH, SSD_WIDTH + BC_WIDTH

ROT_PASS = (DA_HEAD_DIM - ROT_DIM) // 2

LANES = 128
SUBLANES = 8
BF16_ROWS = 16
VMEM_LIMIT_BYTES = 56 * 1024 * 1024

IN_TM = 512
IN_TM_WIDE = 1024
SSD_ROWS = 512
ATT_TQ = 2048
ATT_TK = 1024
ATT_BAND = 256
ATT_SAFE_LOG2 = 64.0
NORM_SLACK = 1.0 + 2.0 ** -7
OUT_TM = 1024

NEG_BIG = -1e30
LOG2E = 1.4426950408889634


def _sigmoid(t):
    return 0.5 * jnp.tanh(0.5 * t) + 0.5


def _split2(v):
    hi = v.astype(BF16)
    mid = (v - hi.astype(F32)).astype(BF16)
    return hi, mid


def _dot01_right(v, m01):
    hi, mid = _split2(v)
    d = functools.partial(jnp.dot, preferred_element_type=F32)
    return d(hi, m01) + d(mid, m01)


def _dot01_left(m01, v):
    hi, mid = _split2(v)
    d = functools.partial(jnp.dot, preferred_element_type=F32)
    return d(m01, hi) + d(m01, mid)


def _proj_params():
    return pltpu.CompilerParams(dimension_semantics=("arbitrary",),
                                vmem_limit_bytes=VMEM_LIMIT_BYTES)


def _proj_gate_kernel(x_ref, w_ref, bias_ref, o_ref):
    acc = jnp.dot(x_ref[...].astype(BF16), w_ref[...], preferred_element_type=F32)
    t = acc[:, :GA_GM]
    o_ref[:, :GA_GM] = (t * _sigmoid(t)).astype(o_ref.dtype)
    o_ref[:, GA_GM:] = _sigmoid(acc[:, GA_GM:] + bias_ref[...]).astype(o_ref.dtype)


def _proj_gate(x2d, w, b_gate_row):
    t_rows = x2d.shape[0]
    tm = IN_TM_WIDE
    assert t_rows % tm == 0
    full = lambda shape: pl.BlockSpec(shape, lambda i: (0, 0))
    return pl.pallas_call(
        _proj_gate_kernel,
        grid=(t_rows // tm,),
        in_specs=[
            pl.BlockSpec((tm, D_MODEL), lambda i: (i, 0)),
            pl.BlockSpec((D_MODEL, GA_COLS), lambda i: (0, 0), pipeline_mode=pl.Buffered(1)),
            full((1, N_BRANCH * D_MODEL)),
        ],
        out_specs=pl.BlockSpec((tm, GA_COLS), lambda i: (i, 0)),
        out_shape=jax.ShapeDtypeStruct((t_rows, GA_COLS), BF16),
        compiler_params=_proj_params(),
        name="proj_gate",
    )(x2d, w, b_gate_row)


def _proj_attn_kernel(x_ref, w_ref, cos_ref, sin_ref, sel_ref, o_ref, nrm_ref, *, q_scale, tiles_per_seq):
    @pl.when(pl.program_id(0) % tiles_per_seq == 0)
    def _():
        nrm_ref[...] = jnp.zeros_like(nrm_ref)

    acc = jnp.dot(x_ref[...].astype(BF16), w_ref[...], preferred_element_type=F32)
    cos = cos_ref[...]
    sin = sin_ref[...]
    cos_q = cos * q_scale
    sin_q = sin * q_scale
    squares = []
    for hd in range(AT_V // LANES):
        sl = slice(hd * LANES, (hd + 1) * LANES)
        a = acc[:, sl]
        partner = pltpu.roll(a, LANES // 2, axis=1)
        c, s = (cos_q, sin_q) if hd < AT_K // LANES else (cos, sin)
        rot = (a * c + partner * s).astype(o_ref.dtype)
        o_ref[:, sl] = rot
        rf = rot.astype(F32)
        squares.append((rf * rf).astype(BF16))
    o_ref[:, AT_V:] = acc[:, AT_V:].astype(o_ref.dtype)

    norms = jnp.dot(jnp.concatenate(squares, axis=1), sel_ref[...], preferred_element_type=F32)
    top = jnp.max(norms, axis=0, keepdims=True) * NORM_SLACK
    nrm_ref[0] = jnp.maximum(nrm_ref[0], jnp.broadcast_to(top, nrm_ref.shape[1:]))


def _proj_attn(x2d, w, cos_t, sin_t, seq):
    t_rows = x2d.shape[0]
    tm = IN_TM_WIDE
    assert t_rows % tm == 0 and seq % tm == 0
    tiles_per_seq = seq // tm
    table = lambda: pl.BlockSpec((tm, LANES), lambda i: (i % tiles_per_seq, 0))
    selector = jnp.asarray((np.arange(AT_V)[:, None] // LANES) == np.arange(LANES)[None, :], BF16)
    return pl.pallas_call(
        functools.partial(_proj_attn_kernel, q_scale=DA_HEAD_DIM ** -0.5 * LOG2E, tiles_per_seq=tiles_per_seq),
        grid=(t_rows // tm,),
        in_specs=[
            pl.BlockSpec((tm, D_MODEL), lambda i: (i, 0)),
            pl.BlockSpec((D_MODEL, AT_COLS), lambda i: (0, 0), pipeline_mode=pl.Buffered(1)),
            table(), table(),
            pl.BlockSpec((AT_V, LANES), lambda i: (0, 0)),
        ],
        out_specs=[
            pl.BlockSpec((tm, AT_COLS), lambda i: (i, 0)),
            pl.BlockSpec((1, SUBLANES, LANES), lambda i: (i // tiles_per_seq, 0, 0)),
        ],
        out_shape=[
            jax.ShapeDtypeStruct((t_rows, AT_COLS), BF16),
            jax.ShapeDtypeStruct((t_rows // seq, SUBLANES, LANES), F32),
        ],
        compiler_params=_proj_params(),
        name="proj_attn",
    )(x2d, w, cos_t, sin_t, selector)


def _proj_conv_kernel(x_ref, xh_ref, w_ref, p_ref, wdt_ref, dtb_ref, o_ref, dt_ref, *, tiles_per_seq):
    i = pl.program_id(0)
    tm = o_ref.shape[0]
    xb = x_ref[...].astype(BF16)
    w = w_ref[...]
    acc = jnp.dot(xb, w, preferred_element_type=F32)
    halo = jnp.dot(xh_ref[...].astype(BF16), w, preferred_element_type=F32)
    halo = jnp.where(i % tiles_per_seq == 0, 0.0, halo)

    bias = p_ref[CONV_WIDTH:CONV_WIDTH + 1, :]
    taps = [p_ref[k:k + 1, :] for k in range(CONV_WIDTH)]

    def conv(rows_val):
        out = taps[0] * rows_val
        for k in range(1, CONV_WIDTH):
            out = pltpu.roll(out, 1, axis=0) + taps[k] * rows_val
        out = out + bias
        return out * _sigmoid(out)

    head = BF16_ROWS
    ext = jnp.concatenate([halo, acc[:head]], axis=0)
    o_ref[:head, :] = conv(ext)[SUBLANES:, :].astype(o_ref.dtype)
    o_ref[head:, :] = conv(acc)[head:, :].astype(o_ref.dtype)

    raw = jnp.dot(xb, wdt_ref[...], preferred_element_type=F32) + dtb_ref[...]
    dt_ref[...] = jnp.maximum(raw, 0.0) + jnp.log(1.0 + jnp.exp(-jnp.abs(raw)))


def _proj_conv(x2d, w, params, w_dt, dt_bias, seq):
    t_rows = x2d.shape[0]
    tm = IN_TM
    assert t_rows % tm == 0 and seq % tm == 0
    tiles_per_seq = seq // tm
    halo_blocks = tm // SUBLANES
    full = lambda shape: pl.BlockSpec(shape, lambda i: (0, 0))
    return pl.pallas_call(
        functools.partial(_proj_conv_kernel, tiles_per_seq=tiles_per_seq),
        grid=(t_rows // tm,),
        in_specs=[
            pl.BlockSpec((tm, D_MODEL), lambda i: (i, 0)),
            pl.BlockSpec((SUBLANES, D_MODEL), lambda i: (jnp.maximum(i * halo_blocks - 1, 0), 0)),
            full((D_MODEL, CONV_CH)),
            full((SUBLANES, CONV_CH)),
            full((D_MODEL, SSD_HEADS)),
            full((1, SSD_HEADS)),
        ],
        out_specs=[
            pl.BlockSpec((tm, CONV_CH), lambda i: (i, 0)),
            pl.BlockSpec((tm, SSD_HEADS), lambda i: (i, 0)),
        ],
        out_shape=[
            jax.ShapeDtypeStruct((t_rows, CONV_CH), BF16),
            jax.ShapeDtypeStruct((t_rows, SSD_HEADS), F32),
        ],
        compiler_params=pltpu.CompilerParams(dimension_semantics=("arbitrary",),
                                             vmem_limit_bytes=VMEM_LIMIT_BYTES),
        name="proj_conv",
    )(x2d, x2d, w, params, w_dt, dt_bias)


def _ssd_kernel(xs_ref, b_ref, c_ref, sz_ref, dt_ref, a_ref, dskip_ref, normw_ref,
                expand_ref, bdtril_ref, tile8_ref,
                y_ref, acol_s, dtx_s, ht_s):
    rows = xs_ref.shape[0]
    n_chunks = rows // CHUNK
    gw = SSD_WIDTH // SSD_GROUPS

    @pl.when(pl.program_id(1) == 0)
    def _():
        ht_s[...] = jnp.zeros_like(ht_s)

    dt = dt_ref[...]
    a_cs = _dot01_left(bdtril_ref[...], dt * (a_ref[...] * LOG2E))
    expand = expand_ref[...]
    acol_s[...] = _dot01_right(a_cs, expand)
    dtx_s[...] = _dot01_right(dt, expand)

    qw = 4 * SSD_HEAD_DIM
    quads_per_group = gw // qw
    lane = lax.broadcasted_iota(jnp.int32, (CHUNK, qw), 1)
    row = lax.broadcasted_iota(jnp.int32, (CHUNK, qw), 0)
    s_of_lane = lane & (CHUNK - 1)
    diag_mask = s_of_lane == row
    causal_mask = s_of_lane <= row
    bd_r = lax.broadcasted_iota(jnp.int32, (4 * CHUNK, qw), 0) // CHUNK
    bd_c = lax.broadcasted_iota(jnp.int32, (4 * CHUNK, qw), 1) // SSD_HEAD_DIM
    bd_mask = bd_r == bd_c
    tile4 = tile8_ref[:, :qw]

    def chunk_body(c, carry):
        r0 = pl.multiple_of(c * CHUNK, CHUNK)
        rows_c = pl.ds(r0, CHUNK)
        bc = b_ref[rows_c, :]
        cc = c_ref[rows_c, :]
        for g in range(SSD_GROUPS):
            cg = cc[:, g * SSD_STATE:(g + 1) * SSD_STATE]
            bg = bc[:, g * SSD_STATE:(g + 1) * SSD_STATE]
            cb_g = lax.dot_general(cg, bg, (((1,), (1,)), ((), ())), preferred_element_type=F32)
            cb_rep = _dot01_right(cb_g, tile4)
            gated = []
            for qd in range(quads_per_group):
                sl = slice(g * gw + qd * qw, g * gw + (qd + 1) * qw)
                acol = acol_s[rows_c, sl]
                a_last = acol[CHUNK - 1:CHUNK, :]
                xs = xs_ref[rows_c, sl].astype(F32)
                xdt_f = xs * dtx_s[rows_c, sl]

                arow = jnp.sum(jnp.where(diag_mask, acol, 0.0), axis=0, keepdims=True)
                decay_ls = jnp.exp2(jnp.where(causal_mask, acol - arow, NEG_BIG))
                w_ls = (cb_rep * decay_ls).astype(BF16)
                xq = xdt_f.astype(BF16)
                bd = jnp.where(bd_mask, jnp.concatenate([xq] * 4, axis=0), jnp.zeros((), BF16))
                y_diag = jnp.dot(w_ls, bd, preferred_element_type=F32)

                ht = ht_s[:, sl]
                y_off = jnp.dot(cg, ht.astype(BF16), preferred_element_type=F32)
                y = y_diag + y_off * jnp.exp2(acol) + dskip_ref[:, sl] * xs
                gated.append(y * sz_ref[rows_c, sl].astype(F32))

                xd = (xdt_f * jnp.exp2(a_last - acol)).astype(BF16)
                st = lax.dot_general(bg, xd, (((0,), (0,)), ((), ())), preferred_element_type=F32)
                ht_s[:, sl] = ht * jnp.exp2(a_last) + st

            ssq = sum(jnp.sum(t * t, axis=-1, keepdims=True) for t in gated)
            inv = lax.rsqrt(ssq * (1.0 / gw) + EPS)
            for qd in range(quads_per_group):
                sl = slice(g * gw + qd * qw, g * gw + (qd + 1) * qw)
                y_ref[rows_c, sl] = (gated[qd] * inv * normw_ref[:, sl]).astype(y_ref.dtype)
        return carry

    lax.fori_loop(0, n_chunks, chunk_body, 0, unroll=4)


def _ssd(conv_act, gate_act, dt, a_row, dskip_row, normw_row, batch, seq):
    rows = SSD_ROWS
    assert seq % rows == 0 and rows % (4 * CHUNK) == 0
    blocks_per_seq = seq // rows
    hh = np.arange(SSD_WIDTH) // SSD_HEAD_DIM
    expand = jnp.asarray(np.arange(SSD_HEADS)[:, None] == hh[None, :], BF16)
    rr = np.arange(rows)
    same_chunk = (rr[:, None] // CHUNK) == (rr[None, :] // CHUNK)
    bdtril = jnp.asarray(same_chunk & (rr[None, :] <= rr[:, None]), BF16)
    gw = SSD_WIDTH // SSD_GROUPS
    tile8 = jnp.asarray(np.arange(CHUNK)[:, None] == (np.arange(gw)[None, :] % CHUNK), BF16)

    def row_map(b, r):
        return b * blocks_per_seq + r

    def col_spec(width, off):
        assert off % width == 0
        return pl.BlockSpec((rows, width), lambda b, r: (row_map(b, r), off // width))

    const = lambda shape: pl.BlockSpec(shape, lambda b, r: (0, 0))
    return pl.pallas_call(
        _ssd_kernel,
        grid=(batch, blocks_per_seq),
        in_specs=[
            col_spec(SSD_WIDTH, CV_XS),
            col_spec(BC_WIDTH, CV_B),
            col_spec(BC_WIDTH, CV_C),
            col_spec(SSD_WIDTH, GA_SZ),
            pl.BlockSpec((rows, SSD_HEADS), lambda b, r: (row_map(b, r), 0)),
            const((1, SSD_HEADS)),
            const((1, SSD_WIDTH)),
            const((1, SSD_WIDTH)),
            const((SSD_HEADS, SSD_WIDTH)),
            const((rows, rows)),
            const((CHUNK, gw)),
        ],
        out_specs=pl.BlockSpec((rows, SSD_WIDTH), lambda b, r: (row_map(b, r), 0)),
        out_shape=jax.ShapeDtypeStruct((batch * seq, SSD_WIDTH), BF16),
        scratch_shapes=[
            pltpu.VMEM((rows, SSD_WIDTH), F32),
            pltpu.VMEM((rows, SSD_WIDTH), F32),
            pltpu.VMEM((SSD_STATE, SSD_WIDTH), F32),
        ],
        compiler_params=pltpu.CompilerParams(
            dimension_semantics=("arbitrary", "arbitrary"),
            vmem_limit_bytes=VMEM_LIMIT_BYTES),
        name="ssd",
    )(conv_act, conv_act, conv_act, gate_act, dt, a_row, dskip_row, normw_row, expand, bdtril, tile8)


def _attn_kernel(q_ref, k_ref, v_ref, gb_ref, nrm_ref, lq1_ref, lk1_ref, lq2_ref, lk2_ref, subw_ref,
                 o_ref, vaug_s, m_s, acc_s, *, lambda_init):
    hd = pl.program_id(1)
    qi = pl.program_id(2)
    tq = q_ref.shape[0]
    tk = ATT_TK
    seq = k_ref.shape[0]

    @pl.when(qi == 0)
    def _():
        vaug_s[:, :DA_VDIM] = v_ref[...].astype(vaug_s.dtype)
        vaug_s[:, DA_VDIM:] = jnp.ones((seq, DA_VDIM), vaug_s.dtype)

    assert tq % tk == 0
    band = ATT_BAND
    n_bands = tq // band
    n_full = qi * (tq // tk)
    half_lane = lax.broadcasted_iota(jnp.int32, (band, DA_VDIM), 1) & (LANES // 2 - 1)
    in_map0 = (half_lane < ROT_HALF) | ((half_lane >= ROT_DIM) & (half_lane < ROT_DIM + ROT_PASS))
    zero = jnp.zeros((), q_ref.dtype)

    qq = []
    for c in range(n_bands):
        q = q_ref[c * band:(c + 1) * band, :]
        qq.append(jnp.concatenate([jnp.where(in_map0, q, zero), jnp.where(in_map0, zero, q)], axis=0))

    def band_scores(c, k):
        return lax.dot_general(qq[c], k, (((1,), (1,)), ((), ())), preferred_element_type=F32)

    def band_rows(c):
        return slice(c * 2 * band, (c + 1) * 2 * band)

    def diagonal_operands(c):
        nk = (c + 1) * band
        k0 = pl.multiple_of(qi * tq, tq)
        r_chunk = (c * band + lax.broadcasted_iota(jnp.int32, (band, nk), 0)) // CHUNK
        k_chunk = lax.broadcasted_iota(jnp.int32, (band, nk), 1) // CHUNK
        ok = k_chunk <= r_chunk
        return k_ref[pl.ds(k0, nk), :], vaug_s[pl.ds(k0, nk), :], jnp.concatenate([ok, ok], axis=0)

    def full_operands(kt):
        k0 = pl.multiple_of(kt * tk, tk)
        return k_ref[pl.ds(k0, tk), :], vaug_s[pl.ds(k0, tk), :]

    norm_row = nrm_ref[0, 0:1, :]
    norm_lane = lax.broadcasted_iota(jnp.int32, norm_row.shape, 1)
    q_norm2 = jnp.max(jnp.where(norm_lane == hd, norm_row, 0.0))
    k_norm2 = jnp.max(jnp.where(norm_lane == DA_HEADS + hd, norm_row, 0.0))
    unshifted_ok = q_norm2 * k_norm2 <= ATT_SAFE_LOG2 * ATT_SAFE_LOG2

    @pl.when(unshifted_ok)
    def _():
        for c in range(n_bands):
            k, vt, ok = diagonal_operands(c)
            p = jnp.exp2(jnp.where(ok, band_scores(c, k), NEG_BIG)).astype(vaug_s.dtype)
            acc_s[band_rows(c), :] = jnp.dot(p, vt, preferred_element_type=F32)

        def tile(kt, carry):
            k, vt = full_operands(kt)
            for c in range(n_bands):
                p = jnp.exp2(band_scores(c, k)).astype(vaug_s.dtype)
                acc_s[band_rows(c), :] += jnp.dot(p, vt, preferred_element_type=F32)
            return carry

        lax.fori_loop(0, n_full, tile, 0)

    @pl.when(jnp.logical_not(unshifted_ok))
    def _():
        m_s[...] = jnp.full_like(m_s, NEG_BIG)
        acc_s[...] = jnp.zeros_like(acc_s)

        def softmax_pv(c, s, vt):
            rows = band_rows(c)
            m_old = m_s[rows, :]
            m_new = jnp.maximum(m_old, jnp.max(s, axis=-1, keepdims=True))
            alpha = jnp.exp2(m_old - m_new)
            p = jnp.exp2(s - m_new[:, 0:1]).astype(vaug_s.dtype)
            pv = jnp.dot(p, vt, preferred_element_type=F32)
            acc_s[rows, :] = jnp.concatenate([alpha, alpha], axis=1) * acc_s[rows, :] + pv
            m_s[rows, :] = m_new

        def tile(kt, carry):
            k, vt = full_operands(kt)
            for c in range(n_bands):
                softmax_pv(c, band_scores(c, k), vt)
            return carry

        lax.fori_loop(0, n_full, tile, 0)
        for c in range(n_bands):
            k, vt, ok = diagonal_operands(c)
            softmax_pv(c, jnp.where(ok, band_scores(c, k), NEG_BIG), vt)

    lam = (jnp.exp(jnp.sum(lq1_ref[...] * lk1_ref[...], axis=-1, keepdims=True))
           - jnp.exp(jnp.sum(lq2_ref[...] * lk2_ref[...], axis=-1, keepdims=True)) + lambda_init)
    scale = subw_ref[...] * (1.0 - lambda_init)
    for c in range(n_bands):
        acc = acc_s[c * 2 * band:(c + 1) * 2 * band, :]
        o_all = acc[:, :DA_VDIM] / acc[:, DA_VDIM:]
        o = o_all[:band] - lam * o_all[band:]
        ms = jnp.mean(o * o, axis=-1, keepdims=True)
        o = o * lax.rsqrt(ms + EPS) * scale
        rows = slice(c * band, (c + 1) * band)
        o_ref[rows, :] = (o * gb_ref[rows, :].astype(F32)).astype(o_ref.dtype)


def _attention(attn_act, gate_act, norms, lq1, lk1, lq2, lk2, subw, batch, seq, lambda_init):
    tq = ATT_TQ
    assert seq % tq == 0 and tq % ATT_TK == 0 and tq % ATT_BAND == 0 and ATT_BAND % CHUNK == 0
    q_tiles = seq // tq

    def head_spec(rows_blk, off):
        assert off % DA_VDIM == 0
        if rows_blk == seq:
            return pl.BlockSpec((seq, DA_VDIM), lambda b, hd, qi: (b, off // DA_VDIM + hd))
        return pl.BlockSpec((rows_blk, DA_VDIM), lambda b, hd, qi: (b * q_tiles + qi, off // DA_VDIM + hd))

    vec = lambda n: pl.BlockSpec((1, n), lambda b, hd, qi: (0, 0))
    kern = functools.partial(_attn_kernel, lambda_init=lambda_init)
    return pl.pallas_call(
        kern,
        grid=(batch, DA_HEADS, q_tiles),
        in_specs=[
            head_spec(tq, AT_Q),
            head_spec(seq, AT_K),
            head_spec(seq, AT_V),
            head_spec(tq, GA_GB),
            pl.BlockSpec((1, SUBLANES, LANES), lambda b, hd, qi: (b, 0, 0)),
            vec(DA_HEAD_DIM), vec(DA_HEAD_DIM), vec(DA_HEAD_DIM), vec(DA_HEAD_DIM),
            vec(DA_VDIM),
        ],
        out_specs=pl.BlockSpec((tq, DA_VDIM), lambda b, hd, qi: (b * q_tiles + qi, hd)),
        out_shape=jax.ShapeDtypeStruct((batch * seq, DA_WIDTH), BF16),
        scratch_shapes=[
            pltpu.VMEM((seq, 2 * DA_VDIM), BF16),
            pltpu.VMEM((2 * tq, LANES), F32),
            pltpu.VMEM((2 * tq, 2 * DA_VDIM), F32),
        ],
        compiler_params=pltpu.CompilerParams(
            dimension_semantics=("arbitrary", "arbitrary", "arbitrary"),
            vmem_limit_bytes=VMEM_LIMIT_BYTES),
        name="diffattn",
    )(attn_act, attn_act, attn_act, gate_act, norms, lq1, lk1, lq2, lk2, subw)


def _out_kernel(x_ref, ya_ref, ob_ref, g0_ref, g1_ref, wa_ref, wb_ref, wo_ref, lng_ref, lnb_ref,
                out_ref, *, alpha):
    d = functools.partial(jnp.dot, preferred_element_type=F32)
    branch_a = d(ya_ref[...], wa_ref[...])
    branch_b = d(ob_ref[...], wb_ref[...])
    merged = g0_ref[...].astype(F32) * branch_a + g1_ref[...].astype(F32) * branch_b
    y = d(merged.astype(BF16), wo_ref[...])
    r = alpha * x_ref[...] + y
    mu = jnp.mean(r, axis=-1, keepdims=True)
    rc = r - mu
    var = jnp.mean(rc * rc, axis=-1, keepdims=True)
    out_ref[...] = rc * lax.rsqrt(var + EPS) * lng_ref[...] + lnb_ref[...]


def _output_stage(x2d, y_ssd, o_att, gate_act, w_a, w_b, w_o, ln_g, ln_b, alpha):
    t_rows = x2d.shape[0]
    tm = OUT_TM
    assert t_rows % tm == 0 and GA_GM % D_MODEL == 0
    row = lambda width: pl.BlockSpec((tm, width), lambda i: (i, 0))
    full = lambda shape: pl.BlockSpec(shape, lambda i: (0, 0))
    weight = lambda shape: pl.BlockSpec(shape, lambda i: (0, 0), pipeline_mode=pl.Buffered(1))
    gate_spec = lambda k: pl.BlockSpec((tm, D_MODEL), lambda i: (i, GA_GM // D_MODEL + k))
    return pl.pallas_call(
        functools.partial(_out_kernel, alpha=alpha),
        grid=(t_rows // tm,),
        in_specs=[row(D_MODEL), row(SSD_WIDTH), row(DA_WIDTH), gate_spec(0), gate_spec(1),
                  weight((SSD_WIDTH, D_MODEL)), weight((DA_WIDTH, D_MODEL)), weight((D_MODEL, D_MODEL)),
                  full((1, D_MODEL)), full((1, D_MODEL))],
        out_specs=row(D_MODEL),
        out_shape=jax.ShapeDtypeStruct((t_rows, D_MODEL), F32),
        compiler_params=pltpu.CompilerParams(
            dimension_semantics=("arbitrary",),
            vmem_limit_bytes=VMEM_LIMIT_BYTES),
        name="outproj",
    )(x2d, y_ssd, o_att, gate_act, gate_act, w_a, w_b, w_o, ln_g, ln_b)


def _rope_tables(seq):
    pos = np.arange(seq, dtype=np.float64)
    inv_freq = ROPE_THETA ** (-np.arange(0, ROT_DIM, 2, dtype=np.float64) / ROT_DIM)
    ang = pos[:, None] * inv_freq[None, :]
    cos, sin = np.cos(ang).astype(np.float32), np.sin(ang).astype(np.float32)
    ones = np.ones((seq, LANES // 2 - ROT_DIM), np.float32)
    zeros = np.zeros_like(ones)
    cos_t = np.concatenate([cos, cos, ones, cos, cos, ones], axis=1)
    sin_t = np.concatenate([-sin, -sin, zeros, sin, sin, zeros], axis=1)
    return jnp.asarray(cos_t), jnp.asarray(sin_t)


def _head_lane_permutation():
    src = np.zeros(LANES, np.int64)
    for m in range(2):
        for d in range(DA_HEAD_DIM):
            if d < ROT_HALF:
                lane = m * ROT_HALF + d
            elif d < ROT_DIM:
                lane = LANES // 2 + m * ROT_HALF + (d - ROT_HALF)
            elif d < ROT_DIM + ROT_PASS:
                lane = ROT_DIM + m * ROT_PASS + (d - ROT_DIM)
            else:
                lane = LANES // 2 + ROT_DIM + m * ROT_PASS + (d - ROT_DIM - ROT_PASS)
            src[lane] = m * DA_HEAD_DIM + d
    perm = np.zeros((DA_WIDTH, DA_WIDTH), np.float32)
    for hd in range(DA_HEADS):
        perm[hd * LANES + src, hd * LANES + np.arange(LANES)] = 1.0
    return jnp.asarray(perm, BF16)


def _permute_head_lanes(w_bf16, perm):
    return jnp.dot(w_bf16, perm, preferred_element_type=BF16)


def _layer(x2d, batch, seq, depth_total, l, w_in, b_gate, conv_w, conv_b, dt_bias, a_log, d_skip,
           ssd_norm_w, lambda_q1, lambda_k1, lambda_q2, lambda_k2, subln_w, w_a, w_b, w_o, ln_g, ln_b,
           tables):
    alpha = (2.0 * depth_total) ** 0.25
    lambda_init = 0.8 - 0.6 * math.exp(-0.3 * l)
    offs = [0]
    for s in IN_SIZES:
        offs.append(offs[-1] + s)
    z_w, xbc_w, dt_w, q_w, k_w, v_w, gb_w, gm_w = [w_in[:, offs[n]:offs[n + 1]] for n in range(len(IN_SIZES))]

    w_gate = jnp.concatenate([z_w, gb_w, gm_w], axis=1).astype(BF16)
    perm = _head_lane_permutation()
    w_attn = jnp.concatenate([_permute_head_lanes(q_w.astype(BF16), perm),
                              _permute_head_lanes(k_w.astype(BF16), perm), v_w.astype(BF16)], axis=1)
    p_conv = jnp.concatenate([conv_w.astype(F32), conv_b[None, :].astype(F32),
                              jnp.zeros((SUBLANES - CONV_WIDTH - 1, CONV_CH), F32)], axis=0)

    gate_act = _proj_gate(x2d, w_gate, b_gate[None, :].astype(F32))
    attn_act, qk_norms = _proj_attn(x2d, w_attn, *tables, seq)
    conv_act, dt = _proj_conv(x2d, xbc_w.astype(BF16), p_conv, dt_w.astype(BF16),
                              dt_bias[None, :].astype(F32), seq)

    a_row = -jnp.exp(a_log.astype(F32))[None, :]
    dskip_row = jnp.repeat(d_skip.astype(F32), SSD_HEAD_DIM)[None, :]
    y_ssd = _ssd(conv_act, gate_act, dt, a_row, dskip_row, ssd_norm_w[None, :].astype(F32), batch, seq)

    o_att = _attention(attn_act, gate_act, qk_norms, lambda_q1[None, :], lambda_k1[None, :],
                       lambda_q2[None, :], lambda_k2[None, :], subln_w[None, :], batch, seq, lambda_init)

    return _output_stage(x2d, y_ssd, o_att, gate_act, w_a.astype(BF16), w_b.astype(BF16), w_o.astype(BF16),
                         ln_g[None, :], ln_b[None, :], alpha)


def kernel(x, w_in, b_gate, conv_w, conv_b, dt_bias, a_log, d_skip, ssd_norm_w, lambda_q1, lambda_k1,
           lambda_q2, lambda_k2, subln_w, w_a, w_b, w_o, ln_g, ln_b):
    batch, seq, _ = x.shape
    depth = w_in.shape[0]
    tables = _rope_tables(seq)
    x2d = x.reshape(batch * seq, D_MODEL)
    for l in range(depth):
        x2d = _layer(x2d, batch, seq, depth, l, w_in[l], b_gate[l], conv_w[l], conv_b[l], dt_bias[l],
                     a_log[l], d_skip[l], ssd_norm_w[l], lambda_q1[l], lambda_k1[l], lambda_q2[l],
                     lambda_k2[l], subln_w[l], w_a[l], w_b[l], w_o[l], ln_g[l], ln_b[l], tables)
    return x2d.reshape(batch, seq, D_MODEL)
```

```python
import functools
import math

import jax
import jax.numpy as jnp
import numpy as np
from jax import lax
from jax.experimental import pallas as pl
from jax.experimental.pallas import tpu as pltpu

F32 = jnp.float32
BF16 = jnp.bfloat16

D_MODEL = 1024
CHUNK = 64
SSD_HEADS = 16
SSD_HEAD_DIM = 64
SSD_WIDTH = SSD_HEADS * SSD_HEAD_DIM
SSD_GROUPS = 2
SSD_STATE = 128
CONV_WIDTH = 4
BC_WIDTH = SSD_GROUPS * SSD_STATE
CONV_CH = SSD_WIDTH + 2 * BC_WIDTH
DA_HEADS = 8
DA_HEAD_DIM = 64
DA_VDIM = 2 * DA_HEAD_DIM
DA_WIDTH = DA_HEADS * DA_VDIM
ROPE_THETA = 500000.0
ROT_DIM = DA_HEAD_DIM // 4
ROT_HALF = ROT_DIM // 2
N_BRANCH = 2
EPS = 1e-5
IN_SIZES = (SSD_WIDTH, CONV_CH, SSD_HEADS, DA_WIDTH, DA_WIDTH, DA_WIDTH, DA_WIDTH, N_BRANCH * D_MODEL)

GA_SZ, GA_GB, GA_GM = 0, SSD_WIDTH, SSD_WIDTH + DA_WIDTH
GA_COLS = GA_GM + N_BRANCH * D_MODEL
AT_Q, AT_K, AT_V = 0, DA_WIDTH, 2 * DA_WIDTH
AT_COLS = 3 * DA_WIDTH
CV_XS, CV_B, CV_C = 0, SSD_WIDTH, SSD_WIDTH + BC_WIDTH

ROT_PASS = (DA_HEAD_DIM - ROT_DIM) // 2

LANES = 128
SUBLANES = 8
BF16_ROWS = 16
VMEM_LIMIT_BYTES = 56 * 1024 * 1024

IN_TM = 512
IN_TM_WIDE = 1024
SSD_ROWS = 512
ATT_TQ = 2048
ATT_TK = 2048
ATT_BAND = 256
ATT_SAFE_LOG2 = 64.0
NORM_SLACK = 1.0 + 2.0 ** -7
OUT_TM = 1024

NEG_BIG = -1e30
LOG2E = 1.4426950408889634


def _sigmoid(t):
    return 0.5 * jnp.tanh(0.5 * t) + 0.5


def _split2(v):
    hi = v.astype(BF16)
    mid = (v - hi.astype(F32)).astype(BF16)
    return hi, mid


def _dot01_right(v, m01):
    hi, mid = _split2(v)
    d = functools.partial(jnp.dot, preferred_element_type=F32)
    return d(hi, m01) + d(mid, m01)


def _dot01_left(m01, v):
    hi, mid = _split2(v)
    d = functools.partial(jnp.dot, preferred_element_type=F32)
    return d(m01, hi) + d(m01, mid)


def _proj_params():
    return pltpu.CompilerParams(dimension_semantics=("arbitrary",),
                                vmem_limit_bytes=VMEM_LIMIT_BYTES)


def _proj_gate_kernel(x_ref, w_ref, bias_ref, o_ref):
    acc = jnp.dot(x_ref[...].astype(BF16), w_ref[...], preferred_element_type=F32)
    t = acc[:, :GA_GM]
    o_ref[:, :GA_GM] = (t * _sigmoid(t)).astype(o_ref.dtype)
    o_ref[:, GA_GM:] = _sigmoid(acc[:, GA_GM:] + bias_ref[...]).astype(o_ref.dtype)


def _proj_gate(x2d, w, b_gate_row):
    t_rows = x2d.shape[0]
    tm = IN_TM_WIDE
    assert t_rows % tm == 0
    full = lambda shape: pl.BlockSpec(shape, lambda i: (0, 0))
    return pl.pallas_call(
        _proj_gate_kernel,
        grid=(t_rows // tm,),
        in_specs=[
            pl.BlockSpec((tm, D_MODEL), lambda i: (i, 0)),
            pl.BlockSpec((D_MODEL, GA_COLS), lambda i: (0, 0), pipeline_mode=pl.Buffered(1)),
            full((1, N_BRANCH * D_MODEL)),
        ],
        out_specs=pl.BlockSpec((tm, GA_COLS), lambda i: (i, 0)),
        out_shape=jax.ShapeDtypeStruct((t_rows, GA_COLS), BF16),
        compiler_params=_proj_params(),
        name="proj_gate",
    )(x2d, w, b_gate_row)


def _proj_attn_kernel(x_ref, w_ref, cos_ref, sin_ref, sel_ref, o_ref, nrm_ref, *, q_scale, tiles_per_seq):
    @pl.when(pl.program_id(0) % tiles_per_seq == 0)
    def _():
        nrm_ref[...] = jnp.zeros_like(nrm_ref)

    acc = jnp.dot(x_ref[...].astype(BF16), w_ref[...], preferred_element_type=F32)
    cos = cos_ref[...]
    sin = sin_ref[...]
    cos_q = cos * q_scale
    sin_q = sin * q_scale
    squares = []
    for hd in range(AT_V // LANES):
        sl = slice(hd * LANES, (hd + 1) * LANES)
        a = acc[:, sl]
        partner = pltpu.roll(a, LANES // 2, axis=1)
        c, s = (cos_q, sin_q) if hd < AT_K // LANES else (cos, sin)
        rot = (a * c + partner * s).astype(o_ref.dtype)
        o_ref[:, sl] = rot
        rf = rot.astype(F32)
        squares.append((rf * rf).astype(BF16))
    o_ref[:, AT_V:] = acc[:, AT_V:].astype(o_ref.dtype)

    norms = jnp.dot(jnp.concatenate(squares, axis=1), sel_ref[...], preferred_element_type=F32)
    top = jnp.max(norms, axis=0, keepdims=True) * NORM_SLACK
    nrm_ref[0] = jnp.maximum(nrm_ref[0], jnp.broadcast_to(top, nrm_ref.shape[1:]))


def _proj_attn(x2d, w, cos_t, sin_t, seq):
    t_rows = x2d.shape[0]
    tm = IN_TM_WIDE
    assert t_rows % tm == 0 and seq % tm == 0
    tiles_per_seq = seq // tm
    table = lambda: pl.BlockSpec((tm, LANES), lambda i: (i % tiles_per_seq, 0))
    selector = jnp.asarray((np.arange(AT_V)[:, None] // LANES) == np.arange(LANES)[None, :], BF16)
    return pl.pallas_call(
        functools.partial(_proj_attn_kernel, q_scale=DA_HEAD_DIM ** -0.5 * LOG2E, tiles_per_seq=tiles_per_seq),
        grid=(t_rows // tm,),
        in_specs=[
            pl.BlockSpec((tm, D_MODEL), lambda i: (i, 0)),
            pl.BlockSpec((D_MODEL, AT_COLS), lambda i: (0, 0), pipeline_mode=pl.Buffered(1)),
            table(), table(),
            pl.BlockSpec((AT_V, LANES), lambda i: (0, 0)),
        ],
        out_specs=[
            pl.BlockSpec((tm, AT_COLS), lambda i: (i, 0)),
            pl.BlockSpec((1, SUBLANES, LANES), lambda i: (i // tiles_per_seq, 0, 0)),
        ],
        out_shape=[
            jax.ShapeDtypeStruct((t_rows, AT_COLS), BF16),
            jax.ShapeDtypeStruct((t_rows // seq, SUBLANES, LANES), F32),
        ],
        compiler_params=_proj_params(),
        name="proj_attn",
    )(x2d, w, cos_t, sin_t, selector)


def _proj_conv_kernel(x_ref, xh_ref, w_ref, p_ref, wdt_ref, dtb_ref, o_ref, dt_ref, *, tiles_per_seq):
    i = pl.program_id(0)
    tm = o_ref.shape[0]
    xb = x_ref[...].astype(BF16)
    w = w_ref[...]
    acc = jnp.dot(xb, w, preferred_element_type=F32)
    halo = jnp.dot(xh_ref[...].astype(BF16), w, preferred_element_type=F32)
    halo = jnp.where(i % tiles_per_seq == 0, 0.0, halo)

    bias = p_ref[CONV_WIDTH:CONV_WIDTH + 1, :]
    taps = [p_ref[k:k + 1, :] for k in range(CONV_WIDTH)]

    def conv(rows_val):
        out = taps[0] * rows_val
        for k in range(1, CONV_WIDTH):
            out = pltpu.roll(out, 1, axis=0) + taps[k] * rows_val
        out = out + bias
        return out * _sigmoid(out)

    head = BF16_ROWS
    ext = jnp.concatenate([halo, acc[:head]], axis=0)
    o_ref[:head, :] = conv(ext)[SUBLANES:, :].astype(o_ref.dtype)
    o_ref[head:, :] = conv(acc)[head:, :].astype(o_ref.dtype)

    raw = jnp.dot(xb, wdt_ref[...], preferred_element_type=F32) + dtb_ref[...]
    dt_ref[...] = jnp.maximum(raw, 0.0) + jnp.log(1.0 + jnp.exp(-jnp.abs(raw)))


def _proj_conv(x2d, w, params, w_dt, dt_bias, seq):
    t_rows = x2d.shape[0]
    tm = IN_TM
    assert t_rows % tm == 0 and seq % tm == 0
    tiles_per_seq = seq // tm
    halo_blocks = tm // SUBLANES
    full = lambda shape: pl.BlockSpec(shape, lambda i: (0, 0))
    return pl.pallas_call(
        functools.partial(_proj_conv_kernel, tiles_per_seq=tiles_per_seq),
        grid=(t_rows // tm,),
        in_specs=[
            pl.BlockSpec((tm, D_MODEL), lambda i: (i, 0)),
            pl.BlockSpec((SUBLANES, D_MODEL), lambda i: (jnp.maximum(i * halo_blocks - 1, 0), 0)),
            full((D_MODEL, CONV_CH)),
            full((SUBLANES, CONV_CH)),
            full((D_MODEL, SSD_HEADS)),
            full((1, SSD_HEADS)),
        ],
        out_specs=[
            pl.BlockSpec((tm, CONV_CH), lambda i: (i, 0)),
            pl.BlockSpec((tm, SSD_HEADS), lambda i: (i, 0)),
        ],
        out_shape=[
            jax.ShapeDtypeStruct((t_rows, CONV_CH), BF16),
            jax.ShapeDtypeStruct((t_rows, SSD_HEADS), F32),
        ],
        compiler_params=pltpu.CompilerParams(dimension_semantics=("arbitrary",),
                                             vmem_limit_bytes=VMEM_LIMIT_BYTES),
        name="proj_conv",
    )(x2d, x2d, w, params, w_dt, dt_bias)


def _ssd_kernel(xs_ref, b_ref, c_ref, sz_ref, dt_ref, a_ref, dskip_ref, normw_ref,
                expand_ref, bdtril_ref, tile8_ref,
                y_ref, acol_s, dtx_s, ht_s):
    rows = xs_ref.shape[0]
    n_chunks = rows // CHUNK
    gw = SSD_WIDTH // SSD_GROUPS

    @pl.when(pl.program_id(1) == 0)
    def _():
        ht_s[...] = jnp.zeros_like(ht_s)

    dt = dt_ref[...]
    a_cs = _dot01_left(bdtril_ref[...], dt * (a_ref[...] * LOG2E))
    expand = expand_ref[...]
    acol_s[...] = _dot01_right(a_cs, expand)
    dtx_s[...] = _dot01_right(dt, expand)

    qw = 4 * SSD_HEAD_DIM
    quads_per_group = gw // qw
    lane = lax.broadcasted_iota(jnp.int32, (CHUNK, qw), 1)
    row = lax.broadcasted_iota(jnp.int32, (CHUNK, qw), 0)
    s_of_lane = lane & (CHUNK - 1)
    diag_mask = s_of_lane == row
    causal_mask = s_of_lane <= row
    bd_r = lax.broadcasted_iota(jnp.int32, (4 * CHUNK, qw), 0) // CHUNK
    bd_c = lax.broadcasted_iota(jnp.int32, (4 * CHUNK, qw), 1) // SSD_HEAD_DIM
    bd_mask = bd_r == bd_c
    tile4 = tile8_ref[:, :qw]

    def chunk_body(c, carry):
        r0 = pl.multiple_of(c * CHUNK, CHUNK)
        rows_c = pl.ds(r0, CHUNK)
        bc = b_ref[rows_c, :]
        cc = c_ref[rows_c, :]
        for g in range(SSD_GROUPS):
            cg = cc[:, g * SSD_STATE:(g + 1) * SSD_STATE]
            bg = bc[:, g * SSD_STATE:(g + 1) * SSD_STATE]
            cb_g = lax.dot_general(cg, bg, (((1,), (1,)), ((), ())), preferred_element_type=F32)
            cb_rep = _dot01_right(cb_g, tile4)
            gated = []
            for qd in range(quads_per_group):
                sl = slice(g * gw + qd * qw, g * gw + (qd + 1) * qw)
                acol = acol_s[rows_c, sl]
                a_last = acol[CHUNK - 1:CHUNK, :]
                xs = xs_ref[rows_c, sl].astype(F32)
                xdt_f = xs * dtx_s[rows_c, sl]

                arow = jnp.sum(jnp.where(diag_mask, acol, 0.0), axis=0, keepdims=True)
                decay_ls = jnp.exp2(jnp.where(causal_mask, acol - arow, NEG_BIG))
                w_ls = (cb_rep * decay_ls).astype(BF16)
                xq = xdt_f.astype(BF16)
                bd = jnp.where(bd_mask, jnp.concatenate([xq] * 4, axis=0), jnp.zeros((), BF16))
                y_diag = jnp.dot(w_ls, bd, preferred_element_type=F32)

                ht = ht_s[:, sl]
                y_off = jnp.dot(cg, ht.astype(BF16), preferred_element_type=F32)
                y = y_diag + y_off * jnp.exp2(acol) + dskip_ref[:, sl] * xs
                gated.append(y * sz_ref[rows_c, sl].astype(F32))

                xd = (xdt_f * jnp.exp2(a_last - acol)).astype(BF16)
                st = lax.dot_general(bg, xd, (((0,), (0,)), ((), ())), preferred_element_type=F32)
                ht_s[:, sl] = ht * jnp.exp2(a_last) + st

            ssq = sum(jnp.sum(t * t, axis=-1, keepdims=True) for t in gated)
            inv = lax.rsqrt(ssq * (1.0 / gw) + EPS)
            for qd in range(quads_per_group):
                sl = slice(g * gw + qd * qw, g * gw + (qd + 1) * qw)
                y_ref[rows_c, sl] = (gated[qd] * inv * normw_ref[:, sl]).astype(y_ref.dtype)
        return carry

    lax.fori_loop(0, n_chunks, chunk_body, 0, unroll=4)


def _ssd(conv_act, gate_act, dt, a_row, dskip_row, normw_row, batch, seq):
    rows = SSD_ROWS
    assert seq % rows == 0 and rows % (4 * CHUNK) == 0
    blocks_per_seq = seq // rows
    hh = np.arange(SSD_WIDTH) // SSD_HEAD_DIM
    expand = jnp.asarray(np.arange(SSD_HEADS)[:, None] == hh[None, :], BF16)
    rr = np.arange(rows)
    same_chunk = (rr[:, None] // CHUNK) == (rr[None, :] // CHUNK)
    bdtril = jnp.asarray(same_chunk & (rr[None, :] <= rr[:, None]), BF16)
    gw = SSD_WIDTH // SSD_GROUPS
    tile8 = jnp.asarray(np.arange(CHUNK)[:, None] == (np.arange(gw)[None, :] % CHUNK), BF16)

    def row_map(b, r):
        return b * blocks_per_seq + r

    def col_spec(width, off):
        assert off % width == 0
        return pl.BlockSpec((rows, width), lambda b, r: (row_map(b, r), off // width))

    const = lambda shape: pl.BlockSpec(shape, lambda b, r: (0, 0))
    return pl.pallas_call(
        _ssd_kernel,
        grid=(batch, blocks_per_seq),
        in_specs=[
            col_spec(SSD_WIDTH, CV_XS),
            col_spec(BC_WIDTH, CV_B),
            col_spec(BC_WIDTH, CV_C),
            col_spec(SSD_WIDTH, GA_SZ),
            pl.BlockSpec((rows, SSD_HEADS), lambda b, r: (row_map(b, r), 0)),
            const((1, SSD_HEADS)),
            const((1, SSD_WIDTH)),
            const((1, SSD_WIDTH)),
            const((SSD_HEADS, SSD_WIDTH)),
            const((rows, rows)),
            const((CHUNK, gw)),
        ],
        out_specs=pl.BlockSpec((rows, SSD_WIDTH), lambda b, r: (row_map(b, r), 0)),
        out_shape=jax.ShapeDtypeStruct((batch * seq, SSD_WIDTH), BF16),
        scratch_shapes=[
            pltpu.VMEM((rows, SSD_WIDTH), F32),
            pltpu.VMEM((rows, SSD_WIDTH), F32),
            pltpu.VMEM((SSD_STATE, SSD_WIDTH), F32),
        ],
        compiler_params=pltpu.CompilerParams(
            dimension_semantics=("arbitrary", "arbitrary"),
            vmem_limit_bytes=VMEM_LIMIT_BYTES),
        name="ssd",
    )(conv_act, conv_act, conv_act, gate_act, dt, a_row, dskip_row, normw_row, expand, bdtril, tile8)


def _attn_kernel(q_ref, k_ref, v_ref, gb_ref, nrm_ref, lq1_ref, lk1_ref, lq2_ref, lk2_ref, subw_ref,
                 o_ref, vaug_s, m_s, acc_s, qq_s, *, lambda_init):
    hd = pl.program_id(1)
    qi = pl.program_id(2)
    tq = q_ref.shape[0]
    tk = ATT_TK
    seq = k_ref.shape[0]

    @pl.when(qi == 0)
    def _():
        vaug_s[:, :DA_VDIM] = v_ref[...].astype(vaug_s.dtype)
        vaug_s[:, DA_VDIM:] = jnp.ones((seq, DA_VDIM), vaug_s.dtype)

    assert tq % tk == 0
    band = ATT_BAND
    n_bands = tq // band
    n_full = qi * (tq // tk)
    half_lane = lax.broadcasted_iota(jnp.int32, (band, DA_VDIM), 1) & (LANES // 2 - 1)
    in_map0 = (half_lane < ROT_HALF) | ((half_lane >= ROT_DIM) & (half_lane < ROT_DIM + ROT_PASS))
    zero = jnp.zeros((), q_ref.dtype)

    qq = []
    for c in range(n_bands):
        q = q_ref[c * band:(c + 1) * band, :]
        qq.append(jnp.concatenate([jnp.where(in_map0, q, zero), jnp.where(in_map0, zero, q)], axis=0))

    def band_scores(c, k):
        return lax.dot_general(qq[c], k, (((1,), (1,)), ((), ())), preferred_element_type=F32)

    def band_rows(c):
        return slice(c * 2 * band, (c + 1) * 2 * band)

    def diagonal_operands(c):
        nk = (c + 1) * band
        k0 = pl.multiple_of(qi * tq, tq)
        r_chunk = (c * band + lax.broadcasted_iota(jnp.int32, (band, nk), 0)) // CHUNK
        k_chunk = lax.broadcasted_iota(jnp.int32, (band, nk), 1) // CHUNK
        ok = k_chunk <= r_chunk
        return k_ref[pl.ds(k0, nk), :], vaug_s[pl.ds(k0, nk), :], jnp.concatenate([ok, ok], axis=0)

    def full_operands(kt):
        k0 = pl.multiple_of(kt * tk, tk)
        return k_ref[pl.ds(k0, tk), :], vaug_s[pl.ds(k0, tk), :]

    norm_row = nrm_ref[0, 0:1, :]
    norm_lane = lax.broadcasted_iota(jnp.int32, norm_row.shape, 1)
    q_norm2 = jnp.max(jnp.where(norm_lane == hd, norm_row, 0.0))
    k_norm2 = jnp.max(jnp.where(norm_lane == DA_HEADS + hd, norm_row, 0.0))
    unshifted_ok = q_norm2 * k_norm2 <= ATT_SAFE_LOG2 * ATT_SAFE_LOG2

    @pl.when(unshifted_ok)
    def _():
        for c in range(n_bands):
            k, vt, ok = diagonal_operands(c)
            p = jnp.exp2(jnp.where(ok, band_scores(c, k), NEG_BIG)).astype(vaug_s.dtype)
            acc_s[band_rows(c), :] = jnp.dot(p, vt, preferred_element_type=F32)

        def tile(kt, carry):
            k, vt = full_operands(kt)
            for c in range(n_bands):
                p = jnp.exp2(band_scores(c, k)).astype(vaug_s.dtype)
                acc_s[band_rows(c), :] += jnp.dot(p, vt, preferred_element_type=F32)
            return carry

        lax.fori_loop(0, n_full, tile, 0)

    @pl.when(jnp.logical_not(unshifted_ok))
    def _():
        m_s[...] = jnp.full_like(m_s, NEG_BIG)
        acc_s[...] = jnp.zeros_like(acc_s)
        for c in range(n_bands):
            qq_s[c] = qq[c]

        def softmax_pv(c, k0, nk, diagonal):
            rows = pl.ds(pl.multiple_of(c * 2 * band, 2 * band), 2 * band)
            k, vt = k_ref[pl.ds(k0, nk), :], vaug_s[pl.ds(k0, nk), :]
            s = lax.dot_general(qq_s[c], k, (((1,), (1,)), ((), ())), preferred_element_type=F32)
            if diagonal:
                r_chunk = (c * band + lax.broadcasted_iota(jnp.int32, (band, nk), 0)) // CHUNK
                k_chunk = lax.broadcasted_iota(jnp.int32, (band, nk), 1) // CHUNK
                ok = k_chunk <= r_chunk
                s = jnp.where(jnp.concatenate([ok, ok], axis=0), s, NEG_BIG)
            m_old = m_s[rows, :]
            m_new = jnp.maximum(m_old, jnp.max(s, axis=-1, keepdims=True))
            alpha = jnp.exp2(m_old - m_new)
            p = jnp.exp2(s - m_new[:, 0:1]).astype(vaug_s.dtype)
            pv = jnp.dot(p, vt, preferred_element_type=F32)
            acc_s[rows, :] = jnp.concatenate([alpha, alpha], axis=1) * acc_s[rows, :] + pv
            m_s[rows, :] = m_new

        def band_loop(k0, nk, diagonal):
            def body(c, carry):
                softmax_pv(c, k0, nk, diagonal)
                return carry
            lax.fori_loop(0, n_bands, body, 0)

        def tile(kt, carry):
            band_loop(pl.multiple_of(kt * tk, tk), tk, False)
            return carry

        lax.fori_loop(0, n_full, tile, 0)
        band_loop(pl.multiple_of(qi * tq, tq), tq, True)

    lam = (jnp.exp(jnp.sum(lq1_ref[...] * lk1_ref[...], axis=-1, keepdims=True))
           - jnp.exp(jnp.sum(lq2_ref[...] * lk2_ref[...], axis=-1, keepdims=True)) + lambda_init)
    scale = subw_ref[...] * (1.0 - lambda_init)
    for c in range(n_bands):
        acc = acc_s[c * 2 * band:(c + 1) * 2 * band, :]
        o_all = acc[:, :DA_VDIM] / acc[:, DA_VDIM:]
        o = o_all[:band] - lam * o_all[band:]
        ms = jnp.mean(o * o, axis=-1, keepdims=True)
        o = o * lax.rsqrt(ms + EPS) * scale
        rows = slice(c * band, (c + 1) * band)
        o_ref[rows, :] = (o * gb_ref[rows, :].astype(F32)).astype(o_ref.dtype)


def _attention(attn_act, gate_act, norms, lq1, lk1, lq2, lk2, subw, batch, seq, lambda_init):
    tq = ATT_TQ
    assert seq % tq == 0 and tq % ATT_TK == 0 and tq % ATT_BAND == 0 and ATT_BAND % CHUNK == 0
    q_tiles = seq // tq

    def head_spec(rows_blk, off):
        assert off % DA_VDIM == 0
        if rows_blk == seq:
            return pl.BlockSpec((seq, DA_VDIM), lambda b, hd, qi: (b, off // DA_VDIM + hd))
        return pl.BlockSpec((rows_blk, DA_VDIM), lambda b, hd, qi: (b * q_tiles + qi, off // DA_VDIM + hd))

    vec = lambda n: pl.BlockSpec((1, n), lambda b, hd, qi: (0, 0))
    kern = functools.partial(_attn_kernel, lambda_init=lambda_init)
    return pl.pallas_call(
        kern,
        grid=(batch, DA_HEADS, q_tiles),
        in_specs=[
            head_spec(tq, AT_Q),
            head_spec(seq, AT_K),
            head_spec(seq, AT_V),
            head_spec(tq, GA_GB),
            pl.BlockSpec((1, SUBLANES, LANES), lambda b, hd, qi: (b, 0, 0)),
            vec(DA_HEAD_DIM), vec(DA_HEAD_DIM), vec(DA_HEAD_DIM), vec(DA_HEAD_DIM),
            vec(DA_VDIM),
        ],
        out_specs=pl.BlockSpec((tq, DA_VDIM), lambda b, hd, qi: (b * q_tiles + qi, hd)),
        out_shape=jax.ShapeDtypeStruct((batch * seq, DA_WIDTH), BF16),
        scratch_shapes=[
            pltpu.VMEM((seq, 2 * DA_VDIM), BF16),
            pltpu.VMEM((2 * tq, LANES), F32),
            pltpu.VMEM((2 * tq, 2 * DA_VDIM), F32),
            pltpu.VMEM((tq // ATT_BAND, 2 * ATT_BAND, DA_VDIM), BF16),
        ],
        compiler_params=pltpu.CompilerParams(
            dimension_semantics=("arbitrary", "arbitrary", "arbitrary"),
            vmem_limit_bytes=VMEM_LIMIT_BYTES),
        name="diffattn",
    )(attn_act, attn_act, attn_act, gate_act, norms, lq1, lk1, lq2, lk2, subw)


def _out_kernel(x_ref, ya_ref, ob_ref, g0_ref, g1_ref, wa_ref, wb_ref, wo_ref, lng_ref, lnb_ref,
                out_ref, *, alpha):
    d = functools.partial(jnp.dot, preferred_element_type=F32)
    branch_a = d(ya_ref[...], wa_ref[...])
    branch_b = d(ob_ref[...], wb_ref[...])
    merged = g0_ref[...].astype(F32) * branch_a + g1_ref[...].astype(F32) * branch_b
    y = d(merged.astype(BF16), wo_ref[...])
    r = alpha * x_ref[...] + y
    mu = jnp.mean(r, axis=-1, keepdims=True)
    rc = r - mu
    var = jnp.mean(rc * rc, axis=-1, keepdims=True)
    out_ref[...] = rc * lax.rsqrt(var + EPS) * lng_ref[...] + lnb_ref[...]


def _output_stage(x2d, y_ssd, o_att, gate_act, w_a, w_b, w_o, ln_g, ln_b, alpha):
    t_rows = x2d.shape[0]
    tm = OUT_TM
    assert t_rows % tm == 0 and GA_GM % D_MODEL == 0
    row = lambda width: pl.BlockSpec((tm, width), lambda i: (i, 0))
    full = lambda shape: pl.BlockSpec(shape, lambda i: (0, 0))
    weight = lambda shape: pl.BlockSpec(shape, lambda i: (0, 0), pipeline_mode=pl.Buffered(1))
    gate_spec = lambda k: pl.BlockSpec((tm, D_MODEL), lambda i: (i, GA_GM // D_MODEL + k))
    return pl.pallas_call(
        functools.partial(_out_kernel, alpha=alpha),
        grid=(t_rows // tm,),
        in_specs=[row(D_MODEL), row(SSD_WIDTH), row(DA_WIDTH), gate_spec(0), gate_spec(1),
                  weight((SSD_WIDTH, D_MODEL)), weight((DA_WIDTH, D_MODEL)), weight((D_MODEL, D_MODEL)),
                  full((1, D_MODEL)), full((1, D_MODEL))],
        out_specs=row(D_MODEL),
        out_shape=jax.ShapeDtypeStruct((t_rows, D_MODEL), F32),
        compiler_params=pltpu.CompilerParams(
            dimension_semantics=("arbitrary",),
            vmem_limit_bytes=VMEM_LIMIT_BYTES),
        name="outproj",
    )(x2d, y_ssd, o_att, gate_act, gate_act, w_a, w_b, w_o, ln_g, ln_b)


def _rope_tables(seq):
    pos = np.arange(seq, dtype=np.float64)
    inv_freq = ROPE_THETA ** (-np.arange(0, ROT_DIM, 2, dtype=np.float64) / ROT_DIM)
    ang = pos[:, None] * inv_freq[None, :]
    cos, sin = np.cos(ang).astype(np.float32), np.sin(ang).astype(np.float32)
    ones = np.ones((seq, LANES // 2 - ROT_DIM), np.float32)
    zeros = np.zeros_like(ones)
    cos_t = np.concatenate([cos, cos, ones, cos, cos, ones], axis=1)
    sin_t = np.concatenate([-sin, -sin, zeros, sin, sin, zeros], axis=1)
    return jnp.asarray(cos_t), jnp.asarray(sin_t)


def _head_lane_permutation():
    src = np.zeros(LANES, np.int64)
    for m in range(2):
        for d in range(DA_HEAD_DIM):
            if d < ROT_HALF:
                lane = m * ROT_HALF + d
            elif d < ROT_DIM:
                lane = LANES // 2 + m * ROT_HALF + (d - ROT_HALF)
            elif d < ROT_DIM + ROT_PASS:
                lane = ROT_DIM + m * ROT_PASS + (d - ROT_DIM)
            else:
                lane = LANES // 2 + ROT_DIM + m * ROT_PASS + (d - ROT_DIM - ROT_PASS)
            src[lane] = m * DA_HEAD_DIM + d
    perm = np.zeros((DA_WIDTH, DA_WIDTH), np.float32)
    for hd in range(DA_HEADS):
        perm[hd * LANES + src, hd * LANES + np.arange(LANES)] = 1.0
    return jnp.asarray(perm, BF16)


def _permute_head_lanes(w_bf16, perm):
    return jnp.dot(w_bf16, perm, preferred_element_type=BF16)


def _layer(x2d, batch, seq, depth_total, l, w_in, b_gate, conv_w, conv_b, dt_bias, a_log, d_skip,
           ssd_norm_w, lambda_q1, lambda_k1, lambda_q2, lambda_k2, subln_w, w_a, w_b, w_o, ln_g, ln_b,
           tables):
    alpha = (2.0 * depth_total) ** 0.25
    lambda_init = 0.8 - 0.6 * math.exp(-0.3 * l)
    offs = [0]
    for s in IN_SIZES:
        offs.append(offs[-1] + s)
    z_w, xbc_w, dt_w, q_w, k_w, v_w, gb_w, gm_w = [w_in[:, offs[n]:offs[n + 1]] for n in range(len(IN_SIZES))]

    w_gate = jnp.concatenate([z_w, gb_w, gm_w], axis=1).astype(BF16)
    perm = _head_lane_permutation()
    w_attn = jnp.concatenate([_permute_head_lanes(q_w.astype(BF16), perm),
                              _permute_head_lanes(k_w.astype(BF16), perm), v_w.astype(BF16)], axis=1)
    p_conv = jnp.concatenate([conv_w.astype(F32), conv_b[None, :].astype(F32),
                              jnp.zeros((SUBLANES - CONV_WIDTH - 1, CONV_CH), F32)], axis=0)

    gate_act = _proj_gate(x2d, w_gate, b_gate[None, :].astype(F32))
    attn_act, qk_norms = _proj_attn(x2d, w_attn, *tables, seq)
    conv_act, dt = _proj_conv(x2d, xbc_w.astype(BF16), p_conv, dt_w.astype(BF16),
                              dt_bias[None, :].astype(F32), seq)

    a_row = -jnp.exp(a_log.astype(F32))[None, :]
    dskip_row = jnp.repeat(d_skip.astype(F32), SSD_HEAD_DIM)[None, :]
    y_ssd = _ssd(conv_act, gate_act, dt, a_row, dskip_row, ssd_norm_w[None, :].astype(F32), batch, seq)

    o_att = _attention(attn_act, gate_act, qk_norms, lambda_q1[None, :], lambda_k1[None, :],
                       lambda_q2[None, :], lambda_k2[None, :], subln_w[None, :], batch, seq, lambda_init)

    return _output_stage(x2d, y_ssd, o_att, gate_act, w_a.astype(BF16), w_b.astype(BF16), w_o.astype(BF16),
                         ln_g[None, :], ln_b[None, :], alpha)


def kernel(x, w_in, b_gate, conv_w, conv_b, dt_bias, a_log, d_skip, ssd_norm_w, lambda_q1, lambda_k1,
           lambda_q2, lambda_k2, subln_w, w_a, w_b, w_o, ln_g, ln_b):
    batch, seq, _ = x.shape
    depth = w_in.shape[0]
    tables = _rope_tables(seq)
    x2d = x.reshape(batch * seq, D_MODEL)
    for l in range(depth):
        x2d = _layer(x2d, batch, seq, depth, l, w_in[l], b_gate[l], conv_w[l], conv_b[l], dt_bias[l],
                     a_log[l], d_skip[l], ssd_norm_w[l], lambda_q1[l], lambda_k1[l], lambda_q2[l],
                     lambda_k2[l], subln_w[l], w_a[l], w_b[l], w_o[l], ln_g[l], ln_b[l], tables)
    return x2d.reshape(batch, seq, D_MODEL)
```

```python
import functools
import math

import jax
import jax.numpy as jnp
import numpy as np
from jax import lax
from jax.experimental import pallas as pl
from jax.experimental.pallas import tpu as pltpu

F32 = jnp.float32
BF16 = jnp.bfloat16

D_MODEL = 1024
CHUNK = 64
SSD_HEADS = 16
SSD_HEAD_DIM = 64
SSD_WIDTH = SSD_HEADS * SSD_HEAD_DIM
SSD_GROUPS = 2
SSD_STATE = 128
CONV_WIDTH = 4
BC_WIDTH = SSD_GROUPS * SSD_STATE
CONV_CH = SSD_WIDTH + 2 * BC_WIDTH
DA_HEADS = 8
DA_HEAD_DIM = 64
DA_VDIM = 2 * DA_HEAD_DIM
DA_WIDTH = DA_HEADS * DA_VDIM
ROPE_THETA = 500000.0
ROT_DIM = DA_HEAD_DIM // 4
ROT_HALF = ROT_DIM // 2
N_BRANCH = 2
EPS = 1e-5
IN_SIZES = (SSD_WIDTH, CONV_CH, SSD_HEADS, DA_WIDTH, DA_WIDTH, DA_WIDTH, DA_WIDTH, N_BRANCH * D_MODEL)

GA_SZ, GA_GB, GA_GM = 0, SSD_WIDTH, SSD_WIDTH + DA_WIDTH
GA_COLS = GA_GM + N_BRANCH * D_MODEL
AT_Q, AT_K, AT_V = 0, DA_WIDTH, 2 * DA_WIDTH
AT_COLS = 3 * DA_WIDTH
CV_XS, CV_B, CV_C = 0, SSD_WIDTH, SSD_WIDTH + BC_WIDTH

ROT_PASS = (DA_HEAD_DIM - ROT_DIM) // 2

LANES = 128
SUBLANES = 8
BF16_ROWS = 16
VMEM_LIMIT_BYTES = 56 * 1024 * 1024

IN_TM = 512
IN_TM_WIDE = 1024
SSD_ROWS = 512
ATT_TQ = 2048
ATT_TK = 2048
ATT_BAND = 256
ATT_SAFE_LOG2 = 64.0
NORM_SLACK = 1.0 + 2.0 ** -7
OUT_TM = 1024

NEG_BIG = -1e30
LOG2E = 1.4426950408889634


def _sigmoid(t):
    return 0.5 * jnp.tanh(0.5 * t) + 0.5


def _split2(v):
    hi = v.astype(BF16)
    mid = (v - hi.astype(F32)).astype(BF16)
    return hi, mid


def _dot01_right(v, m01):
    hi, mid = _split2(v)
    d = functools.partial(jnp.dot, preferred_element_type=F32)
    return d(hi, m01) + d(mid, m01)


def _dot01_left(m01, v):
    hi, mid = _split2(v)
    d = functools.partial(jnp.dot, preferred_element_type=F32)
    return d(m01, hi) + d(m01, mid)


def _proj_params():
    return pltpu.CompilerParams(dimension_semantics=("arbitrary",),
                                vmem_limit_bytes=VMEM_LIMIT_BYTES)


def _proj_gate_kernel(x_ref, w_ref, bias_ref, o_ref):
    acc = jnp.dot(x_ref[...].astype(BF16), w_ref[...], preferred_element_type=F32)
    t = acc[:, :GA_GM]
    o_ref[:, :GA_GM] = (t * _sigmoid(t)).astype(o_ref.dtype)
    o_ref[:, GA_GM:] = _sigmoid(acc[:, GA_GM:] + bias_ref[...]).astype(o_ref.dtype)


def _proj_gate(x2d, w, b_gate_row):
    t_rows = x2d.shape[0]
    tm = IN_TM_WIDE
    assert t_rows % tm == 0
    full = lambda shape: pl.BlockSpec(shape, lambda i: (0, 0))
    return pl.pallas_call(
        _proj_gate_kernel,
        grid=(t_rows // tm,),
        in_specs=[
            pl.BlockSpec((tm, D_MODEL), lambda i: (i, 0)),
            pl.BlockSpec((D_MODEL, GA_COLS), lambda i: (0, 0), pipeline_mode=pl.Buffered(1)),
            full((1, N_BRANCH * D_MODEL)),
        ],
        out_specs=pl.BlockSpec((tm, GA_COLS), lambda i: (i, 0)),
        out_shape=jax.ShapeDtypeStruct((t_rows, GA_COLS), BF16),
        compiler_params=_proj_params(),
        name="proj_gate",
    )(x2d, w, b_gate_row)


def _proj_attn_kernel(x_ref, w_ref, cos_ref, sin_ref, sel_ref, o_ref, nrm_ref, *, q_scale, tiles_per_seq):
    @pl.when(pl.program_id(0) % tiles_per_seq == 0)
    def _():
        nrm_ref[...] = jnp.zeros_like(nrm_ref)

    acc = jnp.dot(x_ref[...].astype(BF16), w_ref[...], preferred_element_type=F32)
    cos = cos_ref[...]
    sin = sin_ref[...]
    cos_q = cos * q_scale
    sin_q = sin * q_scale
    squares = []
    for hd in range(AT_V // LANES):
        sl = slice(hd * LANES, (hd + 1) * LANES)
        a = acc[:, sl]
        partner = pltpu.roll(a, LANES // 2, axis=1)
        c, s = (cos_q, sin_q) if hd < AT_K // LANES else (cos, sin)
        rot = (a * c + partner * s).astype(o_ref.dtype)
        o_ref[:, sl] = rot
        rf = rot.astype(F32)
        squares.append((rf * rf).astype(BF16))
    o_ref[:, AT_V:] = acc[:, AT_V:].astype(o_ref.dtype)

    norms = jnp.dot(jnp.concatenate(squares, axis=1), sel_ref[...], preferred_element_type=F32)
    top = jnp.max(norms, axis=0, keepdims=True) * NORM_SLACK
    nrm_ref[0] = jnp.maximum(nrm_ref[0], jnp.broadcast_to(top, nrm_ref.shape[1:]))


def _proj_attn(x2d, w, cos_t, sin_t, seq):
    t_rows = x2d.shape[0]
    tm = IN_TM_WIDE
    assert t_rows % tm == 0 and seq % tm == 0
    tiles_per_seq = seq // tm
    table = lambda: pl.BlockSpec((tm, LANES), lambda i: (i % tiles_per_seq, 0))
    selector = jnp.asarray((np.arange(AT_V)[:, None] // LANES) == np.arange(LANES)[None, :], BF16)
    return pl.pallas_call(
        functools.partial(_proj_attn_kernel, q_scale=DA_HEAD_DIM ** -0.5 * LOG2E, tiles_per_seq=tiles_per_seq),
        grid=(t_rows // tm,),
        in_specs=[
            pl.BlockSpec((tm, D_MODEL), lambda i: (i, 0)),
            pl.BlockSpec((D_MODEL, AT_COLS), lambda i: (0, 0), pipeline_mode=pl.Buffered(1)),
            table(), table(),
            pl.BlockSpec((AT_V, LANES), lambda i: (0, 0)),
        ],
        out_specs=[
            pl.BlockSpec((tm, AT_COLS), lambda i: (i, 0)),
            pl.BlockSpec((1, SUBLANES, LANES), lambda i: (i // tiles_per_seq, 0, 0)),
        ],
        out_shape=[
            jax.ShapeDtypeStruct((t_rows, AT_COLS), BF16),
            jax.ShapeDtypeStruct((t_rows // seq, SUBLANES, LANES), F32),
        ],
        compiler_params=_proj_params(),
        name="proj_attn",
    )(x2d, w, cos_t, sin_t, selector)


def _proj_conv_kernel(x_ref, xh_ref, w_ref, p_ref, wdt_ref, dtb_ref, o_ref, dt_ref, *, tiles_per_seq):
    i = pl.program_id(0)
    tm = o_ref.shape[0]
    xb = x_ref[...].astype(BF16)
    w = w_ref[...]
    acc = jnp.dot(xb, w, preferred_element_type=F32)
    halo = jnp.dot(xh_ref[...].astype(BF16), w, preferred_element_type=F32)
    halo = jnp.where(i % tiles_per_seq == 0, 0.0, halo)

    bias = p_ref[CONV_WIDTH:CONV_WIDTH + 1, :]
    taps = [p_ref[k:k + 1, :] for k in range(CONV_WIDTH)]

    def conv(rows_val):
        out = taps[0] * rows_val
        for k in range(1, CONV_WIDTH):
            out = pltpu.roll(out, 1, axis=0) + taps[k] * rows_val
        out = out + bias
        return out * _sigmoid(out)

    head = BF16_ROWS
    ext = jnp.concatenate([halo, acc[:head]], axis=0)
    o_ref[:head, :] = conv(ext)[SUBLANES:, :].astype(o_ref.dtype)
    o_ref[head:, :] = conv(acc)[head:, :].astype(o_ref.dtype)

    raw = jnp.dot(xb, wdt_ref[...], preferred_element_type=F32) + dtb_ref[...]
    dt_ref[...] = jnp.maximum(raw, 0.0) + jnp.log(1.0 + jnp.exp(-jnp.abs(raw)))


def _proj_conv(x2d, w, params, w_dt, dt_bias, seq):
    t_rows = x2d.shape[0]
    tm = IN_TM
    assert t_rows % tm == 0 and seq % tm == 0
    tiles_per_seq = seq // tm
    halo_blocks = tm // SUBLANES
    full = lambda shape: pl.BlockSpec(shape, lambda i: (0, 0))
    return pl.pallas_call(
        functools.partial(_proj_conv_kernel, tiles_per_seq=tiles_per_seq),
        grid=(t_rows // tm,),
        in_specs=[
            pl.BlockSpec((tm, D_MODEL), lambda i: (i, 0)),
            pl.BlockSpec((SUBLANES, D_MODEL), lambda i: (jnp.maximum(i * halo_blocks - 1, 0), 0)),
            full((D_MODEL, CONV_CH)),
            full((SUBLANES, CONV_CH)),
            full((D_MODEL, SSD_HEADS)),
            full((1, SSD_HEADS)),
        ],
        out_specs=[
            pl.BlockSpec((tm, CONV_CH), lambda i: (i, 0)),
            pl.BlockSpec((tm, SSD_HEADS), lambda i: (i, 0)),
        ],
        out_shape=[
            jax.ShapeDtypeStruct((t_rows, CONV_CH), BF16),
            jax.ShapeDtypeStruct((t_rows, SSD_HEADS), F32),
        ],
        compiler_params=pltpu.CompilerParams(dimension_semantics=("arbitrary",),
                                             vmem_limit_bytes=VMEM_LIMIT_BYTES),
        name="proj_conv",
    )(x2d, x2d, w, params, w_dt, dt_bias)


def _ssd_kernel(xs_ref, b_ref, c_ref, sz_ref, dt_ref, a_ref, dskip_ref, normw_ref,
                expand_ref, bdtril_ref,
                y_ref, acol_s, dtx_s, ht_s):
    rows = xs_ref.shape[0]
    n_chunks = rows // CHUNK
    gw = SSD_WIDTH // SSD_GROUPS

    @pl.when(pl.program_id(1) == 0)
    def _():
        ht_s[...] = jnp.zeros_like(ht_s)

    dt = dt_ref[...]
    a_cs = _dot01_left(bdtril_ref[...], dt * (a_ref[...] * LOG2E))
    expand = expand_ref[...]
    acol_s[...] = _dot01_right(a_cs, expand)
    dtx_s[...] = _dot01_right(dt, expand)

    qw = 4 * SSD_HEAD_DIM
    quads_per_group = gw // qw
    lane = lax.broadcasted_iota(jnp.int32, (CHUNK, qw), 1)
    row = lax.broadcasted_iota(jnp.int32, (CHUNK, qw), 0)
    s_of_lane = lane & (CHUNK - 1)
    diag_mask = s_of_lane == row
    causal_mask = s_of_lane <= row
    bd_r = lax.broadcasted_iota(jnp.int32, (4 * CHUNK, qw), 0) // CHUNK
    bd_c = lax.broadcasted_iota(jnp.int32, (4 * CHUNK, qw), 1) // SSD_HEAD_DIM
    bd_mask = bd_r == bd_c

    def chunk_body(c, carry):
        r0 = pl.multiple_of(c * CHUNK, CHUNK)
        rows_c = pl.ds(r0, CHUNK)
        bc = b_ref[rows_c, :]
        cc = c_ref[rows_c, :]
        for g in range(SSD_GROUPS):
            cg = cc[:, g * SSD_STATE:(g + 1) * SSD_STATE]
            bg = bc[:, g * SSD_STATE:(g + 1) * SSD_STATE]
            cb_g = lax.dot_general(cg, bg, (((1,), (1,)), ((), ())), preferred_element_type=F32)
            cb_rep = jnp.concatenate([cb_g] * 4, axis=1)
            gated = []
            for qd in range(quads_per_group):
                sl = slice(g * gw + qd * qw, g * gw + (qd + 1) * qw)
                acol = acol_s[rows_c, sl]
                a_last = acol[CHUNK - 1:CHUNK, :]
                xs = xs_ref[rows_c, sl].astype(F32)
                xdt_f = xs * dtx_s[rows_c, sl]

                arow = jnp.sum(jnp.where(diag_mask, acol, 0.0), axis=0, keepdims=True)
                decay_ls = jnp.exp2(jnp.where(causal_mask, acol - arow, NEG_BIG))
                w_ls = (cb_rep * decay_ls).astype(BF16)
                xq = xdt_f.astype(BF16)
                bd = jnp.where(bd_mask, jnp.concatenate([xq] * 4, axis=0), jnp.zeros((), BF16))
                y_diag = jnp.dot(w_ls, bd, preferred_element_type=F32)

                ht = ht_s[:, sl]
                y_off = jnp.dot(cg, ht.astype(BF16), preferred_element_type=F32)
                y = y_diag + y_off * jnp.exp2(acol) + dskip_ref[:, sl] * xs
                gated.append(y * sz_ref[rows_c, sl].astype(F32))

                xd = (xdt_f * jnp.exp2(a_last - acol)).astype(BF16)
                st = lax.dot_general(bg, xd, (((0,), (0,)), ((), ())), preferred_element_type=F32)
                ht_s[:, sl] = ht * jnp.exp2(a_last) + st

            ssq = sum(jnp.sum(t * t, axis=-1, keepdims=True) for t in gated)
            inv = lax.rsqrt(ssq * (1.0 / gw) + EPS)
            for qd in range(quads_per_group):
                sl = slice(g * gw + qd * qw, g * gw + (qd + 1) * qw)
                y_ref[rows_c, sl] = (gated[qd] * inv * normw_ref[:, sl]).astype(y_ref.dtype)
        return carry

    lax.fori_loop(0, n_chunks, chunk_body, 0, unroll=4)


def _ssd(conv_act, gate_act, dt, a_row, dskip_row, normw_row, batch, seq):
    rows = SSD_ROWS
    assert seq % rows == 0 and rows % (4 * CHUNK) == 0
    blocks_per_seq = seq // rows
    hh = np.arange(SSD_WIDTH) // SSD_HEAD_DIM
    expand = jnp.asarray(np.arange(SSD_HEADS)[:, None] == hh[None, :], BF16)
    rr = np.arange(rows)
    same_chunk = (rr[:, None] // CHUNK) == (rr[None, :] // CHUNK)
    bdtril = jnp.asarray(same_chunk & (rr[None, :] <= rr[:, None]), BF16)

    def row_map(b, r):
        return b * blocks_per_seq + r

    def col_spec(width, off):
        assert off % width == 0
        return pl.BlockSpec((rows, width), lambda b, r: (row_map(b, r), off // width))

    const = lambda shape: pl.BlockSpec(shape, lambda b, r: (0, 0))
    return pl.pallas_call(
        _ssd_kernel,
        grid=(batch, blocks_per_seq),
        in_specs=[
            col_spec(SSD_WIDTH, CV_XS),
            col_spec(BC_WIDTH, CV_B),
            col_spec(BC_WIDTH, CV_C),
            col_spec(SSD_WIDTH, GA_SZ),
            pl.BlockSpec((rows, SSD_HEADS), lambda b, r: (row_map(b, r), 0)),
            const((1, SSD_HEADS)),
            const((1, SSD_WIDTH)),
            const((1, SSD_WIDTH)),
            const((SSD_HEADS, SSD_WIDTH)),
            const((rows, rows)),
        ],
        out_specs=pl.BlockSpec((rows, SSD_WIDTH), lambda b, r: (row_map(b, r), 0)),
        out_shape=jax.ShapeDtypeStruct((batch * seq, SSD_WIDTH), BF16),
        scratch_shapes=[
            pltpu.VMEM((rows, SSD_WIDTH), F32),
            pltpu.VMEM((rows, SSD_WIDTH), F32),
            pltpu.VMEM((SSD_STATE, SSD_WIDTH), F32),
        ],
        compiler_params=pltpu.CompilerParams(
            dimension_semantics=("arbitrary", "arbitrary"),
            vmem_limit_bytes=VMEM_LIMIT_BYTES),
        name="ssd",
    )(conv_act, conv_act, conv_act, gate_act, dt, a_row, dskip_row, normw_row, expand, bdtril)


def _attn_kernel(q_ref, k_ref, v_ref, gb_ref, nrm_ref, lq1_ref, lk1_ref, lq2_ref, lk2_ref, subw_ref,
                 o_ref, vaug_s, m_s, acc_s, qq_s, *, lambda_init):
    hd = pl.program_id(1)
    qi = pl.program_id(2)
    tq = q_ref.shape[0]
    tk = ATT_TK
    seq = k_ref.shape[0]

    @pl.when(qi == 0)
    def _():
        vaug_s[:, :DA_VDIM] = v_ref[...].astype(vaug_s.dtype)
        vaug_s[:, DA_VDIM:] = jnp.ones((seq, DA_VDIM), vaug_s.dtype)

    assert tq % tk == 0
    band = ATT_BAND
    n_bands = tq // band
    n_full = qi * (tq // tk)
    half_lane = lax.broadcasted_iota(jnp.int32, (band, DA_VDIM), 1) & (LANES // 2 - 1)
    in_map0 = (half_lane < ROT_HALF) | ((half_lane >= ROT_DIM) & (half_lane < ROT_DIM + ROT_PASS))
    zero = jnp.zeros((), q_ref.dtype)

    qq = []
    for c in range(n_bands):
        q = q_ref[c * band:(c + 1) * band, :]
        qq.append(jnp.concatenate([jnp.where(in_map0, q, zero), jnp.where(in_map0, zero, q)], axis=0))

    def band_scores(c, k):
        return lax.dot_general(qq[c], k, (((1,), (1,)), ((), ())), preferred_element_type=F32)

    def band_rows(c):
        return slice(c * 2 * band, (c + 1) * 2 * band)

    def diagonal_operands(c):
        nk = (c + 1) * band
        k0 = pl.multiple_of(qi * tq, tq)
        r_chunk = (c * band + lax.broadcasted_iota(jnp.int32, (band, nk), 0)) // CHUNK
        k_chunk = lax.broadcasted_iota(jnp.int32, (band, nk), 1) // CHUNK
        ok = k_chunk <= r_chunk
        return k_ref[pl.ds(k0, nk), :], vaug_s[pl.ds(k0, nk), :], jnp.concatenate([ok, ok], axis=0)

    def full_operands(kt):
        k0 = pl.multiple_of(kt * tk, tk)
        return k_ref[pl.ds(k0, tk), :], vaug_s[pl.ds(k0, tk), :]

    norm_row = nrm_ref[0, 0:1, :]
    norm_lane = lax.broadcasted_iota(jnp.int32, norm_row.shape, 1)
    q_norm2 = jnp.max(jnp.where(norm_lane == hd, norm_row, 0.0))
    k_norm2 = jnp.max(jnp.where(norm_lane == DA_HEADS + hd, norm_row, 0.0))
    unshifted_ok = q_norm2 * k_norm2 <= ATT_SAFE_LOG2 * ATT_SAFE_LOG2

    @pl.when(unshifted_ok)
    def _():
        for c in range(n_bands):
            k, vt, ok = diagonal_operands(c)
            p = jnp.exp2(jnp.where(ok, band_scores(c, k), NEG_BIG)).astype(vaug_s.dtype)
            acc_s[band_rows(c), :] = jnp.dot(p, vt, preferred_element_type=F32)

        def tile(kt, carry):
            k, vt = full_operands(kt)
            for c in range(n_bands):
                p = jnp.exp2(band_scores(c, k)).astype(vaug_s.dtype)
                acc_s[band_rows(c), :] += jnp.dot(p, vt, preferred_element_type=F32)
            return carry

        lax.fori_loop(0, n_full, tile, 0)

    @pl.when(jnp.logical_not(unshifted_ok))
    def _():
        m_s[...] = jnp.full_like(m_s, NEG_BIG)
        acc_s[...] = jnp.zeros_like(acc_s)
        for c in range(n_bands):
            qq_s[c] = qq[c]

        def softmax_pv(c, k0, nk, diagonal):
            rows = pl.ds(pl.multiple_of(c * 2 * band, 2 * band), 2 * band)
            k, vt = k_ref[pl.ds(k0, nk), :], vaug_s[pl.ds(k0, nk), :]
            s = lax.dot_general(qq_s[c], k, (((1,), (1,)), ((), ())), preferred_element_type=F32)
            if diagonal:
                r_chunk = (c * band + lax.broadcasted_iota(jnp.int32, (band, nk), 0)) // CHUNK
                k_chunk = lax.broadcasted_iota(jnp.int32, (band, nk), 1) // CHUNK
                ok = k_chunk <= r_chunk
                s = jnp.where(jnp.concatenate([ok, ok], axis=0), s, NEG_BIG)
            m_old = m_s[rows, :]
            m_new = jnp.maximum(m_old, jnp.max(s, axis=-1, keepdims=True))
            alpha = jnp.exp2(m_old - m_new)
            p = jnp.exp2(s - m_new[:, 0:1]).astype(vaug_s.dtype)
            pv = jnp.dot(p, vt, preferred_element_type=F32)
            acc_s[rows, :] = jnp.concatenate([alpha, alpha], axis=1) * acc_s[rows, :] + pv
            m_s[rows, :] = m_new

        def band_loop(k0, nk, diagonal):
            def body(c, carry):
                softmax_pv(c, k0, nk, diagonal)
                return carry
            lax.fori_loop(0, n_bands, body, 0)

        def tile(kt, carry):
            band_loop(pl.multiple_of(kt * tk, tk), tk, False)
            return carry

        lax.fori_loop(0, n_full, tile, 0)
        band_loop(pl.multiple_of(qi * tq, tq), tq, True)

    lam = (jnp.exp(jnp.sum(lq1_ref[...] * lk1_ref[...], axis=-1, keepdims=True))
           - jnp.exp(jnp.sum(lq2_ref[...] * lk2_ref[...], axis=-1, keepdims=True)) + lambda_init)
    scale = subw_ref[...] * (1.0 - lambda_init)
    for c in range(n_bands):
        acc = acc_s[c * 2 * band:(c + 1) * 2 * band, :]
        o_all = acc[:, :DA_VDIM] / acc[:, DA_VDIM:]
        o = o_all[:band] - lam * o_all[band:]
        ms = jnp.mean(o * o, axis=-1, keepdims=True)
        o = o * lax.rsqrt(ms + EPS) * scale
        rows = slice(c * band, (c + 1) * band)
        o_ref[rows, :] = (o * gb_ref[rows, :].astype(F32)).astype(o_ref.dtype)


def _attention(attn_act, gate_act, norms, lq1, lk1, lq2, lk2, subw, batch, seq, lambda_init):
    tq = ATT_TQ
    assert seq % tq == 0 and tq % ATT_TK == 0 and tq % ATT_BAND == 0 and ATT_BAND % CHUNK == 0
    q_tiles = seq // tq

    def head_spec(rows_blk, off):
        assert off % DA_VDIM == 0
        if rows_blk == seq:
            return pl.BlockSpec((seq, DA_VDIM), lambda b, hd, qi: (b, off // DA_VDIM + hd))
        return pl.BlockSpec((rows_blk, DA_VDIM), lambda b, hd, qi: (b * q_tiles + qi, off // DA_VDIM + hd))

    vec = lambda n: pl.BlockSpec((1, n), lambda b, hd, qi: (0, 0))
    kern = functools.partial(_attn_kernel, lambda_init=lambda_init)
    return pl.pallas_call(
        kern,
        grid=(batch, DA_HEADS, q_tiles),
        in_specs=[
            head_spec(tq, AT_Q),
            head_spec(seq, AT_K),
            head_spec(seq, AT_V),
            head_spec(tq, GA_GB),
            pl.BlockSpec((1, SUBLANES, LANES), lambda b, hd, qi: (b, 0, 0)),
            vec(DA_HEAD_DIM), vec(DA_HEAD_DIM), vec(DA_HEAD_DIM), vec(DA_HEAD_DIM),
            vec(DA_VDIM),
        ],
        out_specs=pl.BlockSpec((tq, DA_VDIM), lambda b, hd, qi: (b * q_tiles + qi, hd)),
        out_shape=jax.ShapeDtypeStruct((batch * seq, DA_WIDTH), BF16),
        scratch_shapes=[
            pltpu.VMEM((seq, 2 * DA_VDIM), BF16),
            pltpu.VMEM((2 * tq, LANES), F32),
            pltpu.VMEM((2 * tq, 2 * DA_VDIM), F32),
            pltpu.VMEM((tq // ATT_BAND, 2 * ATT_BAND, DA_VDIM), BF16),
        ],
        compiler_params=pltpu.CompilerParams(
            dimension_semantics=("arbitrary", "arbitrary", "arbitrary"),
            vmem_limit_bytes=VMEM_LIMIT_BYTES),
        name="diffattn",
    )(attn_act, attn_act, attn_act, gate_act, norms, lq1, lk1, lq2, lk2, subw)


def _out_kernel(x_ref, ya_ref, ob_ref, g0_ref, g1_ref, wa_ref, wb_ref, wo_ref, lng_ref, lnb_ref,
                out_ref, *, alpha):
    d = functools.partial(jnp.dot, preferred_element_type=F32)
    branch_a = d(ya_ref[...], wa_ref[...])
    branch_b = d(ob_ref[...], wb_ref[...])
    merged = g0_ref[...].astype(F32) * branch_a + g1_ref[...].astype(F32) * branch_b
    y = d(merged.astype(BF16), wo_ref[...])
    r = alpha * x_ref[...] + y
    mu = jnp.mean(r, axis=-1, keepdims=True)
    rc = r - mu
    var = jnp.mean(rc * rc, axis=-1, keepdims=True)
    out_ref[...] = rc * lax.rsqrt(var + EPS) * lng_ref[...] + lnb_ref[...]


def _output_stage(x2d, y_ssd, o_att, gate_act, w_a, w_b, w_o, ln_g, ln_b, alpha):
    t_rows = x2d.shape[0]
    tm = OUT_TM
    assert t_rows % tm == 0 and GA_GM % D_MODEL == 0
    row = lambda width: pl.BlockSpec((tm, width), lambda i: (i, 0))
    full = lambda shape: pl.BlockSpec(shape, lambda i: (0, 0))
    weight = lambda shape: pl.BlockSpec(shape, lambda i: (0, 0), pipeline_mode=pl.Buffered(1))
    gate_spec = lambda k: pl.BlockSpec((tm, D_MODEL), lambda i: (i, GA_GM // D_MODEL + k))
    return pl.pallas_call(
        functools.partial(_out_kernel, alpha=alpha),
        grid=(t_rows // tm,),
        in_specs=[row(D_MODEL), row(SSD_WIDTH), row(DA_WIDTH), gate_spec(0), gate_spec(1),
                  weight((SSD_WIDTH, D_MODEL)), weight((DA_WIDTH, D_MODEL)), weight((D_MODEL, D_MODEL)),
                  full((1, D_MODEL)), full((1, D_MODEL))],
        out_specs=row(D_MODEL),
        out_shape=jax.ShapeDtypeStruct((t_rows, D_MODEL), F32),
        compiler_params=pltpu.CompilerParams(
            dimension_semantics=("arbitrary",),
            vmem_limit_bytes=VMEM_LIMIT_BYTES),
        name="outproj",
    )(x2d, y_ssd, o_att, gate_act, gate_act, w_a, w_b, w_o, ln_g, ln_b)


def _rope_tables(seq):
    pos = np.arange(seq, dtype=np.float64)
    inv_freq = ROPE_THETA ** (-np.arange(0, ROT_DIM, 2, dtype=np.float64) / ROT_DIM)
    ang = pos[:, None] * inv_freq[None, :]
    cos, sin = np.cos(ang).astype(np.float32), np.sin(ang).astype(np.float32)
    ones = np.ones((seq, LANES // 2 - ROT_DIM), np.float32)
    zeros = np.zeros_like(ones)
    cos_t = np.concatenate([cos, cos, ones, cos, cos, ones], axis=1)
    sin_t = np.concatenate([-sin, -sin, zeros, sin, sin, zeros], axis=1)
    return jnp.asarray(cos_t), jnp.asarray(sin_t)


def _head_lane_permutation():
    src = np.zeros(LANES, np.int64)
    for m in range(2):
        for d in range(DA_HEAD_DIM):
            if d < ROT_HALF:
                lane = m * ROT_HALF + d
            elif d < ROT_DIM:
                lane = LANES // 2 + m * ROT_HALF + (d - ROT_HALF)
            elif d < ROT_DIM + ROT_PASS:
                lane = ROT_DIM + m * ROT_PASS + (d - ROT_DIM)
            else:
                lane = LANES // 2 + ROT_DIM + m * ROT_PASS + (d - ROT_DIM - ROT_PASS)
            src[lane] = m * DA_HEAD_DIM + d
    perm = np.zeros((DA_WIDTH, DA_WIDTH), np.float32)
    for hd in range(DA_HEADS):
        perm[hd * LANES + src, hd * LANES + np.arange(LANES)] = 1.0
    return jnp.asarray(perm, BF16)


def _permute_head_lanes(w_bf16, perm):
    return jnp.dot(w_bf16, perm, preferred_element_type=BF16)


def _layer(x2d, batch, seq, depth_total, l, w_in, b_gate, conv_w, conv_b, dt_bias, a_log, d_skip,
           ssd_norm_w, lambda_q1, lambda_k1, lambda_q2, lambda_k2, subln_w, w_a, w_b, w_o, ln_g, ln_b,
           tables):
    alpha = (2.0 * depth_total) ** 0.25
    lambda_init = 0.8 - 0.6 * math.exp(-0.3 * l)
    offs = [0]
    for s in IN_SIZES:
        offs.append(offs[-1] + s)
    z_w, xbc_w, dt_w, q_w, k_w, v_w, gb_w, gm_w = [w_in[:, offs[n]:offs[n + 1]] for n in range(len(IN_SIZES))]

    w_gate = jnp.concatenate([z_w, gb_w, gm_w], axis=1).astype(BF16)
    perm = _head_lane_permutation()
    w_attn = jnp.concatenate([_permute_head_lanes(q_w.astype(BF16), perm),
                              _permute_head_lanes(k_w.astype(BF16), perm), v_w.astype(BF16)], axis=1)
    p_conv = jnp.concatenate([conv_w.astype(F32), conv_b[None, :].astype(F32),
                              jnp.zeros((SUBLANES - CONV_WIDTH - 1, CONV_CH), F32)], axis=0)

    gate_act = _proj_gate(x2d, w_gate, b_gate[None, :].astype(F32))
    attn_act, qk_norms = _proj_attn(x2d, w_attn, *tables, seq)
    conv_act, dt = _proj_conv(x2d, xbc_w.astype(BF16), p_conv, dt_w.astype(BF16),
                              dt_bias[None, :].astype(F32), seq)

    a_row = -jnp.exp(a_log.astype(F32))[None, :]
    dskip_row = jnp.repeat(d_skip.astype(F32), SSD_HEAD_DIM)[None, :]
    y_ssd = _ssd(conv_act, gate_act, dt, a_row, dskip_row, ssd_norm_w[None, :].astype(F32), batch, seq)

    o_att = _attention(attn_act, gate_act, qk_norms, lambda_q1[None, :], lambda_k1[None, :],
                       lambda_q2[None, :], lambda_k2[None, :], subln_w[None, :], batch, seq, lambda_init)

    return _output_stage(x2d, y_ssd, o_att, gate_act, w_a.astype(BF16), w_b.astype(BF16), w_o.astype(BF16),
                         ln_g[None, :], ln_b[None, :], alpha)


def kernel(x, w_in, b_gate, conv_w, conv_b, dt_bias, a_log, d_skip, ssd_norm_w, lambda_q1, lambda_k1,
           lambda_q2, lambda_k2, subln_w, w_a, w_b, w_o, ln_g, ln_b):
    batch, seq, _ = x.shape
    depth = w_in.shape[0]
    tables = _rope_tables(seq)
    x2d = x.reshape(batch * seq, D_MODEL)
    for l in range(depth):
        x2d = _layer(x2d, batch, seq, depth, l, w_in[l], b_gate[l], conv_w[l], conv_b[l], dt_bias[l],
                     a_log[l], d_skip[l], ssd_norm_w[l], lambda_q1[l], lambda_k1[l], lambda_q2[l],
                     lambda_k2[l], subln_w[l], w_a[l], w_b[l], w_o[l], ln_g[l], ln_b[l], tables)
    return x2d.reshape(batch, seq, D_MODEL)
```

```python
import functools
import math

import jax
import jax.numpy as jnp
import numpy as np
from jax import lax
from jax.experimental import pallas as pl
from jax.experimental.pallas import tpu as pltpu

F32 = jnp.float32
BF16 = jnp.bfloat16

D_MODEL = 1024
CHUNK = 64
SSD_HEADS = 16
SSD_HEAD_DIM = 64
SSD_WIDTH = SSD_HEADS * SSD_HEAD_DIM
SSD_GROUPS = 2
SSD_STATE = 128
CONV_WIDTH = 4
BC_WIDTH = SSD_GROUPS * SSD_STATE
CONV_CH = SSD_WIDTH + 2 * BC_WIDTH
DA_HEADS = 8
DA_HEAD_DIM = 64
DA_VDIM = 2 * DA_HEAD_DIM
DA_WIDTH = DA_HEADS * DA_VDIM
ROPE_THETA = 500000.0
ROT_DIM = DA_HEAD_DIM // 4
ROT_HALF = ROT_DIM // 2
N_BRANCH = 2
EPS = 1e-5
IN_SIZES = (SSD_WIDTH, CONV_CH, SSD_HEADS, DA_WIDTH, DA_WIDTH, DA_WIDTH, DA_WIDTH, N_BRANCH * D_MODEL)
W_OFFS = tuple(sum(IN_SIZES[:n]) for n in range(len(IN_SIZES) + 1))

GA_SZ, GA_GB, GA_GM = 0, SSD_WIDTH, SSD_WIDTH + DA_WIDTH
GA_COLS = GA_GM + N_BRANCH * D_MODEL
AT_Q, AT_K, AT_V = 0, DA_WIDTH, 2 * DA_WIDTH
AT_COLS = 3 * DA_WIDTH
CV_XS, CV_B, CV_C = 0, SSD_WIDTH, SSD_WIDTH + BC_WIDTH

ROT_PASS = (DA_HEAD_DIM - ROT_DIM) // 2

LANES = 128
SUBLANES = 8
BF16_ROWS = 16
VMEM_LIMIT_BYTES = 56 * 1024 * 1024

PREP_TR = 128
IN_TM = 512
IN_TM_WIDE = 1024
SSD_ROWS = 512
ATT_TQ = 2048
ATT_TK = 2048
ATT_BAND = 256
ATT_SAFE_LOG2 = 64.0
NORM_SLACK = 1.0 + 2.0 ** -7
OUT_TM = 1024

NEG_BIG = -1e30
LOG2E = 1.4426950408889634


def _sigmoid(t):
    return 0.5 * jnp.tanh(0.5 * t) + 0.5


def _split2(v):
    hi = v.astype(BF16)
    mid = (v - hi.astype(F32)).astype(BF16)
    return hi, mid


def _dot01_right(v, m01):
    hi, mid = _split2(v)
    d = functools.partial(jnp.dot, preferred_element_type=F32)
    return d(hi, m01) + d(mid, m01)


def _dot01_left(m01, v):
    hi, mid = _split2(v)
    d = functools.partial(jnp.dot, preferred_element_type=F32)
    return d(m01, hi) + d(m01, mid)


def _proj_params():
    return pltpu.CompilerParams(dimension_semantics=("arbitrary",),
                                vmem_limit_bytes=VMEM_LIMIT_BYTES)


def _prep_kernel(w_ref, perm_ref, wg_ref, wa_ref, wc_ref, wd_ref):
    def cols(n):
        return w_ref[:, W_OFFS[n]:W_OFFS[n + 1]].astype(BF16)

    wg_ref[:, GA_SZ:GA_GB] = cols(0)
    wg_ref[:, GA_GB:GA_GM] = cols(6)
    wg_ref[:, GA_GM:] = cols(7)
    wc_ref[...] = cols(1)
    wd_ref[...] = cols(2)
    perm = perm_ref[...]
    for base, n in ((AT_Q, 3), (AT_K, 4)):
        w = cols(n)
        for hd in range(DA_HEADS):
            moved = jnp.dot(w[:, hd * LANES:(hd + 1) * LANES], perm, preferred_element_type=F32)
            wa_ref[:, base + hd * LANES:base + (hd + 1) * LANES] = moved.astype(BF16)
    wa_ref[:, AT_V:] = cols(5)


def _prep_weights(w_in_l):
    tr = PREP_TR
    d_in, total = w_in_l.shape
    assert total == W_OFFS[-1] and d_in % tr == 0
    perm = _head_lane_permutation()
    widths = (GA_COLS, AT_COLS, CONV_CH, SSD_HEADS)
    return pl.pallas_call(
        _prep_kernel,
        grid=(d_in // tr,),
        in_specs=[
            pl.BlockSpec((tr, total), lambda i: (i, 0)),
            pl.BlockSpec((LANES, LANES), lambda i: (0, 0)),
        ],
        out_specs=tuple(pl.BlockSpec((tr, wd), lambda i: (i, 0)) for wd in widths),
        out_shape=tuple(jax.ShapeDtypeStruct((d_in, wd), BF16) for wd in widths),
        compiler_params=_proj_params(),
        name="prep_weights",
    )(w_in_l, perm)


def _proj_gate_kernel(x_ref, w_ref, bias_ref, o_ref):
    acc = jnp.dot(x_ref[...].astype(BF16), w_ref[...], preferred_element_type=F32)
    t = acc[:, :GA_GM]
    o_ref[:, :GA_GM] = (t * _sigmoid(t)).astype(o_ref.dtype)
    o_ref[:, GA_GM:] = _sigmoid(acc[:, GA_GM:] + bias_ref[...]).astype(o_ref.dtype)


def _proj_gate(x2d, w, b_gate_row):
    t_rows = x2d.shape[0]
    tm = IN_TM_WIDE
    assert t_rows % tm == 0
    full = lambda shape: pl.BlockSpec(shape, lambda i: (0, 0))
    return pl.pallas_call(
        _proj_gate_kernel,
        grid=(t_rows // tm,),
        in_specs=[
            pl.BlockSpec((tm, D_MODEL), lambda i: (i, 0)),
            pl.BlockSpec((D_MODEL, GA_COLS), lambda i: (0, 0), pipeline_mode=pl.Buffered(1)),
            full((1, N_BRANCH * D_MODEL)),
        ],
        out_specs=pl.BlockSpec((tm, GA_COLS), lambda i: (i, 0)),
        out_shape=jax.ShapeDtypeStruct((t_rows, GA_COLS), BF16),
        compiler_params=_proj_params(),
        name="proj_gate",
    )(x2d, w, b_gate_row)


def _proj_attn_kernel(x_ref, w_ref, cos_ref, sin_ref, sel_ref, o_ref, nrm_ref, *, q_scale, tiles_per_seq):
    @pl.when(pl.program_id(0) % tiles_per_seq == 0)
    def _():
        nrm_ref[...] = jnp.zeros_like(nrm_ref)

    acc = jnp.dot(x_ref[...].astype(BF16), w_ref[...], preferred_element_type=F32)
    cos = cos_ref[...]
    sin = sin_ref[...]
    cos_q = cos * q_scale
    sin_q = sin * q_scale
    squares = []
    for hd in range(AT_V // LANES):
        sl = slice(hd * LANES, (hd + 1) * LANES)
        a = acc[:, sl]
        partner = pltpu.roll(a, LANES // 2, axis=1)
        c, s = (cos_q, sin_q) if hd < AT_K // LANES else (cos, sin)
        rot = (a * c + partner * s).astype(o_ref.dtype)
        o_ref[:, sl] = rot
        rf = rot.astype(F32)
        squares.append((rf * rf).astype(BF16))
    o_ref[:, AT_V:] = acc[:, AT_V:].astype(o_ref.dtype)

    norms = jnp.dot(jnp.concatenate(squares, axis=1), sel_ref[...], preferred_element_type=F32)
    top = jnp.max(norms, axis=0, keepdims=True) * NORM_SLACK
    nrm_ref[0] = jnp.maximum(nrm_ref[0], jnp.broadcast_to(top, nrm_ref.shape[1:]))


def _proj_attn(x2d, w, cos_t, sin_t, seq):
    t_rows = x2d.shape[0]
    tm = IN_TM_WIDE
    assert t_rows % tm == 0 and seq % tm == 0
    tiles_per_seq = seq // tm
    table = lambda: pl.BlockSpec((tm, LANES), lambda i: (i % tiles_per_seq, 0))
    selector = jnp.asarray((np.arange(AT_V)[:, None] // LANES) == np.arange(LANES)[None, :], BF16)
    return pl.pallas_call(
        functools.partial(_proj_attn_kernel, q_scale=DA_HEAD_DIM ** -0.5 * LOG2E, tiles_per_seq=tiles_per_seq),
        grid=(t_rows // tm,),
        in_specs=[
            pl.BlockSpec((tm, D_MODEL), lambda i: (i, 0)),
            pl.BlockSpec((D_MODEL, AT_COLS), lambda i: (0, 0), pipeline_mode=pl.Buffered(1)),
            table(), table(),
            pl.BlockSpec((AT_V, LANES), lambda i: (0, 0)),
        ],
        out_specs=[
            pl.BlockSpec((tm, AT_COLS), lambda i: (i, 0)),
            pl.BlockSpec((1, SUBLANES, LANES), lambda i: (i // tiles_per_seq, 0, 0)),
        ],
        out_shape=[
            jax.ShapeDtypeStruct((t_rows, AT_COLS), BF16),
            jax.ShapeDtypeStruct((t_rows // seq, SUBLANES, LANES), F32),
        ],
        compiler_params=_proj_params(),
        name="proj_attn",
    )(x2d, w, cos_t, sin_t, selector)


def _proj_conv_kernel(x_ref, xh_ref, w_ref, p_ref, wdt_ref, dtb_ref, o_ref, dt_ref, *, tiles_per_seq):
    i = pl.program_id(0)
    tm = o_ref.shape[0]
    xb = x_ref[...].astype(BF16)
    w = w_ref[...]
    acc = jnp.dot(xb, w, preferred_element_type=F32)
    halo = jnp.dot(xh_ref[...].astype(BF16), w, preferred_element_type=F32)
    halo = jnp.where(i % tiles_per_seq == 0, 0.0, halo)

    bias = p_ref[CONV_WIDTH:CONV_WIDTH + 1, :]
    taps = [p_ref[k:k + 1, :] for k in range(CONV_WIDTH)]

    def conv(rows_val):
        out = taps[0] * rows_val
        for k in range(1, CONV_WIDTH):
            out = pltpu.roll(out, 1, axis=0) + taps[k] * rows_val
        out = out + bias
        return out * _sigmoid(out)

    head = BF16_ROWS
    ext = jnp.concatenate([halo, acc[:head]], axis=0)
    o_ref[:head, :] = conv(ext)[SUBLANES:, :].astype(o_ref.dtype)
    o_ref[head:, :] = conv(acc)[head:, :].astype(o_ref.dtype)

    raw = jnp.dot(xb, wdt_ref[...], preferred_element_type=F32) + dtb_ref[...]
    dt_ref[...] = jnp.maximum(raw, 0.0) + jnp.log(1.0 + jnp.exp(-jnp.abs(raw)))


def _proj_conv(x2d, w, params, w_dt, dt_bias, seq):
    t_rows = x2d.shape[0]
    tm = IN_TM
    assert t_rows % tm == 0 and seq % tm == 0
    tiles_per_seq = seq // tm
    halo_blocks = tm // SUBLANES
    full = lambda shape: pl.BlockSpec(shape, lambda i: (0, 0))
    return pl.pallas_call(
        functools.partial(_proj_conv_kernel, tiles_per_seq=tiles_per_seq),
        grid=(t_rows // tm,),
        in_specs=[
            pl.BlockSpec((tm, D_MODEL), lambda i: (i, 0)),
            pl.BlockSpec((SUBLANES, D_MODEL), lambda i: (jnp.maximum(i * halo_blocks - 1, 0), 0)),
            full((D_MODEL, CONV_CH)),
            full((SUBLANES, CONV_CH)),
            full((D_MODEL, SSD_HEADS)),
            full((1, SSD_HEADS)),
        ],
        out_specs=[
            pl.BlockSpec((tm, CONV_CH), lambda i: (i, 0)),
            pl.BlockSpec((tm, SSD_HEADS), lambda i: (i, 0)),
        ],
        out_shape=[
            jax.ShapeDtypeStruct((t_rows, CONV_CH), BF16),
            jax.ShapeDtypeStruct((t_rows, SSD_HEADS), F32),
        ],
        compiler_params=pltpu.CompilerParams(dimension_semantics=("arbitrary",),
                                             vmem_limit_bytes=VMEM_LIMIT_BYTES),
        name="proj_conv",
    )(x2d, x2d, w, params, w_dt, dt_bias)


def _ssd_kernel(xs_ref, b_ref, c_ref, sz_ref, dt_ref, a_ref, dskip_ref, normw_ref,
                expand_ref, bdtril_ref,
                y_ref, acol_s, dtx_s, ht_s):
    rows = xs_ref.shape[0]
    n_chunks = rows // CHUNK
    gw = SSD_WIDTH // SSD_GROUPS

    @pl.when(pl.program_id(1) == 0)
    def _():
        ht_s[...] = jnp.zeros_like(ht_s)

    dt = dt_ref[...]
    a_cs = _dot01_left(bdtril_ref[...], dt * (a_ref[...] * LOG2E))
    expand = expand_ref[...]
    acol_s[...] = _dot01_right(a_cs, expand)
    dtx_s[...] = _dot01_right(dt, expand)

    qw = 4 * SSD_HEAD_DIM
    quads_per_group = gw // qw
    lane = lax.broadcasted_iota(jnp.int32, (CHUNK, qw), 1)
    row = lax.broadcasted_iota(jnp.int32, (CHUNK, qw), 0)
    s_of_lane = lane & (CHUNK - 1)
    diag_mask = s_of_lane == row
    causal_mask = s_of_lane <= row
    bd_r = lax.broadcasted_iota(jnp.int32, (4 * CHUNK, qw), 0) // CHUNK
    bd_c = lax.broadcasted_iota(jnp.int32, (4 * CHUNK, qw), 1) // SSD_HEAD_DIM
    bd_mask = bd_r == bd_c

    def chunk_body(c, carry):
        r0 = pl.multiple_of(c * CHUNK, CHUNK)
        rows_c = pl.ds(r0, CHUNK)
        bc = b_ref[rows_c, :]
        cc = c_ref[rows_c, :]
        for g in range(SSD_GROUPS):
            cg = cc[:, g * SSD_STATE:(g + 1) * SSD_STATE]
            bg = bc[:, g * SSD_STATE:(g + 1) * SSD_STATE]
            cb_g = lax.dot_general(cg, bg, (((1,), (1,)), ((), ())), preferred_element_type=F32)
            cb_rep = jnp.concatenate([cb_g] * 4, axis=1)
            gated = []
            for qd in range(quads_per_group):
                sl = slice(g * gw + qd * qw, g * gw + (qd + 1) * qw)
                acol = acol_s[rows_c, sl]
                a_last = acol[CHUNK - 1:CHUNK, :]
                xs = xs_ref[rows_c, sl].astype(F32)
                xdt_f = xs * dtx_s[rows_c, sl]

                arow = jnp.sum(jnp.where(diag_mask, acol, 0.0), axis=0, keepdims=True)
                decay_ls = jnp.exp2(jnp.where(causal_mask, acol - arow, NEG_BIG))
                w_ls = (cb_rep * decay_ls).astype(BF16)
                xq = xdt_f.astype(BF16)
                bd = jnp.where(bd_mask, jnp.concatenate([xq] * 4, axis=0), jnp.zeros((), BF16))
                y_diag = jnp.dot(w_ls, bd, preferred_element_type=F32)

                ht = ht_s[:, sl]
                y_off = jnp.dot(cg, ht.astype(BF16), preferred_element_type=F32)
                y = y_diag + y_off * jnp.exp2(acol) + dskip_ref[:, sl] * xs
                gated.append(y * sz_ref[rows_c, sl].astype(F32))

                xd = (xdt_f * jnp.exp2(a_last - acol)).astype(BF16)
                st = lax.dot_general(bg, xd, (((0,), (0,)), ((), ())), preferred_element_type=F32)
                ht_s[:, sl] = ht * jnp.exp2(a_last) + st

            ssq = sum(jnp.sum(t * t, axis=-1, keepdims=True) for t in gated)
            inv = lax.rsqrt(ssq * (1.0 / gw) + EPS)
            for qd in range(quads_per_group):
                sl = slice(g * gw + qd * qw, g * gw + (qd + 1) * qw)
                y_ref[rows_c, sl] = (gated[qd] * inv * normw_ref[:, sl]).astype(y_ref.dtype)
        return carry

    lax.fori_loop(0, n_chunks, chunk_body, 0, unroll=4)


def _ssd(conv_act, gate_act, dt, a_row, dskip_row, normw_row, batch, seq):
    rows = SSD_ROWS
    assert seq % rows == 0 and rows % (4 * CHUNK) == 0
    blocks_per_seq = seq // rows
    hh = np.arange(SSD_WIDTH) // SSD_HEAD_DIM
    expand = jnp.asarray(np.arange(SSD_HEADS)[:, None] == hh[None, :], BF16)
    rr = np.arange(rows)
    same_chunk = (rr[:, None] // CHUNK) == (rr[None, :] // CHUNK)
    bdtril = jnp.asarray(same_chunk & (rr[None, :] <= rr[:, None]), BF16)

    def row_map(b, r):
        return b * blocks_per_seq + r

    def col_spec(width, off):
        assert off % width == 0
        return pl.BlockSpec((rows, width), lambda b, r: (row_map(b, r), off // width))

    const = lambda shape: pl.BlockSpec(shape, lambda b, r: (0, 0))
    return pl.pallas_call(
        _ssd_kernel,
        grid=(batch, blocks_per_seq),
        in_specs=[
            col_spec(SSD_WIDTH, CV_XS),
            col_spec(BC_WIDTH, CV_B),
            col_spec(BC_WIDTH, CV_C),
            col_spec(SSD_WIDTH, GA_SZ),
            pl.BlockSpec((rows, SSD_HEADS), lambda b, r: (row_map(b, r), 0)),
            const((1, SSD_HEADS)),
            const((1, SSD_WIDTH)),
            const((1, SSD_WIDTH)),
            const((SSD_HEADS, SSD_WIDTH)),
            const((rows, rows)),
        ],
        out_specs=pl.BlockSpec((rows, SSD_WIDTH), lambda b, r: (row_map(b, r), 0)),
        out_shape=jax.ShapeDtypeStruct((batch * seq, SSD_WIDTH), BF16),
        scratch_shapes=[
            pltpu.VMEM((rows, SSD_WIDTH), F32),
            pltpu.VMEM((rows, SSD_WIDTH), F32),
            pltpu.VMEM((SSD_STATE, SSD_WIDTH), F32),
        ],
        compiler_params=pltpu.CompilerParams(
            dimension_semantics=("arbitrary", "arbitrary"),
            vmem_limit_bytes=VMEM_LIMIT_BYTES),
        name="ssd",
    )(conv_act, conv_act, conv_act, gate_act, dt, a_row, dskip_row, normw_row, expand, bdtril)


def _attn_kernel(q_ref, k_ref, v_ref, gb_ref, nrm_ref, lq1_ref, lk1_ref, lq2_ref, lk2_ref, subw_ref,
                 o_ref, vaug_s, m_s, acc_s, qq_s, *, lambda_init):
    hd = pl.program_id(1)
    qi = pl.program_id(2)
    tq = q_ref.shape[0]
    tk = ATT_TK
    seq = k_ref.shape[0]

    @pl.when(qi == 0)
    def _():
        vaug_s[:, :DA_VDIM] = v_ref[...].astype(vaug_s.dtype)
        vaug_s[:, DA_VDIM:] = jnp.ones((seq, DA_VDIM), vaug_s.dtype)

    assert tq % tk == 0
    band = ATT_BAND
    n_bands = tq // band
    n_full = qi * (tq // tk)
    half_lane = lax.broadcasted_iota(jnp.int32, (band, DA_VDIM), 1) & (LANES // 2 - 1)
    in_map0 = (half_lane < ROT_HALF) | ((half_lane >= ROT_DIM) & (half_lane < ROT_DIM + ROT_PASS))
    zero = jnp.zeros((), q_ref.dtype)

    qq = []
    for c in range(n_bands):
        q = q_ref[c * band:(c + 1) * band, :]
        qq.append(jnp.concatenate([jnp.where(in_map0, q, zero), jnp.where(in_map0, zero, q)], axis=0))

    def band_scores(c, k):
        return lax.dot_general(qq[c], k, (((1,), (1,)), ((), ())), preferred_element_type=F32)

    def band_rows(c):
        return slice(c * 2 * band, (c + 1) * 2 * band)

    def diagonal_operands(c):
        nk = (c + 1) * band
        k0 = pl.multiple_of(qi * tq, tq)
        r_chunk = (c * band + lax.broadcasted_iota(jnp.int32, (band, nk), 0)) // CHUNK
        k_chunk = lax.broadcasted_iota(jnp.int32, (band, nk), 1) // CHUNK
        ok = k_chunk <= r_chunk
        return k_ref[pl.ds(k0, nk), :], vaug_s[pl.ds(k0, nk), :], jnp.concatenate([ok, ok], axis=0)

    def full_operands(kt):
        k0 = pl.multiple_of(kt * tk, tk)
        return k_ref[pl.ds(k0, tk), :], vaug_s[pl.ds(k0, tk), :]

    norm_row = nrm_ref[0, 0:1, :]
    norm_lane = lax.broadcasted_iota(jnp.int32, norm_row.shape, 1)
    q_norm2 = jnp.max(jnp.where(norm_lane == hd, norm_row, 0.0))
    k_norm2 = jnp.max(jnp.where(norm_lane == DA_HEADS + hd, norm_row, 0.0))
    unshifted_ok = q_norm2 * k_norm2 <= ATT_SAFE_LOG2 * ATT_SAFE_LOG2

    @pl.when(unshifted_ok)
    def _():
        for c in range(n_bands):
            k, vt, ok = diagonal_operands(c)
            p = jnp.exp2(jnp.where(ok, band_scores(c, k), NEG_BIG)).astype(vaug_s.dtype)
            acc_s[band_rows(c), :] = jnp.dot(p, vt, preferred_element_type=F32)

        def tile(kt, carry):
            k, vt = full_operands(kt)
            for c in range(n_bands):
                p = jnp.exp2(band_scores(c, k)).astype(vaug_s.dtype)
                acc_s[band_rows(c), :] += jnp.dot(p, vt, preferred_element_type=F32)
            return carry

        lax.fori_loop(0, n_full, tile, 0)

    @pl.when(jnp.logical_not(unshifted_ok))
    def _():
        m_s[...] = jnp.full_like(m_s, NEG_BIG)
        acc_s[...] = jnp.zeros_like(acc_s)
        for c in range(n_bands):
            qq_s[c] = qq[c]

        def softmax_pv(c, k0, nk, diagonal):
            rows = pl.ds(pl.multiple_of(c * 2 * band, 2 * band), 2 * band)
            k, vt = k_ref[pl.ds(k0, nk), :], vaug_s[pl.ds(k0, nk), :]
            s = lax.dot_general(qq_s[c], k, (((1,), (1,)), ((), ())), preferred_element_type=F32)
            if diagonal:
                r_chunk = (c * band + lax.broadcasted_iota(jnp.int32, (band, nk), 0)) // CHUNK
                k_chunk = lax.broadcasted_iota(jnp.int32, (band, nk), 1) // CHUNK
                ok = k_chunk <= r_chunk
                s = jnp.where(jnp.concatenate([ok, ok], axis=0), s, NEG_BIG)
            m_old = m_s[rows, :]
            m_new = jnp.maximum(m_old, jnp.max(s, axis=-1, keepdims=True))
            alpha = jnp.exp2(m_old - m_new)
            p = jnp.exp2(s - m_new[:, 0:1]).astype(vaug_s.dtype)
            pv = jnp.dot(p, vt, preferred_element_type=F32)
            acc_s[rows, :] = jnp.concatenate([alpha, alpha], axis=1) * acc_s[rows, :] + pv
            m_s[rows, :] = m_new

        def band_loop(k0, nk, diagonal):
            def body(c, carry):
                softmax_pv(c, k0, nk, diagonal)
                return carry
            lax.fori_loop(0, n_bands, body, 0)

        def tile(kt, carry):
            band_loop(pl.multiple_of(kt * tk, tk), tk, False)
            return carry

        lax.fori_loop(0, n_full, tile, 0)
        band_loop(pl.multiple_of(qi * tq, tq), tq, True)

    lam = (jnp.exp(jnp.sum(lq1_ref[...] * lk1_ref[...], axis=-1, keepdims=True))
           - jnp.exp(jnp.sum(lq2_ref[...] * lk2_ref[...], axis=-1, keepdims=True)) + lambda_init)
    scale = subw_ref[...] * (1.0 - lambda_init)
    for c in range(n_bands):
        acc = acc_s[c * 2 * band:(c + 1) * 2 * band, :]
        o_all = acc[:, :DA_VDIM] / acc[:, DA_VDIM:]
        o = o_all[:band] - lam * o_all[band:]
        ms = jnp.mean(o * o, axis=-1, keepdims=True)
        o = o * lax.rsqrt(ms + EPS) * scale
        rows = slice(c * band, (c + 1) * band)
        o_ref[rows, :] = (o * gb_ref[rows, :].astype(F32)).astype(o_ref.dtype)


def _attention(attn_act, gate_act, norms, lq1, lk1, lq2, lk2, subw, batch, seq, lambda_init):
    tq = ATT_TQ
    assert seq % tq == 0 and tq % ATT_TK == 0 and tq % ATT_BAND == 0 and ATT_BAND % CHUNK == 0
    q_tiles = seq // tq

    def head_spec(rows_blk, off):
        assert off % DA_VDIM == 0
        if rows_blk == seq:
            return pl.BlockSpec((seq, DA_VDIM), lambda b, hd, qi: (b, off // DA_VDIM + hd))
        return pl.BlockSpec((rows_blk, DA_VDIM), lambda b, hd, qi: (b * q_tiles + qi, off // DA_VDIM + hd))

    vec = lambda n: pl.BlockSpec((1, n), lambda b, hd, qi: (0, 0))
    kern = functools.partial(_attn_kernel, lambda_init=lambda_init)
    return pl.pallas_call(
        kern,
        grid=(batch, DA_HEADS, q_tiles),
        in_specs=[
            head_spec(tq, AT_Q),
            head_spec(seq, AT_K),
            head_spec(seq, AT_V),
            head_spec(tq, GA_GB),
            pl.BlockSpec((1, SUBLANES, LANES), lambda b, hd, qi: (b, 0, 0)),
            vec(DA_HEAD_DIM), vec(DA_HEAD_DIM), vec(DA_HEAD_DIM), vec(DA_HEAD_DIM),
            vec(DA_VDIM),
        ],
        out_specs=pl.BlockSpec((tq, DA_VDIM), lambda b, hd, qi: (b * q_tiles + qi, hd)),
        out_shape=jax.ShapeDtypeStruct((batch * seq, DA_WIDTH), BF16),
        scratch_shapes=[
            pltpu.VMEM((seq, 2 * DA_VDIM), BF16),
            pltpu.VMEM((2 * tq, LANES), F32),
            pltpu.VMEM((2 * tq, 2 * DA_VDIM), F32),
            pltpu.VMEM((tq // ATT_BAND, 2 * ATT_BAND, DA_VDIM), BF16),
        ],
        compiler_params=pltpu.CompilerParams(
            dimension_semantics=("arbitrary", "arbitrary", "arbitrary"),
            vmem_limit_bytes=VMEM_LIMIT_BYTES),
        name="diffattn",
    )(attn_act, attn_act, attn_act, gate_act, norms, lq1, lk1, lq2, lk2, subw)


def _out_kernel(x_ref, ya_ref, ob_ref, g0_ref, g1_ref, wa_ref, wb_ref, wo_ref, lng_ref, lnb_ref,
                out_ref, *, alpha):
    d = functools.partial(jnp.dot, preferred_element_type=F32)
    branch_a = d(ya_ref[...], wa_ref[...])
    branch_b = d(ob_ref[...], wb_ref[...])
    merged = g0_ref[...].astype(F32) * branch_a + g1_ref[...].astype(F32) * branch_b
    y = d(merged.astype(BF16), wo_ref[...])
    r = alpha * x_ref[...] + y
    mu = jnp.mean(r, axis=-1, keepdims=True)
    rc = r - mu
    var = jnp.mean(rc * rc, axis=-1, keepdims=True)
    out_ref[...] = rc * lax.rsqrt(var + EPS) * lng_ref[...] + lnb_ref[...]


def _output_stage(x2d, y_ssd, o_att, gate_act, w_a, w_b, w_o, ln_g, ln_b, alpha):
    t_rows = x2d.shape[0]
    tm = OUT_TM
    assert t_rows % tm == 0 and GA_GM % D_MODEL == 0
    row = lambda width: pl.BlockSpec((tm, width), lambda i: (i, 0))
    full = lambda shape: pl.BlockSpec(shape, lambda i: (0, 0))
    weight = lambda shape: pl.BlockSpec(shape, lambda i: (0, 0), pipeline_mode=pl.Buffered(1))
    gate_spec = lambda k: pl.BlockSpec((tm, D_MODEL), lambda i: (i, GA_GM // D_MODEL + k))
    return pl.pallas_call(
        functools.partial(_out_kernel, alpha=alpha),
        grid=(t_rows // tm,),
        in_specs=[row(D_MODEL), row(SSD_WIDTH), row(DA_WIDTH), gate_spec(0), gate_spec(1),
                  weight((SSD_WIDTH, D_MODEL)), weight((DA_WIDTH, D_MODEL)), weight((D_MODEL, D_MODEL)),
                  full((1, D_MODEL)), full((1, D_MODEL))],
        out_specs=row(D_MODEL),
        out_shape=jax.ShapeDtypeStruct((t_rows, D_MODEL), F32),
        compiler_params=pltpu.CompilerParams(
            dimension_semantics=("arbitrary",),
            vmem_limit_bytes=VMEM_LIMIT_BYTES),
        name="outproj",
    )(x2d, y_ssd, o_att, gate_act, gate_act, w_a, w_b, w_o, ln_g, ln_b)


def _rope_tables(seq):
    pos = np.arange(seq, dtype=np.float64)
    inv_freq = ROPE_THETA ** (-np.arange(0, ROT_DIM, 2, dtype=np.float64) / ROT_DIM)
    ang = pos[:, None] * inv_freq[None, :]
    cos, sin = np.cos(ang).astype(np.float32), np.sin(ang).astype(np.float32)
    ones = np.ones((seq, LANES // 2 - ROT_DIM), np.float32)
    zeros = np.zeros_like(ones)
    cos_t = np.concatenate([cos, cos, ones, cos, cos, ones], axis=1)
    sin_t = np.concatenate([-sin, -sin, zeros, sin, sin, zeros], axis=1)
    return jnp.asarray(cos_t), jnp.asarray(sin_t)


def _head_lane_permutation():
    src = np.zeros(LANES, np.int64)
    for m in range(2):
        for d in range(DA_HEAD_DIM):
            if d < ROT_HALF:
                lane = m * ROT_HALF + d
            elif d < ROT_DIM:
                lane = LANES // 2 + m * ROT_HALF + (d - ROT_HALF)
            elif d < ROT_DIM + ROT_PASS:
                lane = ROT_DIM + m * ROT_PASS + (d - ROT_DIM)
            else:
                lane = LANES // 2 + ROT_DIM + m * ROT_PASS + (d - ROT_DIM - ROT_PASS)
            src[lane] = m * DA_HEAD_DIM + d
    perm = np.zeros((LANES, LANES), np.float32)
    perm[src, np.arange(LANES)] = 1.0
    return jnp.asarray(perm, BF16)


def _layer(x2d, batch, seq, depth_total, l, w_in, b_gate, conv_w, conv_b, dt_bias, a_log, d_skip,
           ssd_norm_w, lambda_q1, lambda_k1, lambda_q2, lambda_k2, subln_w, w_a, w_b, w_o, ln_g, ln_b,
           tables):
    alpha = (2.0 * depth_total) ** 0.25
    lambda_init = 0.8 - 0.6 * math.exp(-0.3 * l)
    w_gate, w_attn, w_conv, w_dt = _prep_weights(w_in)
    p_conv = jnp.concatenate([conv_w.astype(F32), conv_b[None, :].astype(F32),
                              jnp.zeros((SUBLANES - CONV_WIDTH - 1, CONV_CH), F32)], axis=0)

    gate_act = _proj_gate(x2d, w_gate, b_gate[None, :].astype(F32))
    attn_act, qk_norms = _proj_attn(x2d, w_attn, *tables, seq)
    conv_act, dt = _proj_conv(x2d, w_conv, p_conv, w_dt, dt_bias[None, :].astype(F32), seq)

    a_row = -jnp.exp(a_log.astype(F32))[None, :]
    dskip_row = jnp.repeat(d_skip.astype(F32), SSD_HEAD_DIM)[None, :]
    y_ssd = _ssd(conv_act, gate_act, dt, a_row, dskip_row, ssd_norm_w[None, :].astype(F32), batch, seq)

    o_att = _attention(attn_act, gate_act, qk_norms, lambda_q1[None, :], lambda_k1[None, :],
                       lambda_q2[None, :], lambda_k2[None, :], subln_w[None, :], batch, seq, lambda_init)

    return _output_stage(x2d, y_ssd, o_att, gate_act, w_a.astype(BF16), w_b.astype(BF16), w_o.astype(BF16),
                         ln_g[None, :], ln_b[None, :], alpha)


def kernel(x, w_in, b_gate, conv_w, conv_b, dt_bias, a_log, d_skip, ssd_norm_w, lambda_q1, lambda_k1,
           lambda_q2, lambda_k2, subln_w, w_a, w_b, w_o, ln_g, ln_b):
    batch, seq, _ = x.shape
    depth = w_in.shape[0]
    tables = _rope_tables(seq)
    x2d = x.reshape(batch * seq, D_MODEL)
    for l in range(depth):
        x2d = _layer(x2d, batch, seq, depth, l, w_in[l], b_gate[l], conv_w[l], conv_b[l], dt_bias[l],
                     a_log[l], d_skip[l], ssd_norm_w[l], lambda_q1[l], lambda_k1[l], lambda_q2[l],
                     lambda_k2[l], subln_w[l], w_a[l], w_b[l], w_o[l], ln_g[l], ln_b[l], tables)
    return x2d.reshape(batch, seq, D_MODEL)
```

```python
import functools
import math

import jax
import jax.numpy as jnp
import numpy as np
from jax import lax
from jax.experimental import pallas as pl
from jax.experimental.pallas import tpu as pltpu

F32 = jnp.float32
BF16 = jnp.bfloat16

D_MODEL = 1024
CHUNK = 64
SSD_HEADS = 16
SSD_HEAD_DIM = 64
SSD_WIDTH = SSD_HEADS * SSD_HEAD_DIM
SSD_GROUPS = 2
SSD_STATE = 128
CONV_WIDTH = 4
BC_WIDTH = SSD_GROUPS * SSD_STATE
CONV_CH = SSD_WIDTH + 2 * BC_WIDTH
DA_HEADS = 8
DA_HEAD_DIM = 64
DA_VDIM = 2 * DA_HEAD_DIM
DA_WIDTH = DA_HEADS * DA_VDIM
ROPE_THETA = 500000.0
ROT_DIM = DA_HEAD_DIM // 4
ROT_HALF = ROT_DIM // 2
N_BRANCH = 2
EPS = 1e-5
IN_SIZES = (SSD_WIDTH, CONV_CH, SSD_HEADS, DA_WIDTH, DA_WIDTH, DA_WIDTH, DA_WIDTH, N_BRANCH * D_MODEL)
W_OFFS = tuple(sum(IN_SIZES[:n]) for n in range(len(IN_SIZES) + 1))

GA_SZ, GA_GB, GA_GM = 0, SSD_WIDTH, SSD_WIDTH + DA_WIDTH
GA_COLS = GA_GM + N_BRANCH * D_MODEL
AT_Q, AT_K, AT_V = 0, DA_WIDTH, 2 * DA_WIDTH
AT_COLS = 3 * DA_WIDTH
CV_XS, CV_B, CV_C = 0, SSD_WIDTH, SSD_WIDTH + BC_WIDTH

ROT_PASS = (DA_HEAD_DIM - ROT_DIM) // 2

LANES = 128
SUBLANES = 8
BF16_ROWS = 16
VMEM_LIMIT_BYTES = 56 * 1024 * 1024

PREP_TC = 128
IN_TM = 512
IN_TM_WIDE = 1024
SSD_ROWS = 512
ATT_TQ = 2048
ATT_TK = 2048
ATT_BAND = 256
ATT_SAFE_LOG2 = 64.0
NORM_SLACK = 1.0 + 2.0 ** -7
OUT_TM = 1024

NEG_BIG = -1e30
LOG2E = 1.4426950408889634


def _sigmoid(t):
    return 0.5 * jnp.tanh(0.5 * t) + 0.5


def _split2(v):
    hi = v.astype(BF16)
    mid = (v - hi.astype(F32)).astype(BF16)
    return hi, mid


def _dot01_right(v, m01):
    hi, mid = _split2(v)
    d = functools.partial(jnp.dot, preferred_element_type=F32)
    return d(hi, m01) + d(mid, m01)


def _dot01_left(m01, v):
    hi, mid = _split2(v)
    d = functools.partial(jnp.dot, preferred_element_type=F32)
    return d(m01, hi) + d(m01, mid)


def _proj_params():
    return pltpu.CompilerParams(dimension_semantics=("arbitrary",),
                                vmem_limit_bytes=VMEM_LIMIT_BYTES)


def _prep_kernel(wt_ref, perm_ref, wg_ref, wa_ref, wc_ref):
    def rows(n):
        return wt_ref[W_OFFS[n]:W_OFFS[n + 1], :].astype(BF16)

    wg_ref[GA_SZ:GA_GB, :] = rows(0)
    wg_ref[GA_GB:GA_GM, :] = rows(6)
    wg_ref[GA_GM:, :] = rows(7)
    wc_ref[...] = wt_ref[W_OFFS[1]:W_OFFS[3], :].astype(BF16)
    perm = perm_ref[...]
    for base, n in ((AT_Q, 3), (AT_K, 4)):
        for hd in range(DA_HEADS):
            r0 = W_OFFS[n] + hd * LANES
            blk = wt_ref[r0:r0 + LANES, :].astype(BF16)
            moved = jnp.dot(perm, blk, preferred_element_type=F32)
            wa_ref[base + hd * LANES:base + (hd + 1) * LANES, :] = moved.astype(BF16)
    wa_ref[AT_V:, :] = rows(5)


def _prep_weights(w_in_l):
    tc = PREP_TC
    d_in, total = w_in_l.shape
    assert total == W_OFFS[-1] and d_in % tc == 0
    perm_t = _head_lane_permutation().T
    heights = (GA_COLS, AT_COLS, CONV_CH + SSD_HEADS)
    return pl.pallas_call(
        _prep_kernel,
        grid=(d_in // tc,),
        in_specs=[
            pl.BlockSpec((total, tc), lambda i: (0, i)),
            pl.BlockSpec((LANES, LANES), lambda i: (0, 0)),
        ],
        out_specs=tuple(pl.BlockSpec((ht, tc), lambda i: (0, i)) for ht in heights),
        out_shape=tuple(jax.ShapeDtypeStruct((ht, d_in), BF16) for ht in heights),
        compiler_params=_proj_params(),
        name="prep_weights",
    )(w_in_l.T, perm_t)


def _dot_nt(a, bt):
    return lax.dot_general(a, bt, (((1,), (1,)), ((), ())), preferred_element_type=F32)


def _proj_gate_kernel(x_ref, w_ref, bias_ref, o_ref):
    acc = _dot_nt(x_ref[...].astype(BF16), w_ref[...])
    t = acc[:, :GA_GM]
    o_ref[:, :GA_GM] = (t * _sigmoid(t)).astype(o_ref.dtype)
    o_ref[:, GA_GM:] = _sigmoid(acc[:, GA_GM:] + bias_ref[...]).astype(o_ref.dtype)


def _proj_gate(x2d, w, b_gate_row):
    t_rows = x2d.shape[0]
    tm = IN_TM_WIDE
    assert t_rows % tm == 0
    full = lambda shape: pl.BlockSpec(shape, lambda i: (0, 0))
    return pl.pallas_call(
        _proj_gate_kernel,
        grid=(t_rows // tm,),
        in_specs=[
            pl.BlockSpec((tm, D_MODEL), lambda i: (i, 0)),
            pl.BlockSpec((GA_COLS, D_MODEL), lambda i: (0, 0), pipeline_mode=pl.Buffered(1)),
            full((1, N_BRANCH * D_MODEL)),
        ],
        out_specs=pl.BlockSpec((tm, GA_COLS), lambda i: (i, 0)),
        out_shape=jax.ShapeDtypeStruct((t_rows, GA_COLS), BF16),
        compiler_params=_proj_params(),
        name="proj_gate",
    )(x2d, w, b_gate_row)


def _proj_attn_kernel(x_ref, w_ref, cos_ref, sin_ref, sel_ref, o_ref, nrm_ref, *, q_scale, tiles_per_seq):
    @pl.when(pl.program_id(0) % tiles_per_seq == 0)
    def _():
        nrm_ref[...] = jnp.zeros_like(nrm_ref)

    acc = _dot_nt(x_ref[...].astype(BF16), w_ref[...])
    cos = cos_ref[...]
    sin = sin_ref[...]
    cos_q = cos * q_scale
    sin_q = sin * q_scale
    squares = []
    for hd in range(AT_V // LANES):
        sl = slice(hd * LANES, (hd + 1) * LANES)
        a = acc[:, sl]
        partner = pltpu.roll(a, LANES // 2, axis=1)
        c, s = (cos_q, sin_q) if hd < AT_K // LANES else (cos, sin)
        rot = (a * c + partner * s).astype(o_ref.dtype)
        o_ref[:, sl] = rot
        rf = rot.astype(F32)
        squares.append((rf * rf).astype(BF16))
    o_ref[:, AT_V:] = acc[:, AT_V:].astype(o_ref.dtype)

    norms = jnp.dot(jnp.concatenate(squares, axis=1), sel_ref[...], preferred_element_type=F32)
    top = jnp.max(norms, axis=0, keepdims=True) * NORM_SLACK
    nrm_ref[0] = jnp.maximum(nrm_ref[0], jnp.broadcast_to(top, nrm_ref.shape[1:]))


def _proj_attn(x2d, w, cos_t, sin_t, seq):
    t_rows = x2d.shape[0]
    tm = IN_TM_WIDE
    assert t_rows % tm == 0 and seq % tm == 0
    tiles_per_seq = seq // tm
    table = lambda: pl.BlockSpec((tm, LANES), lambda i: (i % tiles_per_seq, 0))
    selector = jnp.asarray((np.arange(AT_V)[:, None] // LANES) == np.arange(LANES)[None, :], BF16)
    return pl.pallas_call(
        functools.partial(_proj_attn_kernel, q_scale=DA_HEAD_DIM ** -0.5 * LOG2E, tiles_per_seq=tiles_per_seq),
        grid=(t_rows // tm,),
        in_specs=[
            pl.BlockSpec((tm, D_MODEL), lambda i: (i, 0)),
            pl.BlockSpec((AT_COLS, D_MODEL), lambda i: (0, 0), pipeline_mode=pl.Buffered(1)),
            table(), table(),
            pl.BlockSpec((AT_V, LANES), lambda i: (0, 0)),
        ],
        out_specs=[
            pl.BlockSpec((tm, AT_COLS), lambda i: (i, 0)),
            pl.BlockSpec((1, SUBLANES, LANES), lambda i: (i // tiles_per_seq, 0, 0)),
        ],
        out_shape=[
            jax.ShapeDtypeStruct((t_rows, AT_COLS), BF16),
            jax.ShapeDtypeStruct((t_rows // seq, SUBLANES, LANES), F32),
        ],
        compiler_params=_proj_params(),
        name="proj_attn",
    )(x2d, w, cos_t, sin_t, selector)


def _proj_conv_kernel(x_ref, xh_ref, w_ref, p_ref, dtb_ref, o_ref, dt_ref, *, tiles_per_seq):
    i = pl.program_id(0)
    tm = o_ref.shape[0]
    xb = jnp.concatenate([x_ref[...], xh_ref[...]], axis=0).astype(BF16)
    full = _dot_nt(xb, w_ref[...])
    acc = full[:tm, :CONV_CH]
    halo = jnp.where(i % tiles_per_seq == 0, 0.0, full[tm:, :CONV_CH])

    bias = p_ref[CONV_WIDTH:CONV_WIDTH + 1, :]
    taps = [p_ref[k:k + 1, :] for k in range(CONV_WIDTH)]

    def conv(rows_val):
        out = taps[0] * rows_val
        for k in range(1, CONV_WIDTH):
            out = pltpu.roll(out, 1, axis=0) + taps[k] * rows_val
        out = out + bias
        return out * _sigmoid(out)

    head = BF16_ROWS
    ext = jnp.concatenate([halo, acc[:head]], axis=0)
    o_ref[:head, :] = conv(ext)[SUBLANES:, :].astype(o_ref.dtype)
    o_ref[head:, :] = conv(acc)[head:, :].astype(o_ref.dtype)

    raw = full[:tm, CONV_CH:] + dtb_ref[...]
    dt_ref[...] = jnp.maximum(raw, 0.0) + jnp.log(1.0 + jnp.exp(-jnp.abs(raw)))


def _proj_conv(x2d, w, params, dt_bias, seq):
    t_rows = x2d.shape[0]
    tm = IN_TM
    assert t_rows % tm == 0 and seq % tm == 0
    tiles_per_seq = seq // tm
    halo_blocks = tm // SUBLANES
    full = lambda shape: pl.BlockSpec(shape, lambda i: (0, 0))
    return pl.pallas_call(
        functools.partial(_proj_conv_kernel, tiles_per_seq=tiles_per_seq),
        grid=(t_rows // tm,),
        in_specs=[
            pl.BlockSpec((tm, D_MODEL), lambda i: (i, 0)),
            pl.BlockSpec((SUBLANES, D_MODEL), lambda i: (jnp.maximum(i * halo_blocks - 1, 0), 0)),
            full((CONV_CH + SSD_HEADS, D_MODEL)),
            full((SUBLANES, CONV_CH)),
            full((1, SSD_HEADS)),
        ],
        out_specs=[
            pl.BlockSpec((tm, CONV_CH), lambda i: (i, 0)),
            pl.BlockSpec((tm, SSD_HEADS), lambda i: (i, 0)),
        ],
        out_shape=[
            jax.ShapeDtypeStruct((t_rows, CONV_CH), BF16),
            jax.ShapeDtypeStruct((t_rows, SSD_HEADS), F32),
        ],
        compiler_params=pltpu.CompilerParams(dimension_semantics=("arbitrary",),
                                             vmem_limit_bytes=VMEM_LIMIT_BYTES),
        name="proj_conv",
    )(x2d, x2d, w, params, dt_bias)


def _ssd_kernel(xs_ref, b_ref, c_ref, sz_ref, dt_ref, a_ref, dskip_ref, normw_ref,
                expand_ref, bdtril_ref,
                y_ref, acol_s, dtx_s, ht_s):
    rows = xs_ref.shape[0]
    n_chunks = rows // CHUNK
    gw = SSD_WIDTH // SSD_GROUPS

    @pl.when(pl.program_id(1) == 0)
    def _():
        ht_s[...] = jnp.zeros_like(ht_s)

    dt = dt_ref[...]
    a_cs = _dot01_left(bdtril_ref[...], dt * (a_ref[...] * LOG2E))
    expand = expand_ref[...]
    acol_s[...] = _dot01_right(a_cs, expand)
    dtx_s[...] = _dot01_right(dt, expand)

    qw = 4 * SSD_HEAD_DIM
    quads_per_group = gw // qw
    lane = lax.broadcasted_iota(jnp.int32, (CHUNK, qw), 1)
    row = lax.broadcasted_iota(jnp.int32, (CHUNK, qw), 0)
    s_of_lane = lane & (CHUNK - 1)
    diag_mask = s_of_lane == row
    causal_mask = s_of_lane <= row
    bd_r = lax.broadcasted_iota(jnp.int32, (4 * CHUNK, qw), 0) // CHUNK
    bd_c = lax.broadcasted_iota(jnp.int32, (4 * CHUNK, qw), 1) // SSD_HEAD_DIM
    bd_mask = bd_r == bd_c

    def chunk_body(c, carry):
        r0 = pl.multiple_of(c * CHUNK, CHUNK)
        rows_c = pl.ds(r0, CHUNK)
        bc = b_ref[rows_c, :]
        cc = c_ref[rows_c, :]
        for g in range(SSD_GROUPS):
            cg = cc[:, g * SSD_STATE:(g + 1) * SSD_STATE]
            bg = bc[:, g * SSD_STATE:(g + 1) * SSD_STATE]
            cb_g = lax.dot_general(cg, bg, (((1,), (1,)), ((), ())), preferred_element_type=F32)
            cb_rep = jnp.concatenate([cb_g] * 4, axis=1)
            gated = []
            for qd in range(quads_per_group):
                sl = slice(g * gw + qd * qw, g * gw + (qd + 1) * qw)
                acol = acol_s[rows_c, sl]
                a_last = acol[CHUNK - 1:CHUNK, :]
                xs = xs_ref[rows_c, sl].astype(F32)
                xdt_f = xs * dtx_s[rows_c, sl]

                arow = jnp.sum(jnp.where(diag_mask, acol, 0.0), axis=0, keepdims=True)
                decay_ls = jnp.exp2(jnp.where(causal_mask, acol - arow, NEG_BIG))
                w_ls = (cb_rep * decay_ls).astype(BF16)
                xq = xdt_f.astype(BF16)
                bd = jnp.where(bd_mask, jnp.concatenate([xq] * 4, axis=0), jnp.zeros((), BF16))
                y_diag = jnp.dot(w_ls, bd, preferred_element_type=F32)

                ht = ht_s[:, sl]
                y_off = jnp.dot(cg, ht.astype(BF16), preferred_element_type=F32)
                y = y_diag + y_off * jnp.exp2(acol) + dskip_ref[:, sl] * xs
                gated.append(y * sz_ref[rows_c, sl].astype(F32))

                xd = (xdt_f * jnp.exp2(a_last - acol)).astype(BF16)
                st = lax.dot_general(bg, xd, (((0,), (0,)), ((), ())), preferred_element_type=F32)
                ht_s[:, sl] = ht * jnp.exp2(a_last) + st

            ssq = sum(jnp.sum(t * t, axis=-1, keepdims=True) for t in gated)
            inv = lax.rsqrt(ssq * (1.0 / gw) + EPS)
            for qd in range(quads_per_group):
                sl = slice(g * gw + qd * qw, g * gw + (qd + 1) * qw)
                y_ref[rows_c, sl] = (gated[qd] * inv * normw_ref[:, sl]).astype(y_ref.dtype)
        return carry

    lax.fori_loop(0, n_chunks, chunk_body, 0, unroll=4)


def _ssd(conv_act, gate_act, dt, a_row, dskip_row, normw_row, batch, seq):
    rows = SSD_ROWS
    assert seq % rows == 0 and rows % (4 * CHUNK) == 0
    blocks_per_seq = seq // rows
    hh = np.arange(SSD_WIDTH) // SSD_HEAD_DIM
    expand = jnp.asarray(np.arange(SSD_HEADS)[:, None] == hh[None, :], BF16)
    rr = np.arange(rows)
    same_chunk = (rr[:, None] // CHUNK) == (rr[None, :] // CHUNK)
    bdtril = jnp.asarray(same_chunk & (rr[None, :] <= rr[:, None]), BF16)

    def row_map(b, r):
        return b * blocks_per_seq + r

    def col_spec(width, off):
        assert off % width == 0
        return pl.BlockSpec((rows, width), lambda b, r: (row_map(b, r), off // width))

    const = lambda shape: pl.BlockSpec(shape, lambda b, r: (0, 0))
    return pl.pallas_call(
        _ssd_kernel,
        grid=(batch, blocks_per_seq),
        in_specs=[
            col_spec(SSD_WIDTH, CV_XS),
            col_spec(BC_WIDTH, CV_B),
            col_spec(BC_WIDTH, CV_C),
            col_spec(SSD_WIDTH, GA_SZ),
            pl.BlockSpec((rows, SSD_HEADS), lambda b, r: (row_map(b, r), 0)),
            const((1, SSD_HEADS)),
            const((1, SSD_WIDTH)),
            const((1, SSD_WIDTH)),
            const((SSD_HEADS, SSD_WIDTH)),
            const((rows, rows)),
        ],
        out_specs=pl.BlockSpec((rows, SSD_WIDTH), lambda b, r: (row_map(b, r), 0)),
        out_shape=jax.ShapeDtypeStruct((batch * seq, SSD_WIDTH), BF16),
        scratch_shapes=[
            pltpu.VMEM((rows, SSD_WIDTH), F32),
            pltpu.VMEM((rows, SSD_WIDTH), F32),
            pltpu.VMEM((SSD_STATE, SSD_WIDTH), F32),
        ],
        compiler_params=pltpu.CompilerParams(
            dimension_semantics=("arbitrary", "arbitrary"),
            vmem_limit_bytes=VMEM_LIMIT_BYTES),
        name="ssd",
    )(conv_act, conv_act, conv_act, gate_act, dt, a_row, dskip_row, normw_row, expand, bdtril)


def _attn_kernel(q_ref, k_ref, v_ref, gb_ref, nrm_ref, lq1_ref, lk1_ref, lq2_ref, lk2_ref, subw_ref,
                 o_ref, vaug_s, m_s, acc_s, qq_s, *, lambda_init):
    hd = pl.program_id(1)
    qi = pl.program_id(2)
    tq = q_ref.shape[0]
    tk = ATT_TK
    seq = k_ref.shape[0]

    @pl.when(qi == 0)
    def _():
        vaug_s[:, :DA_VDIM] = v_ref[...].astype(vaug_s.dtype)
        vaug_s[:, DA_VDIM:] = jnp.ones((seq, DA_VDIM), vaug_s.dtype)

    assert tq % tk == 0
    band = ATT_BAND
    n_bands = tq // band
    n_full = qi * (tq // tk)
    half_lane = lax.broadcasted_iota(jnp.int32, (band, DA_VDIM), 1) & (LANES // 2 - 1)
    in_map0 = (half_lane < ROT_HALF) | ((half_lane >= ROT_DIM) & (half_lane < ROT_DIM + ROT_PASS))
    zero = jnp.zeros((), q_ref.dtype)

    qq = []
    for c in range(n_bands):
        q = q_ref[c * band:(c + 1) * band, :]
        qq.append(jnp.concatenate([jnp.where(in_map0, q, zero), jnp.where(in_map0, zero, q)], axis=0))

    def band_scores(c, k):
        return lax.dot_general(qq[c], k, (((1,), (1,)), ((), ())), preferred_element_type=F32)

    def band_rows(c):
        return slice(c * 2 * band, (c + 1) * 2 * band)

    def diagonal_operands(c):
        nk = (c + 1) * band
        k0 = pl.multiple_of(qi * tq, tq)
        r_chunk = (c * band + lax.broadcasted_iota(jnp.int32, (band, nk), 0)) // CHUNK
        k_chunk = lax.broadcasted_iota(jnp.int32, (band, nk), 1) // CHUNK
        ok = k_chunk <= r_chunk
        return k_ref[pl.ds(k0, nk), :], vaug_s[pl.ds(k0, nk), :], jnp.concatenate([ok, ok], axis=0)

    def full_operands(kt):
        k0 = pl.multiple_of(kt * tk, tk)
        return k_ref[pl.ds(k0, tk), :], vaug_s[pl.ds(k0, tk), :]

    norm_row = nrm_ref[0, 0:1, :]
    norm_lane = lax.broadcasted_iota(jnp.int32, norm_row.shape, 1)
    q_norm2 = jnp.max(jnp.where(norm_lane == hd, norm_row, 0.0))
    k_norm2 = jnp.max(jnp.where(norm_lane == DA_HEADS + hd, norm_row, 0.0))
    unshifted_ok = q_norm2 * k_norm2 <= ATT_SAFE_LOG2 * ATT_SAFE_LOG2

    @pl.when(unshifted_ok)
    def _():
        for c in range(n_bands):
            k, vt, ok = diagonal_operands(c)
            p = jnp.exp2(jnp.where(ok, band_scores(c, k), NEG_BIG)).astype(vaug_s.dtype)
            acc_s[band_rows(c), :] = jnp.dot(p, vt, preferred_element_type=F32)

        def tile(kt, carry):
            k, vt = full_operands(kt)
            for c in range(n_bands):
                p = jnp.exp2(band_scores(c, k)).astype(vaug_s.dtype)
                acc_s[band_rows(c), :] += jnp.dot(p, vt, preferred_element_type=F32)
            return carry

        lax.fori_loop(0, n_full, tile, 0)

    @pl.when(jnp.logical_not(unshifted_ok))
    def _():
        m_s[...] = jnp.full_like(m_s, NEG_BIG)
        acc_s[...] = jnp.zeros_like(acc_s)
        for c in range(n_bands):
            qq_s[c] = qq[c]

        def softmax_pv(c, k0, nk, diagonal):
            rows = pl.ds(pl.multiple_of(c * 2 * band, 2 * band), 2 * band)
            k, vt = k_ref[pl.ds(k0, nk), :], vaug_s[pl.ds(k0, nk), :]
            s = lax.dot_general(qq_s[c], k, (((1,), (1,)), ((), ())), preferred_element_type=F32)
            if diagonal:
                r_chunk = (c * band + lax.broadcasted_iota(jnp.int32, (band, nk), 0)) // CHUNK
                k_chunk = lax.broadcasted_iota(jnp.int32, (band, nk), 1) // CHUNK
                ok = k_chunk <= r_chunk
                s = jnp.where(jnp.concatenate([ok, ok], axis=0), s, NEG_BIG)
            m_old = m_s[rows, :]
            m_new = jnp.maximum(m_old, jnp.max(s, axis=-1, keepdims=True))
            alpha = jnp.exp2(m_old - m_new)
            p = jnp.exp2(s - m_new[:, 0:1]).astype(vaug_s.dtype)
            pv = jnp.dot(p, vt, preferred_element_type=F32)
            acc_s[rows, :] = jnp.concatenate([alpha, alpha], axis=1) * acc_s[rows, :] + pv
            m_s[rows, :] = m_new

        def band_loop(k0, nk, diagonal):
            def body(c, carry):
                softmax_pv(c, k0, nk, diagonal)
                return carry
            lax.fori_loop(0, n_bands, body, 0)

        def tile(kt, carry):
            band_loop(pl.multiple_of(kt * tk, tk), tk, False)
            return carry

        lax.fori_loop(0, n_full, tile, 0)
        band_loop(pl.multiple_of(qi * tq, tq), tq, True)

    lam = (jnp.exp(jnp.sum(lq1_ref[...] * lk1_ref[...], axis=-1, keepdims=True))
           - jnp.exp(jnp.sum(lq2_ref[...] * lk2_ref[...], axis=-1, keepdims=True)) + lambda_init)
    scale = subw_ref[...] * (1.0 - lambda_init)
    for c in range(n_bands):
        acc = acc_s[c * 2 * band:(c + 1) * 2 * band, :]
        o_all = acc[:, :DA_VDIM] / acc[:, DA_VDIM:]
        o = o_all[:band] - lam * o_all[band:]
        ms = jnp.mean(o * o, axis=-1, keepdims=True)
        o = o * lax.rsqrt(ms + EPS) * scale
        rows = slice(c * band, (c + 1) * band)
        o_ref[rows, :] = (o * gb_ref[rows, :].astype(F32)).astype(o_ref.dtype)


def _attention(attn_act, gate_act, norms, lq1, lk1, lq2, lk2, subw, batch, seq, lambda_init):
    tq = ATT_TQ
    assert seq % tq == 0 and tq % ATT_TK == 0 and tq % ATT_BAND == 0 and ATT_BAND % CHUNK == 0
    q_tiles = seq // tq

    def head_spec(rows_blk, off):
        assert off % DA_VDIM == 0
        if rows_blk == seq:
            return pl.BlockSpec((seq, DA_VDIM), lambda b, hd, qi: (b, off // DA_VDIM + hd))
        return pl.BlockSpec((rows_blk, DA_VDIM), lambda b, hd, qi: (b * q_tiles + qi, off // DA_VDIM + hd))

    vec = lambda n: pl.BlockSpec((1, n), lambda b, hd, qi: (0, 0))
    kern = functools.partial(_attn_kernel, lambda_init=lambda_init)
    return pl.pallas_call(
        kern,
        grid=(batch, DA_HEADS, q_tiles),
        in_specs=[
            head_spec(tq, AT_Q),
            head_spec(seq, AT_K),
            head_spec(seq, AT_V),
            head_spec(tq, GA_GB),
            pl.BlockSpec((1, SUBLANES, LANES), lambda b, hd, qi: (b, 0, 0)),
            vec(DA_HEAD_DIM), vec(DA_HEAD_DIM), vec(DA_HEAD_DIM), vec(DA_HEAD_DIM),
            vec(DA_VDIM),
        ],
        out_specs=pl.BlockSpec((tq, DA_VDIM), lambda b, hd, qi: (b * q_tiles + qi, hd)),
        out_shape=jax.ShapeDtypeStruct((batch * seq, DA_WIDTH), BF16),
        scratch_shapes=[
            pltpu.VMEM((seq, 2 * DA_VDIM), BF16),
            pltpu.VMEM((2 * tq, LANES), F32),
            pltpu.VMEM((2 * tq, 2 * DA_VDIM), F32),
            pltpu.VMEM((tq // ATT_BAND, 2 * ATT_BAND, DA_VDIM), BF16),
        ],
        compiler_params=pltpu.CompilerParams(
            dimension_semantics=("arbitrary", "arbitrary", "arbitrary"),
            vmem_limit_bytes=VMEM_LIMIT_BYTES),
        name="diffattn",
    )(attn_act, attn_act, attn_act, gate_act, norms, lq1, lk1, lq2, lk2, subw)


def _out_kernel(x_ref, ya_ref, ob_ref, g0_ref, g1_ref, wa_ref, wb_ref, wo_ref, lng_ref, lnb_ref,
                out_ref, *, alpha):
    d = functools.partial(jnp.dot, preferred_element_type=F32)
    branch_a = d(ya_ref[...], wa_ref[...])
    branch_b = d(ob_ref[...], wb_ref[...])
    merged = g0_ref[...].astype(F32) * branch_a + g1_ref[...].astype(F32) * branch_b
    y = d(merged.astype(BF16), wo_ref[...])
    r = alpha * x_ref[...] + y
    mu = jnp.mean(r, axis=-1, keepdims=True)
    rc = r - mu
    var = jnp.mean(rc * rc, axis=-1, keepdims=True)
    out_ref[...] = rc * lax.rsqrt(var + EPS) * lng_ref[...] + lnb_ref[...]


def _output_stage(x2d, y_ssd, o_att, gate_act, w_a, w_b, w_o, ln_g, ln_b, alpha):
    t_rows = x2d.shape[0]
    tm = OUT_TM
    assert t_rows % tm == 0 and GA_GM % D_MODEL == 0
    row = lambda width: pl.BlockSpec((tm, width), lambda i: (i, 0))
    full = lambda shape: pl.BlockSpec(shape, lambda i: (0, 0))
    weight = lambda shape: pl.BlockSpec(shape, lambda i: (0, 0), pipeline_mode=pl.Buffered(1))
    gate_spec = lambda k: pl.BlockSpec((tm, D_MODEL), lambda i: (i, GA_GM // D_MODEL + k))
    return pl.pallas_call(
        functools.partial(_out_kernel, alpha=alpha),
        grid=(t_rows // tm,),
        in_specs=[row(D_MODEL), row(SSD_WIDTH), row(DA_WIDTH), gate_spec(0), gate_spec(1),
                  weight((SSD_WIDTH, D_MODEL)), weight((DA_WIDTH, D_MODEL)), weight((D_MODEL, D_MODEL)),
                  full((1, D_MODEL)), full((1, D_MODEL))],
        out_specs=row(D_MODEL),
        out_shape=jax.ShapeDtypeStruct((t_rows, D_MODEL), F32),
        compiler_params=pltpu.CompilerParams(
            dimension_semantics=("arbitrary",),
            vmem_limit_bytes=VMEM_LIMIT_BYTES),
        name="outproj",
    )(x2d, y_ssd, o_att, gate_act, gate_act, w_a, w_b, w_o, ln_g, ln_b)


def _rope_tables(seq):
    pos = np.arange(seq, dtype=np.float64)
    inv_freq = ROPE_THETA ** (-np.arange(0, ROT_DIM, 2, dtype=np.float64) / ROT_DIM)
    ang = pos[:, None] * inv_freq[None, :]
    cos, sin = np.cos(ang).astype(np.float32), np.sin(ang).astype(np.float32)
    ones = np.ones((seq, LANES // 2 - ROT_DIM), np.float32)
    zeros = np.zeros_like(ones)
    cos_t = np.concatenate([cos, cos, ones, cos, cos, ones], axis=1)
    sin_t = np.concatenate([-sin, -sin, zeros, sin, sin, zeros], axis=1)
    return jnp.asarray(cos_t), jnp.asarray(sin_t)


def _head_lane_permutation():
    src = np.zeros(LANES, np.int64)
    for m in range(2):
        for d in range(DA_HEAD_DIM):
            if d < ROT_HALF:
                lane = m * ROT_HALF + d
            elif d < ROT_DIM:
                lane = LANES // 2 + m * ROT_HALF + (d - ROT_HALF)
            elif d < ROT_DIM + ROT_PASS:
                lane = ROT_DIM + m * ROT_PASS + (d - ROT_DIM)
            else:
                lane = LANES // 2 + ROT_DIM + m * ROT_PASS + (d - ROT_DIM - ROT_PASS)
            src[lane] = m * DA_HEAD_DIM + d
    perm = np.zeros((LANES, LANES), np.float32)
    perm[src, np.arange(LANES)] = 1.0
    return jnp.asarray(perm, BF16)


def _layer(x2d, batch, seq, depth_total, l, w_in, b_gate, conv_w, conv_b, dt_bias, a_log, d_skip,
           ssd_norm_w, lambda_q1, lambda_k1, lambda_q2, lambda_k2, subln_w, w_a, w_b, w_o, ln_g, ln_b,
           tables):
    alpha = (2.0 * depth_total) ** 0.25
    lambda_init = 0.8 - 0.6 * math.exp(-0.3 * l)
    w_gate, w_attn, w_conv = _prep_weights(w_in)
    p_conv = jnp.concatenate([conv_w.astype(F32), conv_b[None, :].astype(F32),
                              jnp.zeros((SUBLANES - CONV_WIDTH - 1, CONV_CH), F32)], axis=0)

    gate_act = _proj_gate(x2d, w_gate, b_gate[None, :].astype(F32))
    attn_act, qk_norms = _proj_attn(x2d, w_attn, *tables, seq)
    conv_act, dt = _proj_conv(x2d, w_conv, p_conv, dt_bias[None, :].astype(F32), seq)

    a_row = -jnp.exp(a_log.astype(F32))[None, :]
    dskip_row = jnp.repeat(d_skip.astype(F32), SSD_HEAD_DIM)[None, :]
    y_ssd = _ssd(conv_act, gate_act, dt, a_row, dskip_row, ssd_norm_w[None, :].astype(F32), batch, seq)

    o_att = _attention(attn_act, gate_act, qk_norms, lambda_q1[None, :], lambda_k1[None, :],
                       lambda_q2[None, :], lambda_k2[None, :], subln_w[None, :], batch, seq, lambda_init)

    return _output_stage(x2d, y_ssd, o_att, gate_act, w_a.astype(BF16), w_b.astype(BF16), w_o.astype(BF16),
                         ln_g[None, :], ln_b[None, :], alpha)


def kernel(x, w_in, b_gate, conv_w, conv_b, dt_bias, a_log, d_skip, ssd_norm_w, lambda_q1, lambda_k1,
           lambda_q2, lambda_k2, subln_w, w_a, w_b, w_o, ln_g, ln_b):
    batch, seq, _ = x.shape
    depth = w_in.shape[0]
    tables = _rope_tables(seq)
    x2d = x.reshape(batch * seq, D_MODEL)
    for l in range(depth):
        x2d = _layer(x2d, batch, seq, depth, l, w_in[l], b_gate[l], conv_w[l], conv_b[l], dt_bias[l],
                     a_log[l], d_skip[l], ssd_norm_w[l], lambda_q1[l], lambda_k1[l], lambda_q2[l],
                     lambda_k2[l], subln_w[l], w_a[l], w_b[l], w_o[l], ln_g[l], ln_b[l], tables)
    return x2d.reshape(batch, seq, D_MODEL)
```

```python
import functools
import math

import jax
import jax.numpy as jnp
import numpy as np
from jax import lax
from jax.experimental import pallas as pl
from jax.experimental.pallas import tpu as pltpu

F32 = jnp.float32
BF16 = jnp.bfloat16

D_MODEL = 1024
CHUNK = 64
SSD_HEADS = 16
SSD_HEAD_DIM = 64
SSD_WIDTH = SSD_HEADS * SSD_HEAD_DIM
SSD_GROUPS = 2
SSD_STATE = 128
CONV_WIDTH = 4
BC_WIDTH = SSD_GROUPS * SSD_STATE
CONV_CH = SSD_WIDTH + 2 * BC_WIDTH
DA_HEADS = 8
DA_HEAD_DIM = 64
DA_VDIM = 2 * DA_HEAD_DIM
DA_WIDTH = DA_HEADS * DA_VDIM
ROPE_THETA = 500000.0
ROT_DIM = DA_HEAD_DIM // 4
ROT_HALF = ROT_DIM // 2
N_BRANCH = 2
EPS = 1e-5
IN_SIZES = (SSD_WIDTH, CONV_CH, SSD_HEADS, DA_WIDTH, DA_WIDTH, DA_WIDTH, DA_WIDTH, N_BRANCH * D_MODEL)
W_OFFS = tuple(sum(IN_SIZES[:n]) for n in range(len(IN_SIZES) + 1))

GA_SZ, GA_GB, GA_GM = 0, SSD_WIDTH, SSD_WIDTH + DA_WIDTH
GA_COLS = GA_GM + N_BRANCH * D_MODEL
AT_Q, AT_K, AT_V = 0, DA_WIDTH, 2 * DA_WIDTH
AT_COLS = 3 * DA_WIDTH
CV_XS, CV_B, CV_C = 0, SSD_WIDTH, SSD_WIDTH + BC_WIDTH

ROT_PASS = (DA_HEAD_DIM - ROT_DIM) // 2

LANES = 128
SUBLANES = 8
BF16_ROWS = 16
VMEM_LIMIT_BYTES = 56 * 1024 * 1024

PREP_TC = 128
IN_TM = 1024
IN_TM_WIDE = 1024
SSD_ROWS = 512
ATT_TQ = 2048
ATT_TK = 2048
ATT_BAND = 256
ATT_SAFE_LOG2 = 64.0
NORM_SLACK = 1.0 + 2.0 ** -7
OUT_TM = 1024

NEG_BIG = -1e30
LOG2E = 1.4426950408889634


def _sigmoid(t):
    return 0.5 * jnp.tanh(0.5 * t) + 0.5


def _split2(v):
    hi = v.astype(BF16)
    mid = (v - hi.astype(F32)).astype(BF16)
    return hi, mid


def _dot01_right(v, m01):
    hi, mid = _split2(v)
    d = functools.partial(jnp.dot, preferred_element_type=F32)
    return d(hi, m01) + d(mid, m01)


def _dot01_left(m01, v):
    hi, mid = _split2(v)
    d = functools.partial(jnp.dot, preferred_element_type=F32)
    return d(m01, hi) + d(m01, mid)


def _proj_params():
    return pltpu.CompilerParams(dimension_semantics=("arbitrary",),
                                vmem_limit_bytes=VMEM_LIMIT_BYTES)


def _prep_kernel(wt_ref, perm_ref, wg_ref, wa_ref, wc_ref):
    def rows(n):
        return wt_ref[W_OFFS[n]:W_OFFS[n + 1], :].astype(BF16)

    wg_ref[GA_SZ:GA_GB, :] = rows(0)
    wg_ref[GA_GB:GA_GM, :] = rows(6)
    wg_ref[GA_GM:, :] = rows(7)
    wc_ref[...] = wt_ref[W_OFFS[1]:W_OFFS[3], :].astype(BF16)
    perm = perm_ref[...]
    for base, n in ((AT_Q, 3), (AT_K, 4)):
        for hd in range(DA_HEADS):
            r0 = W_OFFS[n] + hd * LANES
            blk = wt_ref[r0:r0 + LANES, :].astype(BF16)
            moved = jnp.dot(perm, blk, preferred_element_type=F32)
            wa_ref[base + hd * LANES:base + (hd + 1) * LANES, :] = moved.astype(BF16)
    wa_ref[AT_V:, :] = rows(5)


def _prep_weights(w_in_l):
    tc = PREP_TC
    d_in, total = w_in_l.shape
    assert total == W_OFFS[-1] and d_in % tc == 0
    perm_t = _head_lane_permutation().T
    heights = (GA_COLS, AT_COLS, CONV_CH + SSD_HEADS)
    return pl.pallas_call(
        _prep_kernel,
        grid=(d_in // tc,),
        in_specs=[
            pl.BlockSpec((total, tc), lambda i: (0, i)),
            pl.BlockSpec((LANES, LANES), lambda i: (0, 0)),
        ],
        out_specs=tuple(pl.BlockSpec((ht, tc), lambda i: (0, i)) for ht in heights),
        out_shape=tuple(jax.ShapeDtypeStruct((ht, d_in), BF16) for ht in heights),
        compiler_params=_proj_params(),
        name="prep_weights",
    )(w_in_l.T, perm_t)


def _dot_nt(a, bt):
    return lax.dot_general(a, bt, (((1,), (1,)), ((), ())), preferred_element_type=F32)


def _proj_gate_kernel(x_ref, w_ref, bias_ref, o_ref):
    acc = _dot_nt(x_ref[...].astype(BF16), w_ref[...])
    t = acc[:, :GA_GM]
    o_ref[:, :GA_GM] = (t * _sigmoid(t)).astype(o_ref.dtype)
    o_ref[:, GA_GM:] = _sigmoid(acc[:, GA_GM:] + bias_ref[...]).astype(o_ref.dtype)


def _proj_gate(x2d, w, b_gate_row):
    t_rows = x2d.shape[0]
    tm = IN_TM_WIDE
    assert t_rows % tm == 0
    full = lambda shape: pl.BlockSpec(shape, lambda i: (0, 0))
    return pl.pallas_call(
        _proj_gate_kernel,
        grid=(t_rows // tm,),
        in_specs=[
            pl.BlockSpec((tm, D_MODEL), lambda i: (i, 0)),
            pl.BlockSpec((GA_COLS, D_MODEL), lambda i: (0, 0), pipeline_mode=pl.Buffered(1)),
            full((1, N_BRANCH * D_MODEL)),
        ],
        out_specs=pl.BlockSpec((tm, GA_COLS), lambda i: (i, 0)),
        out_shape=jax.ShapeDtypeStruct((t_rows, GA_COLS), BF16),
        compiler_params=_proj_params(),
        name="proj_gate",
    )(x2d, w, b_gate_row)


def _proj_attn_kernel(x_ref, w_ref, cos_ref, sin_ref, sel_ref, o_ref, nrm_ref, *, q_scale, tiles_per_seq):
    @pl.when(pl.program_id(0) % tiles_per_seq == 0)
    def _():
        nrm_ref[...] = jnp.zeros_like(nrm_ref)

    acc = _dot_nt(x_ref[...].astype(BF16), w_ref[...])
    cos = cos_ref[...]
    sin = sin_ref[...]
    cos_q = cos * q_scale
    sin_q = sin * q_scale
    squares = []
    for hd in range(AT_V // LANES):
        sl = slice(hd * LANES, (hd + 1) * LANES)
        a = acc[:, sl]
        partner = pltpu.roll(a, LANES // 2, axis=1)
        c, s = (cos_q, sin_q) if hd < AT_K // LANES else (cos, sin)
        rot = (a * c + partner * s).astype(o_ref.dtype)
        o_ref[:, sl] = rot
        rf = rot.astype(F32)
        squares.append((rf * rf).astype(BF16))
    o_ref[:, AT_V:] = acc[:, AT_V:].astype(o_ref.dtype)

    norms = jnp.dot(jnp.concatenate(squares, axis=1), sel_ref[...], preferred_element_type=F32)
    top = jnp.max(norms, axis=0, keepdims=True) * NORM_SLACK
    nrm_ref[0] = jnp.maximum(nrm_ref[0], jnp.broadcast_to(top, nrm_ref.shape[1:]))


def _proj_attn(x2d, w, cos_t, sin_t, seq):
    t_rows = x2d.shape[0]
    tm = IN_TM_WIDE
    assert t_rows % tm == 0 and seq % tm == 0
    tiles_per_seq = seq // tm
    table = lambda: pl.BlockSpec((tm, LANES), lambda i: (i % tiles_per_seq, 0))
    selector = jnp.asarray((np.arange(AT_V)[:, None] // LANES) == np.arange(LANES)[None, :], BF16)
    return pl.pallas_call(
        functools.partial(_proj_attn_kernel, q_scale=DA_HEAD_DIM ** -0.5 * LOG2E, tiles_per_seq=tiles_per_seq),
        grid=(t_rows // tm,),
        in_specs=[
            pl.BlockSpec((tm, D_MODEL), lambda i: (i, 0)),
            pl.BlockSpec((AT_COLS, D_MODEL), lambda i: (0, 0), pipeline_mode=pl.Buffered(1)),
            table(), table(),
            pl.BlockSpec((AT_V, LANES), lambda i: (0, 0)),
        ],
        out_specs=[
            pl.BlockSpec((tm, AT_COLS), lambda i: (i, 0)),
            pl.BlockSpec((1, SUBLANES, LANES), lambda i: (i // tiles_per_seq, 0, 0)),
        ],
        out_shape=[
            jax.ShapeDtypeStruct((t_rows, AT_COLS), BF16),
            jax.ShapeDtypeStruct((t_rows // seq, SUBLANES, LANES), F32),
        ],
        compiler_params=_proj_params(),
        name="proj_attn",
    )(x2d, w, cos_t, sin_t, selector)


def _proj_conv_kernel(x_ref, xh_ref, w_ref, p_ref, dtb_ref, o_ref, dt_ref, *, tiles_per_seq):
    i = pl.program_id(0)
    tm = o_ref.shape[0]
    xb = jnp.concatenate([x_ref[...], xh_ref[...]], axis=0).astype(BF16)
    full = _dot_nt(xb, w_ref[...])
    acc = full[:tm, :CONV_CH]
    halo = jnp.where(i % tiles_per_seq == 0, 0.0, full[tm:, :CONV_CH])

    bias = p_ref[CONV_WIDTH:CONV_WIDTH + 1, :]
    taps = [p_ref[k:k + 1, :] for k in range(CONV_WIDTH)]

    def conv(rows_val):
        out = taps[0] * rows_val
        for k in range(1, CONV_WIDTH):
            out = pltpu.roll(out, 1, axis=0) + taps[k] * rows_val
        out = out + bias
        return out * _sigmoid(out)

    head = BF16_ROWS
    ext = jnp.concatenate([halo, acc[:head]], axis=0)
    o_ref[:head, :] = conv(ext)[SUBLANES:, :].astype(o_ref.dtype)
    o_ref[head:, :] = conv(acc)[head:, :].astype(o_ref.dtype)

    raw = full[:tm, CONV_CH:] + dtb_ref[...]
    dt_ref[...] = jnp.maximum(raw, 0.0) + jnp.log(1.0 + jnp.exp(-jnp.abs(raw)))


def _proj_conv(x2d, w, params, dt_bias, seq):
    t_rows = x2d.shape[0]
    tm = IN_TM
    assert t_rows % tm == 0 and seq % tm == 0
    tiles_per_seq = seq // tm
    halo_blocks = tm // SUBLANES
    full = lambda shape: pl.BlockSpec(shape, lambda i: (0, 0))
    return pl.pallas_call(
        functools.partial(_proj_conv_kernel, tiles_per_seq=tiles_per_seq),
        grid=(t_rows // tm,),
        in_specs=[
            pl.BlockSpec((tm, D_MODEL), lambda i: (i, 0)),
            pl.BlockSpec((SUBLANES, D_MODEL), lambda i: (jnp.maximum(i * halo_blocks - 1, 0), 0)),
            full((CONV_CH + SSD_HEADS, D_MODEL)),
            full((SUBLANES, CONV_CH)),
            full((1, SSD_HEADS)),
        ],
        out_specs=[
            pl.BlockSpec((tm, CONV_CH), lambda i: (i, 0)),
            pl.BlockSpec((tm, SSD_HEADS), lambda i: (i, 0)),
        ],
        out_shape=[
            jax.ShapeDtypeStruct((t_rows, CONV_CH), BF16),
            jax.ShapeDtypeStruct((t_rows, SSD_HEADS), F32),
        ],
        compiler_params=pltpu.CompilerParams(dimension_semantics=("arbitrary",),
                                             vmem_limit_bytes=VMEM_LIMIT_BYTES),
        name="proj_conv",
    )(x2d, x2d, w, params, dt_bias)


def _ssd_kernel(xs_ref, b_ref, c_ref, sz_ref, dt_ref, a_ref, dskip_ref, normw_ref,
                expand_ref, bdtril_ref,
                y_ref, acol_s, dtx_s, ht_s):
    rows = xs_ref.shape[0]
    n_chunks = rows // CHUNK
    gw = SSD_WIDTH // SSD_GROUPS

    @pl.when(pl.program_id(1) == 0)
    def _():
        ht_s[...] = jnp.zeros_like(ht_s)

    dt = dt_ref[...]
    a_cs = _dot01_left(bdtril_ref[...], dt * (a_ref[...] * LOG2E))
    expand = expand_ref[...]
    acol_s[...] = _dot01_right(a_cs, expand)
    dtx_s[...] = _dot01_right(dt, expand)

    qw = 4 * SSD_HEAD_DIM
    quads_per_group = gw // qw
    lane = lax.broadcasted_iota(jnp.int32, (CHUNK, qw), 1)
    row = lax.broadcasted_iota(jnp.int32, (CHUNK, qw), 0)
    s_of_lane = lane & (CHUNK - 1)
    diag_mask = s_of_lane == row
    causal_mask = s_of_lane <= row
    bd_r = lax.broadcasted_iota(jnp.int32, (4 * CHUNK, qw), 0) // CHUNK
    bd_c = lax.broadcasted_iota(jnp.int32, (4 * CHUNK, qw), 1) // SSD_HEAD_DIM
    bd_mask = bd_r == bd_c

    def chunk_body(c, carry):
        r0 = pl.multiple_of(c * CHUNK, CHUNK)
        rows_c = pl.ds(r0, CHUNK)
        bc = b_ref[rows_c, :]
        cc = c_ref[rows_c, :]
        for g in range(SSD_GROUPS):
            cg = cc[:, g * SSD_STATE:(g + 1) * SSD_STATE]
            bg = bc[:, g * SSD_STATE:(g + 1) * SSD_STATE]
            cb_g = lax.dot_general(cg, bg, (((1,), (1,)), ((), ())), preferred_element_type=F32)
            cb_rep = jnp.concatenate([cb_g] * 4, axis=1)
            gated = []
            for qd in range(quads_per_group):
                sl = slice(g * gw + qd * qw, g * gw + (qd + 1) * qw)
                acol = acol_s[rows_c, sl]
                a_last = acol[CHUNK - 1:CHUNK, :]
                xs = xs_ref[rows_c, sl].astype(F32)
                xdt_f = xs * dtx_s[rows_c, sl]

                arow = jnp.sum(jnp.where(diag_mask, acol, 0.0), axis=0, keepdims=True)
                decay_ls = jnp.exp2(jnp.where(causal_mask, acol - arow, NEG_BIG))
                w_ls = (cb_rep * decay_ls).astype(BF16)
                xq = xdt_f.astype(BF16)
                bd = jnp.where(bd_mask, jnp.concatenate([xq] * 4, axis=0), jnp.zeros((), BF16))
                y_diag = jnp.dot(w_ls, bd, preferred_element_type=F32)

                ht = ht_s[:, sl]
                y_off = jnp.dot(cg, ht.astype(BF16), preferred_element_type=F32)
                y = y_diag + y_off * jnp.exp2(acol) + dskip_ref[:, sl] * xs
                gated.append(y * sz_ref[rows_c, sl].astype(F32))

                xd = (xdt_f * jnp.exp2(a_last - acol)).astype(BF16)
                st = lax.dot_general(bg, xd, (((0,), (0,)), ((), ())), preferred_element_type=F32)
                ht_s[:, sl] = ht * jnp.exp2(a_last) + st

            ssq = sum(jnp.sum(t * t, axis=-1, keepdims=True) for t in gated)
            inv = lax.rsqrt(ssq * (1.0 / gw) + EPS)
            for qd in range(quads_per_group):
                sl = slice(g * gw + qd * qw, g * gw + (qd + 1) * qw)
                y_ref[rows_c, sl] = (gated[qd] * inv * normw_ref[:, sl]).astype(y_ref.dtype)
        return carry

    lax.fori_loop(0, n_chunks, chunk_body, 0, unroll=4)


def _ssd(conv_act, gate_act, dt, a_row, dskip_row, normw_row, batch, seq):
    rows = SSD_ROWS
    assert seq % rows == 0 and rows % (4 * CHUNK) == 0
    blocks_per_seq = seq // rows
    hh = np.arange(SSD_WIDTH) // SSD_HEAD_DIM
    expand = jnp.asarray(np.arange(SSD_HEADS)[:, None] == hh[None, :], BF16)
    rr = np.arange(rows)
    same_chunk = (rr[:, None] // CHUNK) == (rr[None, :] // CHUNK)
    bdtril = jnp.asarray(same_chunk & (rr[None, :] <= rr[:, None]), BF16)

    def row_map(b, r):
        return b * blocks_per_seq + r

    def col_spec(width, off):
        assert off % width == 0
        return pl.BlockSpec((rows, width), lambda b, r: (row_map(b, r), off // width))

    const = lambda shape: pl.BlockSpec(shape, lambda b, r: (0, 0))
    return pl.pallas_call(
        _ssd_kernel,
        grid=(batch, blocks_per_seq),
        in_specs=[
            col_spec(SSD_WIDTH, CV_XS),
            col_spec(BC_WIDTH, CV_B),
            col_spec(BC_WIDTH, CV_C),
            col_spec(SSD_WIDTH, GA_SZ),
            pl.BlockSpec((rows, SSD_HEADS), lambda b, r: (row_map(b, r), 0)),
            const((1, SSD_HEADS)),
            const((1, SSD_WIDTH)),
            const((1, SSD_WIDTH)),
            const((SSD_HEADS, SSD_WIDTH)),
            const((rows, rows)),
        ],
        out_specs=pl.BlockSpec((rows, SSD_WIDTH), lambda b, r: (row_map(b, r), 0)),
        out_shape=jax.ShapeDtypeStruct((batch * seq, SSD_WIDTH), BF16),
        scratch_shapes=[
            pltpu.VMEM((rows, SSD_WIDTH), F32),
            pltpu.VMEM((rows, SSD_WIDTH), F32),
            pltpu.VMEM((SSD_STATE, SSD_WIDTH), F32),
        ],
        compiler_params=pltpu.CompilerParams(
            dimension_semantics=("arbitrary", "arbitrary"),
            vmem_limit_bytes=VMEM_LIMIT_BYTES),
        name="ssd",
    )(conv_act, conv_act, conv_act, gate_act, dt, a_row, dskip_row, normw_row, expand, bdtril)


def _attn_kernel(q_ref, k_ref, v_ref, gb_ref, nrm_ref, lq1_ref, lk1_ref, lq2_ref, lk2_ref, subw_ref,
                 o_ref, vaug_s, m_s, acc_s, qq_s, *, lambda_init):
    hd = pl.program_id(1)
    qi = pl.program_id(2)
    tq = q_ref.shape[0]
    tk = ATT_TK
    seq = k_ref.shape[0]

    @pl.when(qi == 0)
    def _():
        vaug_s[:, :DA_VDIM] = v_ref[...].astype(vaug_s.dtype)
        vaug_s[:, DA_VDIM:] = jnp.ones((seq, DA_VDIM), vaug_s.dtype)

    assert tq % tk == 0
    band = ATT_BAND
    n_bands = tq // band
    n_full = qi * (tq // tk)
    half_lane = lax.broadcasted_iota(jnp.int32, (band, DA_VDIM), 1) & (LANES // 2 - 1)
    in_map0 = (half_lane < ROT_HALF) | ((half_lane >= ROT_DIM) & (half_lane < ROT_DIM + ROT_PASS))
    zero = jnp.zeros((), q_ref.dtype)

    qq = []
    for c in range(n_bands):
        q = q_ref[c * band:(c + 1) * band, :]
        qq.append(jnp.concatenate([jnp.where(in_map0, q, zero), jnp.where(in_map0, zero, q)], axis=0))

    def band_scores(c, k):
        return lax.dot_general(qq[c], k, (((1,), (1,)), ((), ())), preferred_element_type=F32)

    def band_rows(c):
        return slice(c * 2 * band, (c + 1) * 2 * band)

    def diagonal_operands(c):
        nk = (c + 1) * band
        k0 = pl.multiple_of(qi * tq, tq)
        r_chunk = (c * band + lax.broadcasted_iota(jnp.int32, (band, nk), 0)) // CHUNK
        k_chunk = lax.broadcasted_iota(jnp.int32, (band, nk), 1) // CHUNK
        ok = k_chunk <= r_chunk
        return k_ref[pl.ds(k0, nk), :], vaug_s[pl.ds(k0, nk), :], jnp.concatenate([ok, ok], axis=0)

    def full_operands(kt):
        k0 = pl.multiple_of(kt * tk, tk)
        return k_ref[pl.ds(k0, tk), :], vaug_s[pl.ds(k0, tk), :]

    norm_row = nrm_ref[0, 0:1, :]
    norm_lane = lax.broadcasted_iota(jnp.int32, norm_row.shape, 1)
    q_norm2 = jnp.max(jnp.where(norm_lane == hd, norm_row, 0.0))
    k_norm2 = jnp.max(jnp.where(norm_lane == DA_HEADS + hd, norm_row, 0.0))
    unshifted_ok = q_norm2 * k_norm2 <= ATT_SAFE_LOG2 * ATT_SAFE_LOG2

    @pl.when(unshifted_ok)
    def _():
        for c in range(n_bands):
            k, vt, ok = diagonal_operands(c)
            p = jnp.exp2(jnp.where(ok, band_scores(c, k), NEG_BIG)).astype(vaug_s.dtype)
            acc_s[band_rows(c), :] = jnp.dot(p, vt, preferred_element_type=F32)

        def tile(kt, carry):
            k, vt = full_operands(kt)
            for c in range(n_bands):
                p = jnp.exp2(band_scores(c, k)).astype(vaug_s.dtype)
                acc_s[band_rows(c), :] += jnp.dot(p, vt, preferred_element_type=F32)
            return carry

        lax.fori_loop(0, n_full, tile, 0)

    @pl.when(jnp.logical_not(unshifted_ok))
    def _():
        m_s[...] = jnp.full_like(m_s, NEG_BIG)
        acc_s[...] = jnp.zeros_like(acc_s)
        for c in range(n_bands):
            qq_s[c] = qq[c]

        def softmax_pv(c, k0, nk, diagonal):
            rows = pl.ds(pl.multiple_of(c * 2 * band, 2 * band), 2 * band)
            k, vt = k_ref[pl.ds(k0, nk), :], vaug_s[pl.ds(k0, nk), :]
            s = lax.dot_general(qq_s[c], k, (((1,), (1,)), ((), ())), preferred_element_type=F32)
            if diagonal:
                r_chunk = (c * band + lax.broadcasted_iota(jnp.int32, (band, nk), 0)) // CHUNK
                k_chunk = lax.broadcasted_iota(jnp.int32, (band, nk), 1) // CHUNK
                ok = k_chunk <= r_chunk
                s = jnp.where(jnp.concatenate([ok, ok], axis=0), s, NEG_BIG)
            m_old = m_s[rows, :]
            m_new = jnp.maximum(m_old, jnp.max(s, axis=-1, keepdims=True))
            alpha = jnp.exp2(m_old - m_new)
            p = jnp.exp2(s - m_new[:, 0:1]).astype(vaug_s.dtype)
            pv = jnp.dot(p, vt, preferred_element_type=F32)
            acc_s[rows, :] = jnp.concatenate([alpha, alpha], axis=1) * acc_s[rows, :] + pv
            m_s[rows, :] = m_new

        def band_loop(k0, nk, diagonal):
            def body(c, carry):
                softmax_pv(c, k0, nk, diagonal)
                return carry
            lax.fori_loop(0, n_bands, body, 0)

        def tile(kt, carry):
            band_loop(pl.multiple_of(kt * tk, tk), tk, False)
            return carry

        lax.fori_loop(0, n_full, tile, 0)
        band_loop(pl.multiple_of(qi * tq, tq), tq, True)

    lam = (jnp.exp(jnp.sum(lq1_ref[...] * lk1_ref[...], axis=-1, keepdims=True))
           - jnp.exp(jnp.sum(lq2_ref[...] * lk2_ref[...], axis=-1, keepdims=True)) + lambda_init)
    scale = subw_ref[...] * (1.0 - lambda_init)
    for c in range(n_bands):
        acc = acc_s[c * 2 * band:(c + 1) * 2 * band, :]
        o_all = acc[:, :DA_VDIM] / acc[:, DA_VDIM:]
        o = o_all[:band] - lam * o_all[band:]
        ms = jnp.mean(o * o, axis=-1, keepdims=True)
        o = o * lax.rsqrt(ms + EPS) * scale
        rows = slice(c * band, (c + 1) * band)
        o_ref[rows, :] = (o * gb_ref[rows, :].astype(F32)).astype(o_ref.dtype)


def _attention(attn_act, gate_act, norms, lq1, lk1, lq2, lk2, subw, batch, seq, lambda_init):
    tq = ATT_TQ
    assert seq % tq == 0 and tq % ATT_TK == 0 and tq % ATT_BAND == 0 and ATT_BAND % CHUNK == 0
    q_tiles = seq // tq

    def head_spec(rows_blk, off):
        assert off % DA_VDIM == 0
        if rows_blk == seq:
            return pl.BlockSpec((seq, DA_VDIM), lambda b, hd, qi: (b, off // DA_VDIM + hd))
        return pl.BlockSpec((rows_blk, DA_VDIM), lambda b, hd, qi: (b * q_tiles + qi, off // DA_VDIM + hd))

    vec = lambda n: pl.BlockSpec((1, n), lambda b, hd, qi: (0, 0))
    kern = functools.partial(_attn_kernel, lambda_init=lambda_init)
    return pl.pallas_call(
        kern,
        grid=(batch, DA_HEADS, q_tiles),
        in_specs=[
            head_spec(tq, AT_Q),
            head_spec(seq, AT_K),
            head_spec(seq, AT_V),
            head_spec(tq, GA_GB),
            pl.BlockSpec((1, SUBLANES, LANES), lambda b, hd, qi: (b, 0, 0)),
            vec(DA_HEAD_DIM), vec(DA_HEAD_DIM), vec(DA_HEAD_DIM), vec(DA_HEAD_DIM),
            vec(DA_VDIM),
        ],
        out_specs=pl.BlockSpec((tq, DA_VDIM), lambda b, hd, qi: (b * q_tiles + qi, hd)),
        out_shape=jax.ShapeDtypeStruct((batch * seq, DA_WIDTH), BF16),
        scratch_shapes=[
            pltpu.VMEM((seq, 2 * DA_VDIM), BF16),
            pltpu.VMEM((2 * tq, LANES), F32),
            pltpu.VMEM((2 * tq, 2 * DA_VDIM), F32),
            pltpu.VMEM((tq // ATT_BAND, 2 * ATT_BAND, DA_VDIM), BF16),
        ],
        compiler_params=pltpu.CompilerParams(
            dimension_semantics=("arbitrary", "arbitrary", "arbitrary"),
            vmem_limit_bytes=VMEM_LIMIT_BYTES),
        name="diffattn",
    )(attn_act, attn_act, attn_act, gate_act, norms, lq1, lk1, lq2, lk2, subw)


def _out_kernel(x_ref, ya_ref, ob_ref, g0_ref, g1_ref, wa_ref, wb_ref, wo_ref, lng_ref, lnb_ref,
                out_ref, *, alpha):
    d = functools.partial(jnp.dot, preferred_element_type=F32)
    branch_a = d(ya_ref[...], wa_ref[...])
    branch_b = d(ob_ref[...], wb_ref[...])
    merged = g0_ref[...].astype(F32) * branch_a + g1_ref[...].astype(F32) * branch_b
    y = d(merged.astype(BF16), wo_ref[...])
    r = alpha * x_ref[...] + y
    mu = jnp.mean(r, axis=-1, keepdims=True)
    rc = r - mu
    var = jnp.mean(rc * rc, axis=-1, keepdims=True)
    out_ref[...] = rc * lax.rsqrt(var + EPS) * lng_ref[...] + lnb_ref[...]


def _output_stage(x2d, y_ssd, o_att, gate_act, w_a, w_b, w_o, ln_g, ln_b, alpha):
    t_rows = x2d.shape[0]
    tm = OUT_TM
    assert t_rows % tm == 0 and GA_GM % D_MODEL == 0
    row = lambda width: pl.BlockSpec((tm, width), lambda i: (i, 0))
    full = lambda shape: pl.BlockSpec(shape, lambda i: (0, 0))
    weight = lambda shape: pl.BlockSpec(shape, lambda i: (0, 0), pipeline_mode=pl.Buffered(1))
    gate_spec = lambda k: pl.BlockSpec((tm, D_MODEL), lambda i: (i, GA_GM // D_MODEL + k))
    return pl.pallas_call(
        functools.partial(_out_kernel, alpha=alpha),
        grid=(t_rows // tm,),
        in_specs=[row(D_MODEL), row(SSD_WIDTH), row(DA_WIDTH), gate_spec(0), gate_spec(1),
                  weight((SSD_WIDTH, D_MODEL)), weight((DA_WIDTH, D_MODEL)), weight((D_MODEL, D_MODEL)),
                  full((1, D_MODEL)), full((1, D_MODEL))],
        out_specs=row(D_MODEL),
        out_shape=jax.ShapeDtypeStruct((t_rows, D_MODEL), F32),
        compiler_params=pltpu.CompilerParams(
            dimension_semantics=("arbitrary",),
            vmem_limit_bytes=VMEM_LIMIT_BYTES),
        name="outproj",
    )(x2d, y_ssd, o_att, gate_act, gate_act, w_a, w_b, w_o, ln_g, ln_b)


def _rope_tables(seq):
    pos = np.arange(seq, dtype=np.float64)
    inv_freq = ROPE_THETA ** (-np.arange(0, ROT_DIM, 2, dtype=np.float64) / ROT_DIM)
    ang = pos[:, None] * inv_freq[None, :]
    cos, sin = np.cos(ang).astype(np.float32), np.sin(ang).astype(np.float32)
    ones = np.ones((seq, LANES // 2 - ROT_DIM), np.float32)
    zeros = np.zeros_like(ones)
    cos_t = np.concatenate([cos, cos, ones, cos, cos, ones], axis=1)
    sin_t = np.concatenate([-sin, -sin, zeros, sin, sin, zeros], axis=1)
    return jnp.asarray(cos_t), jnp.asarray(sin_t)


def _head_lane_permutation():
    src = np.zeros(LANES, np.int64)
    for m in range(2):
        for d in range(DA_HEAD_DIM):
            if d < ROT_HALF:
                lane = m * ROT_HALF + d
            elif d < ROT_DIM:
                lane = LANES // 2 + m * ROT_HALF + (d - ROT_HALF)
            elif d < ROT_DIM + ROT_PASS:
                lane = ROT_DIM + m * ROT_PASS + (d - ROT_DIM)
            else:
                lane = LANES // 2 + ROT_DIM + m * ROT_PASS + (d - ROT_DIM - ROT_PASS)
            src[lane] = m * DA_HEAD_DIM + d
    perm = np.zeros((LANES, LANES), np.float32)
    perm[src, np.arange(LANES)] = 1.0
    return jnp.asarray(perm, BF16)


def _layer(x2d, batch, seq, depth_total, l, w_in, b_gate, conv_w, conv_b, dt_bias, a_log, d_skip,
           ssd_norm_w, lambda_q1, lambda_k1, lambda_q2, lambda_k2, subln_w, w_a, w_b, w_o, ln_g, ln_b,
           tables):
    alpha = (2.0 * depth_total) ** 0.25
    lambda_init = 0.8 - 0.6 * math.exp(-0.3 * l)
    w_gate, w_attn, w_conv = _prep_weights(w_in)
    p_conv = jnp.concatenate([conv_w.astype(F32), conv_b[None, :].astype(F32),
                              jnp.zeros((SUBLANES - CONV_WIDTH - 1, CONV_CH), F32)], axis=0)

    gate_act = _proj_gate(x2d, w_gate, b_gate[None, :].astype(F32))
    attn_act, qk_norms = _proj_attn(x2d, w_attn, *tables, seq)
    conv_act, dt = _proj_conv(x2d, w_conv, p_conv, dt_bias[None, :].astype(F32), seq)

    a_row = -jnp.exp(a_log.astype(F32))[None, :]
    dskip_row = jnp.repeat(d_skip.astype(F32), SSD_HEAD_DIM)[None, :]
    y_ssd = _ssd(conv_act, gate_act, dt, a_row, dskip_row, ssd_norm_w[None, :].astype(F32), batch, seq)

    o_att = _attention(attn_act, gate_act, qk_norms, lambda_q1[None, :], lambda_k1[None, :],
                       lambda_q2[None, :], lambda_k2[None, :], subln_w[None, :], batch, seq, lambda_init)

    return _output_stage(x2d, y_ssd, o_att, gate_act, w_a.astype(BF16), w_b.astype(BF16), w_o.astype(BF16),
                         ln_g[None, :], ln_b[None, :], alpha)


def kernel(x, w_in, b_gate, conv_w, conv_b, dt_bias, a_log, d_skip, ssd_norm_w, lambda_q1, lambda_k1,
           lambda_q2, lambda_k2, subln_w, w_a, w_b, w_o, ln_g, ln_b):
    batch, seq, _ = x.shape
    depth = w_in.shape[0]
    tables = _rope_tables(seq)
    x2d = x.reshape(batch * seq, D_MODEL)
    for l in range(depth):
        x2d = _layer(x2d, batch, seq, depth, l, w_in[l], b_gate[l], conv_w[l], conv_b[l], dt_bias[l],
                     a_log[l], d_skip[l], ssd_norm_w[l], lambda_q1[l], lambda_k1[l], lambda_q2[l],
                     lambda_k2[l], subln_w[l], w_a[l], w_b[l], w_o[l], ln_g[l], ln_b[l], tables)
    return x2d.reshape(batch, seq, D_MODEL)
```

```python
import functools
import math

import jax
import jax.numpy as jnp
import numpy as np
from jax import lax
from jax.experimental import pallas as pl
from jax.experimental.pallas import tpu as pltpu

F32 = jnp.float32
BF16 = jnp.bfloat16

D_MODEL = 1024
CHUNK = 64
SSD_HEADS = 16
SSD_HEAD_DIM = 64
SSD_WIDTH = SSD_HEADS * SSD_HEAD_DIM
SSD_GROUPS = 2
SSD_STATE = 128
CONV_WIDTH = 4
BC_WIDTH = SSD_GROUPS * SSD_STATE
CONV_CH = SSD_WIDTH + 2 * BC_WIDTH
DA_HEADS = 8
DA_HEAD_DIM = 64
DA_VDIM = 2 * DA_HEAD_DIM
DA_WIDTH = DA_HEADS * DA_VDIM
ROPE_THETA = 500000.0
ROT_DIM = DA_HEAD_DIM // 4
ROT_HALF = ROT_DIM // 2
N_BRANCH = 2
EPS = 1e-5
IN_SIZES = (SSD_WIDTH, CONV_CH, SSD_HEADS, DA_WIDTH, DA_WIDTH, DA_WIDTH, DA_WIDTH, N_BRANCH * D_MODEL)
W_OFFS = tuple(sum(IN_SIZES[:n]) for n in range(len(IN_SIZES) + 1))

GA_SZ, GA_GB, GA_GM = 0, SSD_WIDTH, SSD_WIDTH + DA_WIDTH
GA_COLS = GA_GM + N_BRANCH * D_MODEL
AT_Q, AT_K, AT_V = 0, DA_WIDTH, 2 * DA_WIDTH
AT_COLS = 3 * DA_WIDTH
CV_XS, CV_B, CV_C = 0, SSD_WIDTH, SSD_WIDTH + BC_WIDTH

ROT_PASS = (DA_HEAD_DIM - ROT_DIM) // 2

LANES = 128
SUBLANES = 8
BF16_ROWS = 16
VMEM_LIMIT_BYTES = 56 * 1024 * 1024

PREP_TC = 128
IN_TM = 1024
IN_TM_WIDE = 1024
SSD_ROWS = 512
ATT_TQ = 2048
ATT_TK = 2048
ATT_BAND = 256
ATT_SAFE_LOG2 = 64.0
NORM_SLACK = 1.0 + 2.0 ** -7
OUT_TM = 1024
OUT_BANDS = 4

NEG_BIG = -1e30
LOG2E = 1.4426950408889634


def _sigmoid(t):
    return 0.5 * jnp.tanh(0.5 * t) + 0.5


def _split2(v):
    hi = v.astype(BF16)
    mid = (v - hi.astype(F32)).astype(BF16)
    return hi, mid


def _dot01_right(v, m01):
    hi, mid = _split2(v)
    d = functools.partial(jnp.dot, preferred_element_type=F32)
    return d(hi, m01) + d(mid, m01)


def _dot01_left(m01, v):
    hi, mid = _split2(v)
    d = functools.partial(jnp.dot, preferred_element_type=F32)
    return d(m01, hi) + d(m01, mid)


def _proj_params():
    return pltpu.CompilerParams(dimension_semantics=("arbitrary",),
                                vmem_limit_bytes=VMEM_LIMIT_BYTES)


def _prep_kernel(wt_ref, perm_ref, wg_ref, wa_ref, wc_ref):
    def rows(n):
        return wt_ref[W_OFFS[n]:W_OFFS[n + 1], :].astype(BF16)

    wg_ref[GA_SZ:GA_GB, :] = rows(0)
    wg_ref[GA_GB:GA_GM, :] = rows(6)
    wg_ref[GA_GM:, :] = rows(7)
    wc_ref[...] = wt_ref[W_OFFS[1]:W_OFFS[3], :].astype(BF16)
    perm = perm_ref[...]
    for base, n in ((AT_Q, 3), (AT_K, 4)):
        for hd in range(DA_HEADS):
            r0 = W_OFFS[n] + hd * LANES
            blk = wt_ref[r0:r0 + LANES, :].astype(BF16)
            moved = jnp.dot(perm, blk, preferred_element_type=F32)
            wa_ref[base + hd * LANES:base + (hd + 1) * LANES, :] = moved.astype(BF16)
    wa_ref[AT_V:, :] = rows(5)


def _prep_weights(w_in_l):
    tc = PREP_TC
    d_in, total = w_in_l.shape
    assert total == W_OFFS[-1] and d_in % tc == 0
    perm_t = _head_lane_permutation().T
    heights = (GA_COLS, AT_COLS, CONV_CH + SSD_HEADS)
    return pl.pallas_call(
        _prep_kernel,
        grid=(d_in // tc,),
        in_specs=[
            pl.BlockSpec((total, tc), lambda i: (0, i)),
            pl.BlockSpec((LANES, LANES), lambda i: (0, 0)),
        ],
        out_specs=tuple(pl.BlockSpec((ht, tc), lambda i: (0, i)) for ht in heights),
        out_shape=tuple(jax.ShapeDtypeStruct((ht, d_in), BF16) for ht in heights),
        compiler_params=_proj_params(),
        name="prep_weights",
    )(w_in_l.T, perm_t)


def _dot_nt(a, bt):
    return lax.dot_general(a, bt, (((1,), (1,)), ((), ())), preferred_element_type=F32)


def _proj_gate_kernel(x_ref, w_ref, bias_ref, o_ref):
    acc = _dot_nt(x_ref[...].astype(BF16), w_ref[...])
    t = acc[:, :GA_GM]
    o_ref[:, :GA_GM] = (t * _sigmoid(t)).astype(o_ref.dtype)
    o_ref[:, GA_GM:] = _sigmoid(acc[:, GA_GM:] + bias_ref[...]).astype(o_ref.dtype)


def _proj_gate(x2d, w, b_gate_row):
    t_rows = x2d.shape[0]
    tm = IN_TM_WIDE
    assert t_rows % tm == 0
    full = lambda shape: pl.BlockSpec(shape, lambda i: (0, 0))
    return pl.pallas_call(
        _proj_gate_kernel,
        grid=(t_rows // tm,),
        in_specs=[
            pl.BlockSpec((tm, D_MODEL), lambda i: (i, 0)),
            pl.BlockSpec((GA_COLS, D_MODEL), lambda i: (0, 0), pipeline_mode=pl.Buffered(1)),
            full((1, N_BRANCH * D_MODEL)),
        ],
        out_specs=pl.BlockSpec((tm, GA_COLS), lambda i: (i, 0)),
        out_shape=jax.ShapeDtypeStruct((t_rows, GA_COLS), BF16),
        compiler_params=_proj_params(),
        name="proj_gate",
    )(x2d, w, b_gate_row)


def _proj_attn_kernel(x_ref, w_ref, cos_ref, sin_ref, sel_ref, o_ref, nrm_ref, *, q_scale, tiles_per_seq):
    @pl.when(pl.program_id(0) % tiles_per_seq == 0)
    def _():
        nrm_ref[...] = jnp.zeros_like(nrm_ref)

    acc = _dot_nt(x_ref[...].astype(BF16), w_ref[...])
    cos = cos_ref[...]
    sin = sin_ref[...]
    cos_q = cos * q_scale
    sin_q = sin * q_scale
    squares = []
    for hd in range(AT_V // LANES):
        sl = slice(hd * LANES, (hd + 1) * LANES)
        a = acc[:, sl]
        partner = pltpu.roll(a, LANES // 2, axis=1)
        c, s = (cos_q, sin_q) if hd < AT_K // LANES else (cos, sin)
        rot = (a * c + partner * s).astype(o_ref.dtype)
        o_ref[:, sl] = rot
        rf = rot.astype(F32)
        squares.append((rf * rf).astype(BF16))
    o_ref[:, AT_V:] = acc[:, AT_V:].astype(o_ref.dtype)

    norms = jnp.dot(jnp.concatenate(squares, axis=1), sel_ref[...], preferred_element_type=F32)
    top = jnp.max(norms, axis=0, keepdims=True) * NORM_SLACK
    nrm_ref[0] = jnp.maximum(nrm_ref[0], jnp.broadcast_to(top, nrm_ref.shape[1:]))


def _proj_attn(x2d, w, cos_t, sin_t, seq):
    t_rows = x2d.shape[0]
    tm = IN_TM_WIDE
    assert t_rows % tm == 0 and seq % tm == 0
    tiles_per_seq = seq // tm
    table = lambda: pl.BlockSpec((tm, LANES), lambda i: (i % tiles_per_seq, 0))
    selector = jnp.asarray((np.arange(AT_V)[:, None] // LANES) == np.arange(LANES)[None, :], BF16)
    return pl.pallas_call(
        functools.partial(_proj_attn_kernel, q_scale=DA_HEAD_DIM ** -0.5 * LOG2E, tiles_per_seq=tiles_per_seq),
        grid=(t_rows // tm,),
        in_specs=[
            pl.BlockSpec((tm, D_MODEL), lambda i: (i, 0)),
            pl.BlockSpec((AT_COLS, D_MODEL), lambda i: (0, 0), pipeline_mode=pl.Buffered(1)),
            table(), table(),
            pl.BlockSpec((AT_V, LANES), lambda i: (0, 0)),
        ],
        out_specs=[
            pl.BlockSpec((tm, AT_COLS), lambda i: (i, 0)),
            pl.BlockSpec((1, SUBLANES, LANES), lambda i: (i // tiles_per_seq, 0, 0)),
        ],
        out_shape=[
            jax.ShapeDtypeStruct((t_rows, AT_COLS), BF16),
            jax.ShapeDtypeStruct((t_rows // seq, SUBLANES, LANES), F32),
        ],
        compiler_params=_proj_params(),
        name="proj_attn",
    )(x2d, w, cos_t, sin_t, selector)


def _proj_conv_kernel(x_ref, xh_ref, w_ref, p_ref, dtb_ref, o_ref, dt_ref, *, tiles_per_seq):
    i = pl.program_id(0)
    tm = o_ref.shape[0]
    xb = jnp.concatenate([x_ref[...], xh_ref[...]], axis=0).astype(BF16)
    full = _dot_nt(xb, w_ref[...])
    acc = full[:tm, :CONV_CH]
    halo = jnp.where(i % tiles_per_seq == 0, 0.0, full[tm:, :CONV_CH])

    bias = p_ref[CONV_WIDTH:CONV_WIDTH + 1, :]
    taps = [p_ref[k:k + 1, :] for k in range(CONV_WIDTH)]

    def conv(rows_val):
        out = taps[0] * rows_val
        for k in range(1, CONV_WIDTH):
            out = pltpu.roll(out, 1, axis=0) + taps[k] * rows_val
        out = out + bias
        return out * _sigmoid(out)

    head = BF16_ROWS
    ext = jnp.concatenate([halo, acc[:head]], axis=0)
    o_ref[:head, :] = conv(ext)[SUBLANES:, :].astype(o_ref.dtype)
    o_ref[head:, :] = conv(acc)[head:, :].astype(o_ref.dtype)

    raw = full[:tm, CONV_CH:] + dtb_ref[...]
    dt_ref[...] = jnp.maximum(raw, 0.0) + jnp.log(1.0 + jnp.exp(-jnp.abs(raw)))


def _proj_conv(x2d, w, params, dt_bias, seq):
    t_rows = x2d.shape[0]
    tm = IN_TM
    assert t_rows % tm == 0 and seq % tm == 0
    tiles_per_seq = seq // tm
    halo_blocks = tm // SUBLANES
    full = lambda shape: pl.BlockSpec(shape, lambda i: (0, 0))
    return pl.pallas_call(
        functools.partial(_proj_conv_kernel, tiles_per_seq=tiles_per_seq),
        grid=(t_rows // tm,),
        in_specs=[
            pl.BlockSpec((tm, D_MODEL), lambda i: (i, 0)),
            pl.BlockSpec((SUBLANES, D_MODEL), lambda i: (jnp.maximum(i * halo_blocks - 1, 0), 0)),
            full((CONV_CH + SSD_HEADS, D_MODEL)),
            full((SUBLANES, CONV_CH)),
            full((1, SSD_HEADS)),
        ],
        out_specs=[
            pl.BlockSpec((tm, CONV_CH), lambda i: (i, 0)),
            pl.BlockSpec((tm, SSD_HEADS), lambda i: (i, 0)),
        ],
        out_shape=[
            jax.ShapeDtypeStruct((t_rows, CONV_CH), BF16),
            jax.ShapeDtypeStruct((t_rows, SSD_HEADS), F32),
        ],
        compiler_params=pltpu.CompilerParams(dimension_semantics=("arbitrary",),
                                             vmem_limit_bytes=VMEM_LIMIT_BYTES),
        name="proj_conv",
    )(x2d, x2d, w, params, dt_bias)


def _ssd_kernel(xs_ref, b_ref, c_ref, sz_ref, dt_ref, a_ref, dskip_ref, normw_ref,
                expand_ref, bdtril_ref,
                y_ref, acol_s, dtx_s, ht_s):
    rows = xs_ref.shape[0]
    n_chunks = rows // CHUNK
    gw = SSD_WIDTH // SSD_GROUPS

    @pl.when(pl.program_id(1) == 0)
    def _():
        ht_s[...] = jnp.zeros_like(ht_s)

    dt = dt_ref[...]
    a_cs = _dot01_left(bdtril_ref[...], dt * (a_ref[...] * LOG2E))
    expand = expand_ref[...]
    acol_s[...] = _dot01_right(a_cs, expand)
    dtx_s[...] = _dot01_right(dt, expand)

    qw = 4 * SSD_HEAD_DIM
    quads_per_group = gw // qw
    lane = lax.broadcasted_iota(jnp.int32, (CHUNK, qw), 1)
    row = lax.broadcasted_iota(jnp.int32, (CHUNK, qw), 0)
    s_of_lane = lane & (CHUNK - 1)
    diag_mask = s_of_lane == row
    causal_mask = s_of_lane <= row
    bd_r = lax.broadcasted_iota(jnp.int32, (4 * CHUNK, qw), 0) // CHUNK
    bd_c = lax.broadcasted_iota(jnp.int32, (4 * CHUNK, qw), 1) // SSD_HEAD_DIM
    bd_mask = bd_r == bd_c

    def chunk_body(c, carry):
        r0 = pl.multiple_of(c * CHUNK, CHUNK)
        rows_c = pl.ds(r0, CHUNK)
        bc = b_ref[rows_c, :]
        cc = c_ref[rows_c, :]
        for g in range(SSD_GROUPS):
            cg = cc[:, g * SSD_STATE:(g + 1) * SSD_STATE]
            bg = bc[:, g * SSD_STATE:(g + 1) * SSD_STATE]
            cb_g = lax.dot_general(cg, bg, (((1,), (1,)), ((), ())), preferred_element_type=F32)
            cb_rep = jnp.concatenate([cb_g] * 4, axis=1)
            gated = []
            for qd in range(quads_per_group):
                sl = slice(g * gw + qd * qw, g * gw + (qd + 1) * qw)
                acol = acol_s[rows_c, sl]
                a_last = acol[CHUNK - 1:CHUNK, :]
                xs = xs_ref[rows_c, sl].astype(F32)
                xdt_f = xs * dtx_s[rows_c, sl]

                arow = jnp.sum(jnp.where(diag_mask, acol, 0.0), axis=0, keepdims=True)
                decay_ls = jnp.exp2(jnp.where(causal_mask, acol - arow, NEG_BIG))
                w_ls = (cb_rep * decay_ls).astype(BF16)
                xq = xdt_f.astype(BF16)
                bd = jnp.where(bd_mask, jnp.concatenate([xq] * 4, axis=0), jnp.zeros((), BF16))
                y_diag = jnp.dot(w_ls, bd, preferred_element_type=F32)

                ht = ht_s[:, sl]
                y_off = jnp.dot(cg, ht.astype(BF16), preferred_element_type=F32)
                y = y_diag + y_off * jnp.exp2(acol) + dskip_ref[:, sl] * xs
                gated.append(y * sz_ref[rows_c, sl].astype(F32))

                xd = (xdt_f * jnp.exp2(a_last - acol)).astype(BF16)
                st = lax.dot_general(bg, xd, (((0,), (0,)), ((), ())), preferred_element_type=F32)
                ht_s[:, sl] = ht * jnp.exp2(a_last) + st

            ssq = sum(jnp.sum(t * t, axis=-1, keepdims=True) for t in gated)
            inv = lax.rsqrt(ssq * (1.0 / gw) + EPS)
            for qd in range(quads_per_group):
                sl = slice(g * gw + qd * qw, g * gw + (qd + 1) * qw)
                y_ref[rows_c, sl] = (gated[qd] * inv * normw_ref[:, sl]).astype(y_ref.dtype)
        return carry

    lax.fori_loop(0, n_chunks, chunk_body, 0, unroll=4)


def _ssd(conv_act, gate_act, dt, a_row, dskip_row, normw_row, batch, seq):
    rows = SSD_ROWS
    assert seq % rows == 0 and rows % (4 * CHUNK) == 0
    blocks_per_seq = seq // rows
    hh = np.arange(SSD_WIDTH) // SSD_HEAD_DIM
    expand = jnp.asarray(np.arange(SSD_HEADS)[:, None] == hh[None, :], BF16)
    rr = np.arange(rows)
    same_chunk = (rr[:, None] // CHUNK) == (rr[None, :] // CHUNK)
    bdtril = jnp.asarray(same_chunk & (rr[None, :] <= rr[:, None]), BF16)

    def row_map(b, r):
        return b * blocks_per_seq + r

    def col_spec(width, off):
        assert off % width == 0
        return pl.BlockSpec((rows, width), lambda b, r: (row_map(b, r), off // width))

    const = lambda shape: pl.BlockSpec(shape, lambda b, r: (0, 0))
    return pl.pallas_call(
        _ssd_kernel,
        grid=(batch, blocks_per_seq),
        in_specs=[
            col_spec(SSD_WIDTH, CV_XS),
            col_spec(BC_WIDTH, CV_B),
            col_spec(BC_WIDTH, CV_C),
            col_spec(SSD_WIDTH, GA_SZ),
            pl.BlockSpec((rows, SSD_HEADS), lambda b, r: (row_map(b, r), 0)),
            const((1, SSD_HEADS)),
            const((1, SSD_WIDTH)),
            const((1, SSD_WIDTH)),
            const((SSD_HEADS, SSD_WIDTH)),
            const((rows, rows)),
        ],
        out_specs=pl.BlockSpec((rows, SSD_WIDTH), lambda b, r: (row_map(b, r), 0)),
        out_shape=jax.ShapeDtypeStruct((batch * seq, SSD_WIDTH), BF16),
        scratch_shapes=[
            pltpu.VMEM((rows, SSD_WIDTH), F32),
            pltpu.VMEM((rows, SSD_WIDTH), F32),
            pltpu.VMEM((SSD_STATE, SSD_WIDTH), F32),
        ],
        compiler_params=pltpu.CompilerParams(
            dimension_semantics=("arbitrary", "arbitrary"),
            vmem_limit_bytes=VMEM_LIMIT_BYTES),
        name="ssd",
    )(conv_act, conv_act, conv_act, gate_act, dt, a_row, dskip_row, normw_row, expand, bdtril)


def _attn_kernel(q_ref, k_ref, v_ref, gb_ref, nrm_ref, lq1_ref, lk1_ref, lq2_ref, lk2_ref, subw_ref,
                 o_ref, vaug_s, m_s, acc_s, qq_s, *, lambda_init):
    hd = pl.program_id(1)
    qi = pl.program_id(2)
    tq = q_ref.shape[0]
    tk = ATT_TK
    seq = k_ref.shape[0]

    @pl.when(qi == 0)
    def _():
        vaug_s[:, :DA_VDIM] = v_ref[...].astype(vaug_s.dtype)
        vaug_s[:, DA_VDIM:] = jnp.ones((seq, DA_VDIM), vaug_s.dtype)

    assert tq % tk == 0
    band = ATT_BAND
    n_bands = tq // band
    n_full = qi * (tq // tk)
    half_lane = lax.broadcasted_iota(jnp.int32, (band, DA_VDIM), 1) & (LANES // 2 - 1)
    in_map0 = (half_lane < ROT_HALF) | ((half_lane >= ROT_DIM) & (half_lane < ROT_DIM + ROT_PASS))
    zero = jnp.zeros((), q_ref.dtype)

    qq = []
    for c in range(n_bands):
        q = q_ref[c * band:(c + 1) * band, :]
        qq.append(jnp.concatenate([jnp.where(in_map0, q, zero), jnp.where(in_map0, zero, q)], axis=0))

    def band_scores(c, k):
        return lax.dot_general(qq[c], k, (((1,), (1,)), ((), ())), preferred_element_type=F32)

    def band_rows(c):
        return slice(c * 2 * band, (c + 1) * 2 * band)

    def diagonal_operands(c):
        nk = (c + 1) * band
        k0 = pl.multiple_of(qi * tq, tq)
        r_chunk = (c * band + lax.broadcasted_iota(jnp.int32, (band, nk), 0)) // CHUNK
        k_chunk = lax.broadcasted_iota(jnp.int32, (band, nk), 1) // CHUNK
        ok = k_chunk <= r_chunk
        return k_ref[pl.ds(k0, nk), :], vaug_s[pl.ds(k0, nk), :], jnp.concatenate([ok, ok], axis=0)

    def full_operands(kt):
        k0 = pl.multiple_of(kt * tk, tk)
        return k_ref[pl.ds(k0, tk), :], vaug_s[pl.ds(k0, tk), :]

    norm_row = nrm_ref[0, 0:1, :]
    norm_lane = lax.broadcasted_iota(jnp.int32, norm_row.shape, 1)
    q_norm2 = jnp.max(jnp.where(norm_lane == hd, norm_row, 0.0))
    k_norm2 = jnp.max(jnp.where(norm_lane == DA_HEADS + hd, norm_row, 0.0))
    unshifted_ok = q_norm2 * k_norm2 <= ATT_SAFE_LOG2 * ATT_SAFE_LOG2

    @pl.when(unshifted_ok)
    def _():
        for c in range(n_bands):
            k, vt, ok = diagonal_operands(c)
            p = jnp.exp2(jnp.where(ok, band_scores(c, k), NEG_BIG)).astype(vaug_s.dtype)
            acc_s[band_rows(c), :] = jnp.dot(p, vt, preferred_element_type=F32)

        def tile(kt, carry):
            k, vt = full_operands(kt)
            for c in range(n_bands):
                p = jnp.exp2(band_scores(c, k)).astype(vaug_s.dtype)
                acc_s[band_rows(c), :] += jnp.dot(p, vt, preferred_element_type=F32)
            return carry

        lax.fori_loop(0, n_full, tile, 0)

    @pl.when(jnp.logical_not(unshifted_ok))
    def _():
        m_s[...] = jnp.full_like(m_s, NEG_BIG)
        acc_s[...] = jnp.zeros_like(acc_s)
        for c in range(n_bands):
            qq_s[c] = qq[c]

        def softmax_pv(c, k0, nk, diagonal):
            rows = pl.ds(pl.multiple_of(c * 2 * band, 2 * band), 2 * band)
            k, vt = k_ref[pl.ds(k0, nk), :], vaug_s[pl.ds(k0, nk), :]
            s = lax.dot_general(qq_s[c], k, (((1,), (1,)), ((), ())), preferred_element_type=F32)
            if diagonal:
                r_chunk = (c * band + lax.broadcasted_iota(jnp.int32, (band, nk), 0)) // CHUNK
                k_chunk = lax.broadcasted_iota(jnp.int32, (band, nk), 1) // CHUNK
                ok = k_chunk <= r_chunk
                s = jnp.where(jnp.concatenate([ok, ok], axis=0), s, NEG_BIG)
            m_old = m_s[rows, :]
            m_new = jnp.maximum(m_old, jnp.max(s, axis=-1, keepdims=True))
            alpha = jnp.exp2(m_old - m_new)
            p = jnp.exp2(s - m_new[:, 0:1]).astype(vaug_s.dtype)
            pv = jnp.dot(p, vt, preferred_element_type=F32)
            acc_s[rows, :] = jnp.concatenate([alpha, alpha], axis=1) * acc_s[rows, :] + pv
            m_s[rows, :] = m_new

        def band_loop(k0, nk, diagonal):
            def body(c, carry):
                softmax_pv(c, k0, nk, diagonal)
                return carry
            lax.fori_loop(0, n_bands, body, 0)

        def tile(kt, carry):
            band_loop(pl.multiple_of(kt * tk, tk), tk, False)
            return carry

        lax.fori_loop(0, n_full, tile, 0)
        band_loop(pl.multiple_of(qi * tq, tq), tq, True)

    lam = (jnp.exp(jnp.sum(lq1_ref[...] * lk1_ref[...], axis=-1, keepdims=True))
           - jnp.exp(jnp.sum(lq2_ref[...] * lk2_ref[...], axis=-1, keepdims=True)) + lambda_init)
    scale = subw_ref[...] * (1.0 - lambda_init)
    for c in range(n_bands):
        acc = acc_s[c * 2 * band:(c + 1) * 2 * band, :]
        o_all = acc[:, :DA_VDIM] / acc[:, DA_VDIM:]
        o = o_all[:band] - lam * o_all[band:]
        ms = jnp.mean(o * o, axis=-1, keepdims=True)
        o = o * lax.rsqrt(ms + EPS) * scale
        rows = slice(c * band, (c + 1) * band)
        o_ref[rows, :] = (o * gb_ref[rows, :].astype(F32)).astype(o_ref.dtype)


def _attention(attn_act, gate_act, norms, lq1, lk1, lq2, lk2, subw, batch, seq, lambda_init):
    tq = ATT_TQ
    assert seq % tq == 0 and tq % ATT_TK == 0 and tq % ATT_BAND == 0 and ATT_BAND % CHUNK == 0
    q_tiles = seq // tq

    def head_spec(rows_blk, off):
        assert off % DA_VDIM == 0
        if rows_blk == seq:
            return pl.BlockSpec((seq, DA_VDIM), lambda b, hd, qi: (b, off // DA_VDIM + hd))
        return pl.BlockSpec((rows_blk, DA_VDIM), lambda b, hd, qi: (b * q_tiles + qi, off // DA_VDIM + hd))

    vec = lambda n: pl.BlockSpec((1, n), lambda b, hd, qi: (0, 0))
    kern = functools.partial(_attn_kernel, lambda_init=lambda_init)
    return pl.pallas_call(
        kern,
        grid=(batch, DA_HEADS, q_tiles),
        in_specs=[
            head_spec(tq, AT_Q),
            head_spec(seq, AT_K),
            head_spec(seq, AT_V),
            head_spec(tq, GA_GB),
            pl.BlockSpec((1, SUBLANES, LANES), lambda b, hd, qi: (b, 0, 0)),
            vec(DA_HEAD_DIM), vec(DA_HEAD_DIM), vec(DA_HEAD_DIM), vec(DA_HEAD_DIM),
            vec(DA_VDIM),
        ],
        out_specs=pl.BlockSpec((tq, DA_VDIM), lambda b, hd, qi: (b * q_tiles + qi, hd)),
        out_shape=jax.ShapeDtypeStruct((batch * seq, DA_WIDTH), BF16),
        scratch_shapes=[
            pltpu.VMEM((seq, 2 * DA_VDIM), BF16),
            pltpu.VMEM((2 * tq, LANES), F32),
            pltpu.VMEM((2 * tq, 2 * DA_VDIM), F32),
            pltpu.VMEM((tq // ATT_BAND, 2 * ATT_BAND, DA_VDIM), BF16),
        ],
        compiler_params=pltpu.CompilerParams(
            dimension_semantics=("arbitrary", "arbitrary", "arbitrary"),
            vmem_limit_bytes=VMEM_LIMIT_BYTES),
        name="diffattn",
    )(attn_act, attn_act, attn_act, gate_act, norms, lq1, lk1, lq2, lk2, subw)


def _out_kernel(x_ref, ya_ref, ob_ref, g0_ref, g1_ref, wa_ref, wb_ref, wo_ref, lng_ref, lnb_ref,
                out_ref, *, alpha):
    d = functools.partial(jnp.dot, preferred_element_type=F32)
    band = x_ref.shape[0] // OUT_BANDS
    for c in range(OUT_BANDS):
        rows = slice(c * band, (c + 1) * band)
        branch_a = d(ya_ref[rows, :], wa_ref[...])
        branch_b = d(ob_ref[rows, :], wb_ref[...])
        merged = g0_ref[rows, :].astype(F32) * branch_a + g1_ref[rows, :].astype(F32) * branch_b
        y = d(merged.astype(BF16), wo_ref[...])
        r = alpha * x_ref[rows, :] + y
        mu = jnp.mean(r, axis=-1, keepdims=True)
        rc = r - mu
        var = jnp.mean(rc * rc, axis=-1, keepdims=True)
        out_ref[rows, :] = rc * lax.rsqrt(var + EPS) * lng_ref[...] + lnb_ref[...]


def _output_stage(x2d, y_ssd, o_att, gate_act, w_a, w_b, w_o, ln_g, ln_b, alpha):
    t_rows = x2d.shape[0]
    tm = OUT_TM
    assert t_rows % tm == 0 and GA_GM % D_MODEL == 0
    row = lambda width: pl.BlockSpec((tm, width), lambda i: (i, 0))
    full = lambda shape: pl.BlockSpec(shape, lambda i: (0, 0))
    weight = lambda shape: pl.BlockSpec(shape, lambda i: (0, 0), pipeline_mode=pl.Buffered(1))
    gate_spec = lambda k: pl.BlockSpec((tm, D_MODEL), lambda i: (i, GA_GM // D_MODEL + k))
    return pl.pallas_call(
        functools.partial(_out_kernel, alpha=alpha),
        grid=(t_rows // tm,),
        in_specs=[row(D_MODEL), row(SSD_WIDTH), row(DA_WIDTH), gate_spec(0), gate_spec(1),
                  weight((SSD_WIDTH, D_MODEL)), weight((DA_WIDTH, D_MODEL)), weight((D_MODEL, D_MODEL)),
                  full((1, D_MODEL)), full((1, D_MODEL))],
        out_specs=row(D_MODEL),
        out_shape=jax.ShapeDtypeStruct((t_rows, D_MODEL), F32),
        compiler_params=pltpu.CompilerParams(
            dimension_semantics=("arbitrary",),
            vmem_limit_bytes=VMEM_LIMIT_BYTES),
        name="outproj",
    )(x2d, y_ssd, o_att, gate_act, gate_act, w_a, w_b, w_o, ln_g, ln_b)


def _rope_tables(seq):
    pos = np.arange(seq, dtype=np.float64)
    inv_freq = ROPE_THETA ** (-np.arange(0, ROT_DIM, 2, dtype=np.float64) / ROT_DIM)
    ang = pos[:, None] * inv_freq[None, :]
    cos, sin = np.cos(ang).astype(np.float32), np.sin(ang).astype(np.float32)
    ones = np.ones((seq, LANES // 2 - ROT_DIM), np.float32)
    zeros = np.zeros_like(ones)
    cos_t = np.concatenate([cos, cos, ones, cos, cos, ones], axis=1)
    sin_t = np.concatenate([-sin, -sin, zeros, sin, sin, zeros], axis=1)
    return jnp.asarray(cos_t), jnp.asarray(sin_t)


def _head_lane_permutation():
    src = np.zeros(LANES, np.int64)
    for m in range(2):
        for d in range(DA_HEAD_DIM):
            if d < ROT_HALF:
                lane = m * ROT_HALF + d
            elif d < ROT_DIM:
                lane = LANES // 2 + m * ROT_HALF + (d - ROT_HALF)
            elif d < ROT_DIM + ROT_PASS:
                lane = ROT_DIM + m * ROT_PASS + (d - ROT_DIM)
            else:
                lane = LANES // 2 + ROT_DIM + m * ROT_PASS + (d - ROT_DIM - ROT_PASS)
            src[lane] = m * DA_HEAD_DIM + d
    perm = np.zeros((LANES, LANES), np.float32)
    perm[src, np.arange(LANES)] = 1.0
    return jnp.asarray(perm, BF16)


def _layer(x2d, batch, seq, depth_total, l, w_in, b_gate, conv_w, conv_b, dt_bias, a_log, d_skip,
           ssd_norm_w, lambda_q1, lambda_k1, lambda_q2, lambda_k2, subln_w, w_a, w_b, w_o, ln_g, ln_b,
           tables):
    alpha = (2.0 * depth_total) ** 0.25
    lambda_init = 0.8 - 0.6 * math.exp(-0.3 * l)
    w_gate, w_attn, w_conv = _prep_weights(w_in)
    p_conv = jnp.concatenate([conv_w.astype(F32), conv_b[None, :].astype(F32),
                              jnp.zeros((SUBLANES - CONV_WIDTH - 1, CONV_CH), F32)], axis=0)

    gate_act = _proj_gate(x2d, w_gate, b_gate[None, :].astype(F32))
    attn_act, qk_norms = _proj_attn(x2d, w_attn, *tables, seq)
    conv_act, dt = _proj_conv(x2d, w_conv, p_conv, dt_bias[None, :].astype(F32), seq)

    a_row = -jnp.exp(a_log.astype(F32))[None, :]
    dskip_row = jnp.repeat(d_skip.astype(F32), SSD_HEAD_DIM)[None, :]
    y_ssd = _ssd(conv_act, gate_act, dt, a_row, dskip_row, ssd_norm_w[None, :].astype(F32), batch, seq)

    o_att = _attention(attn_act, gate_act, qk_norms, lambda_q1[None, :], lambda_k1[None, :],
                       lambda_q2[None, :], lambda_k2[None, :], subln_w[None, :], batch, seq, lambda_init)

    return _output_stage(x2d, y_ssd, o_att, gate_act, w_a.astype(BF16), w_b.astype(BF16), w_o.astype(BF16),
                         ln_g[None, :], ln_b[None, :], alpha)


def kernel(x, w_in, b_gate, conv_w, conv_b, dt_bias, a_log, d_skip, ssd_norm_w, lambda_q1, lambda_k1,
           lambda_q2, lambda_k2, subln_w, w_a, w_b, w_o, ln_g, ln_b):
    batch, seq, _ = x.shape
    depth = w_in.shape[0]
    tables = _rope_tables(seq)
    x2d = x.reshape(batch * seq, D_MODEL)
    for l in range(depth):
        x2d = _layer(x2d, batch, seq, depth, l, w_in[l], b_gate[l], conv_w[l], conv_b[l], dt_bias[l],
                     a_log[l], d_skip[l], ssd_norm_w[l], lambda_q1[l], lambda_k1[l], lambda_q2[l],
                     lambda_k2[l], subln_w[l], w_a[l], w_b[l], w_o[l], ln_g[l], ln_b[l], tables)
    return x2d.reshape(batch, seq, D_MODEL)
```

```python
import functools
import math

import jax
import jax.numpy as jnp
import numpy as np
from jax import lax
from jax.experimental import pallas as pl
from jax.experimental.pallas import tpu as pltpu

F32 = jnp.float32
BF16 = jnp.bfloat16

D_MODEL = 1024
CHUNK = 64
SSD_HEADS = 16
SSD_HEAD_DIM = 64
SSD_WIDTH = SSD_HEADS * SSD_HEAD_DIM
SSD_GROUPS = 2
SSD_STATE = 128
CONV_WIDTH = 4
BC_WIDTH = SSD_GROUPS * SSD_STATE
CONV_CH = SSD_WIDTH + 2 * BC_WIDTH
DA_HEADS = 8
DA_HEAD_DIM = 64
DA_VDIM = 2 * DA_HEAD_DIM
DA_WIDTH = DA_HEADS * DA_VDIM
ROPE_THETA = 500000.0
ROT_DIM = DA_HEAD_DIM // 4
ROT_HALF = ROT_DIM // 2
N_BRANCH = 2
EPS = 1e-5
IN_SIZES = (SSD_WIDTH, CONV_CH, SSD_HEADS, DA_WIDTH, DA_WIDTH, DA_WIDTH, DA_WIDTH, N_BRANCH * D_MODEL)
W_OFFS = tuple(sum(IN_SIZES[:n]) for n in range(len(IN_SIZES) + 1))

GA_SZ, GA_GB, GA_GM = 0, SSD_WIDTH, SSD_WIDTH + DA_WIDTH
GA_COLS = GA_GM + N_BRANCH * D_MODEL
AT_Q, AT_K, AT_V = 0, DA_WIDTH, 2 * DA_WIDTH
AT_COLS = 3 * DA_WIDTH
CV_XS, CV_B, CV_C = 0, SSD_WIDTH, SSD_WIDTH + BC_WIDTH

ROT_PASS = (DA_HEAD_DIM - ROT_DIM) // 2

LANES = 128
SUBLANES = 8
BF16_ROWS = 16
VMEM_LIMIT_BYTES = 56 * 1024 * 1024

PREP_TC = 128
IN_TM = 1024
IN_TM_WIDE = 1024
SSD_ROWS = 512
ATT_TQ = 2048
ATT_TK = 2048
ATT_BAND = 256
ATT_SAFE_LOG2 = 64.0
NORM_SLACK = 1.0 + 2.0 ** -7
OUT_TM = 1024
OUT_BANDS = 4

NEG_BIG = -1e30
LOG2E = 1.4426950408889634


def _sigmoid(t):
    return 0.5 * jnp.tanh(0.5 * t) + 0.5


def _split2(v):
    hi = v.astype(BF16)
    mid = (v - hi.astype(F32)).astype(BF16)
    return hi, mid


def _dot01_right(v, m01):
    hi, mid = _split2(v)
    d = functools.partial(jnp.dot, preferred_element_type=F32)
    return d(hi, m01) + d(mid, m01)


def _dot01_left(m01, v):
    hi, mid = _split2(v)
    d = functools.partial(jnp.dot, preferred_element_type=F32)
    return d(m01, hi) + d(m01, mid)


def _proj_params():
    return pltpu.CompilerParams(dimension_semantics=("arbitrary",),
                                vmem_limit_bytes=VMEM_LIMIT_BYTES)


def _prep_kernel(wt_ref, perm_ref, oa_ref, ob_ref, oo_ref, wg_ref, wa_ref, wc_ref, oa16_ref, ob16_ref, oo16_ref):
    oa16_ref[...] = oa_ref[...].astype(BF16)
    ob16_ref[...] = ob_ref[...].astype(BF16)
    oo16_ref[...] = oo_ref[...].astype(BF16)

    def rows(n):
        return wt_ref[W_OFFS[n]:W_OFFS[n + 1], :].astype(BF16)

    wg_ref[GA_SZ:GA_GB, :] = rows(0)
    wg_ref[GA_GB:GA_GM, :] = rows(6)
    wg_ref[GA_GM:, :] = rows(7)
    wc_ref[...] = wt_ref[W_OFFS[1]:W_OFFS[3], :].astype(BF16)
    perm = perm_ref[...]
    for base, n in ((AT_Q, 3), (AT_K, 4)):
        for hd in range(DA_HEADS):
            r0 = W_OFFS[n] + hd * LANES
            blk = wt_ref[r0:r0 + LANES, :].astype(BF16)
            moved = jnp.dot(perm, blk, preferred_element_type=F32)
            wa_ref[base + hd * LANES:base + (hd + 1) * LANES, :] = moved.astype(BF16)
    wa_ref[AT_V:, :] = rows(5)


def _prep_weights(w_in_l, w_a, w_b, w_o):
    tc = PREP_TC
    d_in, total = w_in_l.shape
    assert total == W_OFFS[-1] and d_in % tc == 0
    assert w_a.shape[1] == w_b.shape[1] == w_o.shape[1] == d_in
    perm_t = _head_lane_permutation().T
    heights = (GA_COLS, AT_COLS, CONV_CH + SSD_HEADS, w_a.shape[0], w_b.shape[0], w_o.shape[0])
    col_block = lambda ht: pl.BlockSpec((ht, tc), lambda i: (0, i))
    return pl.pallas_call(
        _prep_kernel,
        grid=(d_in // tc,),
        in_specs=[
            col_block(total),
            pl.BlockSpec((LANES, LANES), lambda i: (0, 0)),
            col_block(w_a.shape[0]), col_block(w_b.shape[0]), col_block(w_o.shape[0]),
        ],
        out_specs=tuple(col_block(ht) for ht in heights),
        out_shape=tuple(jax.ShapeDtypeStruct((ht, d_in), BF16) for ht in heights),
        compiler_params=_proj_params(),
        name="prep_weights",
    )(w_in_l.T, perm_t, w_a, w_b, w_o)


def _dot_nt(a, bt):
    return lax.dot_general(a, bt, (((1,), (1,)), ((), ())), preferred_element_type=F32)


def _proj_gate_kernel(x_ref, w_ref, bias_ref, o_ref):
    acc = _dot_nt(x_ref[...].astype(BF16), w_ref[...])
    t = acc[:, :GA_GM]
    o_ref[:, :GA_GM] = (t * _sigmoid(t)).astype(o_ref.dtype)
    o_ref[:, GA_GM:] = _sigmoid(acc[:, GA_GM:] + bias_ref[...]).astype(o_ref.dtype)


def _proj_gate(x2d, w, b_gate_row):
    t_rows = x2d.shape[0]
    tm = IN_TM_WIDE
    assert t_rows % tm == 0
    full = lambda shape: pl.BlockSpec(shape, lambda i: (0, 0))
    return pl.pallas_call(
        _proj_gate_kernel,
        grid=(t_rows // tm,),
        in_specs=[
            pl.BlockSpec((tm, D_MODEL), lambda i: (i, 0)),
            pl.BlockSpec((GA_COLS, D_MODEL), lambda i: (0, 0), pipeline_mode=pl.Buffered(1)),
            full((1, N_BRANCH * D_MODEL)),
        ],
        out_specs=pl.BlockSpec((tm, GA_COLS), lambda i: (i, 0)),
        out_shape=jax.ShapeDtypeStruct((t_rows, GA_COLS), BF16),
        compiler_params=_proj_params(),
        name="proj_gate",
    )(x2d, w, b_gate_row)


def _proj_attn_kernel(x_ref, w_ref, cos_ref, sin_ref, sel_ref, o_ref, nrm_ref, *, q_scale, tiles_per_seq):
    @pl.when(pl.program_id(0) % tiles_per_seq == 0)
    def _():
        nrm_ref[...] = jnp.zeros_like(nrm_ref)

    acc = _dot_nt(x_ref[...].astype(BF16), w_ref[...])
    cos = cos_ref[...]
    sin = sin_ref[...]
    cos_q = cos * q_scale
    sin_q = sin * q_scale
    squares = []
    for hd in range(AT_V // LANES):
        sl = slice(hd * LANES, (hd + 1) * LANES)
        a = acc[:, sl]
        partner = pltpu.roll(a, LANES // 2, axis=1)
        c, s = (cos_q, sin_q) if hd < AT_K // LANES else (cos, sin)
        rot = (a * c + partner * s).astype(o_ref.dtype)
        o_ref[:, sl] = rot
        rf = rot.astype(F32)
        squares.append((rf * rf).astype(BF16))
    o_ref[:, AT_V:] = acc[:, AT_V:].astype(o_ref.dtype)

    norms = jnp.dot(jnp.concatenate(squares, axis=1), sel_ref[...], preferred_element_type=F32)
    top = jnp.max(norms, axis=0, keepdims=True) * NORM_SLACK
    nrm_ref[0] = jnp.maximum(nrm_ref[0], jnp.broadcast_to(top, nrm_ref.shape[1:]))


def _proj_attn(x2d, w, cos_t, sin_t, seq):
    t_rows = x2d.shape[0]
    tm = IN_TM_WIDE
    assert t_rows % tm == 0 and seq % tm == 0
    tiles_per_seq = seq // tm
    table = lambda: pl.BlockSpec((tm, LANES), lambda i: (i % tiles_per_seq, 0))
    selector = jnp.asarray((np.arange(AT_V)[:, None] // LANES) == np.arange(LANES)[None, :], BF16)
    return pl.pallas_call(
        functools.partial(_proj_attn_kernel, q_scale=DA_HEAD_DIM ** -0.5 * LOG2E, tiles_per_seq=tiles_per_seq),
        grid=(t_rows // tm,),
        in_specs=[
            pl.BlockSpec((tm, D_MODEL), lambda i: (i, 0)),
            pl.BlockSpec((AT_COLS, D_MODEL), lambda i: (0, 0), pipeline_mode=pl.Buffered(1)),
            table(), table(),
            pl.BlockSpec((AT_V, LANES), lambda i: (0, 0)),
        ],
        out_specs=[
            pl.BlockSpec((tm, AT_COLS), lambda i: (i, 0)),
            pl.BlockSpec((1, SUBLANES, LANES), lambda i: (i // tiles_per_seq, 0, 0)),
        ],
        out_shape=[
            jax.ShapeDtypeStruct((t_rows, AT_COLS), BF16),
            jax.ShapeDtypeStruct((t_rows // seq, SUBLANES, LANES), F32),
        ],
        compiler_params=_proj_params(),
        name="proj_attn",
    )(x2d, w, cos_t, sin_t, selector)


def _proj_conv_kernel(x_ref, xh_ref, w_ref, p_ref, dtb_ref, o_ref, dt_ref, *, tiles_per_seq):
    i = pl.program_id(0)
    tm = o_ref.shape[0]
    xb = jnp.concatenate([x_ref[...], xh_ref[...]], axis=0).astype(BF16)
    full = _dot_nt(xb, w_ref[...])
    acc = full[:tm, :CONV_CH]
    halo = jnp.where(i % tiles_per_seq == 0, 0.0, full[tm:, :CONV_CH])

    bias = p_ref[CONV_WIDTH:CONV_WIDTH + 1, :]
    taps = [p_ref[k:k + 1, :] for k in range(CONV_WIDTH)]

    def conv(rows_val):
        out = taps[0] * rows_val
        for k in range(1, CONV_WIDTH):
            out = pltpu.roll(out, 1, axis=0) + taps[k] * rows_val
        out = out + bias
        return out * _sigmoid(out)

    head = BF16_ROWS
    ext = jnp.concatenate([halo, acc[:head]], axis=0)
    o_ref[:head, :] = conv(ext)[SUBLANES:, :].astype(o_ref.dtype)
    o_ref[head:, :] = conv(acc)[head:, :].astype(o_ref.dtype)

    raw = full[:tm, CONV_CH:] + dtb_ref[...]
    dt_ref[...] = jnp.maximum(raw, 0.0) + jnp.log(1.0 + jnp.exp(-jnp.abs(raw)))


def _proj_conv(x2d, w, params, dt_bias, seq):
    t_rows = x2d.shape[0]
    tm = IN_TM
    assert t_rows % tm == 0 and seq % tm == 0
    tiles_per_seq = seq // tm
    halo_blocks = tm // SUBLANES
    full = lambda shape: pl.BlockSpec(shape, lambda i: (0, 0))
    return pl.pallas_call(
        functools.partial(_proj_conv_kernel, tiles_per_seq=tiles_per_seq),
        grid=(t_rows // tm,),
        in_specs=[
            pl.BlockSpec((tm, D_MODEL), lambda i: (i, 0)),
            pl.BlockSpec((SUBLANES, D_MODEL), lambda i: (jnp.maximum(i * halo_blocks - 1, 0), 0)),
            full((CONV_CH + SSD_HEADS, D_MODEL)),
            full((SUBLANES, CONV_CH)),
            full((1, SSD_HEADS)),
        ],
        out_specs=[
            pl.BlockSpec((tm, CONV_CH), lambda i: (i, 0)),
            pl.BlockSpec((tm, SSD_HEADS), lambda i: (i, 0)),
        ],
        out_shape=[
            jax.ShapeDtypeStruct((t_rows, CONV_CH), BF16),
            jax.ShapeDtypeStruct((t_rows, SSD_HEADS), F32),
        ],
        compiler_params=pltpu.CompilerParams(dimension_semantics=("arbitrary",),
                                             vmem_limit_bytes=VMEM_LIMIT_BYTES),
        name="proj_conv",
    )(x2d, x2d, w, params, dt_bias)


def _ssd_kernel(xs_ref, b_ref, c_ref, sz_ref, dt_ref, a_ref, dskip_ref, normw_ref,
                expand_ref, bdtril_ref,
                y_ref, acol_s, dtx_s, ht_s):
    rows = xs_ref.shape[0]
    n_chunks = rows // CHUNK
    gw = SSD_WIDTH // SSD_GROUPS

    @pl.when(pl.program_id(1) == 0)
    def _():
        ht_s[...] = jnp.zeros_like(ht_s)

    dt = dt_ref[...]
    a_cs = _dot01_left(bdtril_ref[...], dt * (a_ref[...] * LOG2E))
    expand = expand_ref[...]
    acol_s[...] = _dot01_right(a_cs, expand)
    dtx_s[...] = _dot01_right(dt, expand)

    qw = 4 * SSD_HEAD_DIM
    quads_per_group = gw // qw
    lane = lax.broadcasted_iota(jnp.int32, (CHUNK, qw), 1)
    row = lax.broadcasted_iota(jnp.int32, (CHUNK, qw), 0)
    s_of_lane = lane & (CHUNK - 1)
    diag_mask = s_of_lane == row
    causal_mask = s_of_lane <= row
    bd_r = lax.broadcasted_iota(jnp.int32, (4 * CHUNK, qw), 0) // CHUNK
    bd_c = lax.broadcasted_iota(jnp.int32, (4 * CHUNK, qw), 1) // SSD_HEAD_DIM
    bd_mask = bd_r == bd_c

    def chunk_body(c, carry):
        r0 = pl.multiple_of(c * CHUNK, CHUNK)
        rows_c = pl.ds(r0, CHUNK)
        bc = b_ref[rows_c, :]
        cc = c_ref[rows_c, :]
        for g in range(SSD_GROUPS):
            cg = cc[:, g * SSD_STATE:(g + 1) * SSD_STATE]
            bg = bc[:, g * SSD_STATE:(g + 1) * SSD_STATE]
            cb_g = lax.dot_general(cg, bg, (((1,), (1,)), ((), ())), preferred_element_type=F32)
            cb_rep = jnp.concatenate([cb_g] * 4, axis=1)
            gated = []
            for qd in range(quads_per_group):
                sl = slice(g * gw + qd * qw, g * gw + (qd + 1) * qw)
                acol = acol_s[rows_c, sl]
                a_last = acol[CHUNK - 1:CHUNK, :]
                xs = xs_ref[rows_c, sl].astype(F32)
                xdt_f = xs * dtx_s[rows_c, sl]

                arow = jnp.sum(jnp.where(diag_mask, acol, 0.0), axis=0, keepdims=True)
                decay_ls = jnp.exp2(jnp.where(causal_mask, acol - arow, NEG_BIG))
                w_ls = (cb_rep * decay_ls).astype(BF16)
                xq = xdt_f.astype(BF16)
                bd = jnp.where(bd_mask, jnp.concatenate([xq] * 4, axis=0), jnp.zeros((), BF16))
                y_diag = jnp.dot(w_ls, bd, preferred_element_type=F32)

                ht = ht_s[:, sl]
                y_off = jnp.dot(cg, ht.astype(BF16), preferred_element_type=F32)
                y = y_diag + y_off * jnp.exp2(acol) + dskip_ref[:, sl] * xs
                gated.append(y * sz_ref[rows_c, sl].astype(F32))

                xd = (xdt_f * jnp.exp2(a_last - acol)).astype(BF16)
                st = lax.dot_general(bg, xd, (((0,), (0,)), ((), ())), preferred_element_type=F32)
                ht_s[:, sl] = ht * jnp.exp2(a_last) + st

            ssq = sum(jnp.sum(t * t, axis=-1, keepdims=True) for t in gated)
            inv = lax.rsqrt(ssq * (1.0 / gw) + EPS)
            for qd in range(quads_per_group):
                sl = slice(g * gw + qd * qw, g * gw + (qd + 1) * qw)
                y_ref[rows_c, sl] = (gated[qd] * inv * normw_ref[:, sl]).astype(y_ref.dtype)
        return carry

    lax.fori_loop(0, n_chunks, chunk_body, 0, unroll=4)


def _ssd(conv_act, gate_act, dt, a_row, dskip_row, normw_row, batch, seq):
    rows = SSD_ROWS
    assert seq % rows == 0 and rows % (4 * CHUNK) == 0
    blocks_per_seq = seq // rows
    hh = np.arange(SSD_WIDTH) // SSD_HEAD_DIM
    expand = jnp.asarray(np.arange(SSD_HEADS)[:, None] == hh[None, :], BF16)
    rr = np.arange(rows)
    same_chunk = (rr[:, None] // CHUNK) == (rr[None, :] // CHUNK)
    bdtril = jnp.asarray(same_chunk & (rr[None, :] <= rr[:, None]), BF16)

    def row_map(b, r):
        return b * blocks_per_seq + r

    def col_spec(width, off):
        assert off % width == 0
        return pl.BlockSpec((rows, width), lambda b, r: (row_map(b, r), off // width))

    const = lambda shape: pl.BlockSpec(shape, lambda b, r: (0, 0))
    return pl.pallas_call(
        _ssd_kernel,
        grid=(batch, blocks_per_seq),
        in_specs=[
            col_spec(SSD_WIDTH, CV_XS),
            col_spec(BC_WIDTH, CV_B),
            col_spec(BC_WIDTH, CV_C),
            col_spec(SSD_WIDTH, GA_SZ),
            pl.BlockSpec((rows, SSD_HEADS), lambda b, r: (row_map(b, r), 0)),
            const((1, SSD_HEADS)),
            const((1, SSD_WIDTH)),
            const((1, SSD_WIDTH)),
            const((SSD_HEADS, SSD_WIDTH)),
            const((rows, rows)),
        ],
        out_specs=pl.BlockSpec((rows, SSD_WIDTH), lambda b, r: (row_map(b, r), 0)),
        out_shape=jax.ShapeDtypeStruct((batch * seq, SSD_WIDTH), BF16),
        scratch_shapes=[
            pltpu.VMEM((rows, SSD_WIDTH), F32),
            pltpu.VMEM((rows, SSD_WIDTH), F32),
            pltpu.VMEM((SSD_STATE, SSD_WIDTH), F32),
        ],
        compiler_params=pltpu.CompilerParams(
            dimension_semantics=("arbitrary", "arbitrary"),
            vmem_limit_bytes=VMEM_LIMIT_BYTES),
        name="ssd",
    )(conv_act, conv_act, conv_act, gate_act, dt, a_row, dskip_row, normw_row, expand, bdtril)


def _attn_kernel(q_ref, k_ref, v_ref, gb_ref, nrm_ref, lq1_ref, lk1_ref, lq2_ref, lk2_ref, subw_ref,
                 o_ref, vaug_s, m_s, acc_s, qq_s, *, lambda_init):
    hd = pl.program_id(1)
    qi = pl.program_id(2)
    tq = q_ref.shape[0]
    tk = ATT_TK
    seq = k_ref.shape[0]

    @pl.when(qi == 0)
    def _():
        vaug_s[:, :DA_VDIM] = v_ref[...].astype(vaug_s.dtype)
        vaug_s[:, DA_VDIM:] = jnp.ones((seq, DA_VDIM), vaug_s.dtype)

    assert tq % tk == 0
    band = ATT_BAND
    n_bands = tq // band
    n_full = qi * (tq // tk)
    half_lane = lax.broadcasted_iota(jnp.int32, (band, DA_VDIM), 1) & (LANES // 2 - 1)
    in_map0 = (half_lane < ROT_HALF) | ((half_lane >= ROT_DIM) & (half_lane < ROT_DIM + ROT_PASS))
    zero = jnp.zeros((), q_ref.dtype)

    qq = []
    for c in range(n_bands):
        q = q_ref[c * band:(c + 1) * band, :]
        qq.append(jnp.concatenate([jnp.where(in_map0, q, zero), jnp.where(in_map0, zero, q)], axis=0))

    def band_scores(c, k):
        return lax.dot_general(qq[c], k, (((1,), (1,)), ((), ())), preferred_element_type=F32)

    def band_rows(c):
        return slice(c * 2 * band, (c + 1) * 2 * band)

    def diagonal_operands(c):
        nk = (c + 1) * band
        k0 = pl.multiple_of(qi * tq, tq)
        r_chunk = (c * band + lax.broadcasted_iota(jnp.int32, (band, nk), 0)) // CHUNK
        k_chunk = lax.broadcasted_iota(jnp.int32, (band, nk), 1) // CHUNK
        ok = k_chunk <= r_chunk
        return k_ref[pl.ds(k0, nk), :], vaug_s[pl.ds(k0, nk), :], jnp.concatenate([ok, ok], axis=0)

    def full_operands(kt):
        k0 = pl.multiple_of(kt * tk, tk)
        return k_ref[pl.ds(k0, tk), :], vaug_s[pl.ds(k0, tk), :]

    norm_row = nrm_ref[0, 0:1, :]
    norm_lane = lax.broadcasted_iota(jnp.int32, norm_row.shape, 1)
    q_norm2 = jnp.max(jnp.where(norm_lane == hd, norm_row, 0.0))
    k_norm2 = jnp.max(jnp.where(norm_lane == DA_HEADS + hd, norm_row, 0.0))
    unshifted_ok = q_norm2 * k_norm2 <= ATT_SAFE_LOG2 * ATT_SAFE_LOG2

    @pl.when(unshifted_ok)
    def _():
        for c in range(n_bands):
            k, vt, ok = diagonal_operands(c)
            p = jnp.exp2(jnp.where(ok, band_scores(c, k), NEG_BIG)).astype(vaug_s.dtype)
            acc_s[band_rows(c), :] = jnp.dot(p, vt, preferred_element_type=F32)

        def tile(kt, carry):
            k, vt = full_operands(kt)
            for c in range(n_bands):
                p = jnp.exp2(band_scores(c, k)).astype(vaug_s.dtype)
                acc_s[band_rows(c), :] += jnp.dot(p, vt, preferred_element_type=F32)
            return carry

        lax.fori_loop(0, n_full, tile, 0)

    @pl.when(jnp.logical_not(unshifted_ok))
    def _():
        m_s[...] = jnp.full_like(m_s, NEG_BIG)
        acc_s[...] = jnp.zeros_like(acc_s)
        for c in range(n_bands):
            qq_s[c] = qq[c]

        def softmax_pv(c, k0, nk, diagonal):
            rows = pl.ds(pl.multiple_of(c * 2 * band, 2 * band), 2 * band)
            k, vt = k_ref[pl.ds(k0, nk), :], vaug_s[pl.ds(k0, nk), :]
            s = lax.dot_general(qq_s[c], k, (((1,), (1,)), ((), ())), preferred_element_type=F32)
            if diagonal:
                r_chunk = (c * band + lax.broadcasted_iota(jnp.int32, (band, nk), 0)) // CHUNK
                k_chunk = lax.broadcasted_iota(jnp.int32, (band, nk), 1) // CHUNK
                ok = k_chunk <= r_chunk
                s = jnp.where(jnp.concatenate([ok, ok], axis=0), s, NEG_BIG)
            m_old = m_s[rows, :]
            m_new = jnp.maximum(m_old, jnp.max(s, axis=-1, keepdims=True))
            alpha = jnp.exp2(m_old - m_new)
            p = jnp.exp2(s - m_new[:, 0:1]).astype(vaug_s.dtype)
            pv = jnp.dot(p, vt, preferred_element_type=F32)
            acc_s[rows, :] = jnp.concatenate([alpha, alpha], axis=1) * acc_s[rows, :] + pv
            m_s[rows, :] = m_new

        def band_loop(k0, nk, diagonal):
            def body(c, carry):
                softmax_pv(c, k0, nk, diagonal)
                return carry
            lax.fori_loop(0, n_bands, body, 0)

        def tile(kt, carry):
            band_loop(pl.multiple_of(kt * tk, tk), tk, False)
            return carry

        lax.fori_loop(0, n_full, tile, 0)
        band_loop(pl.multiple_of(qi * tq, tq), tq, True)

    lam = (jnp.exp(jnp.sum(lq1_ref[...] * lk1_ref[...], axis=-1, keepdims=True))
           - jnp.exp(jnp.sum(lq2_ref[...] * lk2_ref[...], axis=-1, keepdims=True)) + lambda_init)
    scale = subw_ref[...] * (1.0 - lambda_init)
    for c in range(n_bands):
        acc = acc_s[c * 2 * band:(c + 1) * 2 * band, :]
        o_all = acc[:, :DA_VDIM] / acc[:, DA_VDIM:]
        o = o_all[:band] - lam * o_all[band:]
        ms = jnp.mean(o * o, axis=-1, keepdims=True)
        o = o * lax.rsqrt(ms + EPS) * scale
        rows = slice(c * band, (c + 1) * band)
        o_ref[rows, :] = (o * gb_ref[rows, :].astype(F32)).astype(o_ref.dtype)


def _attention(attn_act, gate_act, norms, lq1, lk1, lq2, lk2, subw, batch, seq, lambda_init):
    tq = ATT_TQ
    assert seq % tq == 0 and tq % ATT_TK == 0 and tq % ATT_BAND == 0 and ATT_BAND % CHUNK == 0
    q_tiles = seq // tq

    def head_spec(rows_blk, off):
        assert off % DA_VDIM == 0
        if rows_blk == seq:
            return pl.BlockSpec((seq, DA_VDIM), lambda b, hd, qi: (b, off // DA_VDIM + hd))
        return pl.BlockSpec((rows_blk, DA_VDIM), lambda b, hd, qi: (b * q_tiles + qi, off // DA_VDIM + hd))

    vec = lambda n: pl.BlockSpec((1, n), lambda b, hd, qi: (0, 0))
    kern = functools.partial(_attn_kernel, lambda_init=lambda_init)
    return pl.pallas_call(
        kern,
        grid=(batch, DA_HEADS, q_tiles),
        in_specs=[
            head_spec(tq, AT_Q),
            head_spec(seq, AT_K),
            head_spec(seq, AT_V),
            head_spec(tq, GA_GB),
            pl.BlockSpec((1, SUBLANES, LANES), lambda b, hd, qi: (b, 0, 0)),
            vec(DA_HEAD_DIM), vec(DA_HEAD_DIM), vec(DA_HEAD_DIM), vec(DA_HEAD_DIM),
            vec(DA_VDIM),
        ],
        out_specs=pl.BlockSpec((tq, DA_VDIM), lambda b, hd, qi: (b * q_tiles + qi, hd)),
        out_shape=jax.ShapeDtypeStruct((batch * seq, DA_WIDTH), BF16),
        scratch_shapes=[
            pltpu.VMEM((seq, 2 * DA_VDIM), BF16),
            pltpu.VMEM((2 * tq, LANES), F32),
            pltpu.VMEM((2 * tq, 2 * DA_VDIM), F32),
            pltpu.VMEM((tq // ATT_BAND, 2 * ATT_BAND, DA_VDIM), BF16),
        ],
        compiler_params=pltpu.CompilerParams(
            dimension_semantics=("arbitrary", "arbitrary", "arbitrary"),
            vmem_limit_bytes=VMEM_LIMIT_BYTES),
        name="diffattn",
    )(attn_act, attn_act, attn_act, gate_act, norms, lq1, lk1, lq2, lk2, subw)


def _out_kernel(x_ref, ya_ref, ob_ref, g0_ref, g1_ref, wa_ref, wb_ref, wo_ref, lng_ref, lnb_ref,
                out_ref, *, alpha):
    d = functools.partial(jnp.dot, preferred_element_type=F32)
    band = x_ref.shape[0] // OUT_BANDS
    for c in range(OUT_BANDS):
        rows = slice(c * band, (c + 1) * band)
        branch_a = d(ya_ref[rows, :], wa_ref[...])
        branch_b = d(ob_ref[rows, :], wb_ref[...])
        merged = g0_ref[rows, :].astype(F32) * branch_a + g1_ref[rows, :].astype(F32) * branch_b
        y = d(merged.astype(BF16), wo_ref[...])
        r = alpha * x_ref[rows, :] + y
        mu = jnp.mean(r, axis=-1, keepdims=True)
        rc = r - mu
        var = jnp.mean(rc * rc, axis=-1, keepdims=True)
        out_ref[rows, :] = rc * lax.rsqrt(var + EPS) * lng_ref[...] + lnb_ref[...]


def _output_stage(x2d, y_ssd, o_att, gate_act, w_a, w_b, w_o, ln_g, ln_b, alpha):
    t_rows = x2d.shape[0]
    tm = OUT_TM
    assert t_rows % tm == 0 and GA_GM % D_MODEL == 0
    row = lambda width: pl.BlockSpec((tm, width), lambda i: (i, 0))
    full = lambda shape: pl.BlockSpec(shape, lambda i: (0, 0))
    weight = lambda shape: pl.BlockSpec(shape, lambda i: (0, 0), pipeline_mode=pl.Buffered(1))
    gate_spec = lambda k: pl.BlockSpec((tm, D_MODEL), lambda i: (i, GA_GM // D_MODEL + k))
    return pl.pallas_call(
        functools.partial(_out_kernel, alpha=alpha),
        grid=(t_rows // tm,),
        in_specs=[row(D_MODEL), row(SSD_WIDTH), row(DA_WIDTH), gate_spec(0), gate_spec(1),
                  weight((SSD_WIDTH, D_MODEL)), weight((DA_WIDTH, D_MODEL)), weight((D_MODEL, D_MODEL)),
                  full((1, D_MODEL)), full((1, D_MODEL))],
        out_specs=row(D_MODEL),
        out_shape=jax.ShapeDtypeStruct((t_rows, D_MODEL), F32),
        compiler_params=pltpu.CompilerParams(
            dimension_semantics=("arbitrary",),
            vmem_limit_bytes=VMEM_LIMIT_BYTES),
        name="outproj",
    )(x2d, y_ssd, o_att, gate_act, gate_act, w_a, w_b, w_o, ln_g, ln_b)


def _rope_tables(seq):
    pos = np.arange(seq, dtype=np.float64)
    inv_freq = ROPE_THETA ** (-np.arange(0, ROT_DIM, 2, dtype=np.float64) / ROT_DIM)
    ang = pos[:, None] * inv_freq[None, :]
    cos, sin = np.cos(ang).astype(np.float32), np.sin(ang).astype(np.float32)
    ones = np.ones((seq, LANES // 2 - ROT_DIM), np.float32)
    zeros = np.zeros_like(ones)
    cos_t = np.concatenate([cos, cos, ones, cos, cos, ones], axis=1)
    sin_t = np.concatenate([-sin, -sin, zeros, sin, sin, zeros], axis=1)
    return jnp.asarray(cos_t), jnp.asarray(sin_t)


def _head_lane_permutation():
    src = np.zeros(LANES, np.int64)
    for m in range(2):
        for d in range(DA_HEAD_DIM):
            if d < ROT_HALF:
                lane = m * ROT_HALF + d
            elif d < ROT_DIM:
                lane = LANES // 2 + m * ROT_HALF + (d - ROT_HALF)
            elif d < ROT_DIM + ROT_PASS:
                lane = ROT_DIM + m * ROT_PASS + (d - ROT_DIM)
            else:
                lane = LANES // 2 + ROT_DIM + m * ROT_PASS + (d - ROT_DIM - ROT_PASS)
            src[lane] = m * DA_HEAD_DIM + d
    perm = np.zeros((LANES, LANES), np.float32)
    perm[src, np.arange(LANES)] = 1.0
    return jnp.asarray(perm, BF16)


def _layer(x2d, batch, seq, depth_total, l, w_in, b_gate, conv_w, conv_b, dt_bias, a_log, d_skip,
           ssd_norm_w, lambda_q1, lambda_k1, lambda_q2, lambda_k2, subln_w, w_a, w_b, w_o, ln_g, ln_b,
           tables):
    alpha = (2.0 * depth_total) ** 0.25
    lambda_init = 0.8 - 0.6 * math.exp(-0.3 * l)
    w_gate, w_attn, w_conv, w_a16, w_b16, w_o16 = _prep_weights(w_in, w_a, w_b, w_o)
    p_conv = jnp.concatenate([conv_w.astype(F32), conv_b[None, :].astype(F32),
                              jnp.zeros((SUBLANES - CONV_WIDTH - 1, CONV_CH), F32)], axis=0)

    gate_act = _proj_gate(x2d, w_gate, b_gate[None, :].astype(F32))
    attn_act, qk_norms = _proj_attn(x2d, w_attn, *tables, seq)
    conv_act, dt = _proj_conv(x2d, w_conv, p_conv, dt_bias[None, :].astype(F32), seq)

    a_row = -jnp.exp(a_log.astype(F32))[None, :]
    dskip_row = jnp.repeat(d_skip.astype(F32), SSD_HEAD_DIM)[None, :]
    y_ssd = _ssd(conv_act, gate_act, dt, a_row, dskip_row, ssd_norm_w[None, :].astype(F32), batch, seq)

    o_att = _attention(attn_act, gate_act, qk_norms, lambda_q1[None, :], lambda_k1[None, :],
                       lambda_q2[None, :], lambda_k2[None, :], subln_w[None, :], batch, seq, lambda_init)

    return _output_stage(x2d, y_ssd, o_att, gate_act, w_a16, w_b16, w_o16, ln_g[None, :], ln_b[None, :], alpha)


def kernel(x, w_in, b_gate, conv_w, conv_b, dt_bias, a_log, d_skip, ssd_norm_w, lambda_q1, lambda_k1,
           lambda_q2, lambda_k2, subln_w, w_a, w_b, w_o, ln_g, ln_b):
    batch, seq, _ = x.shape
    depth = w_in.shape[0]
    tables = _rope_tables(seq)
    x2d = x.reshape(batch * seq, D_MODEL)
    for l in range(depth):
        x2d = _layer(x2d, batch, seq, depth, l, w_in[l], b_gate[l], conv_w[l], conv_b[l], dt_bias[l],
                     a_log[l], d_skip[l], ssd_norm_w[l], lambda_q1[l], lambda_k1[l], lambda_q2[l],
                     lambda_k2[l], subln_w[l], w_a[l], w_b[l], w_o[l], ln_g[l], ln_b[l], tables)
    return x2d.reshape(batch, seq, D_MODEL)
```
